```python
import jax, jax.numpy as jnp
from jax import lax
import numpy as np

D_MODEL = 1024
BATCH = 8
SEQ = 2048
DEPTH = 1
DEC_BATCH = 16
DEC_SEQ = 32
PAST_LEN = 1024

CHUNK = 64
POOL_WIDTH = 512
POOL_GROUPS = 4
POOL_GROUP_DIM = POOL_WIDTH // POOL_GROUPS
POOL_WINDOWS = (2, 4, 8, 16)
POOL_STATE = max(POOL_WINDOWS) - 1
SB_HEADS = 8
SB_HEAD_DIM = 64
SB_WIDTH = SB_HEADS * SB_HEAD_DIM
Q_BLOCK = 128
D_FF = 4 * D_MODEL
PLE_DIM = 256
IN_WIDTH = POOL_WIDTH + 3 * SB_WIDTH + 2 * D_MODEL
EPS = 1e-6

kernel_name = "stickbreak_pool_hybrid_step"


def rms_norm(x, g):
    xf = x.astype(jnp.float32)
    y = xf * lax.rsqrt(jnp.mean(xf * xf, axis=-1, keepdims=True) + EPS)
    return (y * g.astype(jnp.float32)).astype(x.dtype)


def multiscale_pool(u_hist, u, w_grp, scale):
    B, P, _ = u_hist.shape
    T = u.shape[1]
    full = jnp.concatenate([u_hist, u], axis=1)
    cs = jnp.cumsum(full.astype(jnp.float32), axis=1)
    cs = jnp.pad(cs, ((0, 0), (1, 0), (0, 0)))
    cs = cs.reshape(B, P + T + 1, POOL_GROUPS, POOL_GROUP_DIM)
    windows = jnp.array(POOL_WINDOWS, dtype=jnp.int32)
    end = P + 1 + jnp.arange(T, dtype=jnp.int32)
    start = jnp.maximum(end[:, None] - windows[None, :], 0)
    count = (end[:, None] - start).astype(jnp.float32)
    g_idx = jnp.arange(POOL_GROUPS, dtype=jnp.int32)
    cs_end = cs[:, end]
    cs_start = cs[:, start, g_idx[None, :]]
    mean = (cs_end - cs_start) / count[None, :, :, None]
    u4 = u.reshape(B, T, POOL_GROUPS, POOL_GROUP_DIM)
    y = (mean - u4.astype(jnp.float32)).astype(u.dtype)
    y = jnp.einsum('btgc,gcd->btgd', y, w_grp).reshape(B, T, POOL_WIDTH) * scale
    new_state = full[:, -POOL_STATE:]
    return y, new_state


def stick_breaking_block(q, k, v, q_pos, k_pos):
    z = jnp.einsum('bqhd,bkhd->bhqk', q.astype(jnp.float32), k.astype(jnp.float32)) * (SB_HEAD_DIM ** -0.5)
    causal = k_pos[None, :] < q_pos[:, None]
    log_not = jnp.where(causal, jax.nn.log_sigmoid(-z), 0.0)
    suffix = lax.cumsum(log_not, axis=3, reverse=True) - log_not
    a = jnp.where(causal, jnp.exp(jax.nn.log_sigmoid(z) + suffix), 0.0)
    o = jnp.einsum('bhqk,bkhd->bqhd', a, v.astype(jnp.float32))
    return o.astype(v.dtype)


def stick_breaking_prompt(q, k, v):
    B, T, H, Dh = q.shape
    nb = T // Q_BLOCK
    k_pos = jnp.arange(T, dtype=jnp.int32)
    qb = q.reshape(B, nb, Q_BLOCK, H, Dh).transpose(1, 0, 2, 3, 4)
    pos_b = k_pos.reshape(nb, Q_BLOCK)
    ob = lax.map(lambda args: stick_breaking_block(args[0], k, v, args[1], k_pos), (qb, pos_b))
    return ob.transpose(1, 0, 2, 3, 4).reshape(B, T, H, Dh)


def layer(x, p, pool_hist, k_hist, v_hist, is_prompt,
          g_mix, w_in, b_gate, w_pool_grp, pool_scale, w_pool_up, w_sb_up, w_out,
          g_mlp, w_up, w_down, g_ple, w_ple_gate, w_ple_proj):
    B, T, _ = x.shape
    P = k_hist.shape[1]
    xn = rms_norm(x, g_mix)
    h = xn @ w_in
    u, q, k, v, gates = jnp.split(
        h, [POOL_WIDTH, POOL_WIDTH + SB_WIDTH, POOL_WIDTH + 2 * SB_WIDTH, POOL_WIDTH + 3 * SB_WIDTH], axis=-1)
    gates = jax.nn.sigmoid((gates + b_gate).astype(jnp.float32)).astype(x.dtype)
    gate_a, gate_b = jnp.split(gates, 2, axis=-1)
    y_pool, pool_new = multiscale_pool(pool_hist, u, w_pool_grp, pool_scale)
    q = q.reshape(B, T, SB_HEADS, SB_HEAD_DIM)
    k = k.reshape(B, T, SB_HEADS, SB_HEAD_DIM)
    v = v.reshape(B, T, SB_HEADS, SB_HEAD_DIM)
    if is_prompt:
        o = stick_breaking_prompt(q, k, v)
    else:
        k_all = jnp.concatenate([k_hist, k], axis=1)
        v_all = jnp.concatenate([v_hist, v], axis=1)
        k_pos = jnp.arange(P + T, dtype=jnp.int32)
        q_pos = P + jnp.arange(T, dtype=jnp.int32)
        o = stick_breaking_block(q, k_all, v_all, q_pos, k_pos)
    y_sb = o.reshape(B, T, SB_WIDTH)
    merged = gate_a * (y_pool @ w_pool_up) + gate_b * (y_sb @ w_sb_up)
    x = x + merged @ w_out
    x = x + jnp.square(jax.nn.relu(rms_norm(x, g_mlp) @ w_up)) @ w_down
    x = x + jax.nn.sigmoid(rms_norm(x, g_ple) @ w_ple_gate) * (p @ w_ple_proj)
    return x, k, v, pool_new


def setup_inputs(seed: int = 0) -> dict:
    key = jax.random.key(seed)
    ks = jax.random.split(key, 24)
    f32 = jnp.float32

    def nrm(k, shape, scale):
        return jax.random.normal(k, shape, f32) * scale

    return {
        "x_prompt": nrm(ks[0], (BATCH, SEQ, D_MODEL), 1.0),
        "x_sample": nrm(ks[1], (DEC_BATCH, DEC_SEQ, D_MODEL), 1.0),
        "cache_k": nrm(ks[2], (DEPTH, DEC_BATCH, PAST_LEN, SB_HEADS, SB_HEAD_DIM), 1.0),
        "cache_v": nrm(ks[3], (DEPTH, DEC_BATCH, PAST_LEN, SB_HEADS, SB_HEAD_DIM), 1.0),
        "state_pool": nrm(ks[4], (DEPTH, DEC_BATCH, POOL_STATE, POOL_WIDTH), 1.0),
        "p_prompt": nrm(ks[5], (DEPTH, BATCH, SEQ, PLE_DIM), 1.0),
        "p_sample": nrm(ks[6], (DEPTH, DEC_BATCH, DEC_SEQ, PLE_DIM), 1.0),
        "g_mix": 1.0 + nrm(ks[7], (DEPTH, D_MODEL), 0.05),
        "w_in": nrm(ks[8], (DEPTH, D_MODEL, IN_WIDTH), D_MODEL ** -0.5),
        "b_gate": nrm(ks[9], (DEPTH, 2 * D_MODEL), 0.1),
        "w_pool_grp": nrm(ks[10], (DEPTH, POOL_GROUPS, POOL_GROUP_DIM, POOL_GROUP_DIM), POOL_GROUP_DIM ** -0.5),
        "pool_scale": 1.0 + nrm(ks[11], (DEPTH, POOL_WIDTH), 0.1),
        "w_pool_up": nrm(ks[12], (DEPTH, POOL_WIDTH, D_MODEL), POOL_WIDTH ** -0.5),
        "w_sb_up": nrm(ks[13], (DEPTH, SB_WIDTH, D_MODEL), SB_WIDTH ** -0.5),
        "w_out": nrm(ks[14], (DEPTH, D_MODEL, D_MODEL), D_MODEL ** -0.5),
        "g_mlp": 1.0 + nrm(ks[15], (DEPTH, D_MODEL), 0.05),
        "w_up": nrm(ks[16], (DEPTH, D_MODEL, D_FF), D_MODEL ** -0.5),
        "w_down": nrm(ks[17], (DEPTH, D_FF, D_MODEL), D_FF ** -0.5),
        "g_ple": 1.0 + nrm(ks[18], (DEPTH, D_MODEL), 0.05),
        "w_ple_gate": nrm(ks[19], (DEPTH, D_MODEL, D_MODEL), D_MODEL ** -0.5),
        "w_ple_proj": nrm(ks[20], (DEPTH, PLE_DIM, D_MODEL), PLE_DIM ** -0.5),
        "g_final": 1.0 + nrm(ks[21], (D_MODEL,), 0.05),
    }


def reference(x_prompt, x_sample, cache_k, cache_v, state_pool, p_prompt, p_sample,
              g_mix, w_in, b_gate, w_pool_grp, pool_scale, w_pool_up, w_sb_up, w_out,
              g_mlp, w_up, w_down, g_ple, w_ple_gate, w_ple_proj, g_final):
    xp, xs = x_prompt, x_sample
    empty_pool = jnp.zeros((xp.shape[0], 0, POOL_WIDTH), xp.dtype)
    empty_kv = jnp.zeros((xp.shape[0], 0, SB_HEADS, SB_HEAD_DIM), xp.dtype)
    kp_l, vp_l, pp_l, ks_l, vs_l, ps_l = [], [], [], [], [], []
    for d in range(DEPTH):
        weights = (g_mix[d], w_in[d], b_gate[d], w_pool_grp[d], pool_scale[d], w_pool_up[d], w_sb_up[d],
                   w_out[d], g_mlp[d], w_up[d], w_down[d], g_ple[d], w_ple_gate[d], w_ple_proj[d])
        xp, kp, vp, pp = layer(xp, p_prompt[d], empty_pool, empty_kv, empty_kv, True, *weights)
        xs, ksm, vsm, psm = layer(xs, p_sample[d], state_pool[d], cache_k[d], cache_v[d], False, *weights)
        kp_l.append(kp); vp_l.append(vp); pp_l.append(pp)
        ks_l.append(ksm); vs_l.append(vsm); ps_l.append(psm)
    y_prompt = rms_norm(xp, g_final)
    y_sample = rms_norm(xs, g_final)
    k_prompt = jnp.stack(kp_l); v_prompt = jnp.stack(vp_l); pool_prompt = jnp.stack(pp_l)
    k_sample = jnp.stack(ks_l); v_sample = jnp.stack(vs_l); pool_sample = jnp.stack(ps_l)
    return (y_prompt, y_sample, k_prompt, v_prompt, pool_prompt, k_sample, v_sample, pool_sample)
```

```python
import functools

import jax
import jax.numpy as jnp
from jax import lax
from jax.experimental import pallas as pl
from jax.experimental.pallas import tpu as pltpu

D_MODEL = 1024
POOL_WIDTH = 512
POOL_WINDOWS = (2, 4, 8, 16)
POOL_GROUP_DIM = POOL_WIDTH // len(POOL_WINDOWS)
POOL_STATE = max(POOL_WINDOWS) - 1
POOL_HALO = 16
SB_HEADS = 8
SB_HEAD_DIM = 64
SB_WIDTH = SB_HEADS * SB_HEAD_DIM
HEAD_PAIR = 2 * SB_HEAD_DIM
D_FF = 4 * D_MODEL
PLE_DIM = 256
EPS = 1e-6

V7X_VMEM_BYTES = 64 * 1024 * 1024
VMEM_LIMIT_BYTES = V7X_VMEM_BYTES - 8 * 1024 * 1024

ROW_TILE = 256
SB_BLOCK = 256

BF16 = jnp.bfloat16
F32 = jnp.float32


def _rms_norm(x, g):
    y = x * lax.rsqrt(jnp.mean(x * x, axis=-1, keepdims=True) + EPS)
    return y * g


def _dot(a, b):
    return jnp.dot(a, b, preferred_element_type=F32)


def _resident(shape):
    return pl.BlockSpec(shape, lambda *_: (0,) * len(shape), pipeline_mode=pl.Buffered(1))


def _inproj_kernel(x_ref, g_ref, w_ref, b_ref,
                   u_ref, k_ref, v_ref, qb_ref, kb_ref, vb_ref, gate_ref):
    xn = _rms_norm(x_ref[...], g_ref[...]).astype(BF16)
    c0, c1, c2, c3 = POOL_WIDTH, POOL_WIDTH + SB_WIDTH, POOL_WIDTH + 2 * SB_WIDTH, POOL_WIDTH + 3 * SB_WIDTH
    u_ref[...] = _dot(xn, w_ref[:, :c0])
    qb_ref[...] = (_dot(xn, w_ref[:, c0:c1]) * (SB_HEAD_DIM ** -0.5)).astype(BF16)
    k = _dot(xn, w_ref[:, c1:c2])
    k_ref[...] = k
    kb_ref[...] = k.astype(BF16)
    v = _dot(xn, w_ref[:, c2:c3])
    v_ref[...] = v
    vb_ref[...] = v.astype(BF16)
    gate_ref[...] = jax.nn.sigmoid(_dot(xn, w_ref[:, c3:]) + b_ref[...]).astype(BF16)


def _inproj(x, g_mix, w_in_bf, b_gate, tm):
    n = x.shape[0]
    in_width = w_in_bf.shape[1]
    row = lambda width: pl.BlockSpec((tm, width), lambda i: (i, 0))
    f32_out = jax.ShapeDtypeStruct((n, SB_WIDTH), F32)
    bf_out = jax.ShapeDtypeStruct((n, SB_WIDTH), BF16)
    return pl.pallas_call(
        _inproj_kernel,
        grid=(n // tm,),
        in_specs=[row(D_MODEL), _resident((1, D_MODEL)), _resident((D_MODEL, in_width)),
                  _resident((1, 2 * D_MODEL))],
        out_specs=[row(POOL_WIDTH), row(SB_WIDTH), row(SB_WIDTH), row(SB_WIDTH), row(SB_WIDTH),
                   row(SB_WIDTH), row(2 * D_MODEL)],
        out_shape=[jax.ShapeDtypeStruct((n, POOL_WIDTH), F32), f32_out, f32_out, bf_out, bf_out, bf_out,
                   jax.ShapeDtypeStruct((n, 2 * D_MODEL), BF16)],
        compiler_params=pltpu.CompilerParams(dimension_semantics=("parallel",),
                                             vmem_limit_bytes=VMEM_LIMIT_BYTES),
        name="inproj",
    )(x, g_mix.reshape(1, D_MODEL), w_in_bf, b_gate.reshape(1, 2 * D_MODEL))


def _neg_strict_lower(n):
    r = lax.broadcasted_iota(jnp.int32, (n, n), 0)
    c = lax.broadcasted_iota(jnp.int32, (n, n), 1)
    return jnp.where(r > c, -1.0, 0.0).astype(BF16)


def _head_masks(x):
    lane = lax.broadcasted_iota(jnp.int32, x.shape, 1)
    zero = jnp.zeros_like(x)
    return jnp.where(lane < SB_HEAD_DIM, x, zero), jnp.where(lane >= SB_HEAD_DIM, x, zero)


def _sb_block(q_h, k_blk, v_h, carry, neg_tri, diagonal):
    z = lax.dot_general(q_h, k_blk, (((1,), (1,)), ((), ())), preferred_element_type=F32)
    sp = jnp.maximum(z, 0.0) + jnp.log(1.0 + jnp.exp(-jnp.abs(z)))
    if diagonal:
        tq, tk = z.shape
        causal = lax.broadcasted_iota(jnp.int32, (tq, tk), 1) < lax.broadcasted_iota(jnp.int32, (tq, tk), 0)
        sp = jnp.where(causal, sp, 0.0)
    hi = sp.astype(BF16)
    lo = (sp - hi.astype(F32)).astype(BF16)
    suffix = _dot(hi, neg_tri) + _dot(lo, neg_tri)
    a = jnp.exp(z - sp + suffix + carry)
    if diagonal:
        a = jnp.where(causal, a, 0.0)
    out = _dot(a.astype(BF16), v_h)
    return out, carry - jnp.sum(sp, axis=1, keepdims=True)


def _sb_prompt_kernel(q_ref, k_ref, v_ref, o_ref, acc_ref, carry_ref):
    qi = pl.program_id(2)
    blk = SB_BLOCK
    q0, q1 = _head_masks(q_ref[0])
    neg_tri = _neg_strict_lower(blk)

    row0 = pl.multiple_of(qi * blk, blk)
    k_blk = k_ref[0, pl.ds(row0, blk), :]
    v0, v1 = _head_masks(v_ref[0, pl.ds(row0, blk), :])
    zero_carry = jnp.zeros((blk, 1), F32)
    o0, c0 = _sb_block(q0, k_blk, v0, zero_carry, neg_tri, True)
    o1, c1 = _sb_block(q1, k_blk, v1, zero_carry, neg_tri, True)
    acc_ref[...] = o0 + o1
    carry_ref[0] = c0
    carry_ref[1] = c1

    def body(j, _):
        r0 = pl.multiple_of((qi - 1 - j) * blk, blk)
        kb = k_ref[0, pl.ds(r0, blk), :]
        vb0, vb1 = _head_masks(v_ref[0, pl.ds(r0, blk), :])
        p0, n0 = _sb_block(q0, kb, vb0, carry_ref[0], neg_tri, False)
        p1, n1 = _sb_block(q1, kb, vb1, carry_ref[1], neg_tri, False)
        acc_ref[...] += p0 + p1
        carry_ref[0] = n0
        carry_ref[1] = n1
        return 0

    lax.fori_loop(0, qi, body, 0)
    o_ref[0] = acc_ref[...].astype(o_ref.dtype)


def _sb_prompt(qb, kb, vb):
    b, t, _ = qb.shape
    blk = SB_BLOCK
    kv_spec = pl.BlockSpec((1, t, HEAD_PAIR), lambda bi, hp, qi: (bi, 0, hp))
    q_spec = pl.BlockSpec((1, blk, HEAD_PAIR), lambda bi, hp, qi: (bi, qi, hp))
    return pl.pallas_call(
        _sb_prompt_kernel,
        grid=(b, SB_WIDTH // HEAD_PAIR, t // blk),
        in_specs=[q_spec, kv_spec, kv_spec],
        out_specs=q_spec,
        out_shape=jax.ShapeDtypeStruct((b, t, SB_WIDTH), BF16),
        scratch_shapes=[pltpu.VMEM((blk, HEAD_PAIR), F32), pltpu.VMEM((2, blk, 1), F32)],
        compiler_params=pltpu.CompilerParams(dimension_semantics=("parallel", "parallel", "arbitrary"),
                                             vmem_limit_bytes=VMEM_LIMIT_BYTES),
        name="sb_prompt",
    )(qb, kb, vb)


def _sb_sample_kernel(q_ref, k_ref, v_ref, ck_ref, cv_ref, o_ref, *, past_len):
    blk = SB_BLOCK
    q0, q1 = _head_masks(q_ref[0])
    t = q0.shape[0]
    v0, v1 = _head_masks(v_ref[0])
    zero_carry = jnp.zeros((t, 1), F32)
    o0, c0 = _sb_block(q0, k_ref[0], v0, zero_carry, _neg_strict_lower(t), True)
    o1, c1 = _sb_block(q1, k_ref[0], v1, zero_carry, _neg_strict_lower(t), True)
    acc = o0 + o1
    neg_tri = _neg_strict_lower(blk)
    for j in reversed(range(past_len // blk)):
        kb = ck_ref[0, j * blk:(j + 1) * blk, :].astype(BF16)
        vb0, vb1 = _head_masks(cv_ref[0, j * blk:(j + 1) * blk, :].astype(BF16))
        p0, c0 = _sb_block(q0, kb, vb0, c0, neg_tri, False)
        p1, c1 = _sb_block(q1, kb, vb1, c1, neg_tri, False)
        acc = acc + p0 + p1
    o_ref[0] = acc.astype(o_ref.dtype)


def _sb_sample(qb, kb, vb, cache_k, cache_v):
    b, t, _ = qb.shape
    past_len = cache_k.shape[1]
    assert past_len % SB_BLOCK == 0
    new_spec = pl.BlockSpec((1, t, HEAD_PAIR), lambda bi, hp: (bi, 0, hp))
    cache_spec = pl.BlockSpec((1, past_len, HEAD_PAIR), lambda bi, hp: (bi, 0, hp))
    return pl.pallas_call(
        functools.partial(_sb_sample_kernel, past_len=past_len),
        grid=(b, SB_WIDTH // HEAD_PAIR),
        in_specs=[new_spec, new_spec, new_spec, cache_spec, cache_spec],
        out_specs=new_spec,
        out_shape=jax.ShapeDtypeStruct((b, t, SB_WIDTH), BF16),
        compiler_params=pltpu.CompilerParams(dimension_semantics=("parallel", "parallel"),
                                             vmem_limit_bytes=VMEM_LIMIT_BYTES),
        name="sb_sample",
    )(qb, kb, vb, cache_k, cache_v)


def _pool_diff(ext, first_pos):
    rows = ext.shape[0] - POOL_HALO
    pos = first_pos + lax.broadcasted_iota(jnp.int32, (rows, 1), 0)
    outs = []
    for g, window in enumerate(POOL_WINDOWS):
        cols = ext[:, g * POOL_GROUP_DIM:(g + 1) * POOL_GROUP_DIM]
        acc = cols
        shift = 1
        while shift < window:
            acc = acc + pltpu.roll(acc, shift, axis=0)
            shift *= 2
        count = jnp.minimum(pos + 1, window).astype(F32)
        outs.append(acc[POOL_HALO:] / count - cols[POOL_HALO:])
    return jnp.concatenate(outs, axis=1)


def _post_kernel(x_ref, u_ref, halo_ref, o_ref, gate_ref, p_ref,
                 wgrp_ref, scale_ref, wpu_ref, wsu_ref, wout_ref,
                 gmlp_ref, wup_ref, wdown_ref, gple_ref, wpg_ref, wpp_ref, gfin_ref,
                 y_ref, *, tiles_per_stream, streams_per_tile, past_pos, final_norm):
    u = u_ref[...]
    tm = u.shape[0]
    if streams_per_tile == 1:
        step = pl.program_id(0) % tiles_per_stream
        halo = jnp.where(step == 0, 0.0, halo_ref[...])
        diff = _pool_diff(jnp.concatenate([halo, u], axis=0), past_pos + step * tm)
    else:
        t = tm // streams_per_tile
        diff = jnp.concatenate(
            [_pool_diff(jnp.concatenate([halo_ref[s], u[s * t:(s + 1) * t]], axis=0), past_pos)
             for s in range(streams_per_tile)], axis=0)
    diff = diff.astype(BF16)
    y_pool = jnp.concatenate(
        [_dot(diff[:, g * POOL_GROUP_DIM:(g + 1) * POOL_GROUP_DIM], wgrp_ref[g])
         for g in range(len(POOL_WINDOWS))], axis=1) * scale_ref[...]

    gates = gate_ref[...].astype(F32)
    merged = (gates[:, :D_MODEL] * _dot(y_pool.astype(BF16), wpu_ref[...])
              + gates[:, D_MODEL:] * _dot(o_ref[...], wsu_ref[...]))
    x = x_ref[...] + _dot(merged.astype(BF16), wout_ref[...])

    h = _dot(_rms_norm(x, gmlp_ref[...]).astype(BF16), wup_ref[...])
    h = jnp.square(jnp.maximum(h, 0.0)).astype(BF16)
    x = x + _dot(h, wdown_ref[...])

    ple_gate = jax.nn.sigmoid(_dot(_rms_norm(x, gple_ref[...]).astype(BF16), wpg_ref[...]))
    x = x + ple_gate * _dot(p_ref[...].astype(BF16), wpp_ref[...])
    if final_norm:
        x = _rms_norm(x, gfin_ref[...])
    y_ref[...] = x


def _post(x, u, halo, o, gates, p, weights, *, tm, tiles_per_stream, streams_per_tile, past_pos, final_norm):
    n = x.shape[0]
    row = lambda width: pl.BlockSpec((tm, width), lambda i: (i, 0))
    if streams_per_tile == 1:
        per_tile = tm // POOL_HALO
        halo_spec = pl.BlockSpec((POOL_HALO, POOL_WIDTH), lambda i: (jnp.maximum(i * per_tile - 1, 0), 0))
    else:
        halo_spec = pl.BlockSpec((streams_per_tile, POOL_HALO, POOL_WIDTH), lambda i: (i, 0, 0))
    w_specs = [_resident(w.shape) for w in weights]
    return pl.pallas_call(
        functools.partial(_post_kernel, tiles_per_stream=tiles_per_stream, streams_per_tile=streams_per_tile,
                          past_pos=past_pos, final_norm=final_norm),
        grid=(n // tm,),
        in_specs=[row(D_MODEL), row(POOL_WIDTH), halo_spec, row(SB_WIDTH), row(2 * D_MODEL), row(PLE_DIM)] + w_specs,
        out_specs=row(D_MODEL),
        out_shape=jax.ShapeDtypeStruct((n, D_MODEL), F32),
        compiler_params=pltpu.CompilerParams(dimension_semantics=("parallel",),
                                             vmem_limit_bytes=VMEM_LIMIT_BYTES),
        name="post",
    )(x, u, halo, o, gates, p, *weights)


def kernel(x_prompt, x_sample, cache_k, cache_v, state_pool, p_prompt, p_sample, g_mix, w_in, b_gate, w_pool_grp, pool_scale, w_pool_up, w_sb_up, w_out, g_mlp, w_up, w_down, g_ple, w_ple_gate, w_ple_proj, g_final):
    depth = w_in.shape[0]
    bp, tp, _ = x_prompt.shape
    bs, ts, _ = x_sample.shape
    past_len = cache_k.shape[2]
    assert tp % ROW_TILE == 0 and tp % SB_BLOCK == 0 and ts >= POOL_STATE and ts % 8 == 0
    n_s = bs * ts
    xp = x_prompt.reshape(bp * tp, D_MODEL)
    xs = x_sample.reshape(n_s, D_MODEL)
    row_vec = lambda a: a.reshape(1, -1)

    outs = {name: [] for name in ("kp", "vp", "pp", "ks", "vs", "ps")}
    for d in range(depth):
        w_in_bf = w_in[d].astype(BF16)
        weights = (w_pool_grp[d].astype(BF16), row_vec(pool_scale[d]), w_pool_up[d].astype(BF16),
                   w_sb_up[d].astype(BF16), w_out[d].astype(BF16), row_vec(g_mlp[d]), w_up[d].astype(BF16),
                   w_down[d].astype(BF16), row_vec(g_ple[d]), w_ple_gate[d].astype(BF16),
                   w_ple_proj[d].astype(BF16), row_vec(g_final))
        final_norm = d == depth - 1

        u, k, v, qb, kb, vb, gates = _inproj(xp, g_mix[d], w_in_bf, b_gate[d], ROW_TILE)
        o = _sb_prompt(qb.reshape(bp, tp, SB_WIDTH), kb.reshape(bp, tp, SB_WIDTH), vb.reshape(bp, tp, SB_WIDTH))
        xp = _post(xp, u, u, o.reshape(bp * tp, SB_WIDTH), gates, p_prompt[d].reshape(bp * tp, PLE_DIM), weights,
                   tm=ROW_TILE, tiles_per_stream=tp // ROW_TILE, streams_per_tile=1, past_pos=0,
                   final_norm=final_norm)
        outs["kp"].append(k.reshape(bp, tp, SB_HEADS, SB_HEAD_DIM))
        outs["vp"].append(v.reshape(bp, tp, SB_HEADS, SB_HEAD_DIM))
        outs["pp"].append(u.reshape(bp, tp, POOL_WIDTH)[:, tp - POOL_STATE:])

        u, k, v, qb, kb, vb, gates = _inproj(xs, g_mix[d], w_in_bf, b_gate[d], n_s)
        o = _sb_sample(qb.reshape(bs, ts, SB_WIDTH), kb.reshape(bs, ts, SB_WIDTH), vb.reshape(bs, ts, SB_WIDTH),
                       cache_k[d].reshape(bs, past_len, SB_WIDTH), cache_v[d].reshape(bs, past_len, SB_WIDTH))
        halo = jnp.pad(state_pool[d], ((0, 0), (POOL_HALO - POOL_STATE, 0), (0, 0)))
        xs = _post(xs, u, halo, o.reshape(n_s, SB_WIDTH), gates, p_sample[d].reshape(n_s, PLE_DIM), weights,
                   tm=n_s, tiles_per_stream=1, streams_per_tile=bs, past_pos=POOL_STATE,
                   final_norm=final_norm)
        outs["ks"].append(k.reshape(bs, ts, SB_HEADS, SB_HEAD_DIM))
        outs["vs"].append(v.reshape(bs, ts, SB_HEADS, SB_HEAD_DIM))
        outs["ps"].append(u.reshape(bs, ts, POOL_WIDTH)[:, ts - POOL_STATE:])

    stack = lambda name: jnp.stack(outs[name])
    return (xp.reshape(bp, tp, D_MODEL), xs.reshape(bs, ts, D_MODEL),
            stack("kp"), stack("vp"), stack("pp"), stack("ks"), stack("vs"), stack("ps"))
```

```python
import functools

import jax
import jax.numpy as jnp
from jax import lax
from jax.experimental import pallas as pl
from jax.experimental.pallas import tpu as pltpu

D_MODEL = 1024
POOL_WIDTH = 512
POOL_WINDOWS = (2, 4, 8, 16)
POOL_GROUP_DIM = POOL_WIDTH // len(POOL_WINDOWS)
POOL_STATE = max(POOL_WINDOWS) - 1
POOL_HALO = 16
SB_HEADS = 8
SB_HEAD_DIM = 64
SB_WIDTH = SB_HEADS * SB_HEAD_DIM
HEAD_PAIR = 2 * SB_HEAD_DIM
D_FF = 4 * D_MODEL
PLE_DIM = 256
EPS = 1e-6

V7X_VMEM_BYTES = 64 * 1024 * 1024
VMEM_LIMIT_BYTES = V7X_VMEM_BYTES - 8 * 1024 * 1024

ROW_TILE = 256
SB_BLOCK = 256

LOG2E = 1.4426950408889634
SB_SKIP_BELOW = -160.0

BF16 = jnp.bfloat16
F32 = jnp.float32


def _rms_norm(x, g):
    y = x * lax.rsqrt(jnp.mean(x * x, axis=-1, keepdims=True) + EPS)
    return y * g


def _dot(a, b):
    return jnp.dot(a, b, preferred_element_type=F32)


def _resident(shape):
    return pl.BlockSpec(shape, lambda *_: (0,) * len(shape), pipeline_mode=pl.Buffered(1))


def _inproj_kernel(x_ref, g_ref, w_ref, b_ref,
                   u_ref, k_ref, v_ref, qb_ref, kb_ref, vb_ref, gate_ref):
    xn = _rms_norm(x_ref[...], g_ref[...]).astype(BF16)
    c0, c1, c2, c3 = POOL_WIDTH, POOL_WIDTH + SB_WIDTH, POOL_WIDTH + 2 * SB_WIDTH, POOL_WIDTH + 3 * SB_WIDTH
    u_ref[...] = _dot(xn, w_ref[:, :c0])
    qb_ref[...] = (_dot(xn, w_ref[:, c0:c1]) * (SB_HEAD_DIM ** -0.5 * LOG2E)).astype(BF16)
    k = _dot(xn, w_ref[:, c1:c2])
    k_ref[...] = k
    kb_ref[...] = k.astype(BF16)
    v = _dot(xn, w_ref[:, c2:c3])
    v_ref[...] = v
    vb_ref[...] = v.astype(BF16)
    gate_ref[...] = jax.nn.sigmoid(_dot(xn, w_ref[:, c3:]) + b_ref[...]).astype(BF16)


def _inproj(x, g_mix, w_in_bf, b_gate, tm):
    n = x.shape[0]
    in_width = w_in_bf.shape[1]
    row = lambda width: pl.BlockSpec((tm, width), lambda i: (i, 0))
    f32_out = jax.ShapeDtypeStruct((n, SB_WIDTH), F32)
    bf_out = jax.ShapeDtypeStruct((n, SB_WIDTH), BF16)
    return pl.pallas_call(
        _inproj_kernel,
        grid=(n // tm,),
        in_specs=[row(D_MODEL), _resident((1, D_MODEL)), _resident((D_MODEL, in_width)),
                  _resident((1, 2 * D_MODEL))],
        out_specs=[row(POOL_WIDTH), row(SB_WIDTH), row(SB_WIDTH), row(SB_WIDTH), row(SB_WIDTH),
                   row(SB_WIDTH), row(2 * D_MODEL)],
        out_shape=[jax.ShapeDtypeStruct((n, POOL_WIDTH), F32), f32_out, f32_out, bf_out, bf_out, bf_out,
                   jax.ShapeDtypeStruct((n, 2 * D_MODEL), BF16)],
        compiler_params=pltpu.CompilerParams(dimension_semantics=("parallel",),
                                             vmem_limit_bytes=VMEM_LIMIT_BYTES),
        name="inproj",
    )(x, g_mix.reshape(1, D_MODEL), w_in_bf, b_gate.reshape(1, 2 * D_MODEL))


def _neg_strict_lower2(n):
    r = lax.broadcasted_iota(jnp.int32, (2 * n, n), 0)
    c = lax.broadcasted_iota(jnp.int32, (2 * n, n), 1)
    r = jnp.where(r >= n, r - n, r)
    return jnp.where(r > c, -1.0, 0.0).astype(BF16)


def _stack_heads(x):
    lane = lax.broadcasted_iota(jnp.int32, x.shape, 1)
    zero = jnp.zeros_like(x)
    return jnp.concatenate([jnp.where(lane < SB_HEAD_DIM, x, zero), jnp.where(lane >= SB_HEAD_DIM, x, zero)], axis=0)


def _causal_mask(tq, tk):
    r = lax.broadcasted_iota(jnp.int32, (2 * tq, tk), 0)
    c = lax.broadcasted_iota(jnp.int32, (2 * tq, tk), 1)
    return c < jnp.where(r >= tq, r - tq, r)


def _sb_pair_block(q2, k_blk, v2, carry, neg_tri2, causal):
    tq = q2.shape[0] // 2
    z = lax.dot_general(q2, k_blk, (((1,), (1,)), ((), ())), preferred_element_type=F32)
    sp = jnp.maximum(z, 0.0) + jnp.log(1.0 + jnp.exp2(-jnp.abs(z))) * LOG2E
    if causal is not None:
        sp = jnp.where(causal, sp, 0.0)
    hi = sp.astype(BF16)
    lo = (sp - hi.astype(F32)).astype(BF16)
    suffix = _dot(jnp.concatenate([hi, lo], axis=1), neg_tri2)
    a = jnp.exp2(z - sp + suffix + carry)
    if causal is not None:
        a = jnp.where(causal, a, 0.0)
    a = a.astype(BF16)
    out = _dot(jnp.concatenate([a[:tq], a[tq:]], axis=1), v2)
    return out, carry - jnp.sum(sp, axis=1, keepdims=True)


def _sb_prompt_kernel(q_ref, k_ref, v_ref, o_ref, acc_ref, carry_ref):
    qi = pl.program_id(2)
    blk = SB_BLOCK
    q2 = _stack_heads(q_ref[0])
    neg_tri2 = _neg_strict_lower2(blk)

    def key_block(j):
        r0 = pl.multiple_of(j * blk, blk)
        return k_ref[0, pl.ds(r0, blk), :], _stack_heads(v_ref[0, pl.ds(r0, blk), :])

    def alive(carry):
        return (jnp.max(carry) > SB_SKIP_BELOW).astype(jnp.int32)

    k_blk, v2 = key_block(qi)
    out, carry = _sb_pair_block(q2, k_blk, v2, jnp.zeros((2 * blk, 1), F32), neg_tri2, _causal_mask(blk, blk))
    acc_ref[...] = out
    carry_ref[...] = carry

    def body(state):
        j, _ = state
        k_blk, v2 = key_block(j)
        out, carry = _sb_pair_block(q2, k_blk, v2, carry_ref[...], neg_tri2, None)
        acc_ref[...] += out
        carry_ref[...] = carry
        return j - 1, alive(carry)

    lax.while_loop(lambda s: jnp.logical_and(s[0] >= 0, s[1] > 0), body, (qi - 1, alive(carry)))
    o_ref[0] = acc_ref[...].astype(o_ref.dtype)


def _sb_prompt(qb, kb, vb):
    b, t, _ = qb.shape
    blk = SB_BLOCK
    kv_spec = pl.BlockSpec((1, t, HEAD_PAIR), lambda bi, hp, qi: (bi, 0, hp))
    q_spec = pl.BlockSpec((1, blk, HEAD_PAIR), lambda bi, hp, qi: (bi, qi, hp))
    return pl.pallas_call(
        _sb_prompt_kernel,
        grid=(b, SB_WIDTH // HEAD_PAIR, t // blk),
        in_specs=[q_spec, kv_spec, kv_spec],
        out_specs=q_spec,
        out_shape=jax.ShapeDtypeStruct((b, t, SB_WIDTH), BF16),
        scratch_shapes=[pltpu.VMEM((blk, HEAD_PAIR), F32), pltpu.VMEM((2 * blk, 1), F32)],
        compiler_params=pltpu.CompilerParams(dimension_semantics=("parallel", "parallel", "arbitrary"),
                                             vmem_limit_bytes=VMEM_LIMIT_BYTES),
        name="sb_prompt",
    )(qb, kb, vb)


def _sb_sample_kernel(q_ref, k_ref, v_ref, ck_ref, cv_ref, o_ref, *, past_len):
    blk = SB_BLOCK
    q2 = _stack_heads(q_ref[0])
    t = q2.shape[0] // 2
    acc, carry = _sb_pair_block(q2, k_ref[0], _stack_heads(v_ref[0]), jnp.zeros((2 * t, 1), F32),
                                _neg_strict_lower2(t), _causal_mask(t, t))
    neg_tri2 = _neg_strict_lower2(blk)
    for j in reversed(range(past_len // blk)):
        k_blk = ck_ref[0, j * blk:(j + 1) * blk, :].astype(BF16)
        v2 = _stack_heads(cv_ref[0, j * blk:(j + 1) * blk, :].astype(BF16))
        out, carry = _sb_pair_block(q2, k_blk, v2, carry, neg_tri2, None)
        acc = acc + out
    o_ref[0] = acc.astype(o_ref.dtype)


def _sb_sample(qb, kb, vb, cache_k, cache_v):
    b, t, _ = qb.shape
    past_len = cache_k.shape[1]
    assert past_len % SB_BLOCK == 0
    new_spec = pl.BlockSpec((1, t, HEAD_PAIR), lambda bi, hp: (bi, 0, hp))
    cache_spec = pl.BlockSpec((1, past_len, HEAD_PAIR), lambda bi, hp: (bi, 0, hp))
    return pl.pallas_call(
        functools.partial(_sb_sample_kernel, past_len=past_len),
        grid=(b, SB_WIDTH // HEAD_PAIR),
        in_specs=[new_spec, new_spec, new_spec, cache_spec, cache_spec],
        out_specs=new_spec,
        out_shape=jax.ShapeDtypeStruct((b, t, SB_WIDTH), BF16),
        compiler_params=pltpu.CompilerParams(dimension_semantics=("parallel", "parallel"),
                                             vmem_limit_bytes=VMEM_LIMIT_BYTES),
        name="sb_sample",
    )(qb, kb, vb, cache_k, cache_v)


def _pool_diff(ext, first_pos):
    rows = ext.shape[0] - POOL_HALO
    pos = first_pos + lax.broadcasted_iota(jnp.int32, (rows, 1), 0)
    outs = []
    for g, window in enumerate(POOL_WINDOWS):
        cols = ext[:, g * POOL_GROUP_DIM:(g + 1) * POOL_GROUP_DIM]
        acc = cols
        shift = 1
        while shift < window:
            acc = acc + pltpu.roll(acc, shift, axis=0)
            shift *= 2
        count = jnp.minimum(pos + 1, window).astype(F32)
        outs.append(acc[POOL_HALO:] / count - cols[POOL_HALO:])
    return jnp.concatenate(outs, axis=1)


def _post_kernel(x_ref, u_ref, halo_ref, o_ref, gate_ref, p_ref,
                 wgrp_ref, scale_ref, wpu_ref, wsu_ref, wout_ref,
                 gmlp_ref, wup_ref, wdown_ref, gple_ref, wpg_ref, wpp_ref, gfin_ref,
                 y_ref, *, tiles_per_stream, streams_per_tile, past_pos, final_norm):
    u = u_ref[...]
    tm = u.shape[0]
    if streams_per_tile == 1:
        step = pl.program_id(0) % tiles_per_stream
        halo = jnp.where(step == 0, 0.0, halo_ref[...])
        diff = _pool_diff(jnp.concatenate([halo, u], axis=0), past_pos + step * tm)
    else:
        t = tm // streams_per_tile
        diff = jnp.concatenate(
            [_pool_diff(jnp.concatenate([halo_ref[s], u[s * t:(s + 1) * t]], axis=0), past_pos)
             for s in range(streams_per_tile)], axis=0)
    diff = diff.astype(BF16)
    y_pool = jnp.concatenate(
        [_dot(diff[:, g * POOL_GROUP_DIM:(g + 1) * POOL_GROUP_DIM], wgrp_ref[g])
         for g in range(len(POOL_WINDOWS))], axis=1) * scale_ref[...]

    gates = gate_ref[...].astype(F32)
    merged = (gates[:, :D_MODEL] * _dot(y_pool.astype(BF16), wpu_ref[...])
              + gates[:, D_MODEL:] * _dot(o_ref[...], wsu_ref[...]))
    x = x_ref[...] + _dot(merged.astype(BF16), wout_ref[...])

    h = _dot(_rms_norm(x, gmlp_ref[...]).astype(BF16), wup_ref[...])
    h = jnp.square(jnp.maximum(h, 0.0)).astype(BF16)
    x = x + _dot(h, wdown_ref[...])

    ple_gate = jax.nn.sigmoid(_dot(_rms_norm(x, gple_ref[...]).astype(BF16), wpg_ref[...]))
    x = x + ple_gate * _dot(p_ref[...].astype(BF16), wpp_ref[...])
    if final_norm:
        x = _rms_norm(x, gfin_ref[...])
    y_ref[...] = x


def _post(x, u, halo, o, gates, p, weights, *, tm, tiles_per_stream, streams_per_tile, past_pos, final_norm):
    n = x.shape[0]
    row = lambda width: pl.BlockSpec((tm, width), lambda i: (i, 0))
    if streams_per_tile == 1:
        per_tile = tm // POOL_HALO
        halo_spec = pl.BlockSpec((POOL_HALO, POOL_WIDTH), lambda i: (jnp.maximum(i * per_tile - 1, 0), 0))
    else:
        halo_spec = pl.BlockSpec((streams_per_tile, POOL_HALO, POOL_WIDTH), lambda i: (i, 0, 0))
    w_specs = [_resident(w.shape) for w in weights]
    return pl.pallas_call(
        functools.partial(_post_kernel, tiles_per_stream=tiles_per_stream, streams_per_tile=streams_per_tile,
                          past_pos=past_pos, final_norm=final_norm),
        grid=(n // tm,),
        in_specs=[row(D_MODEL), row(POOL_WIDTH), halo_spec, row(SB_WIDTH), row(2 * D_MODEL), row(PLE_DIM)] + w_specs,
        out_specs=row(D_MODEL),
        out_shape=jax.ShapeDtypeStruct((n, D_MODEL), F32),
        compiler_params=pltpu.CompilerParams(dimension_semantics=("parallel",),
                                             vmem_limit_bytes=VMEM_LIMIT_BYTES),
        name="post",
    )(x, u, halo, o, gates, p, *weights)


def kernel(x_prompt, x_sample, cache_k, cache_v, state_pool, p_prompt, p_sample, g_mix, w_in, b_gate, w_pool_grp, pool_scale, w_pool_up, w_sb_up, w_out, g_mlp, w_up, w_down, g_ple, w_ple_gate, w_ple_proj, g_final):
    depth = w_in.shape[0]
    bp, tp, _ = x_prompt.shape
    bs, ts, _ = x_sample.shape
    past_len = cache_k.shape[2]
    assert tp % ROW_TILE == 0 and tp % SB_BLOCK == 0 and ts >= POOL_STATE and ts % 8 == 0
    n_s = bs * ts
    xp = x_prompt.reshape(bp * tp, D_MODEL)
    xs = x_sample.reshape(n_s, D_MODEL)
    row_vec = lambda a: a.reshape(1, -1)

    outs = {name: [] for name in ("kp", "vp", "pp", "ks", "vs", "ps")}
    for d in range(depth):
        w_in_bf = w_in[d].astype(BF16)
        weights = (w_pool_grp[d].astype(BF16), row_vec(pool_scale[d]), w_pool_up[d].astype(BF16),
                   w_sb_up[d].astype(BF16), w_out[d].astype(BF16), row_vec(g_mlp[d]), w_up[d].astype(BF16),
                   w_down[d].astype(BF16), row_vec(g_ple[d]), w_ple_gate[d].astype(BF16),
                   w_ple_proj[d].astype(BF16), row_vec(g_final))
        final_norm = d == depth - 1

        u, k, v, qb, kb, vb, gates = _inproj(xp, g_mix[d], w_in_bf, b_gate[d], ROW_TILE)
        o = _sb_prompt(qb.reshape(bp, tp, SB_WIDTH), kb.reshape(bp, tp, SB_WIDTH), vb.reshape(bp, tp, SB_WIDTH))
        xp = _post(xp, u, u, o.reshape(bp * tp, SB_WIDTH), gates, p_prompt[d].reshape(bp * tp, PLE_DIM), weights,
                   tm=ROW_TILE, tiles_per_stream=tp // ROW_TILE, streams_per_tile=1, past_pos=0,
                   final_norm=final_norm)
        outs["kp"].append(k.reshape(bp, tp, SB_HEADS, SB_HEAD_DIM))
        outs["vp"].append(v.reshape(bp, tp, SB_HEADS, SB_HEAD_DIM))
        outs["pp"].append(u.reshape(bp, tp, POOL_WIDTH)[:, tp - POOL_STATE:])

        u, k, v, qb, kb, vb, gates = _inproj(xs, g_mix[d], w_in_bf, b_gate[d], n_s)
        o = _sb_sample(qb.reshape(bs, ts, SB_WIDTH), kb.reshape(bs, ts, SB_WIDTH), vb.reshape(bs, ts, SB_WIDTH),
                       cache_k[d].reshape(bs, past_len, SB_WIDTH), cache_v[d].reshape(bs, past_len, SB_WIDTH))
        halo = jnp.pad(state_pool[d], ((0, 0), (POOL_HALO - POOL_STATE, 0), (0, 0)))
        xs = _post(xs, u, halo, o.reshape(n_s, SB_WIDTH), gates, p_sample[d].reshape(n_s, PLE_DIM), weights,
                   tm=n_s, tiles_per_stream=1, streams_per_tile=bs, past_pos=POOL_STATE,
                   final_norm=final_norm)
        outs["ks"].append(k.reshape(bs, ts, SB_HEADS, SB_HEAD_DIM))
        outs["vs"].append(v.reshape(bs, ts, SB_HEADS, SB_HEAD_DIM))
        outs["ps"].append(u.reshape(bs, ts, POOL_WIDTH)[:, ts - POOL_STATE:])

    stack = lambda name: jnp.stack(outs[name])
    return (xp.reshape(bp, tp, D_MODEL), xs.reshape(bs, ts, D_MODEL),
            stack("kp"), stack("vp"), stack("pp"), stack("ks"), stack("vs"), stack("ps"))
```

```python
import functools

import jax
import jax.numpy as jnp
from jax import lax
from jax.experimental import pallas as pl
from jax.experimental.pallas import tpu as pltpu

D_MODEL = 1024
POOL_WIDTH = 512
POOL_WINDOWS = (2, 4, 8, 16)
POOL_GROUP_DIM = POOL_WIDTH // len(POOL_WINDOWS)
POOL_STATE = max(POOL_WINDOWS) - 1
POOL_HALO = 16
SB_HEADS = 8
SB_HEAD_DIM = 64
SB_WIDTH = SB_HEADS * SB_HEAD_DIM
HEAD_PAIR = 2 * SB_HEAD_DIM
D_FF = 4 * D_MODEL
PLE_DIM = 256
EPS = 1e-6

V7X_VMEM_BYTES = 64 * 1024 * 1024
VMEM_LIMIT_BYTES = V7X_VMEM_BYTES - 8 * 1024 * 1024

ROW_TILE = 256
SB_BLOCK = 256

LOG2E = 1.4426950408889634
SB_SKIP_BELOW = -160.0

BF16 = jnp.bfloat16
F32 = jnp.float32


def _rms_norm(x, g):
    y = x * lax.rsqrt(jnp.mean(x * x, axis=-1, keepdims=True) + EPS)
    return y * g


def _dot(a, b):
    return jnp.dot(a, b, preferred_element_type=F32)


def _resident(shape):
    return pl.BlockSpec(shape, lambda *_: (0,) * len(shape), pipeline_mode=pl.Buffered(1))


def _store_head_rows(ref, x):
    rows = x.shape[0]
    for h in range(SB_HEADS):
        ref[pl.ds(h, rows, stride=SB_HEADS), :] = x[:, h * SB_HEAD_DIM:(h + 1) * SB_HEAD_DIM]


def _inproj_kernel(x_ref, g_ref, w_ref, b_ref,
                   u_ref, k_ref, v_ref, qb_ref, kb_ref, vb_ref, gate_ref):
    xn = _rms_norm(x_ref[...], g_ref[...]).astype(BF16)
    c0, c1, c2, c3 = POOL_WIDTH, POOL_WIDTH + SB_WIDTH, POOL_WIDTH + 2 * SB_WIDTH, POOL_WIDTH + 3 * SB_WIDTH
    u_ref[...] = _dot(xn, w_ref[:, :c0])
    qb_ref[...] = (_dot(xn, w_ref[:, c0:c1]) * (SB_HEAD_DIM ** -0.5 * LOG2E)).astype(BF16)
    k = _dot(xn, w_ref[:, c1:c2])
    _store_head_rows(k_ref, k)
    kb_ref[...] = k.astype(BF16)
    v = _dot(xn, w_ref[:, c2:c3])
    _store_head_rows(v_ref, v)
    vb_ref[...] = v.astype(BF16)
    gate_ref[...] = jax.nn.sigmoid(_dot(xn, w_ref[:, c3:]) + b_ref[...]).astype(BF16)


def _inproj(x, g_mix, w_in_bf, b_gate, tm):
    n = x.shape[0]
    in_width = w_in_bf.shape[1]
    row = lambda width: pl.BlockSpec((tm, width), lambda i: (i, 0))
    f32_out = jax.ShapeDtypeStruct((n * SB_HEADS, SB_HEAD_DIM), F32)
    head_rows = pl.BlockSpec((tm * SB_HEADS, SB_HEAD_DIM), lambda i: (i, 0))
    bf_out = jax.ShapeDtypeStruct((n, SB_WIDTH), BF16)
    return pl.pallas_call(
        _inproj_kernel,
        grid=(n // tm,),
        in_specs=[row(D_MODEL), _resident((1, D_MODEL)), _resident((D_MODEL, in_width)),
                  _resident((1, 2 * D_MODEL))],
        out_specs=[row(POOL_WIDTH), head_rows, head_rows, row(SB_WIDTH), row(SB_WIDTH),
                   row(SB_WIDTH), row(2 * D_MODEL)],
        out_shape=[jax.ShapeDtypeStruct((n, POOL_WIDTH), F32), f32_out, f32_out, bf_out, bf_out, bf_out,
                   jax.ShapeDtypeStruct((n, 2 * D_MODEL), BF16)],
        compiler_params=pltpu.CompilerParams(dimension_semantics=("parallel",),
                                             vmem_limit_bytes=VMEM_LIMIT_BYTES),
        name="inproj",
    )(x, g_mix.reshape(1, D_MODEL), w_in_bf, b_gate.reshape(1, 2 * D_MODEL))


def _neg_lower2(n):
    r = lax.broadcasted_iota(jnp.int32, (2 * n, n), 0)
    c = lax.broadcasted_iota(jnp.int32, (2 * n, n), 1)
    r = jnp.where(r >= n, r - n, r)
    return jnp.where(r >= c, -1.0, 0.0).astype(BF16)


def _neg_abs(x):
    bits = lax.bitcast_convert_type(x, jnp.uint32) | jnp.uint32(0x80000000)
    return lax.bitcast_convert_type(bits, F32)


def _stack_heads(x):
    lane = lax.broadcasted_iota(jnp.int32, x.shape, 1)
    zero = jnp.zeros_like(x)
    return jnp.concatenate([jnp.where(lane < SB_HEAD_DIM, x, zero), jnp.where(lane >= SB_HEAD_DIM, x, zero)], axis=0)


def _causal_mask(tq, tk):
    r = lax.broadcasted_iota(jnp.int32, (2 * tq, tk), 0)
    c = lax.broadcasted_iota(jnp.int32, (2 * tq, tk), 1)
    return c < jnp.where(r >= tq, r - tq, r)


def _sb_pair_blocks(q2, blocks, carry):
    tq = q2.shape[0] // 2
    logits, softplus = [], []
    for k_blk, _, _, causal in blocks:
        z = lax.dot_general(q2, k_blk, (((1,), (1,)), ((), ())), preferred_element_type=F32)
        sp = jnp.maximum(z, 0.0) + jnp.log(1.0 + jnp.exp2(_neg_abs(z))) * LOG2E
        if causal is not None:
            sp = jnp.where(causal, sp, 0.0)
        logits.append(z)
        softplus.append(sp)
    weights, values = [], []
    for (_, v2, neg_tri2, causal), z, sp in zip(blocks, logits, softplus):
        hi = sp.astype(BF16)
        lo = (sp - hi.astype(F32)).astype(BF16)
        a = jnp.exp2(z + _dot(jnp.concatenate([hi, lo], axis=1), neg_tri2) + carry)
        if causal is not None:
            a = jnp.where(causal, a, 0.0)
        a = a.astype(BF16)
        weights += [a[:tq], a[tq:]]
        values.append(v2)
        carry = carry - jnp.sum(sp, axis=1, keepdims=True)
    return _dot(jnp.concatenate(weights, axis=1), jnp.concatenate(values, axis=0)), carry


def _sb_prompt_kernel(q_ref, k_ref, v_ref, o_ref, acc_ref, carry_ref):
    qi = pl.program_id(2)
    blk = SB_BLOCK
    q2 = _stack_heads(q_ref[0])
    neg_tri2 = _neg_lower2(blk)

    def key_block(j, causal):
        r0 = pl.multiple_of(j * blk, blk)
        return k_ref[0, pl.ds(r0, blk), :], _stack_heads(v_ref[0, pl.ds(r0, blk), :]), neg_tri2, causal

    def alive(carry):
        return (jnp.max(carry) > SB_SKIP_BELOW).astype(jnp.int32)

    def sweep(blocks):
        out, carry = _sb_pair_blocks(q2, blocks, jnp.zeros((2 * blk, 1), F32))
        acc_ref[...] = out
        carry_ref[...] = carry

    diagonal = lambda: key_block(qi, _causal_mask(blk, blk))
    pl.when(qi == 0)(lambda: sweep([diagonal()]))
    pl.when(qi > 0)(lambda: sweep([diagonal(), key_block(qi - 1, None)]))

    def body(state):
        j, _ = state
        out, carry = _sb_pair_blocks(q2, [key_block(j, None)], carry_ref[...])
        acc_ref[...] += out
        carry_ref[...] = carry
        return j - 1, alive(carry)

    lax.while_loop(lambda s: jnp.logical_and(s[0] >= 0, s[1] > 0), body, (qi - 2, alive(carry_ref[...])))
    o_ref[0] = acc_ref[...].astype(o_ref.dtype)


def _sb_prompt(qb, kb, vb):
    b, t, _ = qb.shape
    blk = SB_BLOCK
    kv_spec = pl.BlockSpec((1, t, HEAD_PAIR), lambda bi, hp, qi: (bi, 0, hp))
    q_spec = pl.BlockSpec((1, blk, HEAD_PAIR), lambda bi, hp, qi: (bi, qi, hp))
    return pl.pallas_call(
        _sb_prompt_kernel,
        grid=(b, SB_WIDTH // HEAD_PAIR, t // blk),
        in_specs=[q_spec, kv_spec, kv_spec],
        out_specs=q_spec,
        out_shape=jax.ShapeDtypeStruct((b, t, SB_WIDTH), BF16),
        scratch_shapes=[pltpu.VMEM((blk, HEAD_PAIR), F32), pltpu.VMEM((2 * blk, 1), F32)],
        compiler_params=pltpu.CompilerParams(dimension_semantics=("parallel", "parallel", "arbitrary"),
                                             vmem_limit_bytes=VMEM_LIMIT_BYTES),
        name="sb_prompt",
    )(qb, kb, vb)


def _sb_sample_kernel(q_ref, k_ref, v_ref, ck_ref, cv_ref, o_ref, *, past_len):
    blk = SB_BLOCK
    q2 = _stack_heads(q_ref[0])
    t = q2.shape[0] // 2
    blocks = [(k_ref[0], _stack_heads(v_ref[0]), _neg_lower2(t), _causal_mask(t, t))]
    neg_tri2 = _neg_lower2(blk)
    for j in reversed(range(past_len // blk)):
        k_blk = ck_ref[0, j * blk:(j + 1) * blk, :].astype(BF16)
        v2 = _stack_heads(cv_ref[0, j * blk:(j + 1) * blk, :].astype(BF16))
        blocks.append((k_blk, v2, neg_tri2, None))
    out, _ = _sb_pair_blocks(q2, blocks, jnp.zeros((2 * t, 1), F32))
    o_ref[0] = out.astype(o_ref.dtype)


def _sb_sample(qb, kb, vb, cache_k, cache_v):
    b, t, _ = qb.shape
    past_len = cache_k.shape[1]
    assert past_len % SB_BLOCK == 0
    new_spec = pl.BlockSpec((1, t, HEAD_PAIR), lambda bi, hp: (bi, 0, hp))
    cache_spec = pl.BlockSpec((1, past_len, HEAD_PAIR), lambda bi, hp: (bi, 0, hp))
    return pl.pallas_call(
        functools.partial(_sb_sample_kernel, past_len=past_len),
        grid=(b, SB_WIDTH // HEAD_PAIR),
        in_specs=[new_spec, new_spec, new_spec, cache_spec, cache_spec],
        out_specs=new_spec,
        out_shape=jax.ShapeDtypeStruct((b, t, SB_WIDTH), BF16),
        compiler_params=pltpu.CompilerParams(dimension_semantics=("parallel", "parallel"),
                                             vmem_limit_bytes=VMEM_LIMIT_BYTES),
        name="sb_sample",
    )(qb, kb, vb, cache_k, cache_v)


def _pool_diff(ext, first_pos):
    rows = ext.shape[0] - POOL_HALO
    pos = first_pos + lax.broadcasted_iota(jnp.int32, (rows, 1), 0)
    outs = []
    for g, window in enumerate(POOL_WINDOWS):
        cols = ext[:, g * POOL_GROUP_DIM:(g + 1) * POOL_GROUP_DIM]
        acc = cols
        shift = 1
        while shift < window:
            acc = acc + pltpu.roll(acc, shift, axis=0)
            shift *= 2
        count = jnp.minimum(pos + 1, window).astype(F32)
        outs.append(acc[POOL_HALO:] / count - cols[POOL_HALO:])
    return jnp.concatenate(outs, axis=1)


def _post_kernel(x_ref, u_ref, halo_ref, o_ref, gate_ref, p_ref,
                 wgrp_ref, scale_ref, wpu_ref, wsu_ref, wout_ref,
                 gmlp_ref, wup_ref, wdown_ref, gple_ref, wpg_ref, wpp_ref, gfin_ref,
                 y_ref, *, tiles_per_stream, streams_per_tile, past_pos, final_norm):
    u = u_ref[...]
    tm = u.shape[0]
    if streams_per_tile == 1:
        step = pl.program_id(0) % tiles_per_stream
        halo = jnp.where(step == 0, 0.0, halo_ref[...])
        diff = _pool_diff(jnp.concatenate([halo, u], axis=0), past_pos + step * tm)
    else:
        t = tm // streams_per_tile
        diff = jnp.concatenate(
            [_pool_diff(jnp.concatenate([halo_ref[s], u[s * t:(s + 1) * t]], axis=0), past_pos)
             for s in range(streams_per_tile)], axis=0)
    diff = diff.astype(BF16)
    y_pool = jnp.concatenate(
        [_dot(diff[:, g * POOL_GROUP_DIM:(g + 1) * POOL_GROUP_DIM], wgrp_ref[g])
         for g in range(len(POOL_WINDOWS))], axis=1) * scale_ref[...]

    gates = gate_ref[...].astype(F32)
    merged = (gates[:, :D_MODEL] * _dot(y_pool.astype(BF16), wpu_ref[...])
              + gates[:, D_MODEL:] * _dot(o_ref[...], wsu_ref[...]))
    x = x_ref[...] + _dot(merged.astype(BF16), wout_ref[...])

    h = _dot(_rms_norm(x, gmlp_ref[...]).astype(BF16), wup_ref[...])
    h = jnp.square(jnp.maximum(h, 0.0)).astype(BF16)
    x = x + _dot(h, wdown_ref[...])

    ple_gate = jax.nn.sigmoid(_dot(_rms_norm(x, gple_ref[...]).astype(BF16), wpg_ref[...]))
    x = x + ple_gate * _dot(p_ref[...].astype(BF16), wpp_ref[...])
    if final_norm:
        x = _rms_norm(x, gfin_ref[...])
    y_ref[...] = x


def _post(x, u, halo, o, gates, p, weights, *, tm, tiles_per_stream, streams_per_tile, past_pos, final_norm):
    n = x.shape[0]
    row = lambda width: pl.BlockSpec((tm, width), lambda i: (i, 0))
    if streams_per_tile == 1:
        per_tile = tm // POOL_HALO
        halo_spec = pl.BlockSpec((POOL_HALO, POOL_WIDTH), lambda i: (jnp.maximum(i * per_tile - 1, 0), 0))
    else:
        halo_spec = pl.BlockSpec((streams_per_tile, POOL_HALO, POOL_WIDTH), lambda i: (i, 0, 0))
    w_specs = [_resident(w.shape) for w in weights]
    return pl.pallas_call(
        functools.partial(_post_kernel, tiles_per_stream=tiles_per_stream, streams_per_tile=streams_per_tile,
                          past_pos=past_pos, final_norm=final_norm),
        grid=(n // tm,),
        in_specs=[row(D_MODEL), row(POOL_WIDTH), halo_spec, row(SB_WIDTH), row(2 * D_MODEL), row(PLE_DIM)] + w_specs,
        out_specs=row(D_MODEL),
        out_shape=jax.ShapeDtypeStruct((n, D_MODEL), F32),
        compiler_params=pltpu.CompilerParams(dimension_semantics=("parallel",),
                                             vmem_limit_bytes=VMEM_LIMIT_BYTES),
        name="post",
    )(x, u, halo, o, gates, p, *weights)


def kernel(x_prompt, x_sample, cache_k, cache_v, state_pool, p_prompt, p_sample, g_mix, w_in, b_gate, w_pool_grp, pool_scale, w_pool_up, w_sb_up, w_out, g_mlp, w_up, w_down, g_ple, w_ple_gate, w_ple_proj, g_final):
    depth = w_in.shape[0]
    bp, tp, _ = x_prompt.shape
    bs, ts, _ = x_sample.shape
    past_len = cache_k.shape[2]
    assert tp % ROW_TILE == 0 and tp % SB_BLOCK == 0 and ts >= POOL_STATE and ts % 8 == 0
    n_s = bs * ts
    xp = x_prompt.reshape(bp * tp, D_MODEL)
    xs = x_sample.reshape(n_s, D_MODEL)
    row_vec = lambda a: a.reshape(1, -1)

    outs = {name: [] for name in ("kp", "vp", "pp", "ks", "vs", "ps")}
    for d in range(depth):
        w_in_bf = w_in[d].astype(BF16)
        weights = (w_pool_grp[d].astype(BF16), row_vec(pool_scale[d]), w_pool_up[d].astype(BF16),
                   w_sb_up[d].astype(BF16), w_out[d].astype(BF16), row_vec(g_mlp[d]), w_up[d].astype(BF16),
                   w_down[d].astype(BF16), row_vec(g_ple[d]), w_ple_gate[d].astype(BF16),
                   w_ple_proj[d].astype(BF16), row_vec(g_final))
        final_norm = d == depth - 1

        u, k, v, qb, kb, vb, gates = _inproj(xp, g_mix[d], w_in_bf, b_gate[d], ROW_TILE)
        o = _sb_prompt(qb.reshape(bp, tp, SB_WIDTH), kb.reshape(bp, tp, SB_WIDTH), vb.reshape(bp, tp, SB_WIDTH))
        xp = _post(xp, u, u, o.reshape(bp * tp, SB_WIDTH), gates, p_prompt[d].reshape(bp * tp, PLE_DIM), weights,
                   tm=ROW_TILE, tiles_per_stream=tp // ROW_TILE, streams_per_tile=1, past_pos=0,
                   final_norm=final_norm)
        outs["kp"].append(k.reshape(bp, tp, SB_HEADS, SB_HEAD_DIM))
        outs["vp"].append(v.reshape(bp, tp, SB_HEADS, SB_HEAD_DIM))
        outs["pp"].append(u.reshape(bp, tp, POOL_WIDTH)[:, tp - POOL_STATE:])

        u, k, v, qb, kb, vb, gates = _inproj(xs, g_mix[d], w_in_bf, b_gate[d], n_s)
        o = _sb_sample(qb.reshape(bs, ts, SB_WIDTH), kb.reshape(bs, ts, SB_WIDTH), vb.reshape(bs, ts, SB_WIDTH),
                       cache_k[d].reshape(bs, past_len, SB_WIDTH), cache_v[d].reshape(bs, past_len, SB_WIDTH))
        halo = jnp.pad(state_pool[d], ((0, 0), (POOL_HALO - POOL_STATE, 0), (0, 0)))
        xs = _post(xs, u, halo, o.reshape(n_s, SB_WIDTH), gates, p_sample[d].reshape(n_s, PLE_DIM), weights,
                   tm=n_s, tiles_per_stream=1, streams_per_tile=bs, past_pos=POOL_STATE,
                   final_norm=final_norm)
        outs["ks"].append(k.reshape(bs, ts, SB_HEADS, SB_HEAD_DIM))
        outs["vs"].append(v.reshape(bs, ts, SB_HEADS, SB_HEAD_DIM))
        outs["ps"].append(u.reshape(bs, ts, POOL_WIDTH)[:, ts - POOL_STATE:])

    stack = lambda name: jnp.stack(outs[name])
    return (xp.reshape(bp, tp, D_MODEL), xs.reshape(bs, ts, D_MODEL),
            stack("kp"), stack("vp"), stack("pp"), stack("ks"), stack("vs"), stack("ps"))
```

```python
import functools

import jax
import jax.numpy as jnp
from jax import lax
from jax.experimental import pallas as pl
from jax.experimental.pallas import tpu as pltpu

D_MODEL = 1024
POOL_WIDTH = 512
POOL_WINDOWS = (2, 4, 8, 16)
POOL_GROUP_DIM = POOL_WIDTH // len(POOL_WINDOWS)
POOL_STATE = max(POOL_WINDOWS) - 1
POOL_HALO = 16
SB_HEADS = 8
SB_HEAD_DIM = 64
SB_WIDTH = SB_HEADS * SB_HEAD_DIM
HEAD_PAIR = 2 * SB_HEAD_DIM
D_FF = 4 * D_MODEL
PLE_DIM = 256
EPS = 1e-6

V7X_VMEM_BYTES = 64 * 1024 * 1024
VMEM_LIMIT_BYTES = V7X_VMEM_BYTES - 8 * 1024 * 1024

ROW_TILE = 256
SB_BLOCK = 256
SB_PAIRS_PER_STEP = 4

LOG2E = 1.4426950408889634
SB_SKIP_BELOW = -160.0

BF16 = jnp.bfloat16
F32 = jnp.float32


def _rms_norm(x, g):
    y = x * lax.rsqrt(jnp.mean(x * x, axis=-1, keepdims=True) + EPS)
    return y * g


def _dot(a, b):
    return jnp.dot(a, b, preferred_element_type=F32)


def _resident(shape):
    return pl.BlockSpec(shape, lambda *_: (0,) * len(shape), pipeline_mode=pl.Buffered(1))


def _store_head_rows(ref, x):
    rows = x.shape[0]
    for h in range(SB_HEADS):
        ref[pl.ds(h, rows, stride=SB_HEADS), :] = x[:, h * SB_HEAD_DIM:(h + 1) * SB_HEAD_DIM]


def _inproj_kernel(x_ref, g_ref, w_ref, b_ref,
                   u_ref, k_ref, v_ref, qb_ref, kb_ref, vb_ref, gate_ref):
    xn = _rms_norm(x_ref[...], g_ref[...]).astype(BF16)
    c0, c1, c2, c3 = POOL_WIDTH, POOL_WIDTH + SB_WIDTH, POOL_WIDTH + 2 * SB_WIDTH, POOL_WIDTH + 3 * SB_WIDTH
    u_ref[...] = _dot(xn, w_ref[:, :c0])
    qb_ref[...] = (_dot(xn, w_ref[:, c0:c1]) * (SB_HEAD_DIM ** -0.5 * LOG2E)).astype(BF16)
    k = _dot(xn, w_ref[:, c1:c2])
    _store_head_rows(k_ref, k)
    kb_ref[...] = k.astype(BF16)
    v = _dot(xn, w_ref[:, c2:c3])
    _store_head_rows(v_ref, v)
    vb_ref[...] = v.astype(BF16)
    gate_ref[...] = jax.nn.sigmoid(_dot(xn, w_ref[:, c3:]) + b_ref[...]).astype(BF16)


def _inproj(x, g_mix, w_in_bf, b_gate, tm):
    n = x.shape[0]
    in_width = w_in_bf.shape[1]
    row = lambda width: pl.BlockSpec((tm, width), lambda i: (i, 0))
    f32_out = jax.ShapeDtypeStruct((n * SB_HEADS, SB_HEAD_DIM), F32)
    head_rows = pl.BlockSpec((tm * SB_HEADS, SB_HEAD_DIM), lambda i: (i, 0))
    bf_out = jax.ShapeDtypeStruct((n, SB_WIDTH), BF16)
    return pl.pallas_call(
        _inproj_kernel,
        grid=(n // tm,),
        in_specs=[row(D_MODEL), _resident((1, D_MODEL)), _resident((D_MODEL, in_width)),
                  _resident((1, 2 * D_MODEL))],
        out_specs=[row(POOL_WIDTH), head_rows, head_rows, row(SB_WIDTH), row(SB_WIDTH),
                   row(SB_WIDTH), row(2 * D_MODEL)],
        out_shape=[jax.ShapeDtypeStruct((n, POOL_WIDTH), F32), f32_out, f32_out, bf_out, bf_out, bf_out,
                   jax.ShapeDtypeStruct((n, 2 * D_MODEL), BF16)],
        compiler_params=pltpu.CompilerParams(dimension_semantics=("parallel",),
                                             vmem_limit_bytes=VMEM_LIMIT_BYTES),
        name="inproj",
    )(x, g_mix.reshape(1, D_MODEL), w_in_bf, b_gate.reshape(1, 2 * D_MODEL))


def _neg_lower(n):
    r = lax.broadcasted_iota(jnp.int32, (n, n), 0)
    c = lax.broadcasted_iota(jnp.int32, (n, n), 1)
    return jnp.where(r >= c, -1.0, 0.0).astype(BF16)


def _stack_heads(x):
    lane = lax.broadcasted_iota(jnp.int32, x.shape, 1)
    zero = jnp.zeros_like(x)
    return jnp.concatenate([jnp.where(lane < SB_HEAD_DIM, x, zero), jnp.where(lane >= SB_HEAD_DIM, x, zero)], axis=0)


def _causal_mask(tq, tk):
    r = lax.broadcasted_iota(jnp.int32, (2 * tq, tk), 0)
    c = lax.broadcasted_iota(jnp.int32, (2 * tq, tk), 1)
    return c < jnp.where(r >= tq, r - tq, r)


def _sb_pair_blocks(q2, blocks, carry):
    tq = q2.shape[0] // 2
    logits, softplus = [], []
    for k_blk, _, _, causal in blocks:
        z = lax.dot_general(q2, k_blk, (((1,), (1,)), ((), ())), preferred_element_type=F32)
        if causal is not None:
            z = jnp.where(causal, z, -jnp.inf)
        sp = jnp.maximum(z, 0.0) + jnp.log(1.0 + jnp.exp2(-jnp.abs(z))) * LOG2E
        logits.append(z)
        softplus.append(sp)
    weights, values = [], []
    for (_, v2, neg_tri, causal), z, sp in zip(blocks, logits, softplus):
        a = jnp.exp2(z + _dot(sp.astype(BF16), neg_tri) + carry).astype(BF16)
        weights += [a[:tq], a[tq:]]
        values.append(v2)
        carry = carry - jnp.sum(sp, axis=1, keepdims=True)
    return _dot(jnp.concatenate(weights, axis=1), jnp.concatenate(values, axis=0)), carry


def _sb_prompt_kernel(q_ref, k_ref, v_ref, o_ref, acc_ref, carry_ref):
    qi = pl.program_id(2)
    blk = SB_BLOCK
    pairs = range(SB_PAIRS_PER_STEP)
    lanes = lambda p: slice(p * HEAD_PAIR, (p + 1) * HEAD_PAIR)
    q2 = [_stack_heads(q_ref[0, :, lanes(p)]) for p in pairs]
    neg_tri = _neg_lower(blk)

    def key_block(p, j, causal):
        r0 = pl.multiple_of(j * blk, blk)
        return (k_ref[0, pl.ds(r0, blk), lanes(p)], _stack_heads(v_ref[0, pl.ds(r0, blk), lanes(p)]),
                neg_tri, causal)

    def sweep(block_ids, first):
        for p in pairs:
            carry = jnp.zeros((2 * blk, 1), F32) if first else carry_ref[p]
            out, carry = _sb_pair_blocks(q2[p], [key_block(p, j, causal) for j, causal in block_ids], carry)
            if first:
                acc_ref[:, lanes(p)] = out
            else:
                acc_ref[:, lanes(p)] += out
            carry_ref[p] = carry

    def alive():
        return (jnp.max(carry_ref[...]) > SB_SKIP_BELOW).astype(jnp.int32)

    diagonal = (qi, _causal_mask(blk, blk))
    pl.when(qi == 0)(lambda: sweep([diagonal], True))
    pl.when(qi > 0)(lambda: sweep([diagonal, (qi - 1, None)], True))

    def body(state):
        j, _ = state
        sweep([(j, None)], False)
        return j - 1, alive()

    lax.while_loop(lambda s: jnp.logical_and(s[0] >= 0, s[1] > 0), body, (qi - 2, alive()))
    o_ref[0] = acc_ref[...].astype(o_ref.dtype)


def _sb_prompt(qb, kb, vb):
    b, t, _ = qb.shape
    blk = SB_BLOCK
    width = SB_PAIRS_PER_STEP * HEAD_PAIR
    kv_spec = pl.BlockSpec((1, t, width), lambda bi, hp, qi: (bi, 0, hp))
    q_spec = pl.BlockSpec((1, blk, width), lambda bi, hp, qi: (bi, qi, hp))
    return pl.pallas_call(
        _sb_prompt_kernel,
        grid=(b, SB_WIDTH // width, t // blk),
        in_specs=[q_spec, kv_spec, kv_spec],
        out_specs=q_spec,
        out_shape=jax.ShapeDtypeStruct((b, t, SB_WIDTH), BF16),
        scratch_shapes=[pltpu.VMEM((blk, width), F32), pltpu.VMEM((SB_PAIRS_PER_STEP, 2 * blk, 1), F32)],
        compiler_params=pltpu.CompilerParams(dimension_semantics=("parallel", "parallel", "arbitrary"),
                                             vmem_limit_bytes=VMEM_LIMIT_BYTES),
        name="sb_prompt",
    )(qb, kb, vb)


def _sb_sample_kernel(q_ref, k_ref, v_ref, ck_ref, cv_ref, o_ref, *, past_len):
    blk = SB_BLOCK
    q2 = _stack_heads(q_ref[0])
    t = q2.shape[0] // 2
    blocks = [(k_ref[0], _stack_heads(v_ref[0]), _neg_lower(t), _causal_mask(t, t))]
    neg_tri = _neg_lower(blk)
    for j in reversed(range(past_len // blk)):
        k_blk = ck_ref[0, j * blk:(j + 1) * blk, :].astype(BF16)
        v2 = _stack_heads(cv_ref[0, j * blk:(j + 1) * blk, :].astype(BF16))
        blocks.append((k_blk, v2, neg_tri, None))
    out, _ = _sb_pair_blocks(q2, blocks, jnp.zeros((2 * t, 1), F32))
    o_ref[0] = out.astype(o_ref.dtype)


def _sb_sample(qb, kb, vb, cache_k, cache_v):
    b, t, _ = qb.shape
    past_len = cache_k.shape[1]
    assert past_len % SB_BLOCK == 0
    new_spec = pl.BlockSpec((1, t, HEAD_PAIR), lambda bi, hp: (bi, 0, hp))
    cache_spec = pl.BlockSpec((1, past_len, HEAD_PAIR), lambda bi, hp: (bi, 0, hp))
    return pl.pallas_call(
        functools.partial(_sb_sample_kernel, past_len=past_len),
        grid=(b, SB_WIDTH // HEAD_PAIR),
        in_specs=[new_spec, new_spec, new_spec, cache_spec, cache_spec],
        out_specs=new_spec,
        out_shape=jax.ShapeDtypeStruct((b, t, SB_WIDTH), BF16),
        compiler_params=pltpu.CompilerParams(dimension_semantics=("parallel", "parallel"),
                                             vmem_limit_bytes=VMEM_LIMIT_BYTES),
        name="sb_sample",
    )(qb, kb, vb, cache_k, cache_v)


def _pool_diff(ext, first_pos):
    rows = ext.shape[0] - POOL_HALO
    pos = first_pos + lax.broadcasted_iota(jnp.int32, (rows, 1), 0)
    outs = []
    for g, window in enumerate(POOL_WINDOWS):
        cols = ext[:, g * POOL_GROUP_DIM:(g + 1) * POOL_GROUP_DIM]
        acc = cols
        shift = 1
        while shift < window:
            acc = acc + pltpu.roll(acc, shift, axis=0)
            shift *= 2
        count = jnp.minimum(pos + 1, window).astype(F32)
        outs.append(acc[POOL_HALO:] / count - cols[POOL_HALO:])
    return jnp.concatenate(outs, axis=1)


def _post_kernel(x_ref, u_ref, halo_ref, o_ref, gate_ref, p_ref,
                 wgrp_ref, scale_ref, wpu_ref, wsu_ref, wout_ref,
                 gmlp_ref, wup_ref, wdown_ref, gple_ref, wpg_ref, wpp_ref, gfin_ref,
                 y_ref, *, tiles_per_stream, streams_per_tile, past_pos, final_norm):
    u = u_ref[...]
    tm = u.shape[0]
    if streams_per_tile == 1:
        step = pl.program_id(0) % tiles_per_stream
        halo = jnp.where(step == 0, 0.0, halo_ref[...])
        diff = _pool_diff(jnp.concatenate([halo, u], axis=0), past_pos + step * tm)
    else:
        t = tm // streams_per_tile
        diff = jnp.concatenate(
            [_pool_diff(jnp.concatenate([halo_ref[s], u[s * t:(s + 1) * t]], axis=0), past_pos)
             for s in range(streams_per_tile)], axis=0)
    diff = diff.astype(BF16)
    y_pool = jnp.concatenate(
        [_dot(diff[:, g * POOL_GROUP_DIM:(g + 1) * POOL_GROUP_DIM], wgrp_ref[g])
         for g in range(len(POOL_WINDOWS))], axis=1) * scale_ref[...]

    gates = gate_ref[...].astype(F32)
    merged = (gates[:, :D_MODEL] * _dot(y_pool.astype(BF16), wpu_ref[...])
              + gates[:, D_MODEL:] * _dot(o_ref[...], wsu_ref[...]))
    x = x_ref[...] + _dot(merged.astype(BF16), wout_ref[...])

    h = _dot(_rms_norm(x, gmlp_ref[...]).astype(BF16), wup_ref[...])
    h = jnp.square(jnp.maximum(h, 0.0)).astype(BF16)
    x = x + _dot(h, wdown_ref[...])

    ple_gate = jax.nn.sigmoid(_dot(_rms_norm(x, gple_ref[...]).astype(BF16), wpg_ref[...]))
    x = x + ple_gate * _dot(p_ref[...].astype(BF16), wpp_ref[...])
    if final_norm:
        x = _rms_norm(x, gfin_ref[...])
    y_ref[...] = x


def _post(x, u, halo, o, gates, p, weights, *, tm, tiles_per_stream, streams_per_tile, past_pos, final_norm):
    n = x.shape[0]
    row = lambda width: pl.BlockSpec((tm, width), lambda i: (i, 0))
    if streams_per_tile == 1:
        per_tile = tm // POOL_HALO
        halo_spec = pl.BlockSpec((POOL_HALO, POOL_WIDTH), lambda i: (jnp.maximum(i * per_tile - 1, 0), 0))
    else:
        halo_spec = pl.BlockSpec((streams_per_tile, POOL_HALO, POOL_WIDTH), lambda i: (i, 0, 0))
    w_specs = [_resident(w.shape) for w in weights]
    return pl.pallas_call(
        functools.partial(_post_kernel, tiles_per_stream=tiles_per_stream, streams_per_tile=streams_per_tile,
                          past_pos=past_pos, final_norm=final_norm),
        grid=(n // tm,),
        in_specs=[row(D_MODEL), row(POOL_WIDTH), halo_spec, row(SB_WIDTH), row(2 * D_MODEL), row(PLE_DIM)] + w_specs,
        out_specs=row(D_MODEL),
        out_shape=jax.ShapeDtypeStruct((n, D_MODEL), F32),
        compiler_params=pltpu.CompilerParams(dimension_semantics=("parallel",),
                                             vmem_limit_bytes=VMEM_LIMIT_BYTES),
        name="post",
    )(x, u, halo, o, gates, p, *weights)


def kernel(x_prompt, x_sample, cache_k, cache_v, state_pool, p_prompt, p_sample, g_mix, w_in, b_gate, w_pool_grp, pool_scale, w_pool_up, w_sb_up, w_out, g_mlp, w_up, w_down, g_ple, w_ple_gate, w_ple_proj, g_final):
    depth = w_in.shape[0]
    bp, tp, _ = x_prompt.shape
    bs, ts, _ = x_sample.shape
    past_len = cache_k.shape[2]
    assert tp % ROW_TILE == 0 and tp % SB_BLOCK == 0 and ts >= POOL_STATE and ts % 8 == 0
    n_s = bs * ts
    xp = x_prompt.reshape(bp * tp, D_MODEL)
    xs = x_sample.reshape(n_s, D_MODEL)
    row_vec = lambda a: a.reshape(1, -1)

    outs = {name: [] for name in ("kp", "vp", "pp", "ks", "vs", "ps")}
    for d in range(depth):
        w_in_bf = w_in[d].astype(BF16)
        weights = (w_pool_grp[d].astype(BF16), row_vec(pool_scale[d]), w_pool_up[d].astype(BF16),
                   w_sb_up[d].astype(BF16), w_out[d].astype(BF16), row_vec(g_mlp[d]), w_up[d].astype(BF16),
                   w_down[d].astype(BF16), row_vec(g_ple[d]), w_ple_gate[d].astype(BF16),
                   w_ple_proj[d].astype(BF16), row_vec(g_final))
        final_norm = d == depth - 1

        u, k, v, qb, kb, vb, gates = _inproj(xp, g_mix[d], w_in_bf, b_gate[d], ROW_TILE)
        o = _sb_prompt(qb.reshape(bp, tp, SB_WIDTH), kb.reshape(bp, tp, SB_WIDTH), vb.reshape(bp, tp, SB_WIDTH))
        xp = _post(xp, u, u, o.reshape(bp * tp, SB_WIDTH), gates, p_prompt[d].reshape(bp * tp, PLE_DIM), weights,
                   tm=ROW_TILE, tiles_per_stream=tp // ROW_TILE, streams_per_tile=1, past_pos=0,
                   final_norm=final_norm)
        outs["kp"].append(k.reshape(bp, tp, SB_HEADS, SB_HEAD_DIM))
        outs["vp"].append(v.reshape(bp, tp, SB_HEADS, SB_HEAD_DIM))
        outs["pp"].append(u.reshape(bp, tp, POOL_WIDTH)[:, tp - POOL_STATE:])

        u, k, v, qb, kb, vb, gates = _inproj(xs, g_mix[d], w_in_bf, b_gate[d], n_s)
        o = _sb_sample(qb.reshape(bs, ts, SB_WIDTH), kb.reshape(bs, ts, SB_WIDTH), vb.reshape(bs, ts, SB_WIDTH),
                       cache_k[d].reshape(bs, past_len, SB_WIDTH), cache_v[d].reshape(bs, past_len, SB_WIDTH))
        halo = jnp.pad(state_pool[d], ((0, 0), (POOL_HALO - POOL_STATE, 0), (0, 0)))
        xs = _post(xs, u, halo, o.reshape(n_s, SB_WIDTH), gates, p_sample[d].reshape(n_s, PLE_DIM), weights,
                   tm=n_s, tiles_per_stream=1, streams_per_tile=bs, past_pos=POOL_STATE,
                   final_norm=final_norm)
        outs["ks"].append(k.reshape(bs, ts, SB_HEADS, SB_HEAD_DIM))
        outs["vs"].append(v.reshape(bs, ts, SB_HEADS, SB_HEAD_DIM))
        outs["ps"].append(u.reshape(bs, ts, POOL_WIDTH)[:, ts - POOL_STATE:])

    stack = lambda name: jnp.stack(outs[name])
    return (xp.reshape(bp, tp, D_MODEL), xs.reshape(bs, ts, D_MODEL),
            stack("kp"), stack("vp"), stack("pp"), stack("ks"), stack("vs"), stack("ps"))
```

```python
import functools

import jax
import jax.numpy as jnp
from jax import lax
from jax.experimental import pallas as pl
from jax.experimental.pallas import tpu as pltpu

D_MODEL = 1024
POOL_WIDTH = 512
POOL_WINDOWS = (2, 4, 8, 16)
POOL_GROUP_DIM = POOL_WIDTH // len(POOL_WINDOWS)
POOL_STATE = max(POOL_WINDOWS) - 1
POOL_HALO = 16
SB_HEADS = 8
SB_HEAD_DIM = 64
SB_WIDTH = SB_HEADS * SB_HEAD_DIM
HEAD_PAIR = 2 * SB_HEAD_DIM
D_FF = 4 * D_MODEL
PLE_DIM = 256
EPS = 1e-6

V7X_VMEM_BYTES = 64 * 1024 * 1024
VMEM_LIMIT_BYTES = V7X_VMEM_BYTES - 8 * 1024 * 1024

ROW_TILE = 256
SB_BLOCK = 256
SB_PAIRS_PER_STEP = 4

LOG2E = 1.4426950408889634
SB_SKIP_BELOW = -160.0

BF16 = jnp.bfloat16
F32 = jnp.float32


def _rms_norm(x, g):
    y = x * lax.rsqrt(jnp.mean(x * x, axis=-1, keepdims=True) + EPS)
    return y * g


def _dot(a, b):
    return jnp.dot(a, b, preferred_element_type=F32)


def _resident(shape):
    return pl.BlockSpec(shape, lambda *_: (0,) * len(shape), pipeline_mode=pl.Buffered(1))


def _store_head_rows(ref, x):
    rows = x.shape[0]
    for h in range(SB_HEADS):
        ref[pl.ds(h, rows, stride=SB_HEADS), :] = x[:, h * SB_HEAD_DIM:(h + 1) * SB_HEAD_DIM]


def _inproj_kernel(x_ref, g_ref, w_ref, b_ref,
                   u_ref, k_ref, v_ref, qb_ref, kb_ref, vb_ref, gate_ref):
    xn = _rms_norm(x_ref[...], g_ref[...]).astype(BF16)
    c0, c1, c2, c3 = POOL_WIDTH, POOL_WIDTH + SB_WIDTH, POOL_WIDTH + 2 * SB_WIDTH, POOL_WIDTH + 3 * SB_WIDTH
    u_ref[...] = _dot(xn, w_ref[:, :c0])
    qb_ref[...] = (_dot(xn, w_ref[:, c0:c1]) * (SB_HEAD_DIM ** -0.5 * LOG2E)).astype(BF16)
    k = _dot(xn, w_ref[:, c1:c2])
    _store_head_rows(k_ref, k)
    kb_ref[...] = k.astype(BF16)
    v = _dot(xn, w_ref[:, c2:c3])
    _store_head_rows(v_ref, v)
    vb_ref[...] = v.astype(BF16)
    gate_ref[...] = jax.nn.sigmoid(_dot(xn, w_ref[:, c3:]) + b_ref[...]).astype(BF16)


def _inproj(x, g_mix, w_in_bf, b_gate, tm):
    n = x.shape[0]
    in_width = w_in_bf.shape[1]
    row = lambda width: pl.BlockSpec((tm, width), lambda i: (i, 0))
    f32_out = jax.ShapeDtypeStruct((n * SB_HEADS, SB_HEAD_DIM), F32)
    head_rows = pl.BlockSpec((tm * SB_HEADS, SB_HEAD_DIM), lambda i: (i, 0))
    bf_out = jax.ShapeDtypeStruct((n, SB_WIDTH), BF16)
    return pl.pallas_call(
        _inproj_kernel,
        grid=(n // tm,),
        in_specs=[row(D_MODEL), _resident((1, D_MODEL)), _resident((D_MODEL, in_width)),
                  _resident((1, 2 * D_MODEL))],
        out_specs=[row(POOL_WIDTH), head_rows, head_rows, row(SB_WIDTH), row(SB_WIDTH),
                   row(SB_WIDTH), row(2 * D_MODEL)],
        out_shape=[jax.ShapeDtypeStruct((n, POOL_WIDTH), F32), f32_out, f32_out, bf_out, bf_out, bf_out,
                   jax.ShapeDtypeStruct((n, 2 * D_MODEL), BF16)],
        compiler_params=pltpu.CompilerParams(dimension_semantics=("parallel",),
                                             vmem_limit_bytes=VMEM_LIMIT_BYTES),
        name="inproj",
    )(x, g_mix.reshape(1, D_MODEL), w_in_bf, b_gate.reshape(1, 2 * D_MODEL))


def _neg_lower(n):
    r = lax.broadcasted_iota(jnp.int32, (n, n), 0)
    c = lax.broadcasted_iota(jnp.int32, (n, n), 1)
    return jnp.where(r >= c, -1.0, 0.0).astype(BF16)


def _stack_heads(x):
    lane = lax.broadcasted_iota(jnp.int32, x.shape, 1)
    zero = jnp.zeros_like(x)
    return jnp.concatenate([jnp.where(lane < SB_HEAD_DIM, x, zero), jnp.where(lane >= SB_HEAD_DIM, x, zero)], axis=0)


def _causal_mask(tq, tk):
    r = lax.broadcasted_iota(jnp.int32, (2 * tq, tk), 0)
    c = lax.broadcasted_iota(jnp.int32, (2 * tq, tk), 1)
    return c < jnp.where(r >= tq, r - tq, r)


def _sb_pair_blocks(q2, blocks, carry):
    tq = q2.shape[0] // 2
    logits, softplus = [], []
    for k_blk, _, _, causal in blocks:
        z = lax.dot_general(q2, k_blk, (((1,), (1,)), ((), ())), preferred_element_type=F32)
        if causal is not None:
            z = jnp.where(causal, z, -jnp.inf)
        sp = jnp.maximum(z, 0.0) + jnp.log(1.0 + jnp.exp2(-jnp.abs(z))) * LOG2E
        logits.append(z)
        softplus.append(sp)
    weights, values = [], []
    for (_, v2, neg_tri, causal), z, sp in zip(blocks, logits, softplus):
        a = jnp.exp2(z + _dot(sp.astype(BF16), neg_tri) + carry).astype(BF16)
        weights += [a[:tq], a[tq:]]
        values.append(v2)
        carry = carry - jnp.sum(sp, axis=1, keepdims=True)
    return _dot(jnp.concatenate(weights, axis=1), jnp.concatenate(values, axis=0)), carry


def _sb_prompt_kernel(q_ref, k_ref, v_ref, o_ref, acc_ref, carry_ref):
    qi = pl.program_id(2)
    blk = SB_BLOCK
    pairs = range(SB_PAIRS_PER_STEP)
    lanes = lambda p: slice(p * HEAD_PAIR, (p + 1) * HEAD_PAIR)
    q2 = [_stack_heads(q_ref[0, :, lanes(p)]) for p in pairs]
    neg_tri = _neg_lower(blk)

    def key_block(p, j, causal):
        r0 = pl.multiple_of(j * blk, blk)
        return (k_ref[0, pl.ds(r0, blk), lanes(p)], _stack_heads(v_ref[0, pl.ds(r0, blk), lanes(p)]),
                neg_tri, causal)

    def sweep(block_ids, first):
        for p in pairs:
            carry = jnp.zeros((2 * blk, 1), F32) if first else carry_ref[p]
            out, carry = _sb_pair_blocks(q2[p], [key_block(p, j, causal) for j, causal in block_ids], carry)
            if first:
                acc_ref[:, lanes(p)] = out
            else:
                acc_ref[:, lanes(p)] += out
            carry_ref[p] = carry

    def alive():
        return (jnp.max(carry_ref[...]) > SB_SKIP_BELOW).astype(jnp.int32)

    diagonal = (qi, _causal_mask(blk, blk))
    pl.when(qi == 0)(lambda: sweep([diagonal], True))
    pl.when(qi > 0)(lambda: sweep([diagonal, (qi - 1, None)], True))

    def body(state):
        j, _ = state
        sweep([(j, None)], False)
        return j - 1, alive()

    lax.while_loop(lambda s: jnp.logical_and(s[0] >= 0, s[1] > 0), body, (qi - 2, alive()))
    o_ref[0] = acc_ref[...].astype(o_ref.dtype)


def _sb_prompt(qb, kb, vb):
    b, t, _ = qb.shape
    blk = SB_BLOCK
    width = SB_PAIRS_PER_STEP * HEAD_PAIR
    kv_spec = pl.BlockSpec((1, t, width), lambda bi, hp, qi: (bi, 0, hp))
    q_spec = pl.BlockSpec((1, blk, width), lambda bi, hp, qi: (bi, qi, hp))
    return pl.pallas_call(
        _sb_prompt_kernel,
        grid=(b, SB_WIDTH // width, t // blk),
        in_specs=[q_spec, kv_spec, kv_spec],
        out_specs=q_spec,
        out_shape=jax.ShapeDtypeStruct((b, t, SB_WIDTH), BF16),
        scratch_shapes=[pltpu.VMEM((blk, width), F32), pltpu.VMEM((SB_PAIRS_PER_STEP, 2 * blk, 1), F32)],
        compiler_params=pltpu.CompilerParams(dimension_semantics=("parallel", "parallel", "arbitrary"),
                                             vmem_limit_bytes=VMEM_LIMIT_BYTES),
        name="sb_prompt",
    )(qb, kb, vb)


def _sb_heads_block(q, keys, values, carry, neg_tri, causal):
    t = q[0].shape[0]
    z = jnp.concatenate([lax.dot_general(qh, kh, (((1,), (1,)), ((), ())), preferred_element_type=F32)
                         for qh, kh in zip(q, keys)], axis=0)
    if causal is not None:
        z = jnp.where(causal, z, -jnp.inf)
    sp = jnp.maximum(z, 0.0) + jnp.log(1.0 + jnp.exp2(-jnp.abs(z))) * LOG2E
    a = jnp.exp2(z + _dot(sp.astype(BF16), neg_tri) + carry).astype(BF16)
    outs = [_dot(a[h * t:(h + 1) * t], vh) for h, vh in enumerate(values)]
    return outs, carry - jnp.sum(sp, axis=1, keepdims=True)


def _sb_sample_kernel(q_ref, k_ref, v_ref, ck_hbm, cv_hbm, o_ref, kbuf, vbuf, acc_ref, carry_ref, sem,
                      *, layer, past_len):
    stream = pl.program_id(0)
    blk = SB_BLOCK
    t = q_ref.shape[1]
    heads = range(SB_HEADS)
    cols = lambda h: slice(h * SB_HEAD_DIM, (h + 1) * SB_HEAD_DIM)
    q = [q_ref[0, :, cols(h)] for h in heads]

    def fetch(j):
        rows = pl.ds(pl.multiple_of(j * (blk * SB_HEADS), blk * SB_HEADS), blk * SB_HEADS)
        return (pltpu.make_async_copy(ck_hbm.at[layer, stream, rows], kbuf, sem.at[0]),
                pltpu.make_async_copy(cv_hbm.at[layer, stream, rows], vbuf, sem.at[1]))

    def accumulate(outs, carry, first):
        for h in heads:
            if first:
                acc_ref[:, cols(h)] = outs[h]
            else:
                acc_ref[:, cols(h)] += outs[h]
        carry_ref[...] = carry

    def cached_block():
        head_rows = lambda buf, h: buf[pl.ds(h, blk, stride=SB_HEADS), :].astype(BF16)
        outs, carry = _sb_heads_block(q, [head_rows(kbuf, h) for h in heads], [head_rows(vbuf, h) for h in heads],
                                      carry_ref[...], _neg_lower(blk), None)
        accumulate(outs, carry, False)

    def alive():
        return (jnp.max(carry_ref[...]) > SB_SKIP_BELOW).astype(jnp.int32)

    last = past_len // blk - 1
    copies = fetch(last)
    for c in copies:
        c.start()
    r = lax.broadcasted_iota(jnp.int32, (SB_HEADS * t, t), 0)
    causal = lax.broadcasted_iota(jnp.int32, (SB_HEADS * t, t), 1) < lax.rem(r, t)
    outs, carry = _sb_heads_block(q, [k_ref[0, :, cols(h)] for h in heads], [v_ref[0, :, cols(h)] for h in heads],
                                  jnp.zeros((SB_HEADS * t, 1), F32), _neg_lower(t), causal)
    accumulate(outs, carry, True)
    for c in copies:
        c.wait()
    cached_block()

    def body(state):
        j, _ = state
        copies = fetch(j)
        for c in copies:
            c.start()
        for c in copies:
            c.wait()
        cached_block()
        return j - 1, alive()

    lax.while_loop(lambda s: jnp.logical_and(s[0] >= 0, s[1] > 0), body, (last - 1, alive()))
    o_ref[0] = acc_ref[...].astype(o_ref.dtype)


def _sb_sample(qb, kb, vb, cache_k, cache_v, layer):
    b, t, _ = qb.shape
    past_len = cache_k.shape[2] // SB_HEADS
    assert past_len % SB_BLOCK == 0
    new_spec = pl.BlockSpec((1, t, SB_WIDTH), lambda bi: (bi, 0, 0))
    hbm_spec = pl.BlockSpec(memory_space=pl.ANY)
    block_rows = SB_BLOCK * SB_HEADS
    return pl.pallas_call(
        functools.partial(_sb_sample_kernel, layer=layer, past_len=past_len),
        grid=(b,),
        in_specs=[new_spec, new_spec, new_spec, hbm_spec, hbm_spec],
        out_specs=new_spec,
        out_shape=jax.ShapeDtypeStruct((b, t, SB_WIDTH), BF16),
        scratch_shapes=[pltpu.VMEM((block_rows, SB_HEAD_DIM), F32), pltpu.VMEM((block_rows, SB_HEAD_DIM), F32),
                        pltpu.VMEM((t, SB_WIDTH), F32), pltpu.VMEM((SB_HEADS * t, 1), F32),
                        pltpu.SemaphoreType.DMA((2,))],
        compiler_params=pltpu.CompilerParams(dimension_semantics=("arbitrary",),
                                             vmem_limit_bytes=VMEM_LIMIT_BYTES),
        name="sb_sample",
    )(qb, kb, vb, cache_k, cache_v)


def _pool_diff(ext, first_pos):
    rows = ext.shape[0] - POOL_HALO
    pos = first_pos + lax.broadcasted_iota(jnp.int32, (rows, 1), 0)
    outs = []
    for g, window in enumerate(POOL_WINDOWS):
        cols = ext[:, g * POOL_GROUP_DIM:(g + 1) * POOL_GROUP_DIM]
        acc = cols
        shift = 1
        while shift < window:
            acc = acc + pltpu.roll(acc, shift, axis=0)
            shift *= 2
        count = jnp.minimum(pos + 1, window).astype(F32)
        outs.append(acc[POOL_HALO:] / count - cols[POOL_HALO:])
    return jnp.concatenate(outs, axis=1)


def _post_kernel(x_ref, u_ref, halo_ref, o_ref, gate_ref, p_ref,
                 wgrp_ref, scale_ref, wpu_ref, wsu_ref, wout_ref,
                 gmlp_ref, wup_ref, wdown_ref, gple_ref, wpg_ref, wpp_ref, gfin_ref,
                 y_ref, *, tiles_per_stream, streams_per_tile, past_pos, final_norm):
    u = u_ref[...]
    tm = u.shape[0]
    if streams_per_tile == 1:
        step = pl.program_id(0) % tiles_per_stream
        halo = jnp.where(step == 0, 0.0, halo_ref[...])
        diff = _pool_diff(jnp.concatenate([halo, u], axis=0), past_pos + step * tm)
    else:
        t = tm // streams_per_tile
        diff = jnp.concatenate(
            [_pool_diff(jnp.concatenate([halo_ref[s], u[s * t:(s + 1) * t]], axis=0), past_pos)
             for s in range(streams_per_tile)], axis=0)
    diff = diff.astype(BF16)
    y_pool = jnp.concatenate(
        [_dot(diff[:, g * POOL_GROUP_DIM:(g + 1) * POOL_GROUP_DIM], wgrp_ref[g])
         for g in range(len(POOL_WINDOWS))], axis=1) * scale_ref[...]

    gates = gate_ref[...].astype(F32)
    merged = (gates[:, :D_MODEL] * _dot(y_pool.astype(BF16), wpu_ref[...])
              + gates[:, D_MODEL:] * _dot(o_ref[...], wsu_ref[...]))
    x = x_ref[...] + _dot(merged.astype(BF16), wout_ref[...])

    h = _dot(_rms_norm(x, gmlp_ref[...]).astype(BF16), wup_ref[...])
    h = jnp.square(jnp.maximum(h, 0.0)).astype(BF16)
    x = x + _dot(h, wdown_ref[...])

    ple_gate = jax.nn.sigmoid(_dot(_rms_norm(x, gple_ref[...]).astype(BF16), wpg_ref[...]))
    x = x + ple_gate * _dot(p_ref[...].astype(BF16), wpp_ref[...])
    if final_norm:
        x = _rms_norm(x, gfin_ref[...])
    y_ref[...] = x


def _post(x, u, halo, o, gates, p, weights, *, tm, tiles_per_stream, streams_per_tile, past_pos, final_norm):
    n = x.shape[0]
    row = lambda width: pl.BlockSpec((tm, width), lambda i: (i, 0))
    if streams_per_tile == 1:
        per_tile = tm // POOL_HALO
        halo_spec = pl.BlockSpec((POOL_HALO, POOL_WIDTH), lambda i: (jnp.maximum(i * per_tile - 1, 0), 0))
    else:
        halo_spec = pl.BlockSpec((streams_per_tile, POOL_HALO, POOL_WIDTH), lambda i: (i, 0, 0))
    w_specs = [_resident(w.shape) for w in weights]
    return pl.pallas_call(
        functools.partial(_post_kernel, tiles_per_stream=tiles_per_stream, streams_per_tile=streams_per_tile,
                          past_pos=past_pos, final_norm=final_norm),
        grid=(n // tm,),
        in_specs=[row(D_MODEL), row(POOL_WIDTH), halo_spec, row(SB_WIDTH), row(2 * D_MODEL), row(PLE_DIM)] + w_specs,
        out_specs=row(D_MODEL),
        out_shape=jax.ShapeDtypeStruct((n, D_MODEL), F32),
        compiler_params=pltpu.CompilerParams(dimension_semantics=("parallel",),
                                             vmem_limit_bytes=VMEM_LIMIT_BYTES),
        name="post",
    )(x, u, halo, o, gates, p, *weights)


def kernel(x_prompt, x_sample, cache_k, cache_v, state_pool, p_prompt, p_sample, g_mix, w_in, b_gate, w_pool_grp, pool_scale, w_pool_up, w_sb_up, w_out, g_mlp, w_up, w_down, g_ple, w_ple_gate, w_ple_proj, g_final):
    depth = w_in.shape[0]
    bp, tp, _ = x_prompt.shape
    bs, ts, _ = x_sample.shape
    past_len = cache_k.shape[2]
    assert tp % ROW_TILE == 0 and tp % SB_BLOCK == 0 and ts >= POOL_STATE and ts % 8 == 0
    n_s = bs * ts
    xp = x_prompt.reshape(bp * tp, D_MODEL)
    xs = x_sample.reshape(n_s, D_MODEL)
    row_vec = lambda a: a.reshape(1, -1)
    cache_rows = lambda c: c.reshape(depth, bs, past_len * SB_HEADS, SB_HEAD_DIM)

    outs = {name: [] for name in ("kp", "vp", "pp", "ks", "vs", "ps")}
    for d in range(depth):
        w_in_bf = w_in[d].astype(BF16)
        weights = (w_pool_grp[d].astype(BF16), row_vec(pool_scale[d]), w_pool_up[d].astype(BF16),
                   w_sb_up[d].astype(BF16), w_out[d].astype(BF16), row_vec(g_mlp[d]), w_up[d].astype(BF16),
                   w_down[d].astype(BF16), row_vec(g_ple[d]), w_ple_gate[d].astype(BF16),
                   w_ple_proj[d].astype(BF16), row_vec(g_final))
        final_norm = d == depth - 1

        u, k, v, qb, kb, vb, gates = _inproj(xp, g_mix[d], w_in_bf, b_gate[d], ROW_TILE)
        o = _sb_prompt(qb.reshape(bp, tp, SB_WIDTH), kb.reshape(bp, tp, SB_WIDTH), vb.reshape(bp, tp, SB_WIDTH))
        xp = _post(xp, u, u, o.reshape(bp * tp, SB_WIDTH), gates, p_prompt[d].reshape(bp * tp, PLE_DIM), weights,
                   tm=ROW_TILE, tiles_per_stream=tp // ROW_TILE, streams_per_tile=1, past_pos=0,
                   final_norm=final_norm)
        outs["kp"].append(k.reshape(bp, tp, SB_HEADS, SB_HEAD_DIM))
        outs["vp"].append(v.reshape(bp, tp, SB_HEADS, SB_HEAD_DIM))
        outs["pp"].append(u.reshape(bp, tp, POOL_WIDTH)[:, tp - POOL_STATE:])

        u, k, v, qb, kb, vb, gates = _inproj(xs, g_mix[d], w_in_bf, b_gate[d], n_s)
        o = _sb_sample(qb.reshape(bs, ts, SB_WIDTH), kb.reshape(bs, ts, SB_WIDTH), vb.reshape(bs, ts, SB_WIDTH),
                       cache_rows(cache_k), cache_rows(cache_v), d)
        halo = jnp.pad(state_pool[d], ((0, 0), (POOL_HALO - POOL_STATE, 0), (0, 0)))
        xs = _post(xs, u, halo, o.reshape(n_s, SB_WIDTH), gates, p_sample[d].reshape(n_s, PLE_DIM), weights,
                   tm=n_s, tiles_per_stream=1, streams_per_tile=bs, past_pos=POOL_STATE,
                   final_norm=final_norm)
        outs["ks"].append(k.reshape(bs, ts, SB_HEADS, SB_HEAD_DIM))
        outs["vs"].append(v.reshape(bs, ts, SB_HEADS, SB_HEAD_DIM))
        outs["ps"].append(u.reshape(bs, ts, POOL_WIDTH)[:, ts - POOL_STATE:])

    stack = lambda name: jnp.stack(outs[name])
    return (xp.reshape(bp, tp, D_MODEL), xs.reshape(bs, ts, D_MODEL),
            stack("kp"), stack("vp"), stack("pp"), stack("ks"), stack("vs"), stack("ps"))
```

```python
import functools

import jax
import jax.numpy as jnp
from jax import lax
from jax.experimental import pallas as pl
from jax.experimental.pallas import tpu as pltpu

D_MODEL = 1024
POOL_WIDTH = 512
POOL_WINDOWS = (2, 4, 8, 16)
POOL_GROUP_DIM = POOL_WIDTH // len(POOL_WINDOWS)
POOL_STATE = max(POOL_WINDOWS) - 1
POOL_HALO = 16
SB_HEADS = 8
SB_HEAD_DIM = 64
SB_WIDTH = SB_HEADS * SB_HEAD_DIM
HEAD_PAIR = 2 * SB_HEAD_DIM
D_FF = 4 * D_MODEL
PLE_DIM = 256
EPS = 1e-6

V7X_VMEM_BYTES = 64 * 1024 * 1024
VMEM_LIMIT_BYTES = V7X_VMEM_BYTES - 8 * 1024 * 1024

ROW_TILE = 256
SB_BLOCK = 256
SB_PAIRS_PER_STEP = 4

LOG2E = 1.4426950408889634
SB_SKIP_BELOW = -160.0

BF16 = jnp.bfloat16
F32 = jnp.float32


def _rms_norm(x, g):
    y = x * lax.rsqrt(jnp.mean(x * x, axis=-1, keepdims=True) + EPS)
    return y * g


def _dot(a, b):
    return jnp.dot(a, b, preferred_element_type=F32)


def _dot_nt(a, b):
    return lax.dot_general(a, b, (((1,), (1,)), ((), ())), preferred_element_type=F32)


def _resident(shape):
    return pl.BlockSpec(shape, lambda *_: (0,) * len(shape), pipeline_mode=pl.Buffered(1))


def _store_head_rows(ref, x):
    rows = x.shape[0]
    for h in range(SB_HEADS):
        ref[pl.ds(h, rows, stride=SB_HEADS), :] = x[:, h * SB_HEAD_DIM:(h + 1) * SB_HEAD_DIM]


def _inproj_kernel(x_ref, g_ref, w_ref, b_ref,
                   u_ref, k_ref, v_ref, qb_ref, kb_ref, vb_ref, gate_ref):
    xn = _rms_norm(x_ref[...], g_ref[...]).astype(BF16)
    c0, c1, c2, c3 = POOL_WIDTH, POOL_WIDTH + SB_WIDTH, POOL_WIDTH + 2 * SB_WIDTH, POOL_WIDTH + 3 * SB_WIDTH
    u_ref[...] = _dot(xn, w_ref[:, :c0])
    qb_ref[...] = (_dot(xn, w_ref[:, c0:c1]) * (SB_HEAD_DIM ** -0.5 * LOG2E)).astype(BF16)
    k = _dot(xn, w_ref[:, c1:c2])
    _store_head_rows(k_ref, k)
    kb_ref[...] = k.astype(BF16)
    v = _dot(xn, w_ref[:, c2:c3])
    _store_head_rows(v_ref, v)
    vb_ref[...] = v.astype(BF16)
    gate_ref[...] = jax.nn.sigmoid(_dot(xn, w_ref[:, c3:]) + b_ref[...]).astype(BF16)


def _inproj(x, g_mix, w_in_bf, b_gate, tm):
    n = x.shape[0]
    in_width = w_in_bf.shape[1]
    row = lambda width: pl.BlockSpec((tm, width), lambda i: (i, 0))
    f32_out = jax.ShapeDtypeStruct((n * SB_HEADS, SB_HEAD_DIM), F32)
    head_rows = pl.BlockSpec((tm * SB_HEADS, SB_HEAD_DIM), lambda i: (i, 0))
    bf_out = jax.ShapeDtypeStruct((n, SB_WIDTH), BF16)
    return pl.pallas_call(
        _inproj_kernel,
        grid=(n // tm,),
        in_specs=[row(D_MODEL), _resident((1, D_MODEL)), _resident((D_MODEL, in_width)),
                  _resident((1, 2 * D_MODEL))],
        out_specs=[row(POOL_WIDTH), head_rows, head_rows, row(SB_WIDTH), row(SB_WIDTH),
                   row(SB_WIDTH), row(2 * D_MODEL)],
        out_shape=[jax.ShapeDtypeStruct((n, POOL_WIDTH), F32), f32_out, f32_out, bf_out, bf_out, bf_out,
                   jax.ShapeDtypeStruct((n, 2 * D_MODEL), BF16)],
        compiler_params=pltpu.CompilerParams(dimension_semantics=("parallel",),
                                             vmem_limit_bytes=VMEM_LIMIT_BYTES),
        name="inproj",
    )(x, g_mix.reshape(1, D_MODEL), w_in_bf, b_gate.reshape(1, 2 * D_MODEL))


def _neg_lower(n):
    r = lax.broadcasted_iota(jnp.int32, (n, n), 0)
    c = lax.broadcasted_iota(jnp.int32, (n, n), 1)
    return jnp.where(r >= c, -1.0, 0.0).astype(BF16)


def _stack_heads(x):
    lane = lax.broadcasted_iota(jnp.int32, x.shape, 1)
    zero = jnp.zeros_like(x)
    return jnp.concatenate([jnp.where(lane < SB_HEAD_DIM, x, zero), jnp.where(lane >= SB_HEAD_DIM, x, zero)], axis=0)


def _causal_mask(tq, tk):
    r = lax.broadcasted_iota(jnp.int32, (2 * tq, tk), 0)
    c = lax.broadcasted_iota(jnp.int32, (2 * tq, tk), 1)
    return c < jnp.where(r >= tq, r - tq, r)


def _sb_pair_blocks(q2, blocks, carry):
    tq = q2.shape[0] // 2
    logits, softplus = [], []
    for k_blk, _, _, causal in blocks:
        z = _dot_nt(q2, k_blk)
        if causal is not None:
            z = jnp.where(causal, z, -jnp.inf)
        sp = jnp.maximum(z, 0.0) + jnp.log(1.0 + jnp.exp2(-jnp.abs(z))) * LOG2E
        logits.append(z)
        softplus.append(sp)
    weights, values = [], []
    for (_, v2, neg_tri, causal), z, sp in zip(blocks, logits, softplus):
        a = jnp.exp2(z + _dot(sp.astype(BF16), neg_tri) + carry).astype(BF16)
        weights += [a[:tq], a[tq:]]
        values.append(v2)
        carry = carry - jnp.sum(sp, axis=1, keepdims=True)
    return _dot(jnp.concatenate(weights, axis=1), jnp.concatenate(values, axis=0)), carry


def _sb_prompt_kernel(q_ref, k_ref, v_ref, o_ref, acc_ref, carry_ref):
    qi = pl.program_id(2)
    blk = SB_BLOCK
    pairs = range(SB_PAIRS_PER_STEP)
    lanes = lambda p: slice(p * HEAD_PAIR, (p + 1) * HEAD_PAIR)
    q2 = [_stack_heads(q_ref[0, :, lanes(p)]) for p in pairs]
    neg_tri = _neg_lower(blk)

    def key_block(p, j, causal):
        r0 = pl.multiple_of(j * blk, blk)
        return (k_ref[0, pl.ds(r0, blk), lanes(p)], _stack_heads(v_ref[0, pl.ds(r0, blk), lanes(p)]),
                neg_tri, causal)

    def sweep(block_ids, first):
        for p in pairs:
            carry = jnp.zeros((2 * blk, 1), F32) if first else carry_ref[p]
            out, carry = _sb_pair_blocks(q2[p], [key_block(p, j, causal) for j, causal in block_ids], carry)
            if first:
                acc_ref[:, lanes(p)] = out
            else:
                acc_ref[:, lanes(p)] += out
            carry_ref[p] = carry

    def alive():
        return (jnp.max(carry_ref[...]) > SB_SKIP_BELOW).astype(jnp.int32)

    diagonal = (qi, _causal_mask(blk, blk))
    pl.when(qi == 0)(lambda: sweep([diagonal], True))
    pl.when(qi > 0)(lambda: sweep([diagonal, (qi - 1, None)], True))

    def body(state):
        j, _ = state
        sweep([(j, None)], False)
        return j - 1, alive()

    lax.while_loop(lambda s: jnp.logical_and(s[0] >= 0, s[1] > 0), body, (qi - 2, alive()))
    o_ref[0] = acc_ref[...].astype(o_ref.dtype)


def _sb_prompt(qb, kb, vb):
    b, t, _ = qb.shape
    blk = SB_BLOCK
    width = SB_PAIRS_PER_STEP * HEAD_PAIR
    kv_spec = pl.BlockSpec((1, t, width), lambda bi, hp, qi: (bi, 0, hp))
    q_spec = pl.BlockSpec((1, blk, width), lambda bi, hp, qi: (bi, qi, hp))
    return pl.pallas_call(
        _sb_prompt_kernel,
        grid=(b, SB_WIDTH // width, t // blk),
        in_specs=[q_spec, kv_spec, kv_spec],
        out_specs=q_spec,
        out_shape=jax.ShapeDtypeStruct((b, t, SB_WIDTH), BF16),
        scratch_shapes=[pltpu.VMEM((blk, width), F32), pltpu.VMEM((SB_PAIRS_PER_STEP, 2 * blk, 1), F32)],
        compiler_params=pltpu.CompilerParams(dimension_semantics=("parallel", "parallel", "arbitrary"),
                                             vmem_limit_bytes=VMEM_LIMIT_BYTES),
        name="sb_prompt",
    )(qb, kb, vb)


def _sb_heads_block(q, keys, values, carry, neg_tri, causal, feature_major):
    t = q[0].shape[0]
    score, mix = (_dot, _dot_nt) if feature_major else (_dot_nt, _dot)
    z = jnp.concatenate([score(qh, kh) for qh, kh in zip(q, keys)], axis=0)
    if causal is not None:
        z = jnp.where(causal, z, -jnp.inf)
    sp = jnp.maximum(z, 0.0) + jnp.log(1.0 + jnp.exp2(-jnp.abs(z))) * LOG2E
    a = jnp.exp2(z + _dot(sp.astype(BF16), neg_tri) + carry).astype(BF16)
    outs = [mix(a[h * t:(h + 1) * t], vh) for h, vh in enumerate(values)]
    return outs, carry - jnp.sum(sp, axis=1, keepdims=True)


def _sb_sample_kernel(q_ref, k_ref, v_ref, ck_hbm, cv_hbm, o_ref, kbuf, vbuf, acc_ref, carry_ref, sem,
                      *, layer, past_len):
    stream = pl.program_id(0)
    blk = SB_BLOCK
    t = q_ref.shape[1]
    heads = range(SB_HEADS)
    cols = lambda h: slice(h * SB_HEAD_DIM, (h + 1) * SB_HEAD_DIM)
    q = [q_ref[0, :, cols(h)] for h in heads]

    def fetch(j):
        tokens = pl.ds(pl.multiple_of(j * blk, blk), blk)
        return (pltpu.make_async_copy(ck_hbm.at[layer, stream, :, :, tokens], kbuf, sem.at[0]),
                pltpu.make_async_copy(cv_hbm.at[layer, stream, :, :, tokens], vbuf, sem.at[1]))

    def accumulate(outs, carry, first):
        for h in heads:
            if first:
                acc_ref[:, cols(h)] = outs[h]
            else:
                acc_ref[:, cols(h)] += outs[h]
        carry_ref[...] = carry

    def cached_block():
        outs, carry = _sb_heads_block(q, [kbuf[h].astype(BF16) for h in heads], [vbuf[h].astype(BF16) for h in heads],
                                      carry_ref[...], _neg_lower(blk), None, True)
        accumulate(outs, carry, False)

    def alive():
        return (jnp.max(carry_ref[...]) > SB_SKIP_BELOW).astype(jnp.int32)

    last = past_len // blk - 1
    copies = fetch(last)
    for c in copies:
        c.start()
    r = lax.broadcasted_iota(jnp.int32, (SB_HEADS * t, t), 0)
    causal = lax.broadcasted_iota(jnp.int32, (SB_HEADS * t, t), 1) < lax.rem(r, t)
    outs, carry = _sb_heads_block(q, [k_ref[0, :, cols(h)] for h in heads], [v_ref[0, :, cols(h)] for h in heads],
                                  jnp.zeros((SB_HEADS * t, 1), F32), _neg_lower(t), causal, False)
    accumulate(outs, carry, True)
    for c in copies:
        c.wait()
    cached_block()

    def body(state):
        j, _ = state
        copies = fetch(j)
        for c in copies:
            c.start()
        for c in copies:
            c.wait()
        cached_block()
        return j - 1, alive()

    lax.while_loop(lambda s: jnp.logical_and(s[0] >= 0, s[1] > 0), body, (last - 1, alive()))
    o_ref[0] = acc_ref[...].astype(o_ref.dtype)


def _sb_sample(qb, kb, vb, cache_k, cache_v, layer):
    b, t, _ = qb.shape
    past_len = cache_k.shape[-1]
    assert past_len % SB_BLOCK == 0
    new_spec = pl.BlockSpec((1, t, SB_WIDTH), lambda bi: (bi, 0, 0))
    hbm_spec = pl.BlockSpec(memory_space=pl.ANY)
    block = (SB_HEADS, SB_HEAD_DIM, SB_BLOCK)
    return pl.pallas_call(
        functools.partial(_sb_sample_kernel, layer=layer, past_len=past_len),
        grid=(b,),
        in_specs=[new_spec, new_spec, new_spec, hbm_spec, hbm_spec],
        out_specs=new_spec,
        out_shape=jax.ShapeDtypeStruct((b, t, SB_WIDTH), BF16),
        scratch_shapes=[pltpu.VMEM(block, F32), pltpu.VMEM(block, F32),
                        pltpu.VMEM((t, SB_WIDTH), F32), pltpu.VMEM((SB_HEADS * t, 1), F32),
                        pltpu.SemaphoreType.DMA((2,))],
        compiler_params=pltpu.CompilerParams(dimension_semantics=("arbitrary",),
                                             vmem_limit_bytes=VMEM_LIMIT_BYTES),
        name="sb_sample",
    )(qb, kb, vb, cache_k, cache_v)


def _pool_diff(ext, first_pos):
    rows = ext.shape[0] - POOL_HALO
    pos = first_pos + lax.broadcasted_iota(jnp.int32, (rows, 1), 0)
    outs = []
    for g, window in enumerate(POOL_WINDOWS):
        cols = ext[:, g * POOL_GROUP_DIM:(g + 1) * POOL_GROUP_DIM]
        acc = cols
        shift = 1
        while shift < window:
            acc = acc + pltpu.roll(acc, shift, axis=0)
            shift *= 2
        count = jnp.minimum(pos + 1, window).astype(F32)
        outs.append(acc[POOL_HALO:] / count - cols[POOL_HALO:])
    return jnp.concatenate(outs, axis=1)


def _post_kernel(x_ref, u_ref, halo_ref, o_ref, gate_ref, p_ref,
                 wgrp_ref, scale_ref, wpu_ref, wsu_ref, wout_ref,
                 gmlp_ref, wup_ref, wdown_ref, gple_ref, wpg_ref, wpp_ref, gfin_ref,
                 y_ref, *, tiles_per_stream, streams_per_tile, past_pos, final_norm):
    u = u_ref[...]
    tm = u.shape[0]
    if streams_per_tile == 1:
        step = pl.program_id(0) % tiles_per_stream
        halo = jnp.where(step == 0, 0.0, halo_ref[...])
        diff = _pool_diff(jnp.concatenate([halo, u], axis=0), past_pos + step * tm)
    else:
        t = tm // streams_per_tile
        diff = jnp.concatenate(
            [_pool_diff(jnp.concatenate([halo_ref[s], u[s * t:(s + 1) * t]], axis=0), past_pos)
             for s in range(streams_per_tile)], axis=0)
    diff = diff.astype(BF16)
    y_pool = jnp.concatenate(
        [_dot(diff[:, g * POOL_GROUP_DIM:(g + 1) * POOL_GROUP_DIM], wgrp_ref[g])
         for g in range(len(POOL_WINDOWS))], axis=1) * scale_ref[...]

    gates = gate_ref[...].astype(F32)
    merged = (gates[:, :D_MODEL] * _dot(y_pool.astype(BF16), wpu_ref[...])
              + gates[:, D_MODEL:] * _dot(o_ref[...], wsu_ref[...]))
    x = x_ref[...] + _dot(merged.astype(BF16), wout_ref[...])

    h = _dot(_rms_norm(x, gmlp_ref[...]).astype(BF16), wup_ref[...])
    h = jnp.square(jnp.maximum(h, 0.0)).astype(BF16)
    x = x + _dot(h, wdown_ref[...])

    ple_gate = jax.nn.sigmoid(_dot(_rms_norm(x, gple_ref[...]).astype(BF16), wpg_ref[...]))
    x = x + ple_gate * _dot(p_ref[...].astype(BF16), wpp_ref[...])
    if final_norm:
        x = _rms_norm(x, gfin_ref[...])
    y_ref[...] = x


def _post(x, u, halo, o, gates, p, weights, *, tm, tiles_per_stream, streams_per_tile, past_pos, final_norm):
    n = x.shape[0]
    row = lambda width: pl.BlockSpec((tm, width), lambda i: (i, 0))
    if streams_per_tile == 1:
        per_tile = tm // POOL_HALO
        halo_spec = pl.BlockSpec((POOL_HALO, POOL_WIDTH), lambda i: (jnp.maximum(i * per_tile - 1, 0), 0))
    else:
        halo_spec = pl.BlockSpec((streams_per_tile, POOL_HALO, POOL_WIDTH), lambda i: (i, 0, 0))
    w_specs = [_resident(w.shape) for w in weights]
    return pl.pallas_call(
        functools.partial(_post_kernel, tiles_per_stream=tiles_per_stream, streams_per_tile=streams_per_tile,
                          past_pos=past_pos, final_norm=final_norm),
        grid=(n // tm,),
        in_specs=[row(D_MODEL), row(POOL_WIDTH), halo_spec, row(SB_WIDTH), row(2 * D_MODEL), row(PLE_DIM)] + w_specs,
        out_specs=row(D_MODEL),
        out_shape=jax.ShapeDtypeStruct((n, D_MODEL), F32),
        compiler_params=pltpu.CompilerParams(dimension_semantics=("parallel",),
                                             vmem_limit_bytes=VMEM_LIMIT_BYTES),
        name="post",
    )(x, u, halo, o, gates, p, *weights)


def kernel(x_prompt, x_sample, cache_k, cache_v, state_pool, p_prompt, p_sample, g_mix, w_in, b_gate, w_pool_grp, pool_scale, w_pool_up, w_sb_up, w_out, g_mlp, w_up, w_down, g_ple, w_ple_gate, w_ple_proj, g_final):
    depth = w_in.shape[0]
    bp, tp, _ = x_prompt.shape
    bs, ts, _ = x_sample.shape
    past_len = cache_k.shape[2]
    assert tp % ROW_TILE == 0 and tp % SB_BLOCK == 0 and ts >= POOL_STATE and ts % 8 == 0
    n_s = bs * ts
    xp = x_prompt.reshape(bp * tp, D_MODEL)
    xs = x_sample.reshape(n_s, D_MODEL)
    row_vec = lambda a: a.reshape(1, -1)
    cache_rows = lambda c: jnp.transpose(c, (0, 1, 3, 4, 2))

    outs = {name: [] for name in ("kp", "vp", "pp", "ks", "vs", "ps")}
    for d in range(depth):
        w_in_bf = w_in[d].astype(BF16)
        weights = (w_pool_grp[d].astype(BF16), row_vec(pool_scale[d]), w_pool_up[d].astype(BF16),
                   w_sb_up[d].astype(BF16), w_out[d].astype(BF16), row_vec(g_mlp[d]), w_up[d].astype(BF16),
                   w_down[d].astype(BF16), row_vec(g_ple[d]), w_ple_gate[d].astype(BF16),
                   w_ple_proj[d].astype(BF16), row_vec(g_final))
        final_norm = d == depth - 1

        u, k, v, qb, kb, vb, gates = _inproj(xp, g_mix[d], w_in_bf, b_gate[d], ROW_TILE)
        o = _sb_prompt(qb.reshape(bp, tp, SB_WIDTH), kb.reshape(bp, tp, SB_WIDTH), vb.reshape(bp, tp, SB_WIDTH))
        xp = _post(xp, u, u, o.reshape(bp * tp, SB_WIDTH), gates, p_prompt[d].reshape(bp * tp, PLE_DIM), weights,
                   tm=ROW_TILE, tiles_per_stream=tp // ROW_TILE, streams_per_tile=1, past_pos=0,
                   final_norm=final_norm)
        outs["kp"].append(k.reshape(bp, tp, SB_HEADS, SB_HEAD_DIM))
        outs["vp"].append(v.reshape(bp, tp, SB_HEADS, SB_HEAD_DIM))
        outs["pp"].append(u.reshape(bp, tp, POOL_WIDTH)[:, tp - POOL_STATE:])

        u, k, v, qb, kb, vb, gates = _inproj(xs, g_mix[d], w_in_bf, b_gate[d], n_s)
        o = _sb_sample(qb.reshape(bs, ts, SB_WIDTH), kb.reshape(bs, ts, SB_WIDTH), vb.reshape(bs, ts, SB_WIDTH),
                       cache_rows(cache_k), cache_rows(cache_v), d)
        halo = jnp.pad(state_pool[d], ((0, 0), (POOL_HALO - POOL_STATE, 0), (0, 0)))
        xs = _post(xs, u, halo, o.reshape(n_s, SB_WIDTH), gates, p_sample[d].reshape(n_s, PLE_DIM), weights,
                   tm=n_s, tiles_per_stream=1, streams_per_tile=bs, past_pos=POOL_STATE,
                   final_norm=final_norm)
        outs["ks"].append(k.reshape(bs, ts, SB_HEADS, SB_HEAD_DIM))
        outs["vs"].append(v.reshape(bs, ts, SB_HEADS, SB_HEAD_DIM))
        outs["ps"].append(u.reshape(bs, ts, POOL_WIDTH)[:, ts - POOL_STATE:])

    stack = lambda name: jnp.stack(outs[name])
    return (xp.reshape(bp, tp, D_MODEL), xs.reshape(bs, ts, D_MODEL),
            stack("kp"), stack("vp"), stack("pp"), stack("ks"), stack("vs"), stack("ps"))
```

```python
import functools

import jax
import jax.numpy as jnp
from jax import lax
from jax.experimental import pallas as pl
from jax.experimental.pallas import tpu as pltpu

D_MODEL = 1024
POOL_WIDTH = 512
POOL_WINDOWS = (2, 4, 8, 16)
POOL_GROUP_DIM = POOL_WIDTH // len(POOL_WINDOWS)
POOL_STATE = max(POOL_WINDOWS) - 1
POOL_HALO = 16
SB_HEADS = 8
SB_HEAD_DIM = 64
SB_WIDTH = SB_HEADS * SB_HEAD_DIM
HEAD_PAIR = 2 * SB_HEAD_DIM
D_FF = 4 * D_MODEL
PLE_DIM = 256
EPS = 1e-6

V7X_VMEM_BYTES = 64 * 1024 * 1024
VMEM_LIMIT_BYTES = V7X_VMEM_BYTES - 8 * 1024 * 1024

ROW_TILE = 256
POST_TILE = 512
POST_CHAINS = 2
POST_LAG = 1
SB_BLOCK = 256
SB_PAIRS_PER_STEP = 4

LOG2E = 1.4426950408889634
SB_SKIP_BELOW = -160.0

BF16 = jnp.bfloat16
F32 = jnp.float32


def _rms_norm(x, g):
    y = x * lax.rsqrt(jnp.mean(x * x, axis=-1, keepdims=True) + EPS)
    return y * g


def _dot(a, b):
    return jnp.dot(a, b, preferred_element_type=F32)


def _dot_nt(a, b):
    return lax.dot_general(a, b, (((1,), (1,)), ((), ())), preferred_element_type=F32)


def _resident(shape):
    return pl.BlockSpec(shape, lambda *_: (0,) * len(shape), pipeline_mode=pl.Buffered(1))


def _store_head_rows(ref, x):
    rows = x.shape[0]
    for h in range(SB_HEADS):
        ref[pl.ds(h, rows, stride=SB_HEADS), :] = x[:, h * SB_HEAD_DIM:(h + 1) * SB_HEAD_DIM]


def _inproj_kernel(x_ref, g_ref, w_ref, b_ref,
                   u_ref, k_ref, v_ref, qb_ref, kb_ref, vb_ref, gate_ref):
    xn = _rms_norm(x_ref[...], g_ref[...]).astype(BF16)
    c0, c1, c2, c3 = POOL_WIDTH, POOL_WIDTH + SB_WIDTH, POOL_WIDTH + 2 * SB_WIDTH, POOL_WIDTH + 3 * SB_WIDTH
    u_ref[...] = _dot(xn, w_ref[:, :c0])
    qb_ref[...] = (_dot(xn, w_ref[:, c0:c1]) * (SB_HEAD_DIM ** -0.5 * LOG2E)).astype(BF16)
    k = _dot(xn, w_ref[:, c1:c2])
    _store_head_rows(k_ref, k)
    kb_ref[...] = k.astype(BF16)
    v = _dot(xn, w_ref[:, c2:c3])
    _store_head_rows(v_ref, v)
    vb_ref[...] = v.astype(BF16)
    gate_ref[...] = jax.nn.sigmoid(_dot(xn, w_ref[:, c3:]) + b_ref[...]).astype(BF16)


def _inproj(x, g_mix, w_in_bf, b_gate, tm):
    n = x.shape[0]
    in_width = w_in_bf.shape[1]
    row = lambda width: pl.BlockSpec((tm, width), lambda i: (i, 0))
    f32_out = jax.ShapeDtypeStruct((n * SB_HEADS, SB_HEAD_DIM), F32)
    head_rows = pl.BlockSpec((tm * SB_HEADS, SB_HEAD_DIM), lambda i: (i, 0))
    bf_out = jax.ShapeDtypeStruct((n, SB_WIDTH), BF16)
    return pl.pallas_call(
        _inproj_kernel,
        grid=(n // tm,),
        in_specs=[row(D_MODEL), _resident((1, D_MODEL)), _resident((D_MODEL, in_width)),
                  _resident((1, 2 * D_MODEL))],
        out_specs=[row(POOL_WIDTH), head_rows, head_rows, row(SB_WIDTH), row(SB_WIDTH),
                   row(SB_WIDTH), row(2 * D_MODEL)],
        out_shape=[jax.ShapeDtypeStruct((n, POOL_WIDTH), F32), f32_out, f32_out, bf_out, bf_out, bf_out,
                   jax.ShapeDtypeStruct((n, 2 * D_MODEL), BF16)],
        compiler_params=pltpu.CompilerParams(dimension_semantics=("parallel",),
                                             vmem_limit_bytes=VMEM_LIMIT_BYTES),
        name="inproj",
    )(x, g_mix.reshape(1, D_MODEL), w_in_bf, b_gate.reshape(1, 2 * D_MODEL))


def _neg_lower(n):
    r = lax.broadcasted_iota(jnp.int32, (n, n), 0)
    c = lax.broadcasted_iota(jnp.int32, (n, n), 1)
    return jnp.where(r >= c, -1.0, 0.0).astype(BF16)


def _stack_heads(x):
    lane = lax.broadcasted_iota(jnp.int32, x.shape, 1)
    zero = jnp.zeros_like(x)
    return jnp.concatenate([jnp.where(lane < SB_HEAD_DIM, x, zero), jnp.where(lane >= SB_HEAD_DIM, x, zero)], axis=0)


def _causal_mask(tq, tk):
    r = lax.broadcasted_iota(jnp.int32, (2 * tq, tk), 0)
    c = lax.broadcasted_iota(jnp.int32, (2 * tq, tk), 1)
    return c < jnp.where(r >= tq, r - tq, r)


def _sb_pair_blocks(q2, blocks, carry):
    tq = q2.shape[0] // 2
    logits, softplus = [], []
    for k_blk, _, _, causal in blocks:
        z = _dot_nt(q2, k_blk)
        if causal is not None:
            z = jnp.where(causal, z, -jnp.inf)
        sp = jnp.maximum(z, 0.0) + jnp.log(1.0 + jnp.exp2(-jnp.abs(z))) * LOG2E
        logits.append(z)
        softplus.append(sp)
    weights, values = [], []
    for (_, v2, neg_tri, causal), z, sp in zip(blocks, logits, softplus):
        a = jnp.exp2(z + _dot(sp.astype(BF16), neg_tri) + carry).astype(BF16)
        weights += [a[:tq], a[tq:]]
        values.append(v2)
        carry = carry - jnp.sum(sp, axis=1, keepdims=True)
    return _dot(jnp.concatenate(weights, axis=1), jnp.concatenate(values, axis=0)), carry


def _sb_prompt_kernel(q_ref, k_ref, v_ref, o_ref, acc_ref, carry_ref):
    qi = pl.program_id(2)
    blk = SB_BLOCK
    pairs = range(SB_PAIRS_PER_STEP)
    lanes = lambda p: slice(p * HEAD_PAIR, (p + 1) * HEAD_PAIR)
    q2 = [_stack_heads(q_ref[0, :, lanes(p)]) for p in pairs]
    neg_tri = _neg_lower(blk)

    def key_block(p, j, causal):
        r0 = pl.multiple_of(j * blk, blk)
        return (k_ref[0, pl.ds(r0, blk), lanes(p)], _stack_heads(v_ref[0, pl.ds(r0, blk), lanes(p)]),
                neg_tri, causal)

    def sweep(block_ids, first):
        for p in pairs:
            carry = jnp.zeros((2 * blk, 1), F32) if first else carry_ref[p]
            out, carry = _sb_pair_blocks(q2[p], [key_block(p, j, causal) for j, causal in block_ids], carry)
            if first:
                acc_ref[:, lanes(p)] = out
            else:
                acc_ref[:, lanes(p)] += out
            carry_ref[p] = carry

    def alive():
        return (jnp.max(carry_ref[...]) > SB_SKIP_BELOW).astype(jnp.int32)

    diagonal = (qi, _causal_mask(blk, blk))
    pl.when(qi == 0)(lambda: sweep([diagonal], True))
    pl.when(qi > 0)(lambda: sweep([diagonal, (qi - 1, None)], True))

    def body(state):
        j, _ = state
        sweep([(j, None)], False)
        return j - 1, alive()

    lax.while_loop(lambda s: jnp.logical_and(s[0] >= 0, s[1] > 0), body, (qi - 2, alive()))
    o_ref[0] = acc_ref[...].astype(o_ref.dtype)


def _sb_prompt(qb, kb, vb):
    b, t, _ = qb.shape
    blk = SB_BLOCK
    width = SB_PAIRS_PER_STEP * HEAD_PAIR
    kv_spec = pl.BlockSpec((1, t, width), lambda bi, hp, qi: (bi, 0, hp))
    q_spec = pl.BlockSpec((1, blk, width), lambda bi, hp, qi: (bi, qi, hp))
    return pl.pallas_call(
        _sb_prompt_kernel,
        grid=(b, SB_WIDTH // width, t // blk),
        in_specs=[q_spec, kv_spec, kv_spec],
        out_specs=q_spec,
        out_shape=jax.ShapeDtypeStruct((b, t, SB_WIDTH), BF16),
        scratch_shapes=[pltpu.VMEM((blk, width), F32), pltpu.VMEM((SB_PAIRS_PER_STEP, 2 * blk, 1), F32)],
        compiler_params=pltpu.CompilerParams(dimension_semantics=("parallel", "parallel", "arbitrary"),
                                             vmem_limit_bytes=VMEM_LIMIT_BYTES),
        name="sb_prompt",
    )(qb, kb, vb)


def _sb_heads_block(q, keys, values, carry, neg_tri, causal, feature_major):
    t = q[0].shape[0]
    score, mix = (_dot, _dot_nt) if feature_major else (_dot_nt, _dot)
    z = jnp.concatenate([score(qh, kh) for qh, kh in zip(q, keys)], axis=0)
    if causal is not None:
        z = jnp.where(causal, z, -jnp.inf)
    sp = jnp.maximum(z, 0.0) + jnp.log(1.0 + jnp.exp2(-jnp.abs(z))) * LOG2E
    a = jnp.exp2(z + _dot(sp.astype(BF16), neg_tri) + carry).astype(BF16)
    outs = [mix(a[h * t:(h + 1) * t], vh) for h, vh in enumerate(values)]
    return outs, carry - jnp.sum(sp, axis=1, keepdims=True)


def _sb_sample_kernel(q_ref, k_ref, v_ref, ck_hbm, cv_hbm, o_ref, kbuf, vbuf, acc_ref, carry_ref, sem,
                      *, layer, past_len):
    stream = pl.program_id(0)
    blk = SB_BLOCK
    t = q_ref.shape[1]
    heads = range(SB_HEADS)
    cols = lambda h: slice(h * SB_HEAD_DIM, (h + 1) * SB_HEAD_DIM)
    q = [q_ref[0, :, cols(h)] for h in heads]

    def fetch(j):
        tokens = pl.ds(pl.multiple_of(j * blk, blk), blk)
        return (pltpu.make_async_copy(ck_hbm.at[layer, stream, :, :, tokens], kbuf, sem.at[0]),
                pltpu.make_async_copy(cv_hbm.at[layer, stream, :, :, tokens], vbuf, sem.at[1]))

    def accumulate(outs, carry, first):
        for h in heads:
            if first:
                acc_ref[:, cols(h)] = outs[h]
            else:
                acc_ref[:, cols(h)] += outs[h]
        carry_ref[...] = carry

    def cached_block():
        outs, carry = _sb_heads_block(q, [kbuf[h].astype(BF16) for h in heads], [vbuf[h].astype(BF16) for h in heads],
                                      carry_ref[...], _neg_lower(blk), None, True)
        accumulate(outs, carry, False)

    def alive():
        return (jnp.max(carry_ref[...]) > SB_SKIP_BELOW).astype(jnp.int32)

    last = past_len // blk - 1
    copies = fetch(last)
    for c in copies:
        c.start()
    r = lax.broadcasted_iota(jnp.int32, (SB_HEADS * t, t), 0)
    causal = lax.broadcasted_iota(jnp.int32, (SB_HEADS * t, t), 1) < lax.rem(r, t)
    outs, carry = _sb_heads_block(q, [k_ref[0, :, cols(h)] for h in heads], [v_ref[0, :, cols(h)] for h in heads],
                                  jnp.zeros((SB_HEADS * t, 1), F32), _neg_lower(t), causal, False)
    accumulate(outs, carry, True)
    for c in copies:
        c.wait()
    cached_block()

    def body(state):
        j, _ = state
        copies = fetch(j)
        for c in copies:
            c.start()
        for c in copies:
            c.wait()
        cached_block()
        return j - 1, alive()

    lax.while_loop(lambda s: jnp.logical_and(s[0] >= 0, s[1] > 0), body, (last - 1, alive()))
    o_ref[0] = acc_ref[...].astype(o_ref.dtype)


def _sb_sample(qb, kb, vb, cache_k, cache_v, layer):
    b, t, _ = qb.shape
    past_len = cache_k.shape[-1]
    assert past_len % SB_BLOCK == 0
    new_spec = pl.BlockSpec((1, t, SB_WIDTH), lambda bi: (bi, 0, 0))
    hbm_spec = pl.BlockSpec(memory_space=pl.ANY)
    block = (SB_HEADS, SB_HEAD_DIM, SB_BLOCK)
    return pl.pallas_call(
        functools.partial(_sb_sample_kernel, layer=layer, past_len=past_len),
        grid=(b,),
        in_specs=[new_spec, new_spec, new_spec, hbm_spec, hbm_spec],
        out_specs=new_spec,
        out_shape=jax.ShapeDtypeStruct((b, t, SB_WIDTH), BF16),
        scratch_shapes=[pltpu.VMEM(block, F32), pltpu.VMEM(block, F32),
                        pltpu.VMEM((t, SB_WIDTH), F32), pltpu.VMEM((SB_HEADS * t, 1), F32),
                        pltpu.SemaphoreType.DMA((2,))],
        compiler_params=pltpu.CompilerParams(dimension_semantics=("arbitrary",),
                                             vmem_limit_bytes=VMEM_LIMIT_BYTES),
        name="sb_sample",
    )(qb, kb, vb, cache_k, cache_v)


def _pool_diff(ext, first_pos):
    rows = ext.shape[0] - POOL_HALO
    pos = first_pos + lax.broadcasted_iota(jnp.int32, (rows, 1), 0)
    outs = []
    for g, window in enumerate(POOL_WINDOWS):
        cols = ext[:, g * POOL_GROUP_DIM:(g + 1) * POOL_GROUP_DIM]
        acc = cols
        shift = 1
        while shift < window:
            acc = acc + pltpu.roll(acc, shift, axis=0)
            shift *= 2
        inv_count = 1.0 / jnp.minimum(pos + 1, window).astype(F32)
        outs.append(acc[POOL_HALO:] * inv_count - cols[POOL_HALO:])
    return jnp.concatenate(outs, axis=1)


def _post_kernel(x_ref, u_ref, halo_ref, o_ref, gate_ref, p_ref,
                 wgrp_ref, scale_ref, wpu_ref, wsu_ref, wout_ref,
                 gmlp_ref, wup_ref, wdown_ref, gple_ref, wpg_ref, wpp_ref, gfin_ref,
                 y_ref, *, tiles_per_stream, streams_per_tile, past_pos, final_norm):
    tm = u_ref.shape[0]
    chunk = tm // POST_CHAINS

    def pooled(c):
        u = u_ref[c * chunk:(c + 1) * chunk, :]
        if streams_per_tile == 1:
            step = pl.program_id(0) % tiles_per_stream
            if c == 0:
                halo = jnp.where(step == 0, 0.0, halo_ref[...])
            else:
                halo = u_ref[c * chunk - POOL_HALO:c * chunk, :]
            diff = _pool_diff(jnp.concatenate([halo, u], axis=0), past_pos + step * tm + c * chunk)
        else:
            t = tm // streams_per_tile
            per_chunk = streams_per_tile // POST_CHAINS
            diff = jnp.concatenate(
                [_pool_diff(jnp.concatenate([halo_ref[c * per_chunk + s], u[s * t:(s + 1) * t]], axis=0), past_pos)
                 for s in range(per_chunk)], axis=0)
        return diff.astype(BF16)

    def token_chain(c):
        rows = slice(c * chunk, (c + 1) * chunk)
        diff = pooled(c)
        y_pool = jnp.concatenate(
            [_dot(diff[:, g * POOL_GROUP_DIM:(g + 1) * POOL_GROUP_DIM], wgrp_ref[g])
             for g in range(len(POOL_WINDOWS))], axis=1) * scale_ref[...]
        gates = gate_ref[rows, :].astype(F32)
        merged = (gates[:, :D_MODEL] * _dot(y_pool.astype(BF16), wpu_ref[...])
                  + gates[:, D_MODEL:] * _dot(o_ref[rows, :], wsu_ref[...]))
        yield
        x = x_ref[rows, :] + _dot(merged.astype(BF16), wout_ref[...])
        xn = _rms_norm(x, gmlp_ref[...]).astype(BF16)
        yield
        h = jnp.square(jnp.maximum(_dot(xn, wup_ref[...]), 0.0)).astype(BF16)
        yield
        x = x + _dot(h, wdown_ref[...])
        xn = _rms_norm(x, gple_ref[...]).astype(BF16)
        yield
        ple_gate = jax.nn.sigmoid(_dot(xn, wpg_ref[...]))
        x = x + ple_gate * _dot(p_ref[rows, :].astype(BF16), wpp_ref[...])
        if final_norm:
            x = _rms_norm(x, gfin_ref[...])
        y_ref[rows, :] = x

    waiting = [token_chain(c) for c in range(POST_CHAINS)]
    live, tick = [], 0
    while waiting or live:
        if waiting and tick % POST_LAG == 0:
            live.append(waiting.pop(0))
        live = [c for c in live if next(c, True) is None]
        tick += 1


def _post(x, u, halo, o, gates, p, weights, *, tm, tiles_per_stream, streams_per_tile, past_pos, final_norm):
    n = x.shape[0]
    row = lambda width: pl.BlockSpec((tm, width), lambda i: (i, 0))
    if streams_per_tile == 1:
        per_tile = tm // POOL_HALO
        halo_spec = pl.BlockSpec((POOL_HALO, POOL_WIDTH), lambda i: (jnp.maximum(i * per_tile - 1, 0), 0))
    else:
        halo_spec = pl.BlockSpec((streams_per_tile, POOL_HALO, POOL_WIDTH), lambda i: (i, 0, 0))
    w_specs = [_resident(w.shape) for w in weights]
    return pl.pallas_call(
        functools.partial(_post_kernel, tiles_per_stream=tiles_per_stream, streams_per_tile=streams_per_tile,
                          past_pos=past_pos, final_norm=final_norm),
        grid=(n // tm,),
        in_specs=[row(D_MODEL), row(POOL_WIDTH), halo_spec, row(SB_WIDTH), row(2 * D_MODEL), row(PLE_DIM)] + w_specs,
        out_specs=row(D_MODEL),
        out_shape=jax.ShapeDtypeStruct((n, D_MODEL), F32),
        compiler_params=pltpu.CompilerParams(dimension_semantics=("parallel",),
                                             vmem_limit_bytes=VMEM_LIMIT_BYTES),
        name="post",
    )(x, u, halo, o, gates, p, *weights)


def kernel(x_prompt, x_sample, cache_k, cache_v, state_pool, p_prompt, p_sample, g_mix, w_in, b_gate, w_pool_grp, pool_scale, w_pool_up, w_sb_up, w_out, g_mlp, w_up, w_down, g_ple, w_ple_gate, w_ple_proj, g_final):
    depth = w_in.shape[0]
    bp, tp, _ = x_prompt.shape
    bs, ts, _ = x_sample.shape
    past_len = cache_k.shape[2]
    assert tp % ROW_TILE == 0 and tp % POST_TILE == 0 and tp % SB_BLOCK == 0
    assert ts >= POOL_STATE and ts % 8 == 0 and bs % POST_CHAINS == 0
    n_s = bs * ts
    xp = x_prompt.reshape(bp * tp, D_MODEL)
    xs = x_sample.reshape(n_s, D_MODEL)
    row_vec = lambda a: a.reshape(1, -1)
    cache_rows = lambda c: jnp.transpose(c, (0, 1, 3, 4, 2))

    outs = {name: [] for name in ("kp", "vp", "pp", "ks", "vs", "ps")}
    for d in range(depth):
        w_in_bf = w_in[d].astype(BF16)
        weights = (w_pool_grp[d].astype(BF16), row_vec(pool_scale[d]), w_pool_up[d].astype(BF16),
                   w_sb_up[d].astype(BF16), w_out[d].astype(BF16), row_vec(g_mlp[d]), w_up[d].astype(BF16),
                   w_down[d].astype(BF16), row_vec(g_ple[d]), w_ple_gate[d].astype(BF16),
                   w_ple_proj[d].astype(BF16), row_vec(g_final))
        final_norm = d == depth - 1

        u, k, v, qb, kb, vb, gates = _inproj(xp, g_mix[d], w_in_bf, b_gate[d], ROW_TILE)
        o = _sb_prompt(qb.reshape(bp, tp, SB_WIDTH), kb.reshape(bp, tp, SB_WIDTH), vb.reshape(bp, tp, SB_WIDTH))
        xp = _post(xp, u, u, o.reshape(bp * tp, SB_WIDTH), gates, p_prompt[d].reshape(bp * tp, PLE_DIM), weights,
                   tm=POST_TILE, tiles_per_stream=tp // POST_TILE, streams_per_tile=1, past_pos=0,
                   final_norm=final_norm)
        outs["kp"].append(k.reshape(bp, tp, SB_HEADS, SB_HEAD_DIM))
        outs["vp"].append(v.reshape(bp, tp, SB_HEADS, SB_HEAD_DIM))
        outs["pp"].append(u.reshape(bp, tp, POOL_WIDTH)[:, tp - POOL_STATE:])

        u, k, v, qb, kb, vb, gates = _inproj(xs, g_mix[d], w_in_bf, b_gate[d], n_s)
        o = _sb_sample(qb.reshape(bs, ts, SB_WIDTH), kb.reshape(bs, ts, SB_WIDTH), vb.reshape(bs, ts, SB_WIDTH),
                       cache_rows(cache_k), cache_rows(cache_v), d)
        halo = jnp.pad(state_pool[d], ((0, 0), (POOL_HALO - POOL_STATE, 0), (0, 0)))
        xs = _post(xs, u, halo, o.reshape(n_s, SB_WIDTH), gates, p_sample[d].reshape(n_s, PLE_DIM), weights,
                   tm=n_s, tiles_per_stream=1, streams_per_tile=bs, past_pos=POOL_STATE,
                   final_norm=final_norm)
        outs["ks"].append(k.reshape(bs, ts, SB_HEADS, SB_HEAD_DIM))
        outs["vs"].append(v.reshape(bs, ts, SB_HEADS, SB_HEAD_DIM))
        outs["ps"].append(u.reshape(bs, ts, POOL_WIDTH)[:, ts - POOL_STATE:])

    stack = lambda name: jnp.stack(outs[name])
    return (xp.reshape(bp, tp, D_MODEL), xs.reshape(bs, ts, D_MODEL),
            stack("kp"), stack("vp"), stack("pp"), stack("ks"), stack("vs"), stack("ps"))
```

```python
import functools

import jax
import jax.numpy as jnp
from jax import lax
from jax.experimental import pallas as pl
from jax.experimental.pallas import tpu as pltpu

D_MODEL = 1024
POOL_WIDTH = 512
POOL_WINDOWS = (2, 4, 8, 16)
POOL_GROUP_DIM = POOL_WIDTH // len(POOL_WINDOWS)
POOL_STATE = max(POOL_WINDOWS) - 1
POOL_HALO = 16
SB_HEADS = 8
SB_HEAD_DIM = 64
SB_WIDTH = SB_HEADS * SB_HEAD_DIM
HEAD_PAIR = 2 * SB_HEAD_DIM
D_FF = 4 * D_MODEL
PLE_DIM = 256
EPS = 1e-6

V7X_VMEM_BYTES = 64 * 1024 * 1024
VMEM_LIMIT_BYTES = V7X_VMEM_BYTES - 8 * 1024 * 1024

ROW_TILE = 512
INPROJ_CHAINS = 2
POST_CHAINS = 2
POST_LAG = 1
SB_BLOCK = 256
SB_PAIRS_PER_STEP = 4
SB_PAIR_LAG = 1

LOG2E = 1.4426950408889634
SB_SKIP_BELOW = -160.0

BF16 = jnp.bfloat16
F32 = jnp.float32


def _rms_norm(x, g):
    y = x * lax.rsqrt(jnp.mean(x * x, axis=-1, keepdims=True) + EPS)
    return y * g


def _dot(a, b):
    return jnp.dot(a, b, preferred_element_type=F32)


def _dot_nt(a, b):
    return lax.dot_general(a, b, (((1,), (1,)), ((), ())), preferred_element_type=F32)


def _resident(shape):
    return pl.BlockSpec(shape, lambda *_: (0,) * len(shape), pipeline_mode=pl.Buffered(1))


def _interleave(chains, lag):
    waiting, live, tick = list(chains), [], 0
    while waiting or live:
        if waiting and tick % lag == 0:
            live.append(waiting.pop(0))
        live = [c for c in live if next(c, True) is None]
        tick += 1


def _store_head_rows(ref, row0, x):
    rows = x.shape[0]
    for h in range(SB_HEADS):
        ref[pl.ds(row0 * SB_HEADS + h, rows, stride=SB_HEADS), :] = x[:, h * SB_HEAD_DIM:(h + 1) * SB_HEAD_DIM]


def _inproj_kernel(x_ref, g_ref, w_ref, b_ref,
                   u_ref, k_ref, v_ref, qb_ref, kb_ref, vb_ref, gate_ref):
    chunk = x_ref.shape[0] // INPROJ_CHAINS
    c0, c1, c2, c3 = POOL_WIDTH, POOL_WIDTH + SB_WIDTH, POOL_WIDTH + 2 * SB_WIDTH, POOL_WIDTH + 3 * SB_WIDTH

    def token_chain(c):
        rows = slice(c * chunk, (c + 1) * chunk)
        xn = _rms_norm(x_ref[rows, :], g_ref[...]).astype(BF16)
        yield
        u_ref[rows, :] = _dot(xn, w_ref[:, :c0])
        qb_ref[rows, :] = (_dot(xn, w_ref[:, c0:c1]) * (SB_HEAD_DIM ** -0.5 * LOG2E)).astype(BF16)
        yield
        k = _dot(xn, w_ref[:, c1:c2])
        _store_head_rows(k_ref, c * chunk, k)
        kb_ref[rows, :] = k.astype(BF16)
        yield
        v = _dot(xn, w_ref[:, c2:c3])
        _store_head_rows(v_ref, c * chunk, v)
        vb_ref[rows, :] = v.astype(BF16)
        yield
        gate_ref[rows, :] = jax.nn.sigmoid(_dot(xn, w_ref[:, c3:]) + b_ref[...]).astype(BF16)

    _interleave([token_chain(c) for c in range(INPROJ_CHAINS)], 1)


def _inproj(x, g_mix, w_in_bf, b_gate, tm):
    n = x.shape[0]
    in_width = w_in_bf.shape[1]
    row = lambda width: pl.BlockSpec((tm, width), lambda i: (i, 0))
    f32_out = jax.ShapeDtypeStruct((n * SB_HEADS, SB_HEAD_DIM), F32)
    head_rows = pl.BlockSpec((tm * SB_HEADS, SB_HEAD_DIM), lambda i: (i, 0))
    bf_out = jax.ShapeDtypeStruct((n, SB_WIDTH), BF16)
    return pl.pallas_call(
        _inproj_kernel,
        grid=(n // tm,),
        in_specs=[row(D_MODEL), _resident((1, D_MODEL)), _resident((D_MODEL, in_width)),
                  _resident((1, 2 * D_MODEL))],
        out_specs=[row(POOL_WIDTH), head_rows, head_rows, row(SB_WIDTH), row(SB_WIDTH),
                   row(SB_WIDTH), row(2 * D_MODEL)],
        out_shape=[jax.ShapeDtypeStruct((n, POOL_WIDTH), F32), f32_out, f32_out, bf_out, bf_out, bf_out,
                   jax.ShapeDtypeStruct((n, 2 * D_MODEL), BF16)],
        compiler_params=pltpu.CompilerParams(dimension_semantics=("parallel",),
                                             vmem_limit_bytes=VMEM_LIMIT_BYTES),
        name="inproj",
    )(x, g_mix.reshape(1, D_MODEL), w_in_bf, b_gate.reshape(1, 2 * D_MODEL))


def _neg_lower(n):
    r = lax.broadcasted_iota(jnp.int32, (n, n), 0)
    c = lax.broadcasted_iota(jnp.int32, (n, n), 1)
    return jnp.where(r >= c, -1.0, 0.0).astype(BF16)


def _stack_heads(x):
    lane = lax.broadcasted_iota(jnp.int32, x.shape, 1)
    zero = jnp.zeros_like(x)
    return jnp.concatenate([jnp.where(lane < SB_HEAD_DIM, x, zero), jnp.where(lane >= SB_HEAD_DIM, x, zero)], axis=0)


def _causal_mask(tq, tk):
    r = lax.broadcasted_iota(jnp.int32, (2 * tq, tk), 0)
    c = lax.broadcasted_iota(jnp.int32, (2 * tq, tk), 1)
    return c < jnp.where(r >= tq, r - tq, r)


def _sb_pair_stages(q2, blocks, carry, emit):
    tq = q2.shape[0] // 2
    logits = []
    for k_blk, _, _, causal in blocks:
        z = _dot_nt(q2, k_blk)
        if causal is not None:
            z = jnp.where(causal, z, -jnp.inf)
        logits.append(z)
    yield
    softplus = [jnp.maximum(z, 0.0) + jnp.log(1.0 + jnp.exp2(-jnp.abs(z))) * LOG2E for z in logits]
    yield
    sums = [_dot(sp.astype(BF16), neg_tri) for (_, _, neg_tri, _), sp in zip(blocks, softplus)]
    yield
    weights = []
    for z, incl in zip(logits, sums):
        a = jnp.exp2(z + incl + carry).astype(BF16)
        weights += [a[:tq], a[tq:]]
        carry = carry + incl[:, :1]
    yield
    values = jnp.concatenate([v2 for _, v2, _, _ in blocks], axis=0)
    emit(_dot(jnp.concatenate(weights, axis=1), values), carry)


def _sb_prompt_kernel(q_ref, k_ref, v_ref, o_ref, acc_ref, carry_ref):
    qi = pl.program_id(2)
    blk = SB_BLOCK
    pairs = range(SB_PAIRS_PER_STEP)
    lanes = lambda p: slice(p * HEAD_PAIR, (p + 1) * HEAD_PAIR)
    q2 = [_stack_heads(q_ref[0, :, lanes(p)]) for p in pairs]
    neg_tri = _neg_lower(blk)

    def key_block(p, j, causal):
        r0 = pl.multiple_of(j * blk, blk)
        return (k_ref[0, pl.ds(r0, blk), lanes(p)], _stack_heads(v_ref[0, pl.ds(r0, blk), lanes(p)]),
                neg_tri, causal)

    def sweep(block_ids, first):
        def emit(p, out, carry):
            if first:
                acc_ref[:, lanes(p)] = out
            else:
                acc_ref[:, lanes(p)] += out
            carry_ref[p] = carry

        _interleave([_sb_pair_stages(q2[p], [key_block(p, j, causal) for j, causal in block_ids],
                                     jnp.zeros((2 * blk, 1), F32) if first else carry_ref[p],
                                     functools.partial(emit, p)) for p in pairs], SB_PAIR_LAG)

    def alive():
        return (jnp.max(carry_ref[...]) > SB_SKIP_BELOW).astype(jnp.int32)

    diagonal = (qi, _causal_mask(blk, blk))
    pl.when(qi == 0)(lambda: sweep([diagonal], True))
    pl.when(qi > 0)(lambda: sweep([diagonal, (qi - 1, None)], True))

    def body(state):
        j, _ = state
        sweep([(j, None)], False)
        return j - 1, alive()

    lax.while_loop(lambda s: jnp.logical_and(s[0] >= 0, s[1] > 0), body, (qi - 2, alive()))
    o_ref[0] = acc_ref[...].astype(o_ref.dtype)


def _sb_prompt(qb, kb, vb):
    b, t, _ = qb.shape
    blk = SB_BLOCK
    width = SB_PAIRS_PER_STEP * HEAD_PAIR
    kv_spec = pl.BlockSpec((1, t, width), lambda bi, hp, qi: (bi, 0, hp))
    q_spec = pl.BlockSpec((1, blk, width), lambda bi, hp, qi: (bi, qi, hp))
    return pl.pallas_call(
        _sb_prompt_kernel,
        grid=(b, SB_WIDTH // width, t // blk),
        in_specs=[q_spec, kv_spec, kv_spec],
        out_specs=q_spec,
        out_shape=jax.ShapeDtypeStruct((b, t, SB_WIDTH), BF16),
        scratch_shapes=[pltpu.VMEM((blk, width), F32), pltpu.VMEM((SB_PAIRS_PER_STEP, 2 * blk, 1), F32)],
        compiler_params=pltpu.CompilerParams(dimension_semantics=("parallel", "parallel", "arbitrary"),
                                             vmem_limit_bytes=VMEM_LIMIT_BYTES),
        name="sb_prompt",
    )(qb, kb, vb)


def _sb_heads_block(q, keys, values, carry, neg_tri, causal, feature_major):
    t = q[0].shape[0]
    score, mix = (_dot, _dot_nt) if feature_major else (_dot_nt, _dot)
    z = jnp.concatenate([score(qh, kh) for qh, kh in zip(q, keys)], axis=0)
    if causal is not None:
        z = jnp.where(causal, z, -jnp.inf)
    sp = jnp.maximum(z, 0.0) + jnp.log(1.0 + jnp.exp2(-jnp.abs(z))) * LOG2E
    a = jnp.exp2(z + _dot(sp.astype(BF16), neg_tri) + carry).astype(BF16)
    outs = [mix(a[h * t:(h + 1) * t], vh) for h, vh in enumerate(values)]
    return outs, carry - jnp.sum(sp, axis=1, keepdims=True)


def _sb_sample_kernel(q_ref, k_ref, v_ref, ck_hbm, cv_hbm, o_ref, kbuf, vbuf, acc_ref, carry_ref, sem,
                      *, layer, past_len):
    stream = pl.program_id(0)
    blk = SB_BLOCK
    t = q_ref.shape[1]
    heads = range(SB_HEADS)
    cols = lambda h: slice(h * SB_HEAD_DIM, (h + 1) * SB_HEAD_DIM)
    q = [q_ref[0, :, cols(h)] for h in heads]

    def fetch(j):
        tokens = pl.ds(pl.multiple_of(j * blk, blk), blk)
        return (pltpu.make_async_copy(ck_hbm.at[layer, stream, :, :, tokens], kbuf, sem.at[0]),
                pltpu.make_async_copy(cv_hbm.at[layer, stream, :, :, tokens], vbuf, sem.at[1]))

    def accumulate(outs, carry, first):
        for h in heads:
            if first:
                acc_ref[:, cols(h)] = outs[h]
            else:
                acc_ref[:, cols(h)] += outs[h]
        carry_ref[...] = carry

    def cached_block():
        outs, carry = _sb_heads_block(q, [kbuf[h].astype(BF16) for h in heads], [vbuf[h].astype(BF16) for h in heads],
                                      carry_ref[...], _neg_lower(blk), None, True)
        accumulate(outs, carry, False)

    def alive():
        return (jnp.max(carry_ref[...]) > SB_SKIP_BELOW).astype(jnp.int32)

    last = past_len // blk - 1
    copies = fetch(last)
    for c in copies:
        c.start()
    r = lax.broadcasted_iota(jnp.int32, (SB_HEADS * t, t), 0)
    causal = lax.broadcasted_iota(jnp.int32, (SB_HEADS * t, t), 1) < lax.rem(r, t)
    outs, carry = _sb_heads_block(q, [k_ref[0, :, cols(h)] for h in heads], [v_ref[0, :, cols(h)] for h in heads],
                                  jnp.zeros((SB_HEADS * t, 1), F32), _neg_lower(t), causal, False)
    accumulate(outs, carry, True)
    for c in copies:
        c.wait()
    cached_block()

    def body(state):
        j, _ = state
        copies = fetch(j)
        for c in copies:
            c.start()
        for c in copies:
            c.wait()
        cached_block()
        return j - 1, alive()

    lax.while_loop(lambda s: jnp.logical_and(s[0] >= 0, s[1] > 0), body, (last - 1, alive()))
    o_ref[0] = acc_ref[...].astype(o_ref.dtype)


def _sb_sample(qb, kb, vb, cache_k, cache_v, layer):
    b, t, _ = qb.shape
    past_len = cache_k.shape[-1]
    assert past_len % SB_BLOCK == 0
    new_spec = pl.BlockSpec((1, t, SB_WIDTH), lambda bi: (bi, 0, 0))
    hbm_spec = pl.BlockSpec(memory_space=pl.ANY)
    block = (SB_HEADS, SB_HEAD_DIM, SB_BLOCK)
    return pl.pallas_call(
        functools.partial(_sb_sample_kernel, layer=layer, past_len=past_len),
        grid=(b,),
        in_specs=[new_spec, new_spec, new_spec, hbm_spec, hbm_spec],
        out_specs=new_spec,
        out_shape=jax.ShapeDtypeStruct((b, t, SB_WIDTH), BF16),
        scratch_shapes=[pltpu.VMEM(block, F32), pltpu.VMEM(block, F32),
                        pltpu.VMEM((t, SB_WIDTH), F32), pltpu.VMEM((SB_HEADS * t, 1), F32),
                        pltpu.SemaphoreType.DMA((2,))],
        compiler_params=pltpu.CompilerParams(dimension_semantics=("arbitrary",),
                                             vmem_limit_bytes=VMEM_LIMIT_BYTES),
        name="sb_sample",
    )(qb, kb, vb, cache_k, cache_v)


def _pool_diff(ext, first_pos):
    rows = ext.shape[0] - POOL_HALO
    pos = first_pos + lax.broadcasted_iota(jnp.int32, (rows, 1), 0)
    outs = []
    for g, window in enumerate(POOL_WINDOWS):
        cols = ext[:, g * POOL_GROUP_DIM:(g + 1) * POOL_GROUP_DIM]
        acc = cols
        shift = 1
        while shift < window:
            acc = acc + pltpu.roll(acc, shift, axis=0)
            shift *= 2
        inv_count = 1.0 / jnp.minimum(pos + 1, window).astype(F32)
        outs.append(acc[POOL_HALO:] * inv_count - cols[POOL_HALO:])
    return jnp.concatenate(outs, axis=1)


def _post_kernel(x_ref, u_ref, halo_ref, o_ref, gate_ref, p_ref,
                 wgrp_ref, scale_ref, wpu_ref, wsu_ref, wout_ref,
                 gmlp_ref, wup_ref, wdown_ref, gple_ref, wpg_ref, wpp_ref, gfin_ref,
                 y_ref, *, tiles_per_stream, streams_per_tile, past_pos, final_norm):
    tm = u_ref.shape[0]
    chunk = tm // POST_CHAINS

    def pooled(c):
        u = u_ref[c * chunk:(c + 1) * chunk, :]
        if streams_per_tile == 1:
            step = pl.program_id(0) % tiles_per_stream
            if c == 0:
                halo = jnp.where(step == 0, 0.0, halo_ref[...])
            else:
                halo = u_ref[c * chunk - POOL_HALO:c * chunk, :]
            diff = _pool_diff(jnp.concatenate([halo, u], axis=0), past_pos + step * tm + c * chunk)
        else:
            t = tm // streams_per_tile
            per_chunk = streams_per_tile // POST_CHAINS
            diff = jnp.concatenate(
                [_pool_diff(jnp.concatenate([halo_ref[c * per_chunk + s], u[s * t:(s + 1) * t]], axis=0), past_pos)
                 for s in range(per_chunk)], axis=0)
        return diff.astype(BF16)

    def token_chain(c):
        rows = slice(c * chunk, (c + 1) * chunk)
        diff = pooled(c)
        y_pool = jnp.concatenate(
            [_dot(diff[:, g * POOL_GROUP_DIM:(g + 1) * POOL_GROUP_DIM], wgrp_ref[g])
             for g in range(len(POOL_WINDOWS))], axis=1) * scale_ref[...]
        gates = gate_ref[rows, :].astype(F32)
        merged = (gates[:, :D_MODEL] * _dot(y_pool.astype(BF16), wpu_ref[...])
                  + gates[:, D_MODEL:] * _dot(o_ref[rows, :], wsu_ref[...]))
        yield
        x = x_ref[rows, :] + _dot(merged.astype(BF16), wout_ref[...])
        xn = _rms_norm(x, gmlp_ref[...]).astype(BF16)
        yield
        h = jnp.square(jnp.maximum(_dot(xn, wup_ref[...]), 0.0)).astype(BF16)
        yield
        x = x + _dot(h, wdown_ref[...])
        xn = _rms_norm(x, gple_ref[...]).astype(BF16)
        yield
        ple_gate = jax.nn.sigmoid(_dot(xn, wpg_ref[...]))
        x = x + ple_gate * _dot(p_ref[rows, :].astype(BF16), wpp_ref[...])
        if final_norm:
            x = _rms_norm(x, gfin_ref[...])
        y_ref[rows, :] = x

    _interleave([token_chain(c) for c in range(POST_CHAINS)], POST_LAG)


def _post(x, u, halo, o, gates, p, weights, *, tm, tiles_per_stream, streams_per_tile, past_pos, final_norm):
    n = x.shape[0]
    row = lambda width: pl.BlockSpec((tm, width), lambda i: (i, 0))
    if streams_per_tile == 1:
        per_tile = tm // POOL_HALO
        halo_spec = pl.BlockSpec((POOL_HALO, POOL_WIDTH), lambda i: (jnp.maximum(i * per_tile - 1, 0), 0))
    else:
        halo_spec = pl.BlockSpec((streams_per_tile, POOL_HALO, POOL_WIDTH), lambda i: (i, 0, 0))
    w_specs = [_resident(w.shape) for w in weights]
    return pl.pallas_call(
        functools.partial(_post_kernel, tiles_per_stream=tiles_per_stream, streams_per_tile=streams_per_tile,
                          past_pos=past_pos, final_norm=final_norm),
        grid=(n // tm,),
        in_specs=[row(D_MODEL), row(POOL_WIDTH), halo_spec, row(SB_WIDTH), row(2 * D_MODEL), row(PLE_DIM)] + w_specs,
        out_specs=row(D_MODEL),
        out_shape=jax.ShapeDtypeStruct((n, D_MODEL), F32),
        compiler_params=pltpu.CompilerParams(dimension_semantics=("parallel",),
                                             vmem_limit_bytes=VMEM_LIMIT_BYTES),
        name="post",
    )(x, u, halo, o, gates, p, *weights)


def kernel(x_prompt, x_sample, cache_k, cache_v, state_pool, p_prompt, p_sample, g_mix, w_in, b_gate, w_pool_grp, pool_scale, w_pool_up, w_sb_up, w_out, g_mlp, w_up, w_down, g_ple, w_ple_gate, w_ple_proj, g_final):
    depth = w_in.shape[0]
    bp, tp, _ = x_prompt.shape
    bs, ts, _ = x_sample.shape
    past_len = cache_k.shape[2]
    assert tp % ROW_TILE == 0 and tp % SB_BLOCK == 0
    assert ts >= POOL_STATE and ts % 8 == 0 and bs % POST_CHAINS == 0
    n_s = bs * ts
    xp = x_prompt.reshape(bp * tp, D_MODEL)
    xs = x_sample.reshape(n_s, D_MODEL)
    row_vec = lambda a: a.reshape(1, -1)
    cache_rows = lambda c: jnp.transpose(c, (0, 1, 3, 4, 2))

    outs = {name: [] for name in ("kp", "vp", "pp", "ks", "vs", "ps")}
    for d in range(depth):
        w_in_bf = w_in[d].astype(BF16)
        weights = (w_pool_grp[d].astype(BF16), row_vec(pool_scale[d]), w_pool_up[d].astype(BF16),
                   w_sb_up[d].astype(BF16), w_out[d].astype(BF16), row_vec(g_mlp[d]), w_up[d].astype(BF16),
                   w_down[d].astype(BF16), row_vec(g_ple[d]), w_ple_gate[d].astype(BF16),
                   w_ple_proj[d].astype(BF16), row_vec(g_final))
        final_norm = d == depth - 1

        u, k, v, qb, kb, vb, gates = _inproj(xp, g_mix[d], w_in_bf, b_gate[d], ROW_TILE)
        o = _sb_prompt(qb.reshape(bp, tp, SB_WIDTH), kb.reshape(bp, tp, SB_WIDTH), vb.reshape(bp, tp, SB_WIDTH))
        xp = _post(xp, u, u, o.reshape(bp * tp, SB_WIDTH), gates, p_prompt[d].reshape(bp * tp, PLE_DIM), weights,
                   tm=ROW_TILE, tiles_per_stream=tp // ROW_TILE, streams_per_tile=1, past_pos=0,
                   final_norm=final_norm)
        outs["kp"].append(k.reshape(bp, tp, SB_HEADS, SB_HEAD_DIM))
        outs["vp"].append(v.reshape(bp, tp, SB_HEADS, SB_HEAD_DIM))
        outs["pp"].append(u.reshape(bp, tp, POOL_WIDTH)[:, tp - POOL_STATE:])

        u, k, v, qb, kb, vb, gates = _inproj(xs, g_mix[d], w_in_bf, b_gate[d], n_s)
        o = _sb_sample(qb.reshape(bs, ts, SB_WIDTH), kb.reshape(bs, ts, SB_WIDTH), vb.reshape(bs, ts, SB_WIDTH),
                       cache_rows(cache_k), cache_rows(cache_v), d)
        halo = jnp.pad(state_pool[d], ((0, 0), (POOL_HALO - POOL_STATE, 0), (0, 0)))
        xs = _post(xs, u, halo, o.reshape(n_s, SB_WIDTH), gates, p_sample[d].reshape(n_s, PLE_DIM), weights,
                   tm=n_s, tiles_per_stream=1, streams_per_tile=bs, past_pos=POOL_STATE,
                   final_norm=final_norm)
        outs["ks"].append(k.reshape(bs, ts, SB_HEADS, SB_HEAD_DIM))
        outs["vs"].append(v.reshape(bs, ts, SB_HEADS, SB_HEAD_DIM))
        outs["ps"].append(u.reshape(bs, ts, POOL_WIDTH)[:, ts - POOL_STATE:])

    stack = lambda name: jnp.stack(outs[name])
    return (xp.reshape(bp, tp, D_MODEL), xs.reshape(bs, ts, D_MODEL),
            stack("kp"), stack("vp"), stack("pp"), stack("ks"), stack("vs"), stack("ps"))
```

```python
import functools

import jax
import jax.numpy as jnp
from jax import lax
from jax.experimental import pallas as pl
from jax.experimental.pallas import tpu as pltpu

D_MODEL = 1024
POOL_WIDTH = 512
POOL_WINDOWS = (2, 4, 8, 16)
POOL_GROUP_DIM = POOL_WIDTH // len(POOL_WINDOWS)
POOL_STATE = max(POOL_WINDOWS) - 1
POOL_HALO = 16
SB_HEADS = 8
SB_HEAD_DIM = 64
SB_WIDTH = SB_HEADS * SB_HEAD_DIM
HEAD_PAIR = 2 * SB_HEAD_DIM
D_FF = 4 * D_MODEL
PLE_DIM = 256
EPS = 1e-6

V7X_VMEM_BYTES = 64 * 1024 * 1024
VMEM_LIMIT_BYTES = V7X_VMEM_BYTES - 8 * 1024 * 1024

ROW_TILE = 512
INPROJ_CHAINS = 2
POST_CHAINS = 2
POST_LAG = 1
SB_BLOCK = 256
SB_PAIRS_PER_STEP = 4
SB_PAIR_LAG = 1

LOG2E = 1.4426950408889634
SB_SKIP_BELOW = -160.0

BF16 = jnp.bfloat16
F32 = jnp.float32


def _rms_norm(x, g):
    y = x * lax.rsqrt(jnp.mean(x * x, axis=-1, keepdims=True) + EPS)
    return y * g


def _dot(a, b):
    return jnp.dot(a, b, preferred_element_type=F32)


def _dot_nt(a, b):
    return lax.dot_general(a, b, (((1,), (1,)), ((), ())), preferred_element_type=F32)


def _resident(shape):
    return pl.BlockSpec(shape, lambda *_: (0,) * len(shape), pipeline_mode=pl.Buffered(1))


def _interleave(chains, lag):
    waiting, live, tick = list(chains), [], 0
    while waiting or live:
        if waiting and tick % lag == 0:
            live.append(waiting.pop(0))
        live = [c for c in live if next(c, True) is None]
        tick += 1


def _store_head_rows(ref, row0, x):
    rows = x.shape[0]
    for h in range(SB_HEADS):
        ref[pl.ds(row0 * SB_HEADS + h, rows, stride=SB_HEADS), :] = x[:, h * SB_HEAD_DIM:(h + 1) * SB_HEAD_DIM]


def _group_steps(steps, maps_to):
    starts = [sum(steps[:g]) for g in range(len(steps))]
    return [lambda i, s=s, n=n: maps_to(jnp.clip(i - s, 0, n - 1)) for s, n in zip(starts, steps)]


def _run_group(steps, bodies):
    i = pl.program_id(0)
    start = 0
    for n, body in zip(steps, bodies):
        pl.when(jnp.logical_and(i >= start, i < start + n))(body)
        start += n


def _inproj_kernel(*refs, steps):
    groups = len(steps)
    g_ref, w_ref, b_ref = refs[groups:groups + 3]
    outs = refs[groups + 3:]
    _run_group(steps, [functools.partial(_inproj_tile, refs[g], g_ref, w_ref, b_ref, *outs[7 * g:7 * g + 7])
                       for g in range(groups)])


def _inproj_tile(x_ref, g_ref, w_ref, b_ref, u_ref, k_ref, v_ref, qb_ref, kb_ref, vb_ref, gate_ref):
    chunk = x_ref.shape[0] // INPROJ_CHAINS
    c0, c1, c2, c3 = POOL_WIDTH, POOL_WIDTH + SB_WIDTH, POOL_WIDTH + 2 * SB_WIDTH, POOL_WIDTH + 3 * SB_WIDTH

    def token_chain(c):
        rows = slice(c * chunk, (c + 1) * chunk)
        xn = _rms_norm(x_ref[rows, :], g_ref[...]).astype(BF16)
        yield
        u_ref[rows, :] = _dot(xn, w_ref[:, :c0])
        qb_ref[rows, :] = (_dot(xn, w_ref[:, c0:c1]) * (SB_HEAD_DIM ** -0.5 * LOG2E)).astype(BF16)
        yield
        k = _dot(xn, w_ref[:, c1:c2])
        _store_head_rows(k_ref, c * chunk, k)
        kb_ref[rows, :] = k.astype(BF16)
        yield
        v = _dot(xn, w_ref[:, c2:c3])
        _store_head_rows(v_ref, c * chunk, v)
        vb_ref[rows, :] = v.astype(BF16)
        yield
        gate_ref[rows, :] = jax.nn.sigmoid(_dot(xn, w_ref[:, c3:]) + b_ref[...]).astype(BF16)

    _interleave([token_chain(c) for c in range(INPROJ_CHAINS)], 1)


def _inproj(xs, g_mix, w_in_bf, b_gate, tm):
    in_width = w_in_bf.shape[1]
    steps = [x.shape[0] // tm for x in xs]
    maps = _group_steps(steps, lambda j: (j, 0))
    in_specs = [pl.BlockSpec((tm, D_MODEL), m) for m in maps]
    in_specs += [_resident((1, D_MODEL)), _resident((D_MODEL, in_width)), _resident((1, 2 * D_MODEL))]
    out_specs, out_shape = [], []
    for x, m in zip(xs, maps):
        n = x.shape[0]
        row = lambda width: pl.BlockSpec((tm, width), m)
        head_rows = pl.BlockSpec((tm * SB_HEADS, SB_HEAD_DIM), m)
        f32_out = jax.ShapeDtypeStruct((n * SB_HEADS, SB_HEAD_DIM), F32)
        bf_out = jax.ShapeDtypeStruct((n, SB_WIDTH), BF16)
        out_specs += [row(POOL_WIDTH), head_rows, head_rows, row(SB_WIDTH), row(SB_WIDTH), row(SB_WIDTH),
                      row(2 * D_MODEL)]
        out_shape += [jax.ShapeDtypeStruct((n, POOL_WIDTH), F32), f32_out, f32_out, bf_out, bf_out, bf_out,
                      jax.ShapeDtypeStruct((n, 2 * D_MODEL), BF16)]
    outs = pl.pallas_call(
        functools.partial(_inproj_kernel, steps=steps),
        grid=(sum(steps),),
        in_specs=in_specs,
        out_specs=out_specs,
        out_shape=out_shape,
        compiler_params=pltpu.CompilerParams(dimension_semantics=("arbitrary",),
                                             vmem_limit_bytes=VMEM_LIMIT_BYTES),
        name="inproj",
    )(*xs, g_mix.reshape(1, D_MODEL), w_in_bf, b_gate.reshape(1, 2 * D_MODEL))
    return [outs[7 * g:7 * g + 7] for g in range(len(xs))]


def _neg_lower(n):
    r = lax.broadcasted_iota(jnp.int32, (n, n), 0)
    c = lax.broadcasted_iota(jnp.int32, (n, n), 1)
    return jnp.where(r >= c, -1.0, 0.0).astype(BF16)


def _stack_heads(x):
    lane = lax.broadcasted_iota(jnp.int32, x.shape, 1)
    zero = jnp.zeros_like(x)
    return jnp.concatenate([jnp.where(lane < SB_HEAD_DIM, x, zero), jnp.where(lane >= SB_HEAD_DIM, x, zero)], axis=0)


def _causal_mask(tq, tk):
    r = lax.broadcasted_iota(jnp.int32, (2 * tq, tk), 0)
    c = lax.broadcasted_iota(jnp.int32, (2 * tq, tk), 1)
    return c < jnp.where(r >= tq, r - tq, r)


def _sb_pair_stages(q2, blocks, carry, emit):
    tq = q2.shape[0] // 2
    logits = []
    for k_blk, _, _, causal in blocks:
        z = _dot_nt(q2, k_blk)
        if causal is not None:
            z = jnp.where(causal, z, -jnp.inf)
        logits.append(z)
    yield
    softplus = [jnp.maximum(z, 0.0) + jnp.log(1.0 + jnp.exp2(-jnp.abs(z))) * LOG2E for z in logits]
    yield
    sums = [_dot(sp.astype(BF16), neg_tri) for (_, _, neg_tri, _), sp in zip(blocks, softplus)]
    yield
    weights = []
    for z, incl in zip(logits, sums):
        a = jnp.exp2(z + incl + carry).astype(BF16)
        weights += [a[:tq], a[tq:]]
        carry = carry + incl[:, :1]
    yield
    values = jnp.concatenate([v2 for _, v2, _, _ in blocks], axis=0)
    emit(_dot(jnp.concatenate(weights, axis=1), values), carry)


def _sb_prompt_kernel(q_ref, k_ref, v_ref, o_ref, acc_ref, carry_ref):
    qi = pl.program_id(2)
    blk = SB_BLOCK
    pairs = range(SB_PAIRS_PER_STEP)
    lanes = lambda p: slice(p * HEAD_PAIR, (p + 1) * HEAD_PAIR)
    q2 = [_stack_heads(q_ref[0, :, lanes(p)]) for p in pairs]
    neg_tri = _neg_lower(blk)

    def key_block(p, j, causal):
        r0 = pl.multiple_of(j * blk, blk)
        return (k_ref[0, pl.ds(r0, blk), lanes(p)], _stack_heads(v_ref[0, pl.ds(r0, blk), lanes(p)]),
                neg_tri, causal)

    def sweep(block_ids, first):
        def emit(p, out, carry):
            if first:
                acc_ref[:, lanes(p)] = out
            else:
                acc_ref[:, lanes(p)] += out
            carry_ref[p] = carry

        _interleave([_sb_pair_stages(q2[p], [key_block(p, j, causal) for j, causal in block_ids],
                                     jnp.zeros((2 * blk, 1), F32) if first else carry_ref[p],
                                     functools.partial(emit, p)) for p in pairs], SB_PAIR_LAG)

    def alive():
        return (jnp.max(carry_ref[...]) > SB_SKIP_BELOW).astype(jnp.int32)

    diagonal = (qi, _causal_mask(blk, blk))
    pl.when(qi == 0)(lambda: sweep([diagonal], True))
    pl.when(qi > 0)(lambda: sweep([diagonal, (qi - 1, None)], True))

    def body(state):
        j, _ = state
        sweep([(j, None)], False)
        return j - 1, alive()

    lax.while_loop(lambda s: jnp.logical_and(s[0] >= 0, s[1] > 0), body, (qi - 2, alive()))
    o_ref[0] = acc_ref[...].astype(o_ref.dtype)


def _sb_prompt(qb, kb, vb):
    b, t, _ = qb.shape
    blk = SB_BLOCK
    width = SB_PAIRS_PER_STEP * HEAD_PAIR
    kv_spec = pl.BlockSpec((1, t, width), lambda bi, hp, qi: (bi, 0, hp))
    q_spec = pl.BlockSpec((1, blk, width), lambda bi, hp, qi: (bi, qi, hp))
    return pl.pallas_call(
        _sb_prompt_kernel,
        grid=(b, SB_WIDTH // width, t // blk),
        in_specs=[q_spec, kv_spec, kv_spec],
        out_specs=q_spec,
        out_shape=jax.ShapeDtypeStruct((b, t, SB_WIDTH), BF16),
        scratch_shapes=[pltpu.VMEM((blk, width), F32), pltpu.VMEM((SB_PAIRS_PER_STEP, 2 * blk, 1), F32)],
        compiler_params=pltpu.CompilerParams(dimension_semantics=("parallel", "parallel", "arbitrary"),
                                             vmem_limit_bytes=VMEM_LIMIT_BYTES),
        name="sb_prompt",
    )(qb, kb, vb)


def _sb_heads_block(q, keys, values, carry, neg_tri, causal, feature_major):
    t = q[0].shape[0]
    score, mix = (_dot, _dot_nt) if feature_major else (_dot_nt, _dot)
    z = jnp.concatenate([score(qh, kh) for qh, kh in zip(q, keys)], axis=0)
    if causal is not None:
        z = jnp.where(causal, z, -jnp.inf)
    sp = jnp.maximum(z, 0.0) + jnp.log(1.0 + jnp.exp2(-jnp.abs(z))) * LOG2E
    a = jnp.exp2(z + _dot(sp.astype(BF16), neg_tri) + carry).astype(BF16)
    outs = [mix(a[h * t:(h + 1) * t], vh) for h, vh in enumerate(values)]
    return outs, carry - jnp.sum(sp, axis=1, keepdims=True)


def _sb_sample_kernel(q_ref, k_ref, v_ref, ck_hbm, cv_hbm, o_ref, kbuf, vbuf, acc_ref, carry_ref, sem,
                      *, layer, past_len):
    stream = pl.program_id(0)
    slot = stream % 2
    blk = SB_BLOCK
    t = q_ref.shape[1]
    heads = range(SB_HEADS)
    cols = lambda h: slice(h * SB_HEAD_DIM, (h + 1) * SB_HEAD_DIM)
    q = [q_ref[0, :, cols(h)] for h in heads]
    last = past_len // blk - 1

    def fetch(s, j):
        tokens = pl.ds(pl.multiple_of(j * blk, blk), blk)
        return (pltpu.make_async_copy(ck_hbm.at[layer, s, :, :, tokens], kbuf.at[s % 2], sem.at[s % 2, 0]),
                pltpu.make_async_copy(cv_hbm.at[layer, s, :, :, tokens], vbuf.at[s % 2], sem.at[s % 2, 1]))

    def start(copies):
        for c in copies:
            c.start()

    def wait(copies):
        for c in copies:
            c.wait()

    def accumulate(outs, carry, first):
        for h in heads:
            if first:
                acc_ref[:, cols(h)] = outs[h]
            else:
                acc_ref[:, cols(h)] += outs[h]
        carry_ref[...] = carry

    def cached_block():
        outs, carry = _sb_heads_block(q, [kbuf[slot, h].astype(BF16) for h in heads],
                                      [vbuf[slot, h].astype(BF16) for h in heads],
                                      carry_ref[...], _neg_lower(blk), None, True)
        accumulate(outs, carry, False)

    def alive():
        return (jnp.max(carry_ref[...]) > SB_SKIP_BELOW).astype(jnp.int32)

    pl.when(stream == 0)(lambda: start(fetch(stream, last)))
    pl.when(stream + 1 < pl.num_programs(0))(lambda: start(fetch(stream + 1, last)))
    r = lax.broadcasted_iota(jnp.int32, (SB_HEADS * t, t), 0)
    causal = lax.broadcasted_iota(jnp.int32, (SB_HEADS * t, t), 1) < lax.rem(r, t)
    outs, carry = _sb_heads_block(q, [k_ref[0, :, cols(h)] for h in heads], [v_ref[0, :, cols(h)] for h in heads],
                                  jnp.zeros((SB_HEADS * t, 1), F32), _neg_lower(t), causal, False)
    accumulate(outs, carry, True)
    wait(fetch(stream, last))
    cached_block()

    def body(state):
        j, _ = state
        copies = fetch(stream, j)
        start(copies)
        wait(copies)
        cached_block()
        return j - 1, alive()

    lax.while_loop(lambda s: jnp.logical_and(s[0] >= 0, s[1] > 0), body, (last - 1, alive()))
    o_ref[0] = acc_ref[...].astype(o_ref.dtype)


def _sb_sample(qb, kb, vb, cache_k, cache_v, layer):
    b, t, _ = qb.shape
    past_len = cache_k.shape[-1]
    assert past_len % SB_BLOCK == 0
    new_spec = pl.BlockSpec((1, t, SB_WIDTH), lambda bi: (bi, 0, 0))
    hbm_spec = pl.BlockSpec(memory_space=pl.ANY)
    block = (2, SB_HEADS, SB_HEAD_DIM, SB_BLOCK)
    return pl.pallas_call(
        functools.partial(_sb_sample_kernel, layer=layer, past_len=past_len),
        grid=(b,),
        in_specs=[new_spec, new_spec, new_spec, hbm_spec, hbm_spec],
        out_specs=new_spec,
        out_shape=jax.ShapeDtypeStruct((b, t, SB_WIDTH), BF16),
        scratch_shapes=[pltpu.VMEM(block, F32), pltpu.VMEM(block, F32),
                        pltpu.VMEM((t, SB_WIDTH), F32), pltpu.VMEM((SB_HEADS * t, 1), F32),
                        pltpu.SemaphoreType.DMA((2, 2))],
        compiler_params=pltpu.CompilerParams(dimension_semantics=("arbitrary",),
                                             vmem_limit_bytes=VMEM_LIMIT_BYTES),
        name="sb_sample",
    )(qb, kb, vb, cache_k, cache_v)


def _pool_diff(ext, first_pos):
    rows = ext.shape[0] - POOL_HALO
    pos = first_pos + lax.broadcasted_iota(jnp.int32, (rows, 1), 0)
    outs = []
    for g, window in enumerate(POOL_WINDOWS):
        cols = ext[:, g * POOL_GROUP_DIM:(g + 1) * POOL_GROUP_DIM]
        acc = cols
        shift = 1
        while shift < window:
            acc = acc + pltpu.roll(acc, shift, axis=0)
            shift *= 2
        inv_count = 1.0 / jnp.minimum(pos + 1, window).astype(F32)
        outs.append(acc[POOL_HALO:] * inv_count - cols[POOL_HALO:])
    return jnp.concatenate(outs, axis=1)


POST_GROUP_INPUTS = 6
POST_WEIGHTS = 12


def _post_kernel(*refs, steps, modes, final_norm):
    groups = len(steps)
    n_in = POST_GROUP_INPUTS * groups
    weights = refs[n_in:n_in + POST_WEIGHTS]
    ys = refs[n_in + POST_WEIGHTS:]
    _run_group(steps, [functools.partial(_post_tile, *refs[POST_GROUP_INPUTS * g:POST_GROUP_INPUTS * (g + 1)], *weights,
                                         ys[g], first_step=sum(steps[:g]), final_norm=final_norm, **modes[g])
                       for g in range(groups)])


def _post_tile(x_ref, u_ref, halo_ref, o_ref, gate_ref, p_ref,
               wgrp_ref, scale_ref, wpu_ref, wsu_ref, wout_ref,
               gmlp_ref, wup_ref, wdown_ref, gple_ref, wpg_ref, wpp_ref, gfin_ref,
               y_ref, *, first_step, tiles_per_stream, streams_per_tile, past_pos, final_norm):
    tm = u_ref.shape[0]
    chunk = tm // POST_CHAINS

    def pooled(c):
        u = u_ref[c * chunk:(c + 1) * chunk, :]
        if streams_per_tile == 1:
            step = (pl.program_id(0) - first_step) % tiles_per_stream
            if c == 0:
                halo = jnp.where(step == 0, 0.0, halo_ref[...])
            else:
                halo = u_ref[c * chunk - POOL_HALO:c * chunk, :]
            diff = _pool_diff(jnp.concatenate([halo, u], axis=0), past_pos + step * tm + c * chunk)
        else:
            t = tm // streams_per_tile
            per_chunk = streams_per_tile // POST_CHAINS
            diff = jnp.concatenate(
                [_pool_diff(jnp.concatenate([halo_ref[c * per_chunk + s], u[s * t:(s + 1) * t]], axis=0), past_pos)
                 for s in range(per_chunk)], axis=0)
        return diff.astype(BF16)

    def token_chain(c):
        rows = slice(c * chunk, (c + 1) * chunk)
        diff = pooled(c)
        y_pool = jnp.concatenate(
            [_dot(diff[:, g * POOL_GROUP_DIM:(g + 1) * POOL_GROUP_DIM], wgrp_ref[g])
             for g in range(len(POOL_WINDOWS))], axis=1) * scale_ref[...]
        gates = gate_ref[rows, :].astype(F32)
        merged = (gates[:, :D_MODEL] * _dot(y_pool.astype(BF16), wpu_ref[...])
                  + gates[:, D_MODEL:] * _dot(o_ref[rows, :], wsu_ref[...]))
        yield
        x = x_ref[rows, :] + _dot(merged.astype(BF16), wout_ref[...])
        xn = _rms_norm(x, gmlp_ref[...]).astype(BF16)
        yield
        h = jnp.square(jnp.maximum(_dot(xn, wup_ref[...]), 0.0)).astype(BF16)
        yield
        x = x + _dot(h, wdown_ref[...])
        xn = _rms_norm(x, gple_ref[...]).astype(BF16)
        yield
        ple_gate = jax.nn.sigmoid(_dot(xn, wpg_ref[...]))
        x = x + ple_gate * _dot(p_ref[rows, :].astype(BF16), wpp_ref[...])
        if final_norm:
            x = _rms_norm(x, gfin_ref[...])
        y_ref[rows, :] = x

    _interleave([token_chain(c) for c in range(POST_CHAINS)], POST_LAG)


def _post(groups, weights, modes, *, tm, final_norm):
    assert len(weights) == POST_WEIGHTS and all(len(g) == POST_GROUP_INPUTS for g in groups)
    steps = [g[0].shape[0] // tm for g in groups]
    maps = _group_steps(steps, lambda j: j)
    in_specs, out_specs, out_shape = [], [], []
    for (x, *_), mode, m in zip(groups, modes, maps):
        row = lambda width, m=m: pl.BlockSpec((tm, width), lambda i: (m(i), 0))
        if mode["streams_per_tile"] == 1:
            per_tile = tm // POOL_HALO
            halo_spec = pl.BlockSpec((POOL_HALO, POOL_WIDTH),
                                     lambda i, m=m: (jnp.maximum(m(i) * per_tile - 1, 0), 0))
        else:
            halo_spec = pl.BlockSpec((mode["streams_per_tile"], POOL_HALO, POOL_WIDTH), lambda i, m=m: (m(i), 0, 0))
        in_specs += [row(D_MODEL), row(POOL_WIDTH), halo_spec, row(SB_WIDTH), row(2 * D_MODEL), row(PLE_DIM)]
        out_specs.append(row(D_MODEL))
        out_shape.append(jax.ShapeDtypeStruct((x.shape[0], D_MODEL), F32))
    in_specs += [_resident(w.shape) for w in weights]
    return pl.pallas_call(
        functools.partial(_post_kernel, steps=steps, modes=modes, final_norm=final_norm),
        grid=(sum(steps),),
        in_specs=in_specs,
        out_specs=out_specs,
        out_shape=out_shape,
        compiler_params=pltpu.CompilerParams(dimension_semantics=("arbitrary",),
                                             vmem_limit_bytes=VMEM_LIMIT_BYTES),
        name="post",
    )(*[a for g in groups for a in g], *weights)


def kernel(x_prompt, x_sample, cache_k, cache_v, state_pool, p_prompt, p_sample, g_mix, w_in, b_gate, w_pool_grp, pool_scale, w_pool_up, w_sb_up, w_out, g_mlp, w_up, w_down, g_ple, w_ple_gate, w_ple_proj, g_final):
    depth = w_in.shape[0]
    bp, tp, _ = x_prompt.shape
    bs, ts, _ = x_sample.shape
    past_len = cache_k.shape[2]
    assert tp % ROW_TILE == 0 and tp % SB_BLOCK == 0
    n_s = bs * ts
    assert ts >= POOL_STATE and ts % 8 == 0 and n_s % ROW_TILE == 0 and (ROW_TILE // ts) % POST_CHAINS == 0
    xp = x_prompt.reshape(bp * tp, D_MODEL)
    xs = x_sample.reshape(n_s, D_MODEL)
    row_vec = lambda a: a.reshape(1, -1)
    cache_rows = lambda c: jnp.transpose(c, (0, 1, 3, 4, 2))

    outs = {name: [] for name in ("kp", "vp", "pp", "ks", "vs", "ps")}
    for d in range(depth):
        w_in_bf = w_in[d].astype(BF16)
        weights = (w_pool_grp[d].astype(BF16), row_vec(pool_scale[d]), w_pool_up[d].astype(BF16),
                   w_sb_up[d].astype(BF16), w_out[d].astype(BF16), row_vec(g_mlp[d]), w_up[d].astype(BF16),
                   w_down[d].astype(BF16), row_vec(g_ple[d]), w_ple_gate[d].astype(BF16),
                   w_ple_proj[d].astype(BF16), row_vec(g_final))
        final_norm = d == depth - 1

        (up, kp, vp, qbp, kbp, vbp, gates_p), (us, ks, vs, qbs, kbs, vbs, gates_s) = _inproj(
            [xp, xs], g_mix[d], w_in_bf, b_gate[d], ROW_TILE)
        op = _sb_prompt(qbp.reshape(bp, tp, SB_WIDTH), kbp.reshape(bp, tp, SB_WIDTH), vbp.reshape(bp, tp, SB_WIDTH))
        os_ = _sb_sample(qbs.reshape(bs, ts, SB_WIDTH), kbs.reshape(bs, ts, SB_WIDTH), vbs.reshape(bs, ts, SB_WIDTH),
                         cache_rows(cache_k), cache_rows(cache_v), d)
        halo_s = jnp.pad(state_pool[d], ((0, 0), (POOL_HALO - POOL_STATE, 0), (0, 0)))
        xp, = _post([(xp, up, up, op.reshape(bp * tp, SB_WIDTH), gates_p, p_prompt[d].reshape(bp * tp, PLE_DIM))],
                    weights, [dict(tiles_per_stream=tp // ROW_TILE, streams_per_tile=1, past_pos=0)],
                    tm=ROW_TILE, final_norm=final_norm)
        xs, = _post([(xs, us, halo_s, os_.reshape(n_s, SB_WIDTH), gates_s, p_sample[d].reshape(n_s, PLE_DIM))],
                    weights, [dict(tiles_per_stream=1, streams_per_tile=ROW_TILE // ts, past_pos=POOL_STATE)],
                    tm=ROW_TILE, final_norm=final_norm)
        outs["kp"].append(kp.reshape(bp, tp, SB_HEADS, SB_HEAD_DIM))
        outs["vp"].append(vp.reshape(bp, tp, SB_HEADS, SB_HEAD_DIM))
        outs["pp"].append(up.reshape(bp, tp, POOL_WIDTH)[:, tp - POOL_STATE:])
        outs["ks"].append(ks.reshape(bs, ts, SB_HEADS, SB_HEAD_DIM))
        outs["vs"].append(vs.reshape(bs, ts, SB_HEADS, SB_HEAD_DIM))
        outs["ps"].append(us.reshape(bs, ts, POOL_WIDTH)[:, ts - POOL_STATE:])

    stack = lambda name: jnp.stack(outs[name])
    return (xp.reshape(bp, tp, D_MODEL), xs.reshape(bs, ts, D_MODEL),
            stack("kp"), stack("vp"), stack("pp"), stack("ks"), stack("vs"), stack("ps"))
```

```python
import functools

import jax
import jax.numpy as jnp
from jax import lax
from jax.experimental import pallas as pl
from jax.experimental.pallas import tpu as pltpu

D_MODEL = 1024
POOL_WIDTH = 512
POOL_WINDOWS = (2, 4, 8, 16)
POOL_GROUP_DIM = POOL_WIDTH // len(POOL_WINDOWS)
POOL_STATE = max(POOL_WINDOWS) - 1
POOL_HALO = 16
SB_HEADS = 8
SB_HEAD_DIM = 64
SB_WIDTH = SB_HEADS * SB_HEAD_DIM
HEAD_PAIR = 2 * SB_HEAD_DIM
D_FF = 4 * D_MODEL
PLE_DIM = 256
EPS = 1e-6

V7X_VMEM_BYTES = 64 * 1024 * 1024
VMEM_LIMIT_BYTES = V7X_VMEM_BYTES - 8 * 1024 * 1024

ROW_TILE = 512
INPROJ_CHAINS = 2
POST_CHAINS = 2
POST_LAG = 1
SB_BLOCK = 256
SB_PAIRS_PER_STEP = 4
SB_PAIR_LAG = 1

BF16_SUBLANES = 16
N_WHOLE_CASTS = 2

LOG2E = 1.4426950408889634
SB_SKIP_BELOW = -160.0

BF16 = jnp.bfloat16
F32 = jnp.float32


def _rms_norm(x, g):
    y = x * lax.rsqrt(jnp.mean(x * x, axis=-1, keepdims=True) + EPS)
    return y * g


def _dot(a, b):
    return jnp.dot(a, b, preferred_element_type=F32)


def _dot_nt(a, b):
    return lax.dot_general(a, b, (((1,), (1,)), ((), ())), preferred_element_type=F32)


def _resident(shape):
    return pl.BlockSpec(shape, lambda *_: (0,) * len(shape), pipeline_mode=pl.Buffered(1))


def _interleave(chains, lag):
    waiting, live, tick = list(chains), [], 0
    while waiting or live:
        if waiting and tick % lag == 0:
            live.append(waiting.pop(0))
        live = [c for c in live if next(c, True) is None]
        tick += 1


def _store_head_rows(ref, row0, x):
    rows = x.shape[0]
    for h in range(SB_HEADS):
        ref[pl.ds(row0 * SB_HEADS + h, rows, stride=SB_HEADS), :] = x[:, h * SB_HEAD_DIM:(h + 1) * SB_HEAD_DIM]


def _group_steps(steps, maps_to):
    starts = [sum(steps[:g]) for g in range(len(steps))]
    return [lambda i, s=s, n=n: maps_to(jnp.clip(i - s, 0, n - 1)) for s, n in zip(starts, steps)]


def _run_group(steps, bodies):
    i = pl.program_id(0)
    start = 0
    for n, body in zip(steps, bodies):
        pl.when(jnp.logical_and(i >= start, i < start + n))(body)
        start += n


def _inproj_kernel(*refs, steps, n_cast):
    groups = len(steps)
    g_ref, w_ref, b_ref = refs[groups:groups + 3]
    cast_in = refs[groups + 3:groups + 3 + n_cast]
    outs = refs[groups + 3 + n_cast:]
    cast_out = outs[7 * groups:]
    i = pl.program_id(0)

    @pl.when(i == 0)
    def _():
        for src, dst in zip(cast_in[n_cast - N_WHOLE_CASTS:], cast_out[n_cast - N_WHOLE_CASTS:]):
            dst[...] = src[...].astype(BF16)

    @pl.when(i < steps[0])
    def _():
        for src, dst in zip(cast_in[:n_cast - N_WHOLE_CASTS], cast_out[:n_cast - N_WHOLE_CASTS]):
            dst[...] = src[...].astype(BF16)

    _run_group(steps, [functools.partial(_inproj_tile, refs[g], g_ref, w_ref, b_ref, *outs[7 * g:7 * g + 7])
                       for g in range(groups)])


def _inproj_tile(x_ref, g_ref, w_ref, b_ref, u_ref, k_ref, v_ref, qb_ref, kb_ref, vb_ref, gate_ref):
    chunk = x_ref.shape[0] // INPROJ_CHAINS
    c0, c1, c2, c3 = POOL_WIDTH, POOL_WIDTH + SB_WIDTH, POOL_WIDTH + 2 * SB_WIDTH, POOL_WIDTH + 3 * SB_WIDTH

    def token_chain(c):
        rows = slice(c * chunk, (c + 1) * chunk)
        xn = _rms_norm(x_ref[rows, :], g_ref[...]).astype(BF16)
        yield
        u_ref[rows, :] = _dot(xn, w_ref[:, :c0])
        qb_ref[rows, :] = (_dot(xn, w_ref[:, c0:c1]) * (SB_HEAD_DIM ** -0.5 * LOG2E)).astype(BF16)
        yield
        k = _dot(xn, w_ref[:, c1:c2])
        _store_head_rows(k_ref, c * chunk, k)
        kb_ref[rows, :] = k.astype(BF16)
        yield
        v = _dot(xn, w_ref[:, c2:c3])
        _store_head_rows(v_ref, c * chunk, v)
        vb_ref[rows, :] = v.astype(BF16)
        yield
        gate_ref[rows, :] = jax.nn.sigmoid(_dot(xn, w_ref[:, c3:]) + b_ref[...]).astype(BF16)

    _interleave([token_chain(c) for c in range(INPROJ_CHAINS)], 1)


def _inproj(xs, g_mix, w_in, b_gate, chunked, whole, tm):
    assert len(whole) == N_WHOLE_CASTS
    in_width = w_in.shape[1]
    steps = [x.shape[0] // tm for x in xs]
    maps = _group_steps(steps, lambda j: (j, 0))
    buffers = [dict(pipeline_mode=pl.Buffered(1)) if n == 1 else {} for n in steps]
    in_specs = [pl.BlockSpec((tm, D_MODEL), m, **buf) for m, buf in zip(maps, buffers)]
    in_specs += [_resident((1, D_MODEL)), _resident((D_MODEL, in_width)), _resident((1, 2 * D_MODEL))]
    out_specs, out_shape = [], []
    for x, m, buf in zip(xs, maps, buffers):
        n = x.shape[0]
        row = lambda width: pl.BlockSpec((tm, width), m, **buf)
        head_rows = pl.BlockSpec((tm * SB_HEADS, SB_HEAD_DIM), m, **buf)
        f32_out = jax.ShapeDtypeStruct((n * SB_HEADS, SB_HEAD_DIM), F32)
        bf_out = jax.ShapeDtypeStruct((n, SB_WIDTH), BF16)
        out_specs += [row(POOL_WIDTH), head_rows, head_rows, row(SB_WIDTH), row(SB_WIDTH), row(SB_WIDTH),
                      row(2 * D_MODEL)]
        out_shape += [jax.ShapeDtypeStruct((n, POOL_WIDTH), F32), f32_out, f32_out, bf_out, bf_out, bf_out,
                      jax.ShapeDtypeStruct((n, 2 * D_MODEL), BF16)]
    cast_specs = []
    for w in chunked:
        rows = w.shape[0] // steps[0]
        assert rows * steps[0] == w.shape[0] and rows % BF16_SUBLANES == 0
        cast_specs.append(pl.BlockSpec((rows, w.shape[1]), maps[0]))
    cast_specs += [pl.BlockSpec(w.shape, lambda i, nd=w.ndim: (0,) * nd) for w in whole]
    cast_shape = [jax.ShapeDtypeStruct(w.shape, BF16) for w in (*chunked, *whole)]
    outs = pl.pallas_call(
        functools.partial(_inproj_kernel, steps=steps, n_cast=len(cast_specs)),
        grid=(sum(steps),),
        in_specs=in_specs + cast_specs,
        out_specs=out_specs + cast_specs,
        out_shape=out_shape + cast_shape,
        compiler_params=pltpu.CompilerParams(dimension_semantics=("arbitrary",),
                                             vmem_limit_bytes=VMEM_LIMIT_BYTES),
        name="inproj",
    )(*xs, g_mix.reshape(1, D_MODEL), w_in, b_gate.reshape(1, 2 * D_MODEL), *chunked, *whole)
    n_group_outs = 7 * len(xs)
    return ([outs[7 * g:7 * g + 7] for g in range(len(xs))],
            outs[n_group_outs:n_group_outs + len(chunked)], outs[n_group_outs + len(chunked):])


def _neg_lower(n):
    r = lax.broadcasted_iota(jnp.int32, (n, n), 0)
    c = lax.broadcasted_iota(jnp.int32, (n, n), 1)
    return jnp.where(r >= c, -1.0, 0.0).astype(BF16)


def _stack_heads(x):
    lane = lax.broadcasted_iota(jnp.int32, x.shape, 1)
    zero = jnp.zeros_like(x)
    return jnp.concatenate([jnp.where(lane < SB_HEAD_DIM, x, zero), jnp.where(lane >= SB_HEAD_DIM, x, zero)], axis=0)


def _causal_mask(tq, tk):
    r = lax.broadcasted_iota(jnp.int32, (2 * tq, tk), 0)
    c = lax.broadcasted_iota(jnp.int32, (2 * tq, tk), 1)
    return c < jnp.where(r >= tq, r - tq, r)


def _sb_pair_stages(q2, blocks, carry, emit):
    tq = q2.shape[0] // 2
    logits = []
    for k_blk, _, _, causal in blocks:
        z = _dot_nt(q2, k_blk)
        if causal is not None:
            z = jnp.where(causal, z, -jnp.inf)
        logits.append(z)
    yield
    softplus = [jnp.maximum(z, 0.0) + jnp.log(1.0 + jnp.exp2(-jnp.abs(z))) * LOG2E for z in logits]
    yield
    sums = [_dot(sp.astype(BF16), neg_tri) for (_, _, neg_tri, _), sp in zip(blocks, softplus)]
    yield
    weights = []
    for z, incl in zip(logits, sums):
        a = jnp.exp2(z + incl + carry).astype(BF16)
        weights += [a[:tq], a[tq:]]
        carry = carry + incl[:, :1]
    yield
    values = jnp.concatenate([v2 for _, v2, _, _ in blocks], axis=0)
    emit(_dot(jnp.concatenate(weights, axis=1), values), carry)


def _sb_prompt_kernel(q_ref, k_ref, v_ref, o_ref, acc_ref, carry_ref):
    qi = pl.program_id(2)
    blk = SB_BLOCK
    pairs = range(SB_PAIRS_PER_STEP)
    lanes = lambda p: slice(p * HEAD_PAIR, (p + 1) * HEAD_PAIR)
    q2 = [_stack_heads(q_ref[0, :, lanes(p)]) for p in pairs]
    neg_tri = _neg_lower(blk)

    def key_block(p, j, causal):
        r0 = pl.multiple_of(j * blk, blk)
        return (k_ref[0, pl.ds(r0, blk), lanes(p)], _stack_heads(v_ref[0, pl.ds(r0, blk), lanes(p)]),
                neg_tri, causal)

    def sweep(block_ids, first):
        def emit(p, out, carry):
            if first:
                acc_ref[:, lanes(p)] = out
            else:
                acc_ref[:, lanes(p)] += out
            carry_ref[p] = carry

        _interleave([_sb_pair_stages(q2[p], [key_block(p, j, causal) for j, causal in block_ids],
                                     jnp.zeros((2 * blk, 1), F32) if first else carry_ref[p],
                                     functools.partial(emit, p)) for p in pairs], SB_PAIR_LAG)

    def alive():
        return (jnp.max(carry_ref[...]) > SB_SKIP_BELOW).astype(jnp.int32)

    diagonal = (qi, _causal_mask(blk, blk))
    pl.when(qi == 0)(lambda: sweep([diagonal], True))
    pl.when(qi > 0)(lambda: sweep([diagonal, (qi - 1, None)], True))

    def body(state):
        j, _ = state
        sweep([(j, None)], False)
        return j - 1, alive()

    lax.while_loop(lambda s: jnp.logical_and(s[0] >= 0, s[1] > 0), body, (qi - 2, alive()))
    o_ref[0] = acc_ref[...].astype(o_ref.dtype)


def _sb_prompt(qb, kb, vb):
    b, t, _ = qb.shape
    blk = SB_BLOCK
    width = SB_PAIRS_PER_STEP * HEAD_PAIR
    kv_spec = pl.BlockSpec((1, t, width), lambda bi, hp, qi: (bi, 0, hp))
    q_spec = pl.BlockSpec((1, blk, width), lambda bi, hp, qi: (bi, qi, hp))
    return pl.pallas_call(
        _sb_prompt_kernel,
        grid=(b, SB_WIDTH // width, t // blk),
        in_specs=[q_spec, kv_spec, kv_spec],
        out_specs=q_spec,
        out_shape=jax.ShapeDtypeStruct((b, t, SB_WIDTH), BF16),
        scratch_shapes=[pltpu.VMEM((blk, width), F32), pltpu.VMEM((SB_PAIRS_PER_STEP, 2 * blk, 1), F32)],
        compiler_params=pltpu.CompilerParams(dimension_semantics=("parallel", "parallel", "arbitrary"),
                                             vmem_limit_bytes=VMEM_LIMIT_BYTES),
        name="sb_prompt",
    )(qb, kb, vb)


def _sb_heads_block(q, keys, values, carry, neg_tri, causal, feature_major):
    t = q[0].shape[0]
    score, mix = (_dot, _dot_nt) if feature_major else (_dot_nt, _dot)
    z = jnp.concatenate([score(qh, kh) for qh, kh in zip(q, keys)], axis=0)
    if causal is not None:
        z = jnp.where(causal, z, -jnp.inf)
    sp = jnp.maximum(z, 0.0) + jnp.log(1.0 + jnp.exp2(-jnp.abs(z))) * LOG2E
    a = jnp.exp2(z + _dot(sp.astype(BF16), neg_tri) + carry).astype(BF16)
    outs = [mix(a[h * t:(h + 1) * t], vh) for h, vh in enumerate(values)]
    return outs, carry - jnp.sum(sp, axis=1, keepdims=True)


def _sb_sample_kernel(q_ref, k_ref, v_ref, ck_hbm, cv_hbm, o_ref, kbuf, vbuf, acc_ref, carry_ref, sem,
                      *, layer, past_len):
    stream = pl.program_id(0)
    slot = stream % 2
    blk = SB_BLOCK
    t = q_ref.shape[1]
    heads = range(SB_HEADS)
    cols = lambda h: slice(h * SB_HEAD_DIM, (h + 1) * SB_HEAD_DIM)
    q = [q_ref[0, :, cols(h)] for h in heads]
    last = past_len // blk - 1

    def fetch(s, j):
        tokens = pl.ds(pl.multiple_of(j * blk, blk), blk)
        return (pltpu.make_async_copy(ck_hbm.at[layer, s, :, :, tokens], kbuf.at[s % 2], sem.at[s % 2, 0]),
                pltpu.make_async_copy(cv_hbm.at[layer, s, :, :, tokens], vbuf.at[s % 2], sem.at[s % 2, 1]))

    def start(copies):
        for c in copies:
            c.start()

    def wait(copies):
        for c in copies:
            c.wait()

    def accumulate(outs, carry, first):
        for h in heads:
            if first:
                acc_ref[:, cols(h)] = outs[h]
            else:
                acc_ref[:, cols(h)] += outs[h]
        carry_ref[...] = carry

    def cached_block():
        outs, carry = _sb_heads_block(q, [kbuf[slot, h].astype(BF16) for h in heads],
                                      [vbuf[slot, h].astype(BF16) for h in heads],
                                      carry_ref[...], _neg_lower(blk), None, True)
        accumulate(outs, carry, False)

    def alive():
        return (jnp.max(carry_ref[...]) > SB_SKIP_BELOW).astype(jnp.int32)

    pl.when(stream == 0)(lambda: start(fetch(stream, last)))
    pl.when(stream + 1 < pl.num_programs(0))(lambda: start(fetch(stream + 1, last)))
    r = lax.broadcasted_iota(jnp.int32, (SB_HEADS * t, t), 0)
    causal = lax.broadcasted_iota(jnp.int32, (SB_HEADS * t, t), 1) < lax.rem(r, t)
    outs, carry = _sb_heads_block(q, [k_ref[0, :, cols(h)] for h in heads], [v_ref[0, :, cols(h)] for h in heads],
                                  jnp.zeros((SB_HEADS * t, 1), F32), _neg_lower(t), causal, False)
    accumulate(outs, carry, True)
    wait(fetch(stream, last))
    cached_block()

    def body(state):
        j, _ = state
        copies = fetch(stream, j)
        start(copies)
        wait(copies)
        cached_block()
        return j - 1, alive()

    lax.while_loop(lambda s: jnp.logical_and(s[0] >= 0, s[1] > 0), body, (last - 1, alive()))
    o_ref[0] = acc_ref[...].astype(o_ref.dtype)


def _sb_sample(qb, kb, vb, cache_k, cache_v, layer):
    b, t, _ = qb.shape
    past_len = cache_k.shape[-1]
    assert past_len % SB_BLOCK == 0
    new_spec = pl.BlockSpec((1, t, SB_WIDTH), lambda bi: (bi, 0, 0))
    hbm_spec = pl.BlockSpec(memory_space=pl.ANY)
    block = (2, SB_HEADS, SB_HEAD_DIM, SB_BLOCK)
    return pl.pallas_call(
        functools.partial(_sb_sample_kernel, layer=layer, past_len=past_len),
        grid=(b,),
        in_specs=[new_spec, new_spec, new_spec, hbm_spec, hbm_spec],
        out_specs=new_spec,
        out_shape=jax.ShapeDtypeStruct((b, t, SB_WIDTH), BF16),
        scratch_shapes=[pltpu.VMEM(block, F32), pltpu.VMEM(block, F32),
                        pltpu.VMEM((t, SB_WIDTH), F32), pltpu.VMEM((SB_HEADS * t, 1), F32),
                        pltpu.SemaphoreType.DMA((2, 2))],
        compiler_params=pltpu.CompilerParams(dimension_semantics=("arbitrary",),
                                             vmem_limit_bytes=VMEM_LIMIT_BYTES),
        name="sb_sample",
    )(qb, kb, vb, cache_k, cache_v)


def _pool_diff(ext, first_pos):
    rows = ext.shape[0] - POOL_HALO
    pos = first_pos + lax.broadcasted_iota(jnp.int32, (rows, 1), 0)
    outs = []
    for g, window in enumerate(POOL_WINDOWS):
        cols = ext[:, g * POOL_GROUP_DIM:(g + 1) * POOL_GROUP_DIM]
        acc = cols
        shift = 1
        while shift < window:
            acc = acc + pltpu.roll(acc, shift, axis=0)
            shift *= 2
        inv_count = 1.0 / jnp.minimum(pos + 1, window).astype(F32)
        outs.append(acc[POOL_HALO:] * inv_count - cols[POOL_HALO:])
    return jnp.concatenate(outs, axis=1)


POST_GROUP_INPUTS = 6
POST_WEIGHTS = 12


def _post_kernel(*refs, steps, modes, final_norm):
    groups = len(steps)
    n_in = POST_GROUP_INPUTS * groups
    weights = refs[n_in:n_in + POST_WEIGHTS]
    ys = refs[n_in + POST_WEIGHTS:]
    _run_group(steps, [functools.partial(_post_tile, *refs[POST_GROUP_INPUTS * g:POST_GROUP_INPUTS * (g + 1)], *weights,
                                         ys[g], first_step=sum(steps[:g]), final_norm=final_norm, **modes[g])
                       for g in range(groups)])


def _post_tile(x_ref, u_ref, halo_ref, o_ref, gate_ref, p_ref,
               wgrp_ref, scale_ref, wpu_ref, wsu_ref, wout_ref,
               gmlp_ref, wup_ref, wdown_ref, gple_ref, wpg_ref, wpp_ref, gfin_ref,
               y_ref, *, first_step, tiles_per_stream, streams_per_tile, past_pos, final_norm):
    tm = u_ref.shape[0]
    chunk = tm // POST_CHAINS

    def pooled(c):
        u = u_ref[c * chunk:(c + 1) * chunk, :]
        if streams_per_tile == 1:
            step = (pl.program_id(0) - first_step) % tiles_per_stream
            if c == 0:
                halo = jnp.where(step == 0, 0.0, halo_ref[...])
            else:
                halo = u_ref[c * chunk - POOL_HALO:c * chunk, :]
            diff = _pool_diff(jnp.concatenate([halo, u], axis=0), past_pos + step * tm + c * chunk)
        else:
            t = tm // streams_per_tile
            per_chunk = streams_per_tile // POST_CHAINS
            diff = jnp.concatenate(
                [_pool_diff(jnp.concatenate([halo_ref[c * per_chunk + s], u[s * t:(s + 1) * t]], axis=0), past_pos)
                 for s in range(per_chunk)], axis=0)
        return diff.astype(BF16)

    def token_chain(c):
        rows = slice(c * chunk, (c + 1) * chunk)
        diff = pooled(c)
        y_pool = jnp.concatenate(
            [_dot(diff[:, g * POOL_GROUP_DIM:(g + 1) * POOL_GROUP_DIM], wgrp_ref[g])
             for g in range(len(POOL_WINDOWS))], axis=1) * scale_ref[...]
        gates = gate_ref[rows, :].astype(F32)
        merged = (gates[:, :D_MODEL] * _dot(y_pool.astype(BF16), wpu_ref[...])
                  + gates[:, D_MODEL:] * _dot(o_ref[rows, :], wsu_ref[...]))
        yield
        x = x_ref[rows, :] + _dot(merged.astype(BF16), wout_ref[...])
        xn = _rms_norm(x, gmlp_ref[...]).astype(BF16)
        yield
        h = jnp.square(jnp.maximum(_dot(xn, wup_ref[...]), 0.0)).astype(BF16)
        yield
        x = x + _dot(h, wdown_ref[...])
        xn = _rms_norm(x, gple_ref[...]).astype(BF16)
        yield
        ple_gate = jax.nn.sigmoid(_dot(xn, wpg_ref[...]))
        x = x + ple_gate * _dot(p_ref[rows, :].astype(BF16), wpp_ref[...])
        if final_norm:
            x = _rms_norm(x, gfin_ref[...])
        y_ref[rows, :] = x

    _interleave([token_chain(c) for c in range(POST_CHAINS)], POST_LAG)


def _post(groups, weights, modes, *, tm, final_norm):
    assert len(weights) == POST_WEIGHTS and all(len(g) == POST_GROUP_INPUTS for g in groups)
    steps = [g[0].shape[0] // tm for g in groups]
    maps = _group_steps(steps, lambda j: j)
    in_specs, out_specs, out_shape = [], [], []
    for (x, *_), mode, m in zip(groups, modes, maps):
        row = lambda width, m=m: pl.BlockSpec((tm, width), lambda i: (m(i), 0))
        if mode["streams_per_tile"] == 1:
            per_tile = tm // POOL_HALO
            halo_spec = pl.BlockSpec((POOL_HALO, POOL_WIDTH),
                                     lambda i, m=m: (jnp.maximum(m(i) * per_tile - 1, 0), 0))
        else:
            halo_spec = pl.BlockSpec((mode["streams_per_tile"], POOL_HALO, POOL_WIDTH), lambda i, m=m: (m(i), 0, 0))
        in_specs += [row(D_MODEL), row(POOL_WIDTH), halo_spec, row(SB_WIDTH), row(2 * D_MODEL), row(PLE_DIM)]
        out_specs.append(row(D_MODEL))
        out_shape.append(jax.ShapeDtypeStruct((x.shape[0], D_MODEL), F32))
    in_specs += [_resident(w.shape) for w in weights]
    return pl.pallas_call(
        functools.partial(_post_kernel, steps=steps, modes=modes, final_norm=final_norm),
        grid=(sum(steps),),
        in_specs=in_specs,
        out_specs=out_specs,
        out_shape=out_shape,
        compiler_params=pltpu.CompilerParams(dimension_semantics=("arbitrary",),
                                             vmem_limit_bytes=VMEM_LIMIT_BYTES),
        name="post",
    )(*[a for g in groups for a in g], *weights)


def kernel(x_prompt, x_sample, cache_k, cache_v, state_pool, p_prompt, p_sample, g_mix, w_in, b_gate, w_pool_grp, pool_scale, w_pool_up, w_sb_up, w_out, g_mlp, w_up, w_down, g_ple, w_ple_gate, w_ple_proj, g_final):
    depth = w_in.shape[0]
    bp, tp, _ = x_prompt.shape
    bs, ts, _ = x_sample.shape
    past_len = cache_k.shape[2]
    assert tp % ROW_TILE == 0 and tp % SB_BLOCK == 0
    n_s = bs * ts
    assert ts >= POOL_STATE and ts % 8 == 0 and n_s % ROW_TILE == 0 and (ROW_TILE // ts) % POST_CHAINS == 0
    xp = x_prompt.reshape(bp * tp, D_MODEL)
    xs = x_sample.reshape(n_s, D_MODEL)
    row_vec = lambda a: a.reshape(1, -1)
    cache_rows = lambda c: jnp.transpose(c, (0, 1, 3, 4, 2))

    outs = {name: [] for name in ("kp", "vp", "pp", "ks", "vs", "ps")}
    for d in range(depth):
        final_norm = d == depth - 1

        (((up, kp, vp, qbp, kbp, vbp, gates_p), (us, ks, vs, qbs, kbs, vbs, gates_s)),
         (wpu_bf, wsu_bf, wout_bf, wup_bf, wdown_bf, wpg_bf), (wpp_bf, wgrp_bf)) = _inproj(
            [xp, xs], g_mix[d], w_in[d].astype(BF16), b_gate[d],
            [w_pool_up[d], w_sb_up[d], w_out[d], w_up[d], w_down[d], w_ple_gate[d]],
            [w_ple_proj[d], w_pool_grp[d]], ROW_TILE)
        weights = (wgrp_bf, row_vec(pool_scale[d]), wpu_bf, wsu_bf, wout_bf, row_vec(g_mlp[d]), wup_bf,
                   wdown_bf, row_vec(g_ple[d]), wpg_bf, wpp_bf, row_vec(g_final))
        op = _sb_prompt(qbp.reshape(bp, tp, SB_WIDTH), kbp.reshape(bp, tp, SB_WIDTH), vbp.reshape(bp, tp, SB_WIDTH))
        os_ = _sb_sample(qbs.reshape(bs, ts, SB_WIDTH), kbs.reshape(bs, ts, SB_WIDTH), vbs.reshape(bs, ts, SB_WIDTH),
                         cache_rows(cache_k), cache_rows(cache_v), d)
        halo_s = jnp.pad(state_pool[d], ((0, 0), (POOL_HALO - POOL_STATE, 0), (0, 0)))
        xp, = _post([(xp, up, up, op.reshape(bp * tp, SB_WIDTH), gates_p, p_prompt[d].reshape(bp * tp, PLE_DIM))],
                    weights, [dict(tiles_per_stream=tp // ROW_TILE, streams_per_tile=1, past_pos=0)],
                    tm=ROW_TILE, final_norm=final_norm)
        xs, = _post([(xs, us, halo_s, os_.reshape(n_s, SB_WIDTH), gates_s, p_sample[d].reshape(n_s, PLE_DIM))],
                    weights, [dict(tiles_per_stream=1, streams_per_tile=ROW_TILE // ts, past_pos=POOL_STATE)],
                    tm=ROW_TILE, final_norm=final_norm)
        outs["kp"].append(kp.reshape(bp, tp, SB_HEADS, SB_HEAD_DIM))
        outs["vp"].append(vp.reshape(bp, tp, SB_HEADS, SB_HEAD_DIM))
        outs["pp"].append(up.reshape(bp, tp, POOL_WIDTH)[:, tp - POOL_STATE:])
        outs["ks"].append(ks.reshape(bs, ts, SB_HEADS, SB_HEAD_DIM))
        outs["vs"].append(vs.reshape(bs, ts, SB_HEADS, SB_HEAD_DIM))
        outs["ps"].append(us.reshape(bs, ts, POOL_WIDTH)[:, ts - POOL_STATE:])

    stack = lambda name: jnp.stack(outs[name])
    return (xp.reshape(bp, tp, D_MODEL), xs.reshape(bs, ts, D_MODEL),
            stack("kp"), stack("vp"), stack("pp"), stack("ks"), stack("vs"), stack("ps"))
```

```python
import functools

import jax
import jax.numpy as jnp
from jax import lax
from jax.experimental import pallas as pl
from jax.experimental.pallas import tpu as pltpu

D_MODEL = 1024
POOL_WIDTH = 512
POOL_WINDOWS = (2, 4, 8, 16)
POOL_GROUP_DIM = POOL_WIDTH // len(POOL_WINDOWS)
POOL_STATE = max(POOL_WINDOWS) - 1
POOL_HALO = 16
SB_HEADS = 8
SB_HEAD_DIM = 64
SB_WIDTH = SB_HEADS * SB_HEAD_DIM
HEAD_PAIR = 2 * SB_HEAD_DIM
D_FF = 4 * D_MODEL
PLE_DIM = 256
EPS = 1e-6

V7X_VMEM_BYTES = 64 * 1024 * 1024
VMEM_LIMIT_BYTES = V7X_VMEM_BYTES - 8 * 1024 * 1024

ROW_TILE = 512
INPROJ_CHAINS = 2
POST_CHAINS = 2
POST_LAG = 1
SB_BLOCK = 256
SB_PAIRS_PER_STEP = 4
SB_PAIR_LAG = 1

BF16_SUBLANES = 16
N_WHOLE_CASTS = 2

LOG2E = 1.4426950408889634
SB_SKIP_BELOW = -160.0

BF16 = jnp.bfloat16
F32 = jnp.float32


def _rms_norm(x, g):
    y = x * lax.rsqrt(jnp.mean(x * x, axis=-1, keepdims=True) + EPS)
    return y * g


def _dot(a, b):
    return jnp.dot(a, b, preferred_element_type=F32)


def _dot_nt(a, b):
    return lax.dot_general(a, b, (((1,), (1,)), ((), ())), preferred_element_type=F32)


def _resident(shape):
    return pl.BlockSpec(shape, lambda *_: (0,) * len(shape), pipeline_mode=pl.Buffered(1))


def _interleave(chains, lag):
    waiting, live, tick = list(chains), [], 0
    while waiting or live:
        if waiting and tick % lag == 0:
            live.append(waiting.pop(0))
        live = [c for c in live if next(c, True) is None]
        tick += 1


def _store_head_rows(ref, row0, x):
    rows = x.shape[0]
    for h in range(SB_HEADS):
        ref[pl.ds(row0 * SB_HEADS + h, rows, stride=SB_HEADS), :] = x[:, h * SB_HEAD_DIM:(h + 1) * SB_HEAD_DIM]


def _group_steps(steps, maps_to):
    starts = [sum(steps[:g]) for g in range(len(steps))]
    return [lambda i, s=s, n=n: maps_to(jnp.clip(i - s, 0, n - 1)) for s, n in zip(starts, steps)]


def _run_group(steps, bodies):
    i = pl.program_id(0)
    start = 0
    for n, body in zip(steps, bodies):
        pl.when(jnp.logical_and(i >= start, i < start + n))(body)
        start += n


def _inproj_kernel(*refs, steps, n_cast):
    groups = len(steps)
    g_ref, w_ref, b_ref = refs[groups:groups + 3]
    cast_in = refs[groups + 3:groups + 3 + n_cast]
    outs = refs[groups + 3 + n_cast:]
    cast_out = outs[7 * groups:]
    i = pl.program_id(0)

    @pl.when(i == 0)
    def _():
        for src, dst in zip(cast_in[n_cast - N_WHOLE_CASTS:], cast_out[n_cast - N_WHOLE_CASTS:]):
            dst[...] = src[...].astype(BF16)

    @pl.when(i < steps[0])
    def _():
        for src, dst in zip(cast_in[:n_cast - N_WHOLE_CASTS], cast_out[:n_cast - N_WHOLE_CASTS]):
            dst[...] = src[...].astype(BF16)

    _run_group(steps, [functools.partial(_inproj_tile, refs[g], g_ref, w_ref, b_ref, *outs[7 * g:7 * g + 7])
                       for g in range(groups)])


def _inproj_tile(x_ref, g_ref, w_ref, b_ref, u_ref, k_ref, v_ref, qb_ref, kb_ref, vb_ref, gate_ref):
    chunk = x_ref.shape[0] // INPROJ_CHAINS
    c0, c1, c2, c3 = POOL_WIDTH, POOL_WIDTH + SB_WIDTH, POOL_WIDTH + 2 * SB_WIDTH, POOL_WIDTH + 3 * SB_WIDTH

    def token_chain(c):
        rows = slice(c * chunk, (c + 1) * chunk)
        xn = _rms_norm(x_ref[rows, :], g_ref[...]).astype(BF16)
        yield
        u_ref[rows, :] = _dot(xn, w_ref[:, :c0])
        qb_ref[rows, :] = (_dot(xn, w_ref[:, c0:c1]) * (SB_HEAD_DIM ** -0.5 * LOG2E)).astype(BF16)
        yield
        k = _dot(xn, w_ref[:, c1:c2])
        _store_head_rows(k_ref, c * chunk, k)
        kb_ref[rows, :] = k.astype(BF16)
        yield
        v = _dot(xn, w_ref[:, c2:c3])
        _store_head_rows(v_ref, c * chunk, v)
        vb_ref[rows, :] = v.astype(BF16)
        yield
        gate_ref[rows, :] = jax.nn.sigmoid(_dot(xn, w_ref[:, c3:]) + b_ref[...]).astype(BF16)

    _interleave([token_chain(c) for c in range(INPROJ_CHAINS)], 1)


def _inproj(xs, g_mix, w_in, b_gate, chunked, whole, tm):
    assert len(whole) == N_WHOLE_CASTS
    in_width = w_in.shape[1]
    steps = [x.shape[0] // tm for x in xs]
    maps = _group_steps(steps, lambda j: (j, 0))
    buffers = [dict(pipeline_mode=pl.Buffered(1)) if n == 1 else {} for n in steps]
    in_specs = [pl.BlockSpec((tm, D_MODEL), m, **buf) for m, buf in zip(maps, buffers)]
    in_specs += [_resident((1, D_MODEL)), _resident((D_MODEL, in_width)), _resident((1, 2 * D_MODEL))]
    out_specs, out_shape = [], []
    for x, m, buf in zip(xs, maps, buffers):
        n = x.shape[0]
        row = lambda width: pl.BlockSpec((tm, width), m, **buf)
        head_rows = pl.BlockSpec((tm * SB_HEADS, SB_HEAD_DIM), m, **buf)
        f32_out = jax.ShapeDtypeStruct((n * SB_HEADS, SB_HEAD_DIM), F32)
        bf_out = jax.ShapeDtypeStruct((n, SB_WIDTH), BF16)
        out_specs += [row(POOL_WIDTH), head_rows, head_rows, row(SB_WIDTH), row(SB_WIDTH), row(SB_WIDTH),
                      row(2 * D_MODEL)]
        out_shape += [jax.ShapeDtypeStruct((n, POOL_WIDTH), F32), f32_out, f32_out, bf_out, bf_out, bf_out,
                      jax.ShapeDtypeStruct((n, 2 * D_MODEL), BF16)]
    cast_specs = []
    for w in chunked:
        rows = w.shape[0] // steps[0]
        assert rows * steps[0] == w.shape[0] and rows % BF16_SUBLANES == 0
        cast_specs.append(pl.BlockSpec((rows, w.shape[1]), maps[0]))
    cast_specs += [pl.BlockSpec(w.shape, lambda i, nd=w.ndim: (0,) * nd) for w in whole]
    cast_shape = [jax.ShapeDtypeStruct(w.shape, BF16) for w in (*chunked, *whole)]
    outs = pl.pallas_call(
        functools.partial(_inproj_kernel, steps=steps, n_cast=len(cast_specs)),
        grid=(sum(steps),),
        in_specs=in_specs + cast_specs,
        out_specs=out_specs + cast_specs,
        out_shape=out_shape + cast_shape,
        compiler_params=pltpu.CompilerParams(dimension_semantics=("arbitrary",),
                                             vmem_limit_bytes=VMEM_LIMIT_BYTES),
        name="inproj",
    )(*xs, g_mix.reshape(1, D_MODEL), w_in, b_gate.reshape(1, 2 * D_MODEL), *chunked, *whole)
    n_group_outs = 7 * len(xs)
    return ([outs[7 * g:7 * g + 7] for g in range(len(xs))],
            outs[n_group_outs:n_group_outs + len(chunked)], outs[n_group_outs + len(chunked):])


def _neg_lower(n):
    r = lax.broadcasted_iota(jnp.int32, (n, n), 0)
    c = lax.broadcasted_iota(jnp.int32, (n, n), 1)
    return jnp.where(r >= c, -1.0, 0.0).astype(BF16)


def _stack_heads(x):
    lane = lax.broadcasted_iota(jnp.int32, x.shape, 1)
    zero = jnp.zeros_like(x)
    return jnp.concatenate([jnp.where(lane < SB_HEAD_DIM, x, zero), jnp.where(lane >= SB_HEAD_DIM, x, zero)], axis=0)


def _causal_mask(tq, tk):
    r = lax.broadcasted_iota(jnp.int32, (2 * tq, tk), 0)
    c = lax.broadcasted_iota(jnp.int32, (2 * tq, tk), 1)
    return c < jnp.where(r >= tq, r - tq, r)


def _sb_pair_stages(q2, blocks, carry, emit):
    tq = q2.shape[0] // 2
    logits = []
    for k_blk, _, _, causal in blocks:
        z = _dot_nt(q2, k_blk)
        if causal is not None:
            z = jnp.where(causal, z, -jnp.inf)
        logits.append(z)
    yield
    softplus = [jnp.maximum(z, 0.0) + jnp.log(1.0 + jnp.exp2(-jnp.abs(z))) * LOG2E for z in logits]
    yield
    sums = [_dot(sp.astype(BF16), neg_tri) for (_, _, neg_tri, _), sp in zip(blocks, softplus)]
    yield
    weights = []
    for z, incl in zip(logits, sums):
        a = jnp.exp2(z + incl + carry).astype(BF16)
        weights += [a[:tq], a[tq:]]
        carry = carry + incl[:, :1]
    peak = jnp.max(carry)
    yield
    values = jnp.concatenate([v2 for _, v2, _, _ in blocks], axis=0)
    emit(_dot(jnp.concatenate(weights, axis=1), values), carry, peak)


def _sb_prompt_kernel(q_ref, k_ref, v_ref, o_ref, acc_ref, carry_ref, peak_ref):
    qi = pl.program_id(2)
    blk = SB_BLOCK
    pairs = range(SB_PAIRS_PER_STEP)
    lanes = lambda p: slice(p * HEAD_PAIR, (p + 1) * HEAD_PAIR)
    q2 = [_stack_heads(q_ref[0, :, lanes(p)]) for p in pairs]
    neg_tri = _neg_lower(blk)

    def key_block(p, j, causal):
        r0 = pl.multiple_of(j * blk, blk)
        return (k_ref[0, pl.ds(r0, blk), lanes(p)], _stack_heads(v_ref[0, pl.ds(r0, blk), lanes(p)]),
                neg_tri, causal)

    def sweep(block_ids, first):
        def emit(p, out, carry, peak):
            if first:
                acc_ref[:, lanes(p)] = out
            else:
                acc_ref[:, lanes(p)] += out
            carry_ref[p] = carry
            peak_ref[p] = peak

        _interleave([_sb_pair_stages(q2[p], [key_block(p, j, causal) for j, causal in block_ids],
                                     jnp.zeros((2 * blk, 1), F32) if first else carry_ref[p],
                                     functools.partial(emit, p)) for p in pairs], SB_PAIR_LAG)

    def alive():
        peak = functools.reduce(jnp.maximum, [peak_ref[p] for p in pairs])
        return (peak > SB_SKIP_BELOW).astype(jnp.int32)

    diagonal = (qi, _causal_mask(blk, blk))
    pl.when(qi == 0)(lambda: sweep([diagonal], True))
    pl.when(qi > 0)(lambda: sweep([diagonal, (qi - 1, None)], True))

    def body(state):
        j, _ = state
        sweep([(j, None)], False)
        return j - 1, alive()

    lax.while_loop(lambda s: jnp.logical_and(s[0] >= 0, s[1] > 0), body, (qi - 2, alive()))
    o_ref[0] = acc_ref[...].astype(o_ref.dtype)


def _sb_prompt(qb, kb, vb):
    b, t, _ = qb.shape
    blk = SB_BLOCK
    width = SB_PAIRS_PER_STEP * HEAD_PAIR
    kv_spec = pl.BlockSpec((1, t, width), lambda bi, hp, qi: (bi, 0, hp))
    q_spec = pl.BlockSpec((1, blk, width), lambda bi, hp, qi: (bi, qi, hp))
    return pl.pallas_call(
        _sb_prompt_kernel,
        grid=(b, SB_WIDTH // width, t // blk),
        in_specs=[q_spec, kv_spec, kv_spec],
        out_specs=q_spec,
        out_shape=jax.ShapeDtypeStruct((b, t, SB_WIDTH), BF16),
        scratch_shapes=[pltpu.VMEM((blk, width), F32), pltpu.VMEM((SB_PAIRS_PER_STEP, 2 * blk, 1), F32),
                        pltpu.SMEM((SB_PAIRS_PER_STEP,), F32)],
        compiler_params=pltpu.CompilerParams(dimension_semantics=("parallel", "parallel", "arbitrary"),
                                             vmem_limit_bytes=VMEM_LIMIT_BYTES),
        name="sb_prompt",
    )(qb, kb, vb)


def _sb_heads_block(q, keys, values, carry, neg_tri, causal, feature_major):
    t = q[0].shape[0]
    score, mix = (_dot, _dot_nt) if feature_major else (_dot_nt, _dot)
    z = jnp.concatenate([score(qh, kh) for qh, kh in zip(q, keys)], axis=0)
    if causal is not None:
        z = jnp.where(causal, z, -jnp.inf)
    sp = jnp.maximum(z, 0.0) + jnp.log(1.0 + jnp.exp2(-jnp.abs(z))) * LOG2E
    incl = _dot(sp.astype(BF16), neg_tri)
    a = jnp.exp2(z + incl + carry).astype(BF16)
    carry = carry + incl[:, :1]
    peak = jnp.max(carry)
    outs = [mix(a[h * t:(h + 1) * t], vh) for h, vh in enumerate(values)]
    return outs, carry, peak


def _sb_sample_kernel(q_ref, k_ref, v_ref, ck_hbm, cv_hbm, o_ref, kbuf, vbuf, acc_ref, carry_ref, sem,
                      *, layer, past_len):
    stream = pl.program_id(0)
    slot = stream % 2
    blk = SB_BLOCK
    t = q_ref.shape[1]
    heads = range(SB_HEADS)
    cols = lambda h: slice(h * SB_HEAD_DIM, (h + 1) * SB_HEAD_DIM)
    q = [q_ref[0, :, cols(h)] for h in heads]
    last = past_len // blk - 1

    def fetch(s, j):
        tokens = pl.ds(pl.multiple_of(j * blk, blk), blk)
        return (pltpu.make_async_copy(ck_hbm.at[layer, s, :, :, tokens], kbuf.at[s % 2], sem.at[s % 2, 0]),
                pltpu.make_async_copy(cv_hbm.at[layer, s, :, :, tokens], vbuf.at[s % 2], sem.at[s % 2, 1]))

    def start(copies):
        for c in copies:
            c.start()

    def wait(copies):
        for c in copies:
            c.wait()

    def accumulate(outs, carry, first):
        for h in heads:
            if first:
                acc_ref[:, cols(h)] = outs[h]
            else:
                acc_ref[:, cols(h)] += outs[h]
        carry_ref[...] = carry

    def cached_block():
        outs, carry, peak = _sb_heads_block(q, [kbuf[slot, h].astype(BF16) for h in heads],
                                            [vbuf[slot, h].astype(BF16) for h in heads],
                                            carry_ref[...], _neg_lower(blk), None, True)
        accumulate(outs, carry, False)
        return (peak > SB_SKIP_BELOW).astype(jnp.int32)

    pl.when(stream == 0)(lambda: start(fetch(stream, last)))
    pl.when(stream + 1 < pl.num_programs(0))(lambda: start(fetch(stream + 1, last)))
    r = lax.broadcasted_iota(jnp.int32, (SB_HEADS * t, t), 0)
    causal = lax.broadcasted_iota(jnp.int32, (SB_HEADS * t, t), 1) < lax.rem(r, t)
    outs, carry, _ = _sb_heads_block(q, [k_ref[0, :, cols(h)] for h in heads], [v_ref[0, :, cols(h)] for h in heads],
                                     jnp.zeros((SB_HEADS * t, 1), F32), _neg_lower(t), causal, False)
    accumulate(outs, carry, True)
    wait(fetch(stream, last))
    alive = cached_block()

    def body(state):
        j, _ = state
        copies = fetch(stream, j)
        start(copies)
        wait(copies)
        return j - 1, cached_block()

    lax.while_loop(lambda s: jnp.logical_and(s[0] >= 0, s[1] > 0), body, (last - 1, alive))
    o_ref[0] = acc_ref[...].astype(o_ref.dtype)


def _sb_sample(qb, kb, vb, cache_k, cache_v, layer):
    b, t, _ = qb.shape
    past_len = cache_k.shape[-1]
    assert past_len % SB_BLOCK == 0
    new_spec = pl.BlockSpec((1, t, SB_WIDTH), lambda bi: (bi, 0, 0))
    hbm_spec = pl.BlockSpec(memory_space=pl.ANY)
    block = (2, SB_HEADS, SB_HEAD_DIM, SB_BLOCK)
    return pl.pallas_call(
        functools.partial(_sb_sample_kernel, layer=layer, past_len=past_len),
        grid=(b,),
        in_specs=[new_spec, new_spec, new_spec, hbm_spec, hbm_spec],
        out_specs=new_spec,
        out_shape=jax.ShapeDtypeStruct((b, t, SB_WIDTH), BF16),
        scratch_shapes=[pltpu.VMEM(block, F32), pltpu.VMEM(block, F32),
                        pltpu.VMEM((t, SB_WIDTH), F32), pltpu.VMEM((SB_HEADS * t, 1), F32),
                        pltpu.SemaphoreType.DMA((2, 2))],
        compiler_params=pltpu.CompilerParams(dimension_semantics=("arbitrary",),
                                             vmem_limit_bytes=VMEM_LIMIT_BYTES),
        name="sb_sample",
    )(qb, kb, vb, cache_k, cache_v)


def _pool_diff(ext, first_pos):
    rows = ext.shape[0] - POOL_HALO
    pos = first_pos + lax.broadcasted_iota(jnp.int32, (rows, 1), 0)
    outs = []
    for g, window in enumerate(POOL_WINDOWS):
        cols = ext[:, g * POOL_GROUP_DIM:(g + 1) * POOL_GROUP_DIM]
        acc = cols
        shift = 1
        while shift < window:
            acc = acc + pltpu.roll(acc, shift, axis=0)
            shift *= 2
        inv_count = 1.0 / jnp.minimum(pos + 1, window).astype(F32)
        outs.append(acc[POOL_HALO:] * inv_count - cols[POOL_HALO:])
    return jnp.concatenate(outs, axis=1)


POST_GROUP_INPUTS = 6
POST_WEIGHTS = 12


def _post_kernel(*refs, steps, modes, final_norm):
    groups = len(steps)
    n_in = POST_GROUP_INPUTS * groups
    weights = refs[n_in:n_in + POST_WEIGHTS]
    ys = refs[n_in + POST_WEIGHTS:]
    _run_group(steps, [functools.partial(_post_tile, *refs[POST_GROUP_INPUTS * g:POST_GROUP_INPUTS * (g + 1)], *weights,
                                         ys[g], first_step=sum(steps[:g]), final_norm=final_norm, **modes[g])
                       for g in range(groups)])


def _post_tile(x_ref, u_ref, halo_ref, o_ref, gate_ref, p_ref,
               wgrp_ref, scale_ref, wpu_ref, wsu_ref, wout_ref,
               gmlp_ref, wup_ref, wdown_ref, gple_ref, wpg_ref, wpp_ref, gfin_ref,
               y_ref, *, first_step, tiles_per_stream, streams_per_tile, past_pos, final_norm):
    tm = u_ref.shape[0]
    chunk = tm // POST_CHAINS

    def pooled(c):
        u = u_ref[c * chunk:(c + 1) * chunk, :]
        if streams_per_tile == 1:
            step = (pl.program_id(0) - first_step) % tiles_per_stream
            if c == 0:
                halo = jnp.where(step == 0, 0.0, halo_ref[...])
            else:
                halo = u_ref[c * chunk - POOL_HALO:c * chunk, :]
            diff = _pool_diff(jnp.concatenate([halo, u], axis=0), past_pos + step * tm + c * chunk)
        else:
            t = tm // streams_per_tile
            per_chunk = streams_per_tile // POST_CHAINS
            diff = jnp.concatenate(
                [_pool_diff(jnp.concatenate([halo_ref[c * per_chunk + s], u[s * t:(s + 1) * t]], axis=0), past_pos)
                 for s in range(per_chunk)], axis=0)
        return diff.astype(BF16)

    def token_chain(c):
        rows = slice(c * chunk, (c + 1) * chunk)
        diff = pooled(c)
        y_pool = jnp.concatenate(
            [_dot(diff[:, g * POOL_GROUP_DIM:(g + 1) * POOL_GROUP_DIM], wgrp_ref[g])
             for g in range(len(POOL_WINDOWS))], axis=1) * scale_ref[...]
        gates = gate_ref[rows, :].astype(F32)
        merged = (gates[:, :D_MODEL] * _dot(y_pool.astype(BF16), wpu_ref[...])
                  + gates[:, D_MODEL:] * _dot(o_ref[rows, :], wsu_ref[...]))
        yield
        x = x_ref[rows, :] + _dot(merged.astype(BF16), wout_ref[...])
        xn = _rms_norm(x, gmlp_ref[...]).astype(BF16)
        yield
        h = jnp.square(jnp.maximum(_dot(xn, wup_ref[...]), 0.0)).astype(BF16)
        yield
        x = x + _dot(h, wdown_ref[...])
        xn = _rms_norm(x, gple_ref[...]).astype(BF16)
        yield
        ple_gate = jax.nn.sigmoid(_dot(xn, wpg_ref[...]))
        x = x + ple_gate * _dot(p_ref[rows, :].astype(BF16), wpp_ref[...])
        if final_norm:
            x = _rms_norm(x, gfin_ref[...])
        y_ref[rows, :] = x

    _interleave([token_chain(c) for c in range(POST_CHAINS)], POST_LAG)


def _post(groups, weights, modes, *, tm, final_norm):
    assert len(weights) == POST_WEIGHTS and all(len(g) == POST_GROUP_INPUTS for g in groups)
    steps = [g[0].shape[0] // tm for g in groups]
    maps = _group_steps(steps, lambda j: j)
    in_specs, out_specs, out_shape = [], [], []
    for (x, *_), mode, m in zip(groups, modes, maps):
        row = lambda width, m=m: pl.BlockSpec((tm, width), lambda i: (m(i), 0))
        if mode["streams_per_tile"] == 1:
            per_tile = tm // POOL_HALO
            halo_spec = pl.BlockSpec((POOL_HALO, POOL_WIDTH),
                                     lambda i, m=m: (jnp.maximum(m(i) * per_tile - 1, 0), 0))
        else:
            halo_spec = pl.BlockSpec((mode["streams_per_tile"], POOL_HALO, POOL_WIDTH), lambda i, m=m: (m(i), 0, 0))
        in_specs += [row(D_MODEL), row(POOL_WIDTH), halo_spec, row(SB_WIDTH), row(2 * D_MODEL), row(PLE_DIM)]
        out_specs.append(row(D_MODEL))
        out_shape.append(jax.ShapeDtypeStruct((x.shape[0], D_MODEL), F32))
    in_specs += [_resident(w.shape) for w in weights]
    return pl.pallas_call(
        functools.partial(_post_kernel, steps=steps, modes=modes, final_norm=final_norm),
        grid=(sum(steps),),
        in_specs=in_specs,
        out_specs=out_specs,
        out_shape=out_shape,
        compiler_params=pltpu.CompilerParams(dimension_semantics=("arbitrary",),
                                             vmem_limit_bytes=VMEM_LIMIT_BYTES),
        name="post",
    )(*[a for g in groups for a in g], *weights)


def kernel(x_prompt, x_sample, cache_k, cache_v, state_pool, p_prompt, p_sample, g_mix, w_in, b_gate, w_pool_grp, pool_scale, w_pool_up, w_sb_up, w_out, g_mlp, w_up, w_down, g_ple, w_ple_gate, w_ple_proj, g_final):
    depth = w_in.shape[0]
    bp, tp, _ = x_prompt.shape
    bs, ts, _ = x_sample.shape
    past_len = cache_k.shape[2]
    assert tp % ROW_TILE == 0 and tp % SB_BLOCK == 0
    n_s = bs * ts
    assert ts >= POOL_STATE and ts % 8 == 0 and n_s % ROW_TILE == 0 and (ROW_TILE // ts) % POST_CHAINS == 0
    xp = x_prompt.reshape(bp * tp, D_MODEL)
    xs = x_sample.reshape(n_s, D_MODEL)
    row_vec = lambda a: a.reshape(1, -1)
    cache_rows = lambda c: jnp.transpose(c, (0, 1, 3, 4, 2))

    outs = {name: [] for name in ("kp", "vp", "pp", "ks", "vs", "ps")}
    for d in range(depth):
        final_norm = d == depth - 1

        (((up, kp, vp, qbp, kbp, vbp, gates_p), (us, ks, vs, qbs, kbs, vbs, gates_s)),
         (wpu_bf, wsu_bf, wout_bf, wup_bf, wdown_bf, wpg_bf), (wpp_bf, wgrp_bf)) = _inproj(
            [xp, xs], g_mix[d], w_in[d].astype(BF16), b_gate[d],
            [w_pool_up[d], w_sb_up[d], w_out[d], w_up[d], w_down[d], w_ple_gate[d]],
            [w_ple_proj[d], w_pool_grp[d]], ROW_TILE)
        weights = (wgrp_bf, row_vec(pool_scale[d]), wpu_bf, wsu_bf, wout_bf, row_vec(g_mlp[d]), wup_bf,
                   wdown_bf, row_vec(g_ple[d]), wpg_bf, wpp_bf, row_vec(g_final))
        op = _sb_prompt(qbp.reshape(bp, tp, SB_WIDTH), kbp.reshape(bp, tp, SB_WIDTH), vbp.reshape(bp, tp, SB_WIDTH))
        os_ = _sb_sample(qbs.reshape(bs, ts, SB_WIDTH), kbs.reshape(bs, ts, SB_WIDTH), vbs.reshape(bs, ts, SB_WIDTH),
                         cache_rows(cache_k), cache_rows(cache_v), d)
        halo_s = jnp.pad(state_pool[d], ((0, 0), (POOL_HALO - POOL_STATE, 0), (0, 0)))
        xp, = _post([(xp, up, up, op.reshape(bp * tp, SB_WIDTH), gates_p, p_prompt[d].reshape(bp * tp, PLE_DIM))],
                    weights, [dict(tiles_per_stream=tp // ROW_TILE, streams_per_tile=1, past_pos=0)],
                    tm=ROW_TILE, final_norm=final_norm)
        xs, = _post([(xs, us, halo_s, os_.reshape(n_s, SB_WIDTH), gates_s, p_sample[d].reshape(n_s, PLE_DIM))],
                    weights, [dict(tiles_per_stream=1, streams_per_tile=ROW_TILE // ts, past_pos=POOL_STATE)],
                    tm=ROW_TILE, final_norm=final_norm)
        outs["kp"].append(kp.reshape(bp, tp, SB_HEADS, SB_HEAD_DIM))
        outs["vp"].append(vp.reshape(bp, tp, SB_HEADS, SB_HEAD_DIM))
        outs["pp"].append(up.reshape(bp, tp, POOL_WIDTH)[:, tp - POOL_STATE:])
        outs["ks"].append(ks.reshape(bs, ts, SB_HEADS, SB_HEAD_DIM))
        outs["vs"].append(vs.reshape(bs, ts, SB_HEADS, SB_HEAD_DIM))
        outs["ps"].append(us.reshape(bs, ts, POOL_WIDTH)[:, ts - POOL_STATE:])

    stack = lambda name: jnp.stack(outs[name])
    return (xp.reshape(bp, tp, D_MODEL), xs.reshape(bs, ts, D_MODEL),
            stack("kp"), stack("vp"), stack("pp"), stack("ks"), stack("vs"), stack("ps"))
```

```python
import functools

import jax
import jax.numpy as jnp
from jax import lax
from jax.experimental import pallas as pl
from jax.experimental.pallas import tpu as pltpu

D_MODEL = 1024
POOL_WIDTH = 512
POOL_WINDOWS = (2, 4, 8, 16)
POOL_GROUP_DIM = POOL_WIDTH // len(POOL_WINDOWS)
POOL_STATE = max(POOL_WINDOWS) - 1
POOL_HALO = 16
SB_HEADS = 8
SB_HEAD_DIM = 64
SB_WIDTH = SB_HEADS * SB_HEAD_DIM
HEAD_PAIR = 2 * SB_HEAD_DIM
D_FF = 4 * D_MODEL
PLE_DIM = 256
EPS = 1e-6

V7X_VMEM_BYTES = 64 * 1024 * 1024
VMEM_LIMIT_BYTES = V7X_VMEM_BYTES - 8 * 1024 * 1024

ROW_TILE = 512
INPROJ_CHAINS = 2
POST_CHAINS = 2
POST_LAG = 1
SB_BLOCK = 256
SB_PAIRS_PER_STEP = 4
SB_PAIR_LAG = 1

BF16_SUBLANES = 16
N_WHOLE_CASTS = 2

LOG2E = 1.4426950408889634
SB_SKIP_BELOW = -160.0

BF16 = jnp.bfloat16
F32 = jnp.float32


def _rms_norm(x, g):
    y = x * lax.rsqrt(jnp.mean(x * x, axis=-1, keepdims=True) + EPS)
    return y * g


def _dot(a, b):
    return jnp.dot(a, b, preferred_element_type=F32)


def _dot_nt(a, b):
    return lax.dot_general(a, b, (((1,), (1,)), ((), ())), preferred_element_type=F32)


def _resident(shape):
    return pl.BlockSpec(shape, lambda *_: (0,) * len(shape), pipeline_mode=pl.Buffered(1))


def _interleave(chains, lag):
    waiting, live, tick = list(chains), [], 0
    while waiting or live:
        if waiting and tick % lag == 0:
            live.append(waiting.pop(0))
        live = [c for c in live if next(c, True) is None]
        tick += 1


def _store_head_rows(ref, row0, x):
    rows = x.shape[0]
    for h in range(SB_HEADS):
        ref[pl.ds(row0 * SB_HEADS + h, rows, stride=SB_HEADS), :] = x[:, h * SB_HEAD_DIM:(h + 1) * SB_HEAD_DIM]


def _group_steps(steps, maps_to):
    starts = [sum(steps[:g]) for g in range(len(steps))]
    return [lambda i, s=s, n=n: maps_to(jnp.clip(i - s, 0, n - 1)) for s, n in zip(starts, steps)]


def _run_group(steps, bodies):
    i = pl.program_id(0)
    start = 0
    for n, body in zip(steps, bodies):
        pl.when(jnp.logical_and(i >= start, i < start + n))(body)
        start += n


def _inproj_kernel(*refs, steps, n_cast, feature_major):
    groups = len(steps)
    g_ref, w_ref, b_ref = refs[groups:groups + 3]
    cast_in = refs[groups + 3:groups + 3 + n_cast]
    outs = refs[groups + 3 + n_cast:]
    cast_out = outs[7 * groups:]
    i = pl.program_id(0)

    @pl.when(i == 0)
    def _():
        for src, dst in zip(cast_in[n_cast - N_WHOLE_CASTS:], cast_out[n_cast - N_WHOLE_CASTS:]):
            dst[...] = src[...].astype(BF16)

    @pl.when(i < steps[0])
    def _():
        for src, dst in zip(cast_in[:n_cast - N_WHOLE_CASTS], cast_out[:n_cast - N_WHOLE_CASTS]):
            dst[...] = src[...].astype(BF16)

    _run_group(steps, [functools.partial(_inproj_tile, refs[g], g_ref, w_ref, b_ref, *outs[7 * g:7 * g + 7],
                                         feature_major=feature_major[g]) for g in range(groups)])


def _inproj_tile(x_ref, g_ref, w_ref, b_ref, u_ref, k_ref, v_ref, qb_ref, kb_ref, vb_ref, gate_ref, *, feature_major):
    chunk = x_ref.shape[0] // INPROJ_CHAINS

    def store_kv(ref, c, x):
        if feature_major:
            ref[0, :, c * chunk:(c + 1) * chunk] = x.T
        else:
            _store_head_rows(ref, c * chunk, x)

    c0, c1, c2, c3 = POOL_WIDTH, POOL_WIDTH + SB_WIDTH, POOL_WIDTH + 2 * SB_WIDTH, POOL_WIDTH + 3 * SB_WIDTH

    def token_chain(c):
        rows = slice(c * chunk, (c + 1) * chunk)
        xn = _rms_norm(x_ref[rows, :], g_ref[...]).astype(BF16)
        yield
        u_ref[rows, :] = _dot(xn, w_ref[:, :c0])
        qb_ref[rows, :] = (_dot(xn, w_ref[:, c0:c1]) * (SB_HEAD_DIM ** -0.5 * LOG2E)).astype(BF16)
        yield
        k = _dot(xn, w_ref[:, c1:c2])
        store_kv(k_ref, c, k)
        kb_ref[rows, :] = k.astype(BF16)
        yield
        v = _dot(xn, w_ref[:, c2:c3])
        store_kv(v_ref, c, v)
        vb_ref[rows, :] = v.astype(BF16)
        yield
        gate_ref[rows, :] = jax.nn.sigmoid(_dot(xn, w_ref[:, c3:]) + b_ref[...]).astype(BF16)

    _interleave([token_chain(c) for c in range(INPROJ_CHAINS)], 1)


def _inproj(xs, stream_tiles, g_mix, w_in, b_gate, chunked, whole, tm):
    assert len(whole) == N_WHOLE_CASTS
    in_width = w_in.shape[1]
    steps = [x.shape[0] // tm for x in xs]
    maps = _group_steps(steps, lambda j: (j, 0))
    buffers = [dict(pipeline_mode=pl.Buffered(1)) if n == 1 else {} for n in steps]
    in_specs = [pl.BlockSpec((tm, D_MODEL), m, **buf) for m, buf in zip(maps, buffers)]
    in_specs += [_resident((1, D_MODEL)), _resident((D_MODEL, in_width)), _resident((1, 2 * D_MODEL))]
    out_specs, out_shape = [], []
    for x, m, buf, tiles in zip(xs, maps, buffers, stream_tiles):
        n = x.shape[0]
        row = lambda width: pl.BlockSpec((tm, width), m, **buf)
        if tiles is None:
            kv_spec = pl.BlockSpec((tm * SB_HEADS, SB_HEAD_DIM), m, **buf)
            kv_shape = jax.ShapeDtypeStruct((n * SB_HEADS, SB_HEAD_DIM), F32)
        else:
            kv_spec = pl.BlockSpec((1, SB_WIDTH, tm), lambda i, m=m, tiles=tiles: (m(i)[0] // tiles, 0, m(i)[0] % tiles),
                                   **buf)
            kv_shape = jax.ShapeDtypeStruct((n // (tiles * tm), SB_WIDTH, tiles * tm), F32)
        bf_out = jax.ShapeDtypeStruct((n, SB_WIDTH), BF16)
        out_specs += [row(POOL_WIDTH), kv_spec, kv_spec, row(SB_WIDTH), row(SB_WIDTH), row(SB_WIDTH),
                      row(2 * D_MODEL)]
        out_shape += [jax.ShapeDtypeStruct((n, POOL_WIDTH), F32), kv_shape, kv_shape, bf_out, bf_out, bf_out,
                      jax.ShapeDtypeStruct((n, 2 * D_MODEL), BF16)]
    cast_specs = []
    for w in chunked:
        rows = w.shape[0] // steps[0]
        assert rows * steps[0] == w.shape[0] and rows % BF16_SUBLANES == 0
        cast_specs.append(pl.BlockSpec((rows, w.shape[1]), maps[0]))
    cast_specs += [pl.BlockSpec(w.shape, lambda i, nd=w.ndim: (0,) * nd) for w in whole]
    cast_shape = [jax.ShapeDtypeStruct(w.shape, BF16) for w in (*chunked, *whole)]
    outs = pl.pallas_call(
        functools.partial(_inproj_kernel, steps=steps, n_cast=len(cast_specs),
                          feature_major=[tiles is not None for tiles in stream_tiles]),
        grid=(sum(steps),),
        in_specs=in_specs + cast_specs,
        out_specs=out_specs + cast_specs,
        out_shape=out_shape + cast_shape,
        compiler_params=pltpu.CompilerParams(dimension_semantics=("arbitrary",),
                                             vmem_limit_bytes=VMEM_LIMIT_BYTES),
        name="inproj",
    )(*xs, g_mix.reshape(1, D_MODEL), w_in, b_gate.reshape(1, 2 * D_MODEL), *chunked, *whole)
    n_group_outs = 7 * len(xs)
    return ([outs[7 * g:7 * g + 7] for g in range(len(xs))],
            outs[n_group_outs:n_group_outs + len(chunked)], outs[n_group_outs + len(chunked):])


def _neg_lower(n):
    r = lax.broadcasted_iota(jnp.int32, (n, n), 0)
    c = lax.broadcasted_iota(jnp.int32, (n, n), 1)
    return jnp.where(r >= c, -1.0, 0.0).astype(BF16)


def _stack_heads(x):
    lane = lax.broadcasted_iota(jnp.int32, x.shape, 1)
    zero = jnp.zeros_like(x)
    return jnp.concatenate([jnp.where(lane < SB_HEAD_DIM, x, zero), jnp.where(lane >= SB_HEAD_DIM, x, zero)], axis=0)


def _causal_mask(tq, tk):
    r = lax.broadcasted_iota(jnp.int32, (2 * tq, tk), 0)
    c = lax.broadcasted_iota(jnp.int32, (2 * tq, tk), 1)
    return c < jnp.where(r >= tq, r - tq, r)


def _sb_pair_stages(q2, blocks, carry, emit):
    tq = q2.shape[0] // 2
    logits = []
    for k_blk, _, _, causal in blocks:
        z = _dot_nt(q2, k_blk)
        if causal is not None:
            z = jnp.where(causal, z, -jnp.inf)
        logits.append(z)
    yield
    softplus = [jnp.maximum(z, 0.0) + jnp.log(1.0 + jnp.exp2(-jnp.abs(z))) * LOG2E for z in logits]
    yield
    sums = [_dot(sp.astype(BF16), neg_tri) for (_, _, neg_tri, _), sp in zip(blocks, softplus)]
    yield
    weights = []
    for z, incl in zip(logits, sums):
        a = jnp.exp2(z + incl + carry).astype(BF16)
        weights += [a[:tq], a[tq:]]
        carry = carry + incl[:, :1]
    peak = jnp.max(carry)
    yield
    values = jnp.concatenate([v2 for _, v2, _, _ in blocks], axis=0)
    emit(_dot(jnp.concatenate(weights, axis=1), values), carry, peak)


def _sb_prompt_kernel(q_ref, k_ref, v_ref, o_ref, acc_ref, carry_ref, peak_ref):
    qi = pl.program_id(2)
    blk = SB_BLOCK
    pairs = range(SB_PAIRS_PER_STEP)
    lanes = lambda p: slice(p * HEAD_PAIR, (p + 1) * HEAD_PAIR)
    q2 = [_stack_heads(q_ref[0, :, lanes(p)]) for p in pairs]
    neg_tri = _neg_lower(blk)

    def key_block(p, j, causal):
        r0 = pl.multiple_of(j * blk, blk)
        return (k_ref[0, pl.ds(r0, blk), lanes(p)], _stack_heads(v_ref[0, pl.ds(r0, blk), lanes(p)]),
                neg_tri, causal)

    def sweep(block_ids, first):
        def emit(p, out, carry, peak):
            if first:
                acc_ref[:, lanes(p)] = out
            else:
                acc_ref[:, lanes(p)] += out
            carry_ref[p] = carry
            peak_ref[p] = peak

        _interleave([_sb_pair_stages(q2[p], [key_block(p, j, causal) for j, causal in block_ids],
                                     jnp.zeros((2 * blk, 1), F32) if first else carry_ref[p],
                                     functools.partial(emit, p)) for p in pairs], SB_PAIR_LAG)

    def alive():
        peak = functools.reduce(jnp.maximum, [peak_ref[p] for p in pairs])
        return (peak > SB_SKIP_BELOW).astype(jnp.int32)

    diagonal = (qi, _causal_mask(blk, blk))
    pl.when(qi == 0)(lambda: sweep([diagonal], True))
    pl.when(qi > 0)(lambda: sweep([diagonal, (qi - 1, None)], True))

    def body(state):
        j, _ = state
        sweep([(j, None)], False)
        return j - 1, alive()

    lax.while_loop(lambda s: jnp.logical_and(s[0] >= 0, s[1] > 0), body, (qi - 2, alive()))
    o_ref[0] = acc_ref[...].astype(o_ref.dtype)


def _sb_prompt(qb, kb, vb):
    b, t, _ = qb.shape
    blk = SB_BLOCK
    width = SB_PAIRS_PER_STEP * HEAD_PAIR
    kv_spec = pl.BlockSpec((1, t, width), lambda bi, hp, qi: (bi, 0, hp))
    q_spec = pl.BlockSpec((1, blk, width), lambda bi, hp, qi: (bi, qi, hp))
    return pl.pallas_call(
        _sb_prompt_kernel,
        grid=(b, SB_WIDTH // width, t // blk),
        in_specs=[q_spec, kv_spec, kv_spec],
        out_specs=q_spec,
        out_shape=jax.ShapeDtypeStruct((b, t, SB_WIDTH), BF16),
        scratch_shapes=[pltpu.VMEM((blk, width), F32), pltpu.VMEM((SB_PAIRS_PER_STEP, 2 * blk, 1), F32),
                        pltpu.SMEM((SB_PAIRS_PER_STEP,), F32)],
        compiler_params=pltpu.CompilerParams(dimension_semantics=("parallel", "parallel", "arbitrary"),
                                             vmem_limit_bytes=VMEM_LIMIT_BYTES),
        name="sb_prompt",
    )(qb, kb, vb)


def _sb_heads_block(q, keys, values, carry, neg_tri, causal, feature_major):
    t = q[0].shape[0]
    score, mix = (_dot, _dot_nt) if feature_major else (_dot_nt, _dot)
    z = jnp.concatenate([score(qh, kh) for qh, kh in zip(q, keys)], axis=0)
    if causal is not None:
        z = jnp.where(causal, z, -jnp.inf)
    sp = jnp.maximum(z, 0.0) + jnp.log(1.0 + jnp.exp2(-jnp.abs(z))) * LOG2E
    incl = _dot(sp.astype(BF16), neg_tri)
    a = jnp.exp2(z + incl + carry).astype(BF16)
    carry = carry + incl[:, :1]
    peak = jnp.max(carry)
    outs = [mix(a[h * t:(h + 1) * t], vh) for h, vh in enumerate(values)]
    return outs, carry, peak


def _sb_sample_kernel(q_ref, k_ref, v_ref, ck_hbm, cv_hbm, o_ref, kbuf, vbuf, acc_ref, carry_ref, sem,
                      *, layer, past_len):
    stream = pl.program_id(0)
    slot = stream % 2
    blk = SB_BLOCK
    t = q_ref.shape[1]
    heads = range(SB_HEADS)
    cols = lambda h: slice(h * SB_HEAD_DIM, (h + 1) * SB_HEAD_DIM)
    q = [q_ref[0, :, cols(h)] for h in heads]
    last = past_len // blk - 1

    def fetch(s, j):
        tokens = pl.ds(pl.multiple_of(j * blk, blk), blk)
        return (pltpu.make_async_copy(ck_hbm.at[layer, s, :, :, tokens], kbuf.at[s % 2], sem.at[s % 2, 0]),
                pltpu.make_async_copy(cv_hbm.at[layer, s, :, :, tokens], vbuf.at[s % 2], sem.at[s % 2, 1]))

    def start(copies):
        for c in copies:
            c.start()

    def wait(copies):
        for c in copies:
            c.wait()

    def accumulate(outs, carry, first):
        for h in heads:
            if first:
                acc_ref[:, cols(h)] = outs[h]
            else:
                acc_ref[:, cols(h)] += outs[h]
        carry_ref[...] = carry

    def cached_block():
        outs, carry, peak = _sb_heads_block(q, [kbuf[slot, h].astype(BF16) for h in heads],
                                            [vbuf[slot, h].astype(BF16) for h in heads],
                                            carry_ref[...], _neg_lower(blk), None, True)
        accumulate(outs, carry, False)
        return (peak > SB_SKIP_BELOW).astype(jnp.int32)

    pl.when(stream == 0)(lambda: start(fetch(stream, last)))
    pl.when(stream + 1 < pl.num_programs(0))(lambda: start(fetch(stream + 1, last)))
    r = lax.broadcasted_iota(jnp.int32, (SB_HEADS * t, t), 0)
    causal = lax.broadcasted_iota(jnp.int32, (SB_HEADS * t, t), 1) < lax.rem(r, t)
    outs, carry, _ = _sb_heads_block(q, [k_ref[0, :, cols(h)] for h in heads], [v_ref[0, :, cols(h)] for h in heads],
                                     jnp.zeros((SB_HEADS * t, 1), F32), _neg_lower(t), causal, False)
    accumulate(outs, carry, True)
    wait(fetch(stream, last))
    alive = cached_block()

    def body(state):
        j, _ = state
        copies = fetch(stream, j)
        start(copies)
        wait(copies)
        return j - 1, cached_block()

    lax.while_loop(lambda s: jnp.logical_and(s[0] >= 0, s[1] > 0), body, (last - 1, alive))
    o_ref[0] = acc_ref[...].astype(o_ref.dtype)


def _sb_sample(qb, kb, vb, cache_k, cache_v, layer):
    b, t, _ = qb.shape
    past_len = cache_k.shape[-1]
    assert past_len % SB_BLOCK == 0
    new_spec = pl.BlockSpec((1, t, SB_WIDTH), lambda bi: (bi, 0, 0))
    hbm_spec = pl.BlockSpec(memory_space=pl.ANY)
    block = (2, SB_HEADS, SB_HEAD_DIM, SB_BLOCK)
    return pl.pallas_call(
        functools.partial(_sb_sample_kernel, layer=layer, past_len=past_len),
        grid=(b,),
        in_specs=[new_spec, new_spec, new_spec, hbm_spec, hbm_spec],
        out_specs=new_spec,
        out_shape=jax.ShapeDtypeStruct((b, t, SB_WIDTH), BF16),
        scratch_shapes=[pltpu.VMEM(block, F32), pltpu.VMEM(block, F32),
                        pltpu.VMEM((t, SB_WIDTH), F32), pltpu.VMEM((SB_HEADS * t, 1), F32),
                        pltpu.SemaphoreType.DMA((2, 2))],
        compiler_params=pltpu.CompilerParams(dimension_semantics=("arbitrary",),
                                             vmem_limit_bytes=VMEM_LIMIT_BYTES),
        name="sb_sample",
    )(qb, kb, vb, cache_k, cache_v)


def _pool_diff(ext, first_pos):
    rows = ext.shape[0] - POOL_HALO
    pos = first_pos + lax.broadcasted_iota(jnp.int32, (rows, 1), 0)
    outs = []
    for g, window in enumerate(POOL_WINDOWS):
        cols = ext[:, g * POOL_GROUP_DIM:(g + 1) * POOL_GROUP_DIM]
        acc = cols
        shift = 1
        while shift < window:
            acc = acc + pltpu.roll(acc, shift, axis=0)
            shift *= 2
        inv_count = 1.0 / jnp.minimum(pos + 1, window).astype(F32)
        outs.append(acc[POOL_HALO:] * inv_count - cols[POOL_HALO:])
    return jnp.concatenate(outs, axis=1)


POST_GROUP_INPUTS = 6
POST_WEIGHTS = 12


def _post_kernel(*refs, steps, modes, final_norm):
    groups = len(steps)
    n_in = POST_GROUP_INPUTS * groups
    weights = refs[n_in:n_in + POST_WEIGHTS]
    ys = refs[n_in + POST_WEIGHTS:]
    _run_group(steps, [functools.partial(_post_tile, *refs[POST_GROUP_INPUTS * g:POST_GROUP_INPUTS * (g + 1)], *weights,
                                         ys[g], first_step=sum(steps[:g]), final_norm=final_norm, **modes[g])
                       for g in range(groups)])


def _post_tile(x_ref, u_ref, halo_ref, o_ref, gate_ref, p_ref,
               wgrp_ref, scale_ref, wpu_ref, wsu_ref, wout_ref,
               gmlp_ref, wup_ref, wdown_ref, gple_ref, wpg_ref, wpp_ref, gfin_ref,
               y_ref, *, first_step, tiles_per_stream, streams_per_tile, past_pos, final_norm):
    tm = u_ref.shape[0]
    chunk = tm // POST_CHAINS

    def pooled(c):
        u = u_ref[c * chunk:(c + 1) * chunk, :]
        if streams_per_tile == 1:
            step = (pl.program_id(0) - first_step) % tiles_per_stream
            if c == 0:
                halo = jnp.where(step == 0, 0.0, halo_ref[...])
            else:
                halo = u_ref[c * chunk - POOL_HALO:c * chunk, :]
            diff = _pool_diff(jnp.concatenate([halo, u], axis=0), past_pos + step * tm + c * chunk)
        else:
            t = tm // streams_per_tile
            per_chunk = streams_per_tile // POST_CHAINS
            diff = jnp.concatenate(
                [_pool_diff(jnp.concatenate([halo_ref[c * per_chunk + s], u[s * t:(s + 1) * t]], axis=0), past_pos)
                 for s in range(per_chunk)], axis=0)
        return diff.astype(BF16)

    def token_chain(c):
        rows = slice(c * chunk, (c + 1) * chunk)
        diff = pooled(c)
        y_pool = jnp.concatenate(
            [_dot(diff[:, g * POOL_GROUP_DIM:(g + 1) * POOL_GROUP_DIM], wgrp_ref[g])
             for g in range(len(POOL_WINDOWS))], axis=1) * scale_ref[...]
        gates = gate_ref[rows, :].astype(F32)
        merged = (gates[:, :D_MODEL] * _dot(y_pool.astype(BF16), wpu_ref[...])
                  + gates[:, D_MODEL:] * _dot(o_ref[rows, :], wsu_ref[...]))
        yield
        x = x_ref[rows, :] + _dot(merged.astype(BF16), wout_ref[...])
        xn = _rms_norm(x, gmlp_ref[...]).astype(BF16)
        yield
        h = jnp.square(jnp.maximum(_dot(xn, wup_ref[...]), 0.0)).astype(BF16)
        yield
        x = x + _dot(h, wdown_ref[...])
        xn = _rms_norm(x, gple_ref[...]).astype(BF16)
        yield
        ple_gate = jax.nn.sigmoid(_dot(xn, wpg_ref[...]))
        x = x + ple_gate * _dot(p_ref[rows, :].astype(BF16), wpp_ref[...])
        if final_norm:
            x = _rms_norm(x, gfin_ref[...])
        y_ref[rows, :] = x

    _interleave([token_chain(c) for c in range(POST_CHAINS)], POST_LAG)


def _post(groups, weights, modes, *, tm, final_norm):
    assert len(weights) == POST_WEIGHTS and all(len(g) == POST_GROUP_INPUTS for g in groups)
    steps = [g[0].shape[0] // tm for g in groups]
    maps = _group_steps(steps, lambda j: j)
    in_specs, out_specs, out_shape = [], [], []
    for (x, *_), mode, m in zip(groups, modes, maps):
        row = lambda width, m=m: pl.BlockSpec((tm, width), lambda i: (m(i), 0))
        if mode["streams_per_tile"] == 1:
            per_tile = tm // POOL_HALO
            halo_spec = pl.BlockSpec((POOL_HALO, POOL_WIDTH),
                                     lambda i, m=m: (jnp.maximum(m(i) * per_tile - 1, 0), 0))
        else:
            halo_spec = pl.BlockSpec((mode["streams_per_tile"], POOL_HALO, POOL_WIDTH), lambda i, m=m: (m(i), 0, 0))
        in_specs += [row(D_MODEL), row(POOL_WIDTH), halo_spec, row(SB_WIDTH), row(2 * D_MODEL), row(PLE_DIM)]
        out_specs.append(row(D_MODEL))
        out_shape.append(jax.ShapeDtypeStruct((x.shape[0], D_MODEL), F32))
    in_specs += [_resident(w.shape) for w in weights]
    return pl.pallas_call(
        functools.partial(_post_kernel, steps=steps, modes=modes, final_norm=final_norm),
        grid=(sum(steps),),
        in_specs=in_specs,
        out_specs=out_specs,
        out_shape=out_shape,
        compiler_params=pltpu.CompilerParams(dimension_semantics=("arbitrary",),
                                             vmem_limit_bytes=VMEM_LIMIT_BYTES),
        name="post",
    )(*[a for g in groups for a in g], *weights)


def kernel(x_prompt, x_sample, cache_k, cache_v, state_pool, p_prompt, p_sample, g_mix, w_in, b_gate, w_pool_grp, pool_scale, w_pool_up, w_sb_up, w_out, g_mlp, w_up, w_down, g_ple, w_ple_gate, w_ple_proj, g_final):
    depth = w_in.shape[0]
    bp, tp, _ = x_prompt.shape
    bs, ts, _ = x_sample.shape
    past_len = cache_k.shape[2]
    assert tp % ROW_TILE == 0 and tp % SB_BLOCK == 0
    n_s = bs * ts
    assert ts >= POOL_STATE and ts % 8 == 0 and n_s % ROW_TILE == 0 and (ROW_TILE // ts) % POST_CHAINS == 0
    xp = x_prompt.reshape(bp * tp, D_MODEL)
    xs = x_sample.reshape(n_s, D_MODEL)
    row_vec = lambda a: a.reshape(1, -1)
    cache_rows = lambda c: jnp.transpose(c, (0, 1, 3, 4, 2))

    outs = {name: [] for name in ("kp", "vp", "pp", "ks", "vs", "ps")}
    for d in range(depth):
        final_norm = d == depth - 1

        (((up, kp, vp, qbp, kbp, vbp, gates_p), (us, ks, vs, qbs, kbs, vbs, gates_s)),
         (wpu_bf, wsu_bf, wout_bf, wup_bf, wdown_bf, wpg_bf), (wpp_bf, wgrp_bf)) = _inproj(
            [xp, xs], [tp // ROW_TILE, None], g_mix[d], w_in[d].astype(BF16), b_gate[d],
            [w_pool_up[d], w_sb_up[d], w_out[d], w_up[d], w_down[d], w_ple_gate[d]],
            [w_ple_proj[d], w_pool_grp[d]], ROW_TILE)
        weights = (wgrp_bf, row_vec(pool_scale[d]), wpu_bf, wsu_bf, wout_bf, row_vec(g_mlp[d]), wup_bf,
                   wdown_bf, row_vec(g_ple[d]), wpg_bf, wpp_bf, row_vec(g_final))
        op = _sb_prompt(qbp.reshape(bp, tp, SB_WIDTH), kbp.reshape(bp, tp, SB_WIDTH), vbp.reshape(bp, tp, SB_WIDTH))
        os_ = _sb_sample(qbs.reshape(bs, ts, SB_WIDTH), kbs.reshape(bs, ts, SB_WIDTH), vbs.reshape(bs, ts, SB_WIDTH),
                         cache_rows(cache_k), cache_rows(cache_v), d)
        halo_s = jnp.pad(state_pool[d], ((0, 0), (POOL_HALO - POOL_STATE, 0), (0, 0)))
        xp, = _post([(xp, up, up, op.reshape(bp * tp, SB_WIDTH), gates_p, p_prompt[d].reshape(bp * tp, PLE_DIM))],
                    weights, [dict(tiles_per_stream=tp // ROW_TILE, streams_per_tile=1, past_pos=0)],
                    tm=ROW_TILE, final_norm=final_norm)
        xs, = _post([(xs, us, halo_s, os_.reshape(n_s, SB_WIDTH), gates_s, p_sample[d].reshape(n_s, PLE_DIM))],
                    weights, [dict(tiles_per_stream=1, streams_per_tile=ROW_TILE // ts, past_pos=POOL_STATE)],
                    tm=ROW_TILE, final_norm=final_norm)
        token_major = lambda a: jnp.transpose(a.reshape(bp, SB_HEADS, SB_HEAD_DIM, tp), (0, 3, 1, 2))
        outs["kp"].append(token_major(kp))
        outs["vp"].append(token_major(vp))
        outs["pp"].append(up.reshape(bp, tp, POOL_WIDTH)[:, tp - POOL_STATE:])
        outs["ks"].append(ks.reshape(bs, ts, SB_HEADS, SB_HEAD_DIM))
        outs["vs"].append(vs.reshape(bs, ts, SB_HEADS, SB_HEAD_DIM))
        outs["ps"].append(us.reshape(bs, ts, POOL_WIDTH)[:, ts - POOL_STATE:])

    stack = lambda name: jnp.stack(outs[name])
    return (xp.reshape(bp, tp, D_MODEL), xs.reshape(bs, ts, D_MODEL),
            stack("kp"), stack("vp"), stack("pp"), stack("ks"), stack("vs"), stack("ps"))
```

```python
import functools

import jax
import jax.numpy as jnp
from jax import lax
from jax.experimental import pallas as pl
from jax.experimental.pallas import tpu as pltpu

D_MODEL = 1024
POOL_WIDTH = 512
POOL_WINDOWS = (2, 4, 8, 16)
POOL_GROUP_DIM = POOL_WIDTH // len(POOL_WINDOWS)
POOL_STATE = max(POOL_WINDOWS) - 1
POOL_HALO = 16
SB_HEADS = 8
SB_HEAD_DIM = 64
SB_WIDTH = SB_HEADS * SB_HEAD_DIM
HEAD_PAIR = 2 * SB_HEAD_DIM
D_FF = 4 * D_MODEL
PLE_DIM = 256
EPS = 1e-6

V7X_VMEM_BYTES = 64 * 1024 * 1024
VMEM_LIMIT_BYTES = V7X_VMEM_BYTES - 8 * 1024 * 1024

ROW_TILE = 512
INPROJ_CHAINS = 2
POST_CHAINS = 2
POST_LAG = 1
SB_BLOCK = 256
SB_PAIRS_PER_STEP = 4
SB_PAIR_LAG = 2

BF16_SUBLANES = 16
N_WHOLE_CASTS = 2

LOG2E = 1.4426950408889634
SB_SKIP_BELOW = -160.0

BF16 = jnp.bfloat16
F32 = jnp.float32


def _rms_norm(x, g):
    y = x * lax.rsqrt(jnp.mean(x * x, axis=-1, keepdims=True) + EPS)
    return y * g


def _dot(a, b):
    return jnp.dot(a, b, preferred_element_type=F32)


def _dot_nt(a, b):
    return lax.dot_general(a, b, (((1,), (1,)), ((), ())), preferred_element_type=F32)


def _resident(shape):
    return pl.BlockSpec(shape, lambda *_: (0,) * len(shape), pipeline_mode=pl.Buffered(1))


def _interleave(chains, lag):
    waiting, live, tick = list(chains), [], 0
    while waiting or live:
        if waiting and tick % lag == 0:
            live.append(waiting.pop(0))
        live = [c for c in live if next(c, True) is None]
        tick += 1


def _store_head_rows(ref, row0, x):
    rows = x.shape[0]
    for h in range(SB_HEADS):
        ref[pl.ds(row0 * SB_HEADS + h, rows, stride=SB_HEADS), :] = x[:, h * SB_HEAD_DIM:(h + 1) * SB_HEAD_DIM]


def _group_steps(steps, maps_to):
    starts = [sum(steps[:g]) for g in range(len(steps))]
    return [lambda i, s=s, n=n: maps_to(jnp.clip(i - s, 0, n - 1)) for s, n in zip(starts, steps)]


def _run_group(steps, bodies):
    i = pl.program_id(0)
    start = 0
    for n, body in zip(steps, bodies):
        pl.when(jnp.logical_and(i >= start, i < start + n))(body)
        start += n


def _inproj_kernel(*refs, steps, n_cast, feature_major):
    groups = len(steps)
    cast_group = steps.index(max(steps))
    cast_start = sum(steps[:cast_group])
    g_ref, w_ref, b_ref = refs[groups:groups + 3]
    cast_in = refs[groups + 3:groups + 3 + n_cast]
    outs = refs[groups + 3 + n_cast:]
    cast_out = outs[7 * groups:]
    i = pl.program_id(0)

    @pl.when(i == 0)
    def _():
        for src, dst in zip(cast_in[n_cast - N_WHOLE_CASTS:], cast_out[n_cast - N_WHOLE_CASTS:]):
            dst[...] = src[...].astype(BF16)

    @pl.when(jnp.logical_and(i >= cast_start, i < cast_start + steps[cast_group]))
    def _():
        for src, dst in zip(cast_in[:n_cast - N_WHOLE_CASTS], cast_out[:n_cast - N_WHOLE_CASTS]):
            dst[...] = src[...].astype(BF16)

    _run_group(steps, [functools.partial(_inproj_tile, refs[g], g_ref, w_ref, b_ref, *outs[7 * g:7 * g + 7],
                                         feature_major=feature_major[g]) for g in range(groups)])


def _inproj_tile(x_ref, g_ref, w_ref, b_ref, u_ref, k_ref, v_ref, qb_ref, kb_ref, vb_ref, gate_ref, *, feature_major):
    chunk = x_ref.shape[0] // INPROJ_CHAINS

    def store_kv(ref, c, x):
        if feature_major:
            ref[0, :, c * chunk:(c + 1) * chunk] = x.T
        else:
            _store_head_rows(ref, c * chunk, x)

    c0, c1, c2, c3 = POOL_WIDTH, POOL_WIDTH + SB_WIDTH, POOL_WIDTH + 2 * SB_WIDTH, POOL_WIDTH + 3 * SB_WIDTH

    def token_chain(c):
        rows = slice(c * chunk, (c + 1) * chunk)
        xn = _rms_norm(x_ref[rows, :], g_ref[...]).astype(BF16)
        yield
        u_ref[rows, :] = _dot(xn, w_ref[:, :c0])
        qb_ref[rows, :] = (_dot(xn, w_ref[:, c0:c1]) * (SB_HEAD_DIM ** -0.5 * LOG2E)).astype(BF16)
        yield
        k = _dot(xn, w_ref[:, c1:c2])
        store_kv(k_ref, c, k)
        kb_ref[rows, :] = k.astype(BF16)
        yield
        v = _dot(xn, w_ref[:, c2:c3])
        store_kv(v_ref, c, v)
        vb_ref[rows, :] = v.astype(BF16)
        yield
        gate_ref[rows, :] = jax.nn.sigmoid(_dot(xn, w_ref[:, c3:]) + b_ref[...]).astype(BF16)

    _interleave([token_chain(c) for c in range(INPROJ_CHAINS)], 1)


def _inproj(xs, stream_tiles, g_mix, w_in, b_gate, chunked, whole, tm):
    assert len(whole) == N_WHOLE_CASTS
    in_width = w_in.shape[1]
    steps = [x.shape[0] // tm for x in xs]
    maps = _group_steps(steps, lambda j: (j, 0))
    buffers = [dict(pipeline_mode=pl.Buffered(1)) if n == 1 else {} for n in steps]
    in_specs = [pl.BlockSpec((tm, D_MODEL), m, **buf) for m, buf in zip(maps, buffers)]
    in_specs += [_resident((1, D_MODEL)), _resident((D_MODEL, in_width)), _resident((1, 2 * D_MODEL))]
    out_specs, out_shape = [], []
    for x, m, buf, tiles in zip(xs, maps, buffers, stream_tiles):
        n = x.shape[0]
        row = lambda width: pl.BlockSpec((tm, width), m, **buf)
        if tiles is None:
            kv_spec = pl.BlockSpec((tm * SB_HEADS, SB_HEAD_DIM), m, **buf)
            kv_shape = jax.ShapeDtypeStruct((n * SB_HEADS, SB_HEAD_DIM), F32)
        else:
            kv_spec = pl.BlockSpec((1, SB_WIDTH, tm), lambda i, m=m, tiles=tiles: (m(i)[0] // tiles, 0, m(i)[0] % tiles),
                                   **buf)
            kv_shape = jax.ShapeDtypeStruct((n // (tiles * tm), SB_WIDTH, tiles * tm), F32)
        bf_out = jax.ShapeDtypeStruct((n, SB_WIDTH), BF16)
        out_specs += [row(POOL_WIDTH), kv_spec, kv_spec, row(SB_WIDTH), row(SB_WIDTH), row(SB_WIDTH),
                      row(2 * D_MODEL)]
        out_shape += [jax.ShapeDtypeStruct((n, POOL_WIDTH), F32), kv_shape, kv_shape, bf_out, bf_out, bf_out,
                      jax.ShapeDtypeStruct((n, 2 * D_MODEL), BF16)]
    cast_specs = []
    for w in chunked:
        rows = w.shape[0] // max(steps)
        assert rows * max(steps) == w.shape[0] and rows % BF16_SUBLANES == 0
        cast_specs.append(pl.BlockSpec((rows, w.shape[1]), maps[steps.index(max(steps))]))
    cast_specs += [pl.BlockSpec(w.shape, lambda i, nd=w.ndim: (0,) * nd) for w in whole]
    cast_shape = [jax.ShapeDtypeStruct(w.shape, BF16) for w in (*chunked, *whole)]
    outs = pl.pallas_call(
        functools.partial(_inproj_kernel, steps=steps, n_cast=len(cast_specs),
                          feature_major=[tiles is not None for tiles in stream_tiles]),
        grid=(sum(steps),),
        in_specs=in_specs + cast_specs,
        out_specs=out_specs + cast_specs,
        out_shape=out_shape + cast_shape,
        compiler_params=pltpu.CompilerParams(dimension_semantics=("arbitrary",),
                                             vmem_limit_bytes=VMEM_LIMIT_BYTES),
        name="inproj",
    )(*xs, g_mix.reshape(1, D_MODEL), w_in, b_gate.reshape(1, 2 * D_MODEL), *chunked, *whole)
    n_group_outs = 7 * len(xs)
    return ([outs[7 * g:7 * g + 7] for g in range(len(xs))],
            outs[n_group_outs:n_group_outs + len(chunked)], outs[n_group_outs + len(chunked):])


def _neg_lower(n):
    r = lax.broadcasted_iota(jnp.int32, (n, n), 0)
    c = lax.broadcasted_iota(jnp.int32, (n, n), 1)
    return jnp.where(r >= c, -1.0, 0.0).astype(BF16)


def _stack_heads(x):
    lane = lax.broadcasted_iota(jnp.int32, x.shape, 1)
    zero = jnp.zeros_like(x)
    return jnp.concatenate([jnp.where(lane < SB_HEAD_DIM, x, zero), jnp.where(lane >= SB_HEAD_DIM, x, zero)], axis=0)


def _causal_mask(tq, tk):
    r = lax.broadcasted_iota(jnp.int32, (2 * tq, tk), 0)
    c = lax.broadcasted_iota(jnp.int32, (2 * tq, tk), 1)
    return c < jnp.where(r >= tq, r - tq, r)


def _sb_pair_stages(q2, blocks, carry, emit):
    tq = q2.shape[0] // 2
    logits = []
    for k_blk, _, _, causal in blocks:
        z = _dot_nt(q2, k_blk)
        if causal is not None:
            z = jnp.where(causal, z, -jnp.inf)
        logits.append(z)
    yield
    softplus = [jnp.maximum(z, 0.0) + jnp.log(1.0 + jnp.exp2(-jnp.abs(z))) * LOG2E for z in logits]
    yield
    sums = [_dot(sp.astype(BF16), neg_tri) for (_, _, neg_tri, _), sp in zip(blocks, softplus)]
    yield
    weights = []
    for z, incl in zip(logits, sums):
        a = jnp.exp2(z + incl + carry).astype(BF16)
        weights += [a[:tq], a[tq:]]
        carry = carry + incl[:, :1]
    peak = jnp.max(carry)
    yield
    values = jnp.concatenate([v2 for _, v2, _, _ in blocks], axis=0)
    emit(_dot(jnp.concatenate(weights, axis=1), values), carry, peak)


def _sb_prompt_kernel(q_ref, k_ref, v_ref, o_ref, acc_ref, carry_ref, peak_ref):
    qi = pl.program_id(2)
    blk = SB_BLOCK
    pairs = range(SB_PAIRS_PER_STEP)
    lanes = lambda p: slice(p * HEAD_PAIR, (p + 1) * HEAD_PAIR)
    q2 = [_stack_heads(q_ref[0, :, lanes(p)]) for p in pairs]
    neg_tri = _neg_lower(blk)

    def key_block(p, j, causal):
        r0 = pl.multiple_of(j * blk, blk)
        return (k_ref[0, pl.ds(r0, blk), lanes(p)], _stack_heads(v_ref[0, pl.ds(r0, blk), lanes(p)]),
                neg_tri, causal)

    def sweep(block_ids, first):
        def emit(p, out, carry, peak):
            if first:
                acc_ref[:, lanes(p)] = out
            else:
                acc_ref[:, lanes(p)] += out
            carry_ref[p] = carry
            peak_ref[p] = peak

        _interleave([_sb_pair_stages(q2[p], [key_block(p, j, causal) for j, causal in block_ids],
                                     jnp.zeros((2 * blk, 1), F32) if first else carry_ref[p],
                                     functools.partial(emit, p)) for p in pairs], SB_PAIR_LAG)

    def alive():
        peak = functools.reduce(jnp.maximum, [peak_ref[p] for p in pairs])
        return (peak > SB_SKIP_BELOW).astype(jnp.int32)

    diagonal = (qi, _causal_mask(blk, blk))
    pl.when(qi == 0)(lambda: sweep([diagonal], True))
    pl.when(qi > 0)(lambda: sweep([diagonal, (qi - 1, None)], True))

    def body(state):
        j, _ = state
        sweep([(j, None)], False)
        return j - 1, alive()

    lax.while_loop(lambda s: jnp.logical_and(s[0] >= 0, s[1] > 0), body, (qi - 2, alive()))
    o_ref[0] = acc_ref[...].astype(o_ref.dtype)


def _sb_prompt(qb, kb, vb):
    b, t, _ = qb.shape
    blk = SB_BLOCK
    width = SB_PAIRS_PER_STEP * HEAD_PAIR
    kv_spec = pl.BlockSpec((1, t, width), lambda bi, hp, qi: (bi, 0, hp))
    q_spec = pl.BlockSpec((1, blk, width), lambda bi, hp, qi: (bi, qi, hp))
    return pl.pallas_call(
        _sb_prompt_kernel,
        grid=(b, SB_WIDTH // width, t // blk),
        in_specs=[q_spec, kv_spec, kv_spec],
        out_specs=q_spec,
        out_shape=jax.ShapeDtypeStruct((b, t, SB_WIDTH), BF16),
        scratch_shapes=[pltpu.VMEM((blk, width), F32), pltpu.VMEM((SB_PAIRS_PER_STEP, 2 * blk, 1), F32),
                        pltpu.SMEM((SB_PAIRS_PER_STEP,), F32)],
        compiler_params=pltpu.CompilerParams(dimension_semantics=("parallel", "parallel", "arbitrary"),
                                             vmem_limit_bytes=VMEM_LIMIT_BYTES),
        name="sb_prompt",
    )(qb, kb, vb)


def _sb_heads_block(q, keys, values, carry, neg_tri, causal, feature_major):
    t = q[0].shape[0]
    score, mix = (_dot, _dot_nt) if feature_major else (_dot_nt, _dot)
    z = jnp.concatenate([score(qh, kh) for qh, kh in zip(q, keys)], axis=0)
    if causal is not None:
        z = jnp.where(causal, z, -jnp.inf)
    sp = jnp.maximum(z, 0.0) + jnp.log(1.0 + jnp.exp2(-jnp.abs(z))) * LOG2E
    incl = _dot(sp.astype(BF16), neg_tri)
    a = jnp.exp2(z + incl + carry).astype(BF16)
    carry = carry + incl[:, :1]
    peak = jnp.max(carry)
    outs = [mix(a[h * t:(h + 1) * t], vh) for h, vh in enumerate(values)]
    return outs, carry, peak


def _sb_sample_kernel(q_ref, k_ref, v_ref, ck_hbm, cv_hbm, o_ref, kbuf, vbuf, acc_ref, carry_ref, sem,
                      *, layer, past_len):
    stream = pl.program_id(0)
    slot = stream % 2
    blk = SB_BLOCK
    t = q_ref.shape[1]
    heads = range(SB_HEADS)
    cols = lambda h: slice(h * SB_HEAD_DIM, (h + 1) * SB_HEAD_DIM)
    q = [q_ref[0, :, cols(h)] for h in heads]
    last = past_len // blk - 1

    def fetch(s, j):
        tokens = pl.ds(pl.multiple_of(j * blk, blk), blk)
        return (pltpu.make_async_copy(ck_hbm.at[layer, s, :, :, tokens], kbuf.at[s % 2], sem.at[s % 2, 0]),
                pltpu.make_async_copy(cv_hbm.at[layer, s, :, :, tokens], vbuf.at[s % 2], sem.at[s % 2, 1]))

    def start(copies):
        for c in copies:
            c.start()

    def wait(copies):
        for c in copies:
            c.wait()

    def accumulate(outs, carry, first):
        for h in heads:
            if first:
                acc_ref[:, cols(h)] = outs[h]
            else:
                acc_ref[:, cols(h)] += outs[h]
        carry_ref[...] = carry

    def cached_block():
        outs, carry, peak = _sb_heads_block(q, [kbuf[slot, h].astype(BF16) for h in heads],
                                            [vbuf[slot, h].astype(BF16) for h in heads],
                                            carry_ref[...], _neg_lower(blk), None, True)
        accumulate(outs, carry, False)
        return (peak > SB_SKIP_BELOW).astype(jnp.int32)

    pl.when(stream == 0)(lambda: start(fetch(stream, last)))
    pl.when(stream + 1 < pl.num_programs(0))(lambda: start(fetch(stream + 1, last)))
    r = lax.broadcasted_iota(jnp.int32, (SB_HEADS * t, t), 0)
    causal = lax.broadcasted_iota(jnp.int32, (SB_HEADS * t, t), 1) < lax.rem(r, t)
    outs, carry, _ = _sb_heads_block(q, [k_ref[0, :, cols(h)] for h in heads], [v_ref[0, :, cols(h)] for h in heads],
                                     jnp.zeros((SB_HEADS * t, 1), F32), _neg_lower(t), causal, False)
    accumulate(outs, carry, True)
    wait(fetch(stream, last))
    alive = cached_block()

    def body(state):
        j, _ = state
        copies = fetch(stream, j)
        start(copies)
        wait(copies)
        return j - 1, cached_block()

    lax.while_loop(lambda s: jnp.logical_and(s[0] >= 0, s[1] > 0), body, (last - 1, alive))
    o_ref[0] = acc_ref[...].astype(o_ref.dtype)


def _sb_sample(qb, kb, vb, cache_k, cache_v, layer):
    b, t, _ = qb.shape
    past_len = cache_k.shape[-1]
    assert past_len % SB_BLOCK == 0
    new_spec = pl.BlockSpec((1, t, SB_WIDTH), lambda bi: (bi, 0, 0))
    hbm_spec = pl.BlockSpec(memory_space=pl.ANY)
    block = (2, SB_HEADS, SB_HEAD_DIM, SB_BLOCK)
    return pl.pallas_call(
        functools.partial(_sb_sample_kernel, layer=layer, past_len=past_len),
        grid=(b,),
        in_specs=[new_spec, new_spec, new_spec, hbm_spec, hbm_spec],
        out_specs=new_spec,
        out_shape=jax.ShapeDtypeStruct((b, t, SB_WIDTH), BF16),
        scratch_shapes=[pltpu.VMEM(block, F32), pltpu.VMEM(block, F32),
                        pltpu.VMEM((t, SB_WIDTH), F32), pltpu.VMEM((SB_HEADS * t, 1), F32),
                        pltpu.SemaphoreType.DMA((2, 2))],
        compiler_params=pltpu.CompilerParams(dimension_semantics=("arbitrary",),
                                             vmem_limit_bytes=VMEM_LIMIT_BYTES),
        name="sb_sample",
    )(qb, kb, vb, cache_k, cache_v)


def _pool_diff(ext, first_pos):
    rows = ext.shape[0] - POOL_HALO
    pos = first_pos + lax.broadcasted_iota(jnp.int32, (rows, 1), 0)
    outs = []
    for g, window in enumerate(POOL_WINDOWS):
        cols = ext[:, g * POOL_GROUP_DIM:(g + 1) * POOL_GROUP_DIM]
        acc = cols
        shift = 1
        while shift < window:
            acc = acc + pltpu.roll(acc, shift, axis=0)
            shift *= 2
        inv_count = 1.0 / jnp.minimum(pos + 1, window).astype(F32)
        outs.append(acc[POOL_HALO:] * inv_count - cols[POOL_HALO:])
    return jnp.concatenate(outs, axis=1)


POST_GROUP_INPUTS = 6
POST_WEIGHTS = 12


def _post_kernel(*refs, steps, modes, final_norm):
    groups = len(steps)
    n_in = POST_GROUP_INPUTS * groups
    weights = refs[n_in:n_in + POST_WEIGHTS]
    ys = refs[n_in + POST_WEIGHTS:]
    _run_group(steps, [functools.partial(_post_tile, *refs[POST_GROUP_INPUTS * g:POST_GROUP_INPUTS * (g + 1)], *weights,
                                         ys[g], first_step=sum(steps[:g]), final_norm=final_norm, **modes[g])
                       for g in range(groups)])


def _post_tile(x_ref, u_ref, halo_ref, o_ref, gate_ref, p_ref,
               wgrp_ref, scale_ref, wpu_ref, wsu_ref, wout_ref,
               gmlp_ref, wup_ref, wdown_ref, gple_ref, wpg_ref, wpp_ref, gfin_ref,
               y_ref, *, first_step, tiles_per_stream, streams_per_tile, past_pos, final_norm):
    tm = u_ref.shape[0]
    chunk = tm // POST_CHAINS

    def pooled(c):
        u = u_ref[c * chunk:(c + 1) * chunk, :]
        if streams_per_tile == 1:
            step = (pl.program_id(0) - first_step) % tiles_per_stream
            if c == 0:
                halo = jnp.where(step == 0, 0.0, halo_ref[...])
            else:
                halo = u_ref[c * chunk - POOL_HALO:c * chunk, :]
            diff = _pool_diff(jnp.concatenate([halo, u], axis=0), past_pos + step * tm + c * chunk)
        else:
            t = tm // streams_per_tile
            per_chunk = streams_per_tile // POST_CHAINS
            diff = jnp.concatenate(
                [_pool_diff(jnp.concatenate([halo_ref[c * per_chunk + s], u[s * t:(s + 1) * t]], axis=0), past_pos)
                 for s in range(per_chunk)], axis=0)
        return diff.astype(BF16)

    def token_chain(c):
        rows = slice(c * chunk, (c + 1) * chunk)
        diff = pooled(c)
        y_pool = jnp.concatenate(
            [_dot(diff[:, g * POOL_GROUP_DIM:(g + 1) * POOL_GROUP_DIM], wgrp_ref[g])
             for g in range(len(POOL_WINDOWS))], axis=1) * scale_ref[...]
        gates = gate_ref[rows, :].astype(F32)
        merged = (gates[:, :D_MODEL] * _dot(y_pool.astype(BF16), wpu_ref[...])
                  + gates[:, D_MODEL:] * _dot(o_ref[rows, :], wsu_ref[...]))
        yield
        x = x_ref[rows, :] + _dot(merged.astype(BF16), wout_ref[...])
        xn = _rms_norm(x, gmlp_ref[...]).astype(BF16)
        yield
        h = jnp.square(jnp.maximum(_dot(xn, wup_ref[...]), 0.0)).astype(BF16)
        yield
        x = x + _dot(h, wdown_ref[...])
        xn = _rms_norm(x, gple_ref[...]).astype(BF16)
        yield
        ple_gate = jax.nn.sigmoid(_dot(xn, wpg_ref[...]))
        x = x + ple_gate * _dot(p_ref[rows, :].astype(BF16), wpp_ref[...])
        if final_norm:
            x = _rms_norm(x, gfin_ref[...])
        y_ref[rows, :] = x

    _interleave([token_chain(c) for c in range(POST_CHAINS)], POST_LAG)


def _post(groups, weights, modes, *, tm, final_norm):
    assert len(weights) == POST_WEIGHTS and all(len(g) == POST_GROUP_INPUTS for g in groups)
    steps = [g[0].shape[0] // tm for g in groups]
    maps = _group_steps(steps, lambda j: j)
    in_specs, out_specs, out_shape = [], [], []
    for (x, *_), mode, m in zip(groups, modes, maps):
        row = lambda width, m=m: pl.BlockSpec((tm, width), lambda i: (m(i), 0))
        if mode["streams_per_tile"] == 1:
            per_tile = tm // POOL_HALO
            halo_spec = pl.BlockSpec((POOL_HALO, POOL_WIDTH),
                                     lambda i, m=m: (jnp.maximum(m(i) * per_tile - 1, 0), 0))
        else:
            halo_spec = pl.BlockSpec((mode["streams_per_tile"], POOL_HALO, POOL_WIDTH), lambda i, m=m: (m(i), 0, 0))
        in_specs += [row(D_MODEL), row(POOL_WIDTH), halo_spec, row(SB_WIDTH), row(2 * D_MODEL), row(PLE_DIM)]
        out_specs.append(row(D_MODEL))
        out_shape.append(jax.ShapeDtypeStruct((x.shape[0], D_MODEL), F32))
    in_specs += [_resident(w.shape) for w in weights]
    return pl.pallas_call(
        functools.partial(_post_kernel, steps=steps, modes=modes, final_norm=final_norm),
        grid=(sum(steps),),
        in_specs=in_specs,
        out_specs=out_specs,
        out_shape=out_shape,
        compiler_params=pltpu.CompilerParams(dimension_semantics=("arbitrary",),
                                             vmem_limit_bytes=VMEM_LIMIT_BYTES),
        name="post",
    )(*[a for g in groups for a in g], *weights)


def kernel(x_prompt, x_sample, cache_k, cache_v, state_pool, p_prompt, p_sample, g_mix, w_in, b_gate, w_pool_grp, pool_scale, w_pool_up, w_sb_up, w_out, g_mlp, w_up, w_down, g_ple, w_ple_gate, w_ple_proj, g_final):
    depth = w_in.shape[0]
    bp, tp, _ = x_prompt.shape
    bs, ts, _ = x_sample.shape
    past_len = cache_k.shape[2]
    assert tp % ROW_TILE == 0 and tp % SB_BLOCK == 0
    n_s = bs * ts
    assert ts >= POOL_STATE and ts % 8 == 0 and n_s % ROW_TILE == 0 and (ROW_TILE // ts) % POST_CHAINS == 0
    xp = x_prompt.reshape(bp * tp, D_MODEL)
    xs = x_sample.reshape(n_s, D_MODEL)
    row_vec = lambda a: a.reshape(1, -1)
    cache_rows = lambda c: jnp.transpose(c, (0, 1, 3, 4, 2))

    outs = {name: [] for name in ("kp", "vp", "pp", "ks", "vs", "ps")}
    for d in range(depth):
        final_norm = d == depth - 1

        (((us, ks, vs, qbs, kbs, vbs, gates_s), (up, kp, vp, qbp, kbp, vbp, gates_p)),
         (wpu_bf, wsu_bf, wout_bf, wup_bf, wdown_bf, wpg_bf), (wpp_bf, wgrp_bf)) = _inproj(
            [xs, xp], [None, tp // ROW_TILE], g_mix[d], w_in[d].astype(BF16), b_gate[d],
            [w_pool_up[d], w_sb_up[d], w_out[d], w_up[d], w_down[d], w_ple_gate[d]],
            [w_ple_proj[d], w_pool_grp[d]], ROW_TILE)
        weights = (wgrp_bf, row_vec(pool_scale[d]), wpu_bf, wsu_bf, wout_bf, row_vec(g_mlp[d]), wup_bf,
                   wdown_bf, row_vec(g_ple[d]), wpg_bf, wpp_bf, row_vec(g_final))
        op = _sb_prompt(qbp.reshape(bp, tp, SB_WIDTH), kbp.reshape(bp, tp, SB_WIDTH), vbp.reshape(bp, tp, SB_WIDTH))
        os_ = _sb_sample(qbs.reshape(bs, ts, SB_WIDTH), kbs.reshape(bs, ts, SB_WIDTH), vbs.reshape(bs, ts, SB_WIDTH),
                         cache_rows(cache_k), cache_rows(cache_v), d)
        halo_s = jnp.pad(state_pool[d], ((0, 0), (POOL_HALO - POOL_STATE, 0), (0, 0)))
        xp, = _post([(xp, up, up, op.reshape(bp * tp, SB_WIDTH), gates_p, p_prompt[d].reshape(bp * tp, PLE_DIM))],
                    weights, [dict(tiles_per_stream=tp // ROW_TILE, streams_per_tile=1, past_pos=0)],
                    tm=ROW_TILE, final_norm=final_norm)
        xs, = _post([(xs, us, halo_s, os_.reshape(n_s, SB_WIDTH), gates_s, p_sample[d].reshape(n_s, PLE_DIM))],
                    weights, [dict(tiles_per_stream=1, streams_per_tile=ROW_TILE // ts, past_pos=POOL_STATE)],
                    tm=ROW_TILE, final_norm=final_norm)
        token_major = lambda a: jnp.transpose(a.reshape(bp, SB_HEADS, SB_HEAD_DIM, tp), (0, 3, 1, 2))
        outs["kp"].append(token_major(kp))
        outs["vp"].append(token_major(vp))
        outs["pp"].append(up.reshape(bp, tp, POOL_WIDTH)[:, tp - POOL_STATE:])
        outs["ks"].append(ks.reshape(bs, ts, SB_HEADS, SB_HEAD_DIM))
        outs["vs"].append(vs.reshape(bs, ts, SB_HEADS, SB_HEAD_DIM))
        outs["ps"].append(us.reshape(bs, ts, POOL_WIDTH)[:, ts - POOL_STATE:])

    stack = lambda name: jnp.stack(outs[name])
    return (xp.reshape(bp, tp, D_MODEL), xs.reshape(bs, ts, D_MODEL),
            stack("kp"), stack("vp"), stack("pp"), stack("ks"), stack("vs"), stack("ps"))
```

```python
import functools

import jax
import jax.numpy as jnp
from jax import lax
from jax.experimental import pallas as pl
from jax.experimental.pallas import tpu as pltpu

D_MODEL = 1024
POOL_WIDTH = 512
POOL_WINDOWS = (2, 4, 8, 16)
POOL_GROUP_DIM = POOL_WIDTH // len(POOL_WINDOWS)
POOL_STATE = max(POOL_WINDOWS) - 1
POOL_HALO = 16
SB_HEADS = 8
SB_HEAD_DIM = 64
SB_WIDTH = SB_HEADS * SB_HEAD_DIM
HEAD_PAIR = 2 * SB_HEAD_DIM
D_FF = 4 * D_MODEL
PLE_DIM = 256
EPS = 1e-6

V7X_VMEM_BYTES = 64 * 1024 * 1024
VMEM_LIMIT_BYTES = V7X_VMEM_BYTES - 8 * 1024 * 1024

ROW_TILE = 512
INPROJ_CHAINS = 2
POST_CHAINS = 2
POST_LAG = 1
SB_BLOCK = 256
SB_PAIRS_PER_STEP = 4
SB_SAMPLE_STREAMS = 2
SB_PAIR_LAG = 2

BF16_SUBLANES = 16
N_WHOLE_CASTS = 2

LOG2E = 1.4426950408889634
SB_SKIP_BELOW = -160.0

BF16 = jnp.bfloat16
F32 = jnp.float32


def _rms_norm(x, g):
    y = x * lax.rsqrt(jnp.mean(x * x, axis=-1, keepdims=True) + EPS)
    return y * g


def _dot(a, b):
    return jnp.dot(a, b, preferred_element_type=F32)


def _dot_nt(a, b):
    return lax.dot_general(a, b, (((1,), (1,)), ((), ())), preferred_element_type=F32)


def _resident(shape):
    return pl.BlockSpec(shape, lambda *_: (0,) * len(shape), pipeline_mode=pl.Buffered(1))


def _interleave(chains, lag):
    waiting, live, tick = list(chains), [], 0
    while waiting or live:
        if waiting and tick % lag == 0:
            live.append(waiting.pop(0))
        live = [c for c in live if next(c, True) is None]
        tick += 1


def _store_head_rows(ref, row0, x):
    rows = x.shape[0]
    for h in range(SB_HEADS):
        ref[pl.ds(row0 * SB_HEADS + h, rows, stride=SB_HEADS), :] = x[:, h * SB_HEAD_DIM:(h + 1) * SB_HEAD_DIM]


def _group_steps(steps, maps_to):
    starts = [sum(steps[:g]) for g in range(len(steps))]
    return [lambda i, s=s, n=n: maps_to(jnp.clip(i - s, 0, n - 1)) for s, n in zip(starts, steps)]


def _run_group(steps, bodies):
    i = pl.program_id(0)
    start = 0
    for n, body in zip(steps, bodies):
        pl.when(jnp.logical_and(i >= start, i < start + n))(body)
        start += n


def _inproj_kernel(*refs, steps, n_cast, feature_major):
    groups = len(steps)
    cast_group = steps.index(max(steps))
    cast_start = sum(steps[:cast_group])
    g_ref, w_ref, b_ref = refs[groups:groups + 3]
    cast_in = refs[groups + 3:groups + 3 + n_cast]
    outs = refs[groups + 3 + n_cast:]
    cast_out = outs[7 * groups:]
    i = pl.program_id(0)

    @pl.when(i == 0)
    def _():
        for src, dst in zip(cast_in[n_cast - N_WHOLE_CASTS:], cast_out[n_cast - N_WHOLE_CASTS:]):
            dst[...] = src[...].astype(BF16)

    @pl.when(jnp.logical_and(i >= cast_start, i < cast_start + steps[cast_group]))
    def _():
        for src, dst in zip(cast_in[:n_cast - N_WHOLE_CASTS], cast_out[:n_cast - N_WHOLE_CASTS]):
            dst[...] = src[...].astype(BF16)

    _run_group(steps, [functools.partial(_inproj_tile, refs[g], g_ref, w_ref, b_ref, *outs[7 * g:7 * g + 7],
                                         feature_major=feature_major[g]) for g in range(groups)])


def _inproj_tile(x_ref, g_ref, w_ref, b_ref, u_ref, k_ref, v_ref, qb_ref, kb_ref, vb_ref, gate_ref, *, feature_major):
    chunk = x_ref.shape[0] // INPROJ_CHAINS

    def store_kv(ref, c, x):
        if feature_major:
            ref[0, :, c * chunk:(c + 1) * chunk] = x.T
        else:
            _store_head_rows(ref, c * chunk, x)

    c0, c1, c2, c3 = POOL_WIDTH, POOL_WIDTH + SB_WIDTH, POOL_WIDTH + 2 * SB_WIDTH, POOL_WIDTH + 3 * SB_WIDTH

    def token_chain(c):
        rows = slice(c * chunk, (c + 1) * chunk)
        xn = _rms_norm(x_ref[rows, :], g_ref[...]).astype(BF16)
        yield
        u_ref[rows, :] = _dot(xn, w_ref[:, :c0])
        qb_ref[rows, :] = (_dot(xn, w_ref[:, c0:c1]) * (SB_HEAD_DIM ** -0.5 * LOG2E)).astype(BF16)
        yield
        k = _dot(xn, w_ref[:, c1:c2])
        store_kv(k_ref, c, k)
        kb_ref[rows, :] = k.astype(BF16)
        yield
        v = _dot(xn, w_ref[:, c2:c3])
        store_kv(v_ref, c, v)
        vb_ref[rows, :] = v.astype(BF16)
        yield
        gate_ref[rows, :] = jax.nn.sigmoid(_dot(xn, w_ref[:, c3:]) + b_ref[...]).astype(BF16)

    _interleave([token_chain(c) for c in range(INPROJ_CHAINS)], 1)


def _inproj(xs, stream_tiles, g_mix, w_in, b_gate, chunked, whole, tm):
    assert len(whole) == N_WHOLE_CASTS
    in_width = w_in.shape[1]
    steps = [x.shape[0] // tm for x in xs]
    maps = _group_steps(steps, lambda j: (j, 0))
    buffers = [dict(pipeline_mode=pl.Buffered(1)) if n == 1 else {} for n in steps]
    in_specs = [pl.BlockSpec((tm, D_MODEL), m, **buf) for m, buf in zip(maps, buffers)]
    in_specs += [_resident((1, D_MODEL)), _resident((D_MODEL, in_width)), _resident((1, 2 * D_MODEL))]
    out_specs, out_shape = [], []
    for x, m, buf, tiles in zip(xs, maps, buffers, stream_tiles):
        n = x.shape[0]
        row = lambda width: pl.BlockSpec((tm, width), m, **buf)
        if tiles is None:
            kv_spec = pl.BlockSpec((tm * SB_HEADS, SB_HEAD_DIM), m, **buf)
            kv_shape = jax.ShapeDtypeStruct((n * SB_HEADS, SB_HEAD_DIM), F32)
        else:
            kv_spec = pl.BlockSpec((1, SB_WIDTH, tm), lambda i, m=m, tiles=tiles: (m(i)[0] // tiles, 0, m(i)[0] % tiles),
                                   **buf)
            kv_shape = jax.ShapeDtypeStruct((n // (tiles * tm), SB_WIDTH, tiles * tm), F32)
        bf_out = jax.ShapeDtypeStruct((n, SB_WIDTH), BF16)
        out_specs += [row(POOL_WIDTH), kv_spec, kv_spec, row(SB_WIDTH), row(SB_WIDTH), row(SB_WIDTH),
                      row(2 * D_MODEL)]
        out_shape += [jax.ShapeDtypeStruct((n, POOL_WIDTH), F32), kv_shape, kv_shape, bf_out, bf_out, bf_out,
                      jax.ShapeDtypeStruct((n, 2 * D_MODEL), BF16)]
    cast_specs = []
    for w in chunked:
        rows = w.shape[0] // max(steps)
        assert rows * max(steps) == w.shape[0] and rows % BF16_SUBLANES == 0
        cast_specs.append(pl.BlockSpec((rows, w.shape[1]), maps[steps.index(max(steps))]))
    cast_specs += [pl.BlockSpec(w.shape, lambda i, nd=w.ndim: (0,) * nd) for w in whole]
    cast_shape = [jax.ShapeDtypeStruct(w.shape, BF16) for w in (*chunked, *whole)]
    outs = pl.pallas_call(
        functools.partial(_inproj_kernel, steps=steps, n_cast=len(cast_specs),
                          feature_major=[tiles is not None for tiles in stream_tiles]),
        grid=(sum(steps),),
        in_specs=in_specs + cast_specs,
        out_specs=out_specs + cast_specs,
        out_shape=out_shape + cast_shape,
        compiler_params=pltpu.CompilerParams(dimension_semantics=("arbitrary",),
                                             vmem_limit_bytes=VMEM_LIMIT_BYTES),
        name="inproj",
    )(*xs, g_mix.reshape(1, D_MODEL), w_in, b_gate.reshape(1, 2 * D_MODEL), *chunked, *whole)
    n_group_outs = 7 * len(xs)
    return ([outs[7 * g:7 * g + 7] for g in range(len(xs))],
            outs[n_group_outs:n_group_outs + len(chunked)], outs[n_group_outs + len(chunked):])


def _neg_lower(n):
    r = lax.broadcasted_iota(jnp.int32, (n, n), 0)
    c = lax.broadcasted_iota(jnp.int32, (n, n), 1)
    return jnp.where(r >= c, -1.0, 0.0).astype(BF16)


def _stack_heads(x):
    lane = lax.broadcasted_iota(jnp.int32, x.shape, 1)
    zero = jnp.zeros_like(x)
    return jnp.concatenate([jnp.where(lane < SB_HEAD_DIM, x, zero), jnp.where(lane >= SB_HEAD_DIM, x, zero)], axis=0)


def _causal_mask(tq, tk):
    r = lax.broadcasted_iota(jnp.int32, (2 * tq, tk), 0)
    c = lax.broadcasted_iota(jnp.int32, (2 * tq, tk), 1)
    return c < jnp.where(r >= tq, r - tq, r)


def _sb_pair_stages(q2, blocks, carry, emit):
    tq = q2.shape[0] // 2
    logits = []
    for k_blk, _, _, causal in blocks:
        z = _dot_nt(q2, k_blk)
        if causal is not None:
            z = jnp.where(causal, z, -jnp.inf)
        logits.append(z)
    yield
    softplus = [jnp.maximum(z, 0.0) + jnp.log(1.0 + jnp.exp2(-jnp.abs(z))) * LOG2E for z in logits]
    yield
    sums = [_dot(sp.astype(BF16), neg_tri) for (_, _, neg_tri, _), sp in zip(blocks, softplus)]
    yield
    weights = []
    for z, incl in zip(logits, sums):
        a = jnp.exp2(z + incl + carry).astype(BF16)
        weights += [a[:tq], a[tq:]]
        carry = carry + incl[:, :1]
    peak = jnp.max(carry)
    yield
    values = jnp.concatenate([v2 for _, v2, _, _ in blocks], axis=0)
    emit(_dot(jnp.concatenate(weights, axis=1), values), carry, peak)


def _sb_prompt_kernel(q_ref, k_ref, v_ref, o_ref, acc_ref, carry_ref, peak_ref):
    qi = pl.program_id(2)
    blk = SB_BLOCK
    pairs = range(SB_PAIRS_PER_STEP)
    lanes = lambda p: slice(p * HEAD_PAIR, (p + 1) * HEAD_PAIR)
    q2 = [_stack_heads(q_ref[0, :, lanes(p)]) for p in pairs]
    neg_tri = _neg_lower(blk)

    def key_block(p, j, causal):
        r0 = pl.multiple_of(j * blk, blk)
        return (k_ref[0, pl.ds(r0, blk), lanes(p)], _stack_heads(v_ref[0, pl.ds(r0, blk), lanes(p)]),
                neg_tri, causal)

    def sweep(block_ids, first):
        def emit(p, out, carry, peak):
            if first:
                acc_ref[:, lanes(p)] = out
            else:
                acc_ref[:, lanes(p)] += out
            carry_ref[p] = carry
            peak_ref[p] = peak

        _interleave([_sb_pair_stages(q2[p], [key_block(p, j, causal) for j, causal in block_ids],
                                     jnp.zeros((2 * blk, 1), F32) if first else carry_ref[p],
                                     functools.partial(emit, p)) for p in pairs], SB_PAIR_LAG)

    def alive():
        peak = functools.reduce(jnp.maximum, [peak_ref[p] for p in pairs])
        return (peak > SB_SKIP_BELOW).astype(jnp.int32)

    diagonal = (qi, _causal_mask(blk, blk))
    pl.when(qi == 0)(lambda: sweep([diagonal], True))
    pl.when(qi > 0)(lambda: sweep([diagonal, (qi - 1, None)], True))

    def body(state):
        j, _ = state
        sweep([(j, None)], False)
        return j - 1, alive()

    lax.while_loop(lambda s: jnp.logical_and(s[0] >= 0, s[1] > 0), body, (qi - 2, alive()))
    o_ref[0] = acc_ref[...].astype(o_ref.dtype)


def _sb_prompt(qb, kb, vb):
    b, t, _ = qb.shape
    blk = SB_BLOCK
    width = SB_PAIRS_PER_STEP * HEAD_PAIR
    kv_spec = pl.BlockSpec((1, t, width), lambda bi, hp, qi: (bi, 0, hp))
    q_spec = pl.BlockSpec((1, blk, width), lambda bi, hp, qi: (bi, qi, hp))
    return pl.pallas_call(
        _sb_prompt_kernel,
        grid=(b, SB_WIDTH // width, t // blk),
        in_specs=[q_spec, kv_spec, kv_spec],
        out_specs=q_spec,
        out_shape=jax.ShapeDtypeStruct((b, t, SB_WIDTH), BF16),
        scratch_shapes=[pltpu.VMEM((blk, width), F32), pltpu.VMEM((SB_PAIRS_PER_STEP, 2 * blk, 1), F32),
                        pltpu.SMEM((SB_PAIRS_PER_STEP,), F32)],
        compiler_params=pltpu.CompilerParams(dimension_semantics=("parallel", "parallel", "arbitrary"),
                                             vmem_limit_bytes=VMEM_LIMIT_BYTES),
        name="sb_prompt",
    )(qb, kb, vb)


def _sb_heads_stages(q, blocks, carry, emit):
    t = q[0].shape[0]
    logits = []
    for keys, _, _, causal, feature_major in blocks:
        score = _dot if feature_major else _dot_nt
        z = jnp.concatenate([score(qh, kh) for qh, kh in zip(q, keys)], axis=0)
        logits.append(z if causal is None else jnp.where(causal, z, -jnp.inf))
    yield
    softplus = [jnp.maximum(z, 0.0) + jnp.log(1.0 + jnp.exp2(-jnp.abs(z))) * LOG2E for z in logits]
    yield
    sums = [_dot(sp.astype(BF16), neg_tri) for (_, _, neg_tri, _, _), sp in zip(blocks, softplus)]
    yield
    weights = []
    for z, incl in zip(logits, sums):
        weights.append(jnp.exp2(z + incl + carry).astype(BF16))
        carry = carry + incl[:, :1]
    peak = jnp.max(carry)
    yield
    outs = None
    for (_, values, _, _, feature_major), a in zip(blocks, weights):
        mix = _dot_nt if feature_major else _dot
        part = [mix(a[h * t:(h + 1) * t], vh) for h, vh in enumerate(values)]
        outs = part if outs is None else [o + p for o, p in zip(outs, part)]
    emit(outs, carry, peak)


def _sb_sample_kernel(q_ref, k_ref, v_ref, ck_hbm, cv_hbm, o_ref, kbuf, vbuf, acc_ref, carry_ref, peak_ref, sem,
                      *, layer, past_len):
    step = pl.program_id(0)
    slot = step % 2
    blk = SB_BLOCK
    t = q_ref.shape[1]
    heads = range(SB_HEADS)
    local = range(SB_SAMPLE_STREAMS)
    cols = lambda h: slice(h * SB_HEAD_DIM, (h + 1) * SB_HEAD_DIM)
    last = past_len // blk - 1

    def fetch(at_step, j):
        tokens = pl.ds(pl.multiple_of(j * blk, blk), blk)
        copies = []
        for s in local:
            stream = at_step * SB_SAMPLE_STREAMS + s
            copies.append(pltpu.make_async_copy(ck_hbm.at[layer, stream, :, :, tokens],
                                                kbuf.at[at_step % 2, s], sem.at[at_step % 2, s, 0]))
            copies.append(pltpu.make_async_copy(cv_hbm.at[layer, stream, :, :, tokens],
                                                vbuf.at[at_step % 2, s], sem.at[at_step % 2, s, 1]))
        return copies

    def start(copies):
        for c in copies:
            c.start()

    def wait(copies):
        for c in copies:
            c.wait()

    def cached_block(s):
        return ([kbuf[slot, s, h].astype(BF16) for h in heads], [vbuf[slot, s, h].astype(BF16) for h in heads],
                _neg_lower(blk), None, True)

    def sweep(blocks_of, first):
        def emit(s, outs, carry, peak):
            for h in heads:
                if first:
                    acc_ref[s, :, cols(h)] = outs[h]
                else:
                    acc_ref[s, :, cols(h)] += outs[h]
            carry_ref[s] = carry
            peak_ref[s] = peak

        _interleave([_sb_heads_stages([q_ref[s, :, cols(h)] for h in heads], blocks_of(s),
                                      jnp.zeros((SB_HEADS * t, 1), F32) if first else carry_ref[s],
                                      functools.partial(emit, s)) for s in local], 1)

    def alive():
        peak = functools.reduce(jnp.maximum, [peak_ref[s] for s in local])
        return (peak > SB_SKIP_BELOW).astype(jnp.int32)

    pl.when(step == 0)(lambda: start(fetch(step, last)))
    pl.when(step + 1 < pl.num_programs(0))(lambda: start(fetch(step + 1, last)))
    wait(fetch(step, last))
    r = lax.broadcasted_iota(jnp.int32, (SB_HEADS * t, t), 0)
    causal = lax.broadcasted_iota(jnp.int32, (SB_HEADS * t, t), 1) < lax.rem(r, t)
    new_block = lambda s: ([k_ref[s, :, cols(h)] for h in heads], [v_ref[s, :, cols(h)] for h in heads],
                           _neg_lower(t), causal, False)
    sweep(lambda s: [new_block(s), cached_block(s)], True)

    def body(state):
        j, _ = state
        copies = fetch(step, j)
        start(copies)
        wait(copies)
        sweep(lambda s: [cached_block(s)], False)
        return j - 1, alive()

    lax.while_loop(lambda s: jnp.logical_and(s[0] >= 0, s[1] > 0), body, (last - 1, alive()))
    o_ref[...] = acc_ref[...].astype(o_ref.dtype)


def _sb_sample(qb, kb, vb, cache_k, cache_v, layer):
    b, t, _ = qb.shape
    past_len = cache_k.shape[-1]
    n = SB_SAMPLE_STREAMS
    assert past_len % SB_BLOCK == 0 and b % n == 0
    new_spec = pl.BlockSpec((n, t, SB_WIDTH), lambda i: (i, 0, 0))
    hbm_spec = pl.BlockSpec(memory_space=pl.ANY)
    block = (2, n, SB_HEADS, SB_HEAD_DIM, SB_BLOCK)
    return pl.pallas_call(
        functools.partial(_sb_sample_kernel, layer=layer, past_len=past_len),
        grid=(b // n,),
        in_specs=[new_spec, new_spec, new_spec, hbm_spec, hbm_spec],
        out_specs=new_spec,
        out_shape=jax.ShapeDtypeStruct((b, t, SB_WIDTH), BF16),
        scratch_shapes=[pltpu.VMEM(block, F32), pltpu.VMEM(block, F32),
                        pltpu.VMEM((n, t, SB_WIDTH), F32), pltpu.VMEM((n, SB_HEADS * t, 1), F32),
                        pltpu.SMEM((n,), F32), pltpu.SemaphoreType.DMA((2, n, 2))],
        compiler_params=pltpu.CompilerParams(dimension_semantics=("arbitrary",),
                                             vmem_limit_bytes=VMEM_LIMIT_BYTES),
        name="sb_sample",
    )(qb, kb, vb, cache_k, cache_v)


def _pool_diff(ext, first_pos):
    rows = ext.shape[0] - POOL_HALO
    pos = first_pos + lax.broadcasted_iota(jnp.int32, (rows, 1), 0)
    outs = []
    for g, window in enumerate(POOL_WINDOWS):
        cols = ext[:, g * POOL_GROUP_DIM:(g + 1) * POOL_GROUP_DIM]
        acc = cols
        shift = 1
        while shift < window:
            acc = acc + pltpu.roll(acc, shift, axis=0)
            shift *= 2
        inv_count = 1.0 / jnp.minimum(pos + 1, window).astype(F32)
        outs.append(acc[POOL_HALO:] * inv_count - cols[POOL_HALO:])
    return jnp.concatenate(outs, axis=1)


POST_GROUP_INPUTS = 6
POST_WEIGHTS = 12


def _post_kernel(*refs, steps, modes, final_norm):
    groups = len(steps)
    n_in = POST_GROUP_INPUTS * groups
    weights = refs[n_in:n_in + POST_WEIGHTS]
    ys = refs[n_in + POST_WEIGHTS:]
    _run_group(steps, [functools.partial(_post_tile, *refs[POST_GROUP_INPUTS * g:POST_GROUP_INPUTS * (g + 1)], *weights,
                                         ys[g], first_step=sum(steps[:g]), final_norm=final_norm, **modes[g])
                       for g in range(groups)])


def _post_tile(x_ref, u_ref, halo_ref, o_ref, gate_ref, p_ref,
               wgrp_ref, scale_ref, wpu_ref, wsu_ref, wout_ref,
               gmlp_ref, wup_ref, wdown_ref, gple_ref, wpg_ref, wpp_ref, gfin_ref,
               y_ref, *, first_step, tiles_per_stream, streams_per_tile, past_pos, final_norm):
    tm = u_ref.shape[0]
    chunk = tm // POST_CHAINS

    def pooled(c):
        u = u_ref[c * chunk:(c + 1) * chunk, :]
        if streams_per_tile == 1:
            step = (pl.program_id(0) - first_step) % tiles_per_stream
            if c == 0:
                halo = jnp.where(step == 0, 0.0, halo_ref[...])
            else:
                halo = u_ref[c * chunk - POOL_HALO:c * chunk, :]
            diff = _pool_diff(jnp.concatenate([halo, u], axis=0), past_pos + step * tm + c * chunk)
        else:
            t = tm // streams_per_tile
            per_chunk = streams_per_tile // POST_CHAINS
            diff = jnp.concatenate(
                [_pool_diff(jnp.concatenate([halo_ref[c * per_chunk + s], u[s * t:(s + 1) * t]], axis=0), past_pos)
                 for s in range(per_chunk)], axis=0)
        return diff.astype(BF16)

    def token_chain(c):
        rows = slice(c * chunk, (c + 1) * chunk)
        diff = pooled(c)
        y_pool = jnp.concatenate(
            [_dot(diff[:, g * POOL_GROUP_DIM:(g + 1) * POOL_GROUP_DIM], wgrp_ref[g])
             for g in range(len(POOL_WINDOWS))], axis=1) * scale_ref[...]
        gates = gate_ref[rows, :].astype(F32)
        merged = (gates[:, :D_MODEL] * _dot(y_pool.astype(BF16), wpu_ref[...])
                  + gates[:, D_MODEL:] * _dot(o_ref[rows, :], wsu_ref[...]))
        yield
        x = x_ref[rows, :] + _dot(merged.astype(BF16), wout_ref[...])
        xn = _rms_norm(x, gmlp_ref[...]).astype(BF16)
        yield
        h = jnp.square(jnp.maximum(_dot(xn, wup_ref[...]), 0.0)).astype(BF16)
        yield
        x = x + _dot(h, wdown_ref[...])
        xn = _rms_norm(x, gple_ref[...]).astype(BF16)
        yield
        ple_gate = jax.nn.sigmoid(_dot(xn, wpg_ref[...]))
        x = x + ple_gate * _dot(p_ref[rows, :].astype(BF16), wpp_ref[...])
        if final_norm:
            x = _rms_norm(x, gfin_ref[...])
        y_ref[rows, :] = x

    _interleave([token_chain(c) for c in range(POST_CHAINS)], POST_LAG)


def _post(groups, weights, modes, *, tm, final_norm):
    assert len(weights) == POST_WEIGHTS and all(len(g) == POST_GROUP_INPUTS for g in groups)
    steps = [g[0].shape[0] // tm for g in groups]
    maps = _group_steps(steps, lambda j: j)
    in_specs, out_specs, out_shape = [], [], []
    for (x, *_), mode, m in zip(groups, modes, maps):
        row = lambda width, m=m: pl.BlockSpec((tm, width), lambda i: (m(i), 0))
        if mode["streams_per_tile"] == 1:
            per_tile = tm // POOL_HALO
            halo_spec = pl.BlockSpec((POOL_HALO, POOL_WIDTH),
                                     lambda i, m=m: (jnp.maximum(m(i) * per_tile - 1, 0), 0))
        else:
            halo_spec = pl.BlockSpec((mode["streams_per_tile"], POOL_HALO, POOL_WIDTH), lambda i, m=m: (m(i), 0, 0))
        in_specs += [row(D_MODEL), row(POOL_WIDTH), halo_spec, row(SB_WIDTH), row(2 * D_MODEL), row(PLE_DIM)]
        out_specs.append(row(D_MODEL))
        out_shape.append(jax.ShapeDtypeStruct((x.shape[0], D_MODEL), F32))
    in_specs += [_resident(w.shape) for w in weights]
    return pl.pallas_call(
        functools.partial(_post_kernel, steps=steps, modes=modes, final_norm=final_norm),
        grid=(sum(steps),),
        in_specs=in_specs,
        out_specs=out_specs,
        out_shape=out_shape,
        compiler_params=pltpu.CompilerParams(dimension_semantics=("arbitrary",),
                                             vmem_limit_bytes=VMEM_LIMIT_BYTES),
        name="post",
    )(*[a for g in groups for a in g], *weights)


def kernel(x_prompt, x_sample, cache_k, cache_v, state_pool, p_prompt, p_sample, g_mix, w_in, b_gate, w_pool_grp, pool_scale, w_pool_up, w_sb_up, w_out, g_mlp, w_up, w_down, g_ple, w_ple_gate, w_ple_proj, g_final):
    depth = w_in.shape[0]
    bp, tp, _ = x_prompt.shape
    bs, ts, _ = x_sample.shape
    past_len = cache_k.shape[2]
    assert tp % ROW_TILE == 0 and tp % SB_BLOCK == 0
    n_s = bs * ts
    assert ts >= POOL_STATE and ts % 8 == 0 and n_s % ROW_TILE == 0 and (ROW_TILE // ts) % POST_CHAINS == 0
    xp = x_prompt.reshape(bp * tp, D_MODEL)
    xs = x_sample.reshape(n_s, D_MODEL)
    row_vec = lambda a: a.reshape(1, -1)
    cache_rows = lambda c: jnp.transpose(c, (0, 1, 3, 4, 2))

    outs = {name: [] for name in ("kp", "vp", "pp", "ks", "vs", "ps")}
    for d in range(depth):
        final_norm = d == depth - 1

        (((us, ks, vs, qbs, kbs, vbs, gates_s), (up, kp, vp, qbp, kbp, vbp, gates_p)),
         (wpu_bf, wsu_bf, wout_bf, wup_bf, wdown_bf, wpg_bf), (wpp_bf, wgrp_bf)) = _inproj(
            [xs, xp], [None, tp // ROW_TILE], g_mix[d], w_in[d].astype(BF16), b_gate[d],
            [w_pool_up[d], w_sb_up[d], w_out[d], w_up[d], w_down[d], w_ple_gate[d]],
            [w_ple_proj[d], w_pool_grp[d]], ROW_TILE)
        weights = (wgrp_bf, row_vec(pool_scale[d]), wpu_bf, wsu_bf, wout_bf, row_vec(g_mlp[d]), wup_bf,
                   wdown_bf, row_vec(g_ple[d]), wpg_bf, wpp_bf, row_vec(g_final))
        op = _sb_prompt(qbp.reshape(bp, tp, SB_WIDTH), kbp.reshape(bp, tp, SB_WIDTH), vbp.reshape(bp, tp, SB_WIDTH))
        os_ = _sb_sample(qbs.reshape(bs, ts, SB_WIDTH), kbs.reshape(bs, ts, SB_WIDTH), vbs.reshape(bs, ts, SB_WIDTH),
                         cache_rows(cache_k), cache_rows(cache_v), d)
        halo_s = jnp.pad(state_pool[d], ((0, 0), (POOL_HALO - POOL_STATE, 0), (0, 0)))
        xp, = _post([(xp, up, up, op.reshape(bp * tp, SB_WIDTH), gates_p, p_prompt[d].reshape(bp * tp, PLE_DIM))],
                    weights, [dict(tiles_per_stream=tp // ROW_TILE, streams_per_tile=1, past_pos=0)],
                    tm=ROW_TILE, final_norm=final_norm)
        xs, = _post([(xs, us, halo_s, os_.reshape(n_s, SB_WIDTH), gates_s, p_sample[d].reshape(n_s, PLE_DIM))],
                    weights, [dict(tiles_per_stream=1, streams_per_tile=ROW_TILE // ts, past_pos=POOL_STATE)],
                    tm=ROW_TILE, final_norm=final_norm)
        token_major = lambda a: jnp.transpose(a.reshape(bp, SB_HEADS, SB_HEAD_DIM, tp), (0, 3, 1, 2))
        outs["kp"].append(token_major(kp))
        outs["vp"].append(token_major(vp))
        outs["pp"].append(up.reshape(bp, tp, POOL_WIDTH)[:, tp - POOL_STATE:])
        outs["ks"].append(ks.reshape(bs, ts, SB_HEADS, SB_HEAD_DIM))
        outs["vs"].append(vs.reshape(bs, ts, SB_HEADS, SB_HEAD_DIM))
        outs["ps"].append(us.reshape(bs, ts, POOL_WIDTH)[:, ts - POOL_STATE:])

    stack = lambda name: jnp.stack(outs[name])
    return (xp.reshape(bp, tp, D_MODEL), xs.reshape(bs, ts, D_MODEL),
            stack("kp"), stack("vp"), stack("pp"), stack("ks"), stack("vs"), stack("ps"))
```

```python
import functools

import jax
import jax.numpy as jnp
from jax import lax
from jax.experimental import pallas as pl
from jax.experimental.pallas import tpu as pltpu

D_MODEL = 1024
POOL_WIDTH = 512
POOL_WINDOWS = (2, 4, 8, 16)
POOL_GROUP_DIM = POOL_WIDTH // len(POOL_WINDOWS)
POOL_STATE = max(POOL_WINDOWS) - 1
POOL_HALO = 16
SB_HEADS = 8
SB_HEAD_DIM = 64
SB_WIDTH = SB_HEADS * SB_HEAD_DIM
HEAD_PAIR = 2 * SB_HEAD_DIM
D_FF = 4 * D_MODEL
PLE_DIM = 256
EPS = 1e-6

V7X_VMEM_BYTES = 64 * 1024 * 1024
VMEM_LIMIT_BYTES = V7X_VMEM_BYTES - 8 * 1024 * 1024

ROW_TILE = 512
INPROJ_CHAINS = 2
POST_CHAINS = 2
POST_LAG = 1
SB_BLOCK = 256
SB_PAIRS_PER_STEP = 4
SB_SAMPLE_STREAMS = 2
SB_PAIR_LAG = 2

BF16_SUBLANES = 16
N_WHOLE_CASTS = 2

LOG2E = 1.4426950408889634
SB_SKIP_BELOW = -160.0

BF16 = jnp.bfloat16
F32 = jnp.float32


def _rms_norm(x, g):
    y = x * lax.rsqrt(jnp.mean(x * x, axis=-1, keepdims=True) + EPS)
    return y * g


def _dot(a, b):
    return jnp.dot(a, b, preferred_element_type=F32)


def _dot_nt(a, b):
    return lax.dot_general(a, b, (((1,), (1,)), ((), ())), preferred_element_type=F32)


def _resident(shape):
    return pl.BlockSpec(shape, lambda *_: (0,) * len(shape), pipeline_mode=pl.Buffered(1))


def _interleave(chains, lag):
    waiting, live, tick = list(chains), [], 0
    while waiting or live:
        if waiting and tick % lag == 0:
            live.append(waiting.pop(0))
        live = [c for c in live if next(c, True) is None]
        tick += 1


def _store_head_rows(ref, row0, x):
    rows = x.shape[0]
    for h in range(SB_HEADS):
        ref[pl.ds(row0 * SB_HEADS + h, rows, stride=SB_HEADS), :] = x[:, h * SB_HEAD_DIM:(h + 1) * SB_HEAD_DIM]


def _group_steps(steps, maps_to):
    starts = [sum(steps[:g]) for g in range(len(steps))]
    return [lambda i, s=s, n=n: maps_to(jnp.clip(i - s, 0, n - 1)) for s, n in zip(starts, steps)]


def _run_group(steps, bodies):
    i = pl.program_id(0)
    start = 0
    for n, body in zip(steps, bodies):
        pl.when(jnp.logical_and(i >= start, i < start + n))(body)
        start += n


def _inproj_kernel(*refs, steps, n_cast, feature_major):
    groups = len(steps)
    cast_group = steps.index(max(steps))
    cast_start = sum(steps[:cast_group])
    g_ref, w_ref, b_ref = refs[groups:groups + 3]
    cast_in = refs[groups + 3:groups + 3 + n_cast]
    outs = refs[groups + 3 + n_cast:]
    cast_out = outs[7 * groups:]
    i = pl.program_id(0)

    @pl.when(i == 0)
    def _():
        for src, dst in zip(cast_in[n_cast - N_WHOLE_CASTS:], cast_out[n_cast - N_WHOLE_CASTS:]):
            dst[...] = src[...].astype(BF16)

    @pl.when(jnp.logical_and(i >= cast_start, i < cast_start + steps[cast_group]))
    def _():
        for src, dst in zip(cast_in[:n_cast - N_WHOLE_CASTS], cast_out[:n_cast - N_WHOLE_CASTS]):
            dst[...] = src[...].astype(BF16)

    _run_group(steps, [functools.partial(_inproj_tile, refs[g], g_ref, w_ref, b_ref, *outs[7 * g:7 * g + 7],
                                         feature_major=feature_major[g]) for g in range(groups)])


def _inproj_tile(x_ref, g_ref, w_ref, b_ref, u_ref, k_ref, v_ref, qb_ref, kb_ref, vb_ref, gate_ref, *, feature_major):
    chunk = x_ref.shape[0] // INPROJ_CHAINS

    def store_kv(ref, c, x):
        if feature_major:
            ref[0, :, c * chunk:(c + 1) * chunk] = x.T
        else:
            _store_head_rows(ref, c * chunk, x)

    c0, c1, c2, c3 = POOL_WIDTH, POOL_WIDTH + SB_WIDTH, POOL_WIDTH + 2 * SB_WIDTH, POOL_WIDTH + 3 * SB_WIDTH

    def token_chain(c):
        rows = slice(c * chunk, (c + 1) * chunk)
        xn = _rms_norm(x_ref[rows, :], g_ref[...]).astype(BF16)
        yield
        u_ref[rows, :] = _dot(xn, w_ref[:, :c0])
        qb_ref[rows, :] = (_dot(xn, w_ref[:, c0:c1]) * (SB_HEAD_DIM ** -0.5 * LOG2E)).astype(BF16)
        yield
        k = _dot(xn, w_ref[:, c1:c2])
        store_kv(k_ref, c, k)
        kb_ref[rows, :] = k.astype(BF16)
        yield
        v = _dot(xn, w_ref[:, c2:c3])
        store_kv(v_ref, c, v)
        vb_ref[rows, :] = v.astype(BF16)
        yield
        gate_ref[rows, :] = jax.nn.sigmoid(_dot(xn, w_ref[:, c3:]) + b_ref[...]).astype(BF16)

    _interleave([token_chain(c) for c in range(INPROJ_CHAINS)], 1)


def _inproj(xs, stream_tiles, g_mix, w_in, b_gate, chunked, whole, tm):
    assert len(whole) == N_WHOLE_CASTS
    in_width = w_in.shape[1]
    steps = [x.shape[0] // tm for x in xs]
    maps = _group_steps(steps, lambda j: (j, 0))
    buffers = [dict(pipeline_mode=pl.Buffered(1)) if n == 1 else {} for n in steps]
    in_specs = [pl.BlockSpec((tm, D_MODEL), m, **buf) for m, buf in zip(maps, buffers)]
    in_specs += [_resident((1, D_MODEL)), _resident((D_MODEL, in_width)), _resident((1, 2 * D_MODEL))]
    out_specs, out_shape = [], []
    for x, m, buf, tiles in zip(xs, maps, buffers, stream_tiles):
        n = x.shape[0]
        row = lambda width: pl.BlockSpec((tm, width), m, **buf)
        if tiles is None:
            kv_spec = pl.BlockSpec((tm * SB_HEADS, SB_HEAD_DIM), m, **buf)
            kv_shape = jax.ShapeDtypeStruct((n * SB_HEADS, SB_HEAD_DIM), F32)
        else:
            kv_spec = pl.BlockSpec((1, SB_WIDTH, tm), lambda i, m=m, tiles=tiles: (m(i)[0] // tiles, 0, m(i)[0] % tiles),
                                   **buf)
            kv_shape = jax.ShapeDtypeStruct((n // (tiles * tm), SB_WIDTH, tiles * tm), F32)
        bf_out = jax.ShapeDtypeStruct((n, SB_WIDTH), BF16)
        out_specs += [row(POOL_WIDTH), kv_spec, kv_spec, row(SB_WIDTH), row(SB_WIDTH), row(SB_WIDTH),
                      row(2 * D_MODEL)]
        out_shape += [jax.ShapeDtypeStruct((n, POOL_WIDTH), F32), kv_shape, kv_shape, bf_out, bf_out, bf_out,
                      jax.ShapeDtypeStruct((n, 2 * D_MODEL), BF16)]
    cast_specs = []
    for w in chunked:
        rows = w.shape[0] // max(steps)
        assert rows * max(steps) == w.shape[0] and rows % BF16_SUBLANES == 0
        cast_specs.append(pl.BlockSpec((rows, w.shape[1]), maps[steps.index(max(steps))]))
    cast_specs += [pl.BlockSpec(w.shape, lambda i, nd=w.ndim: (0,) * nd) for w in whole]
    cast_shape = [jax.ShapeDtypeStruct(w.shape, BF16) for w in (*chunked, *whole)]
    outs = pl.pallas_call(
        functools.partial(_inproj_kernel, steps=steps, n_cast=len(cast_specs),
                          feature_major=[tiles is not None for tiles in stream_tiles]),
        grid=(sum(steps),),
        in_specs=in_specs + cast_specs,
        out_specs=out_specs + cast_specs,
        out_shape=out_shape + cast_shape,
        compiler_params=pltpu.CompilerParams(dimension_semantics=("arbitrary",),
                                             vmem_limit_bytes=VMEM_LIMIT_BYTES),
        name="inproj",
    )(*xs, g_mix.reshape(1, D_MODEL), w_in, b_gate.reshape(1, 2 * D_MODEL), *chunked, *whole)
    n_group_outs = 7 * len(xs)
    return ([outs[7 * g:7 * g + 7] for g in range(len(xs))],
            outs[n_group_outs:n_group_outs + len(chunked)], outs[n_group_outs + len(chunked):])


def _neg_lower(n):
    r = lax.broadcasted_iota(jnp.int32, (n, n), 0)
    c = lax.broadcasted_iota(jnp.int32, (n, n), 1)
    return jnp.where(r >= c, -1.0, 0.0).astype(BF16)


def _stack_heads(x):
    lane = lax.broadcasted_iota(jnp.int32, x.shape, 1)
    zero = jnp.zeros_like(x)
    return jnp.concatenate([jnp.where(lane < SB_HEAD_DIM, x, zero), jnp.where(lane >= SB_HEAD_DIM, x, zero)], axis=0)


def _causal_mask(tq, tk):
    r = lax.broadcasted_iota(jnp.int32, (2 * tq, tk), 0)
    c = lax.broadcasted_iota(jnp.int32, (2 * tq, tk), 1)
    return c < jnp.where(r >= tq, r - tq, r)


def _sb_pair_stages(q2, blocks, carry, emit):
    tq = q2.shape[0] // 2
    logits = []
    for k_blk, _, _, causal in blocks:
        z = _dot_nt(q2, k_blk)
        if causal is not None:
            z = jnp.where(causal, z, -jnp.inf)
        logits.append(z)
    yield
    softplus = [jnp.maximum(z, 0.0) + jnp.log(1.0 + jnp.exp2(-jnp.abs(z))) * LOG2E for z in logits]
    yield
    sums = [_dot(sp.astype(BF16), neg_tri) for (_, _, neg_tri, _), sp in zip(blocks, softplus)]
    yield
    weights = []
    for z, incl in zip(logits, sums):
        a = jnp.exp2(z + incl + carry).astype(BF16)
        weights += [a[:tq], a[tq:]]
        carry = carry + incl[:, :1]
    peak = jnp.max(carry)
    yield
    values = jnp.concatenate([v2 for _, v2, _, _ in blocks], axis=0)
    emit(_dot(jnp.concatenate(weights, axis=1), values), carry, peak)


def _sb_prompt_kernel(q_ref, k_ref, v_ref, o_ref, acc_ref, carry_ref, peak_ref):
    qi = pl.program_id(2)
    blk = SB_BLOCK
    pairs = range(SB_PAIRS_PER_STEP)
    lanes = lambda p: slice(p * HEAD_PAIR, (p + 1) * HEAD_PAIR)
    q2 = [_stack_heads(q_ref[0, :, lanes(p)]) for p in pairs]
    neg_tri = _neg_lower(blk)

    def key_block(p, j, causal):
        r0 = pl.multiple_of(j * blk, blk)
        return (k_ref[0, pl.ds(r0, blk), lanes(p)], _stack_heads(v_ref[0, pl.ds(r0, blk), lanes(p)]),
                neg_tri, causal)

    def sweep(block_ids, first):
        def emit(p, out, carry, peak):
            if first:
                acc_ref[:, lanes(p)] = out
            else:
                acc_ref[:, lanes(p)] += out
            carry_ref[p] = carry
            peak_ref[p] = peak

        _interleave([_sb_pair_stages(q2[p], [key_block(p, j, causal) for j, causal in block_ids],
                                     jnp.zeros((2 * blk, 1), F32) if first else carry_ref[p],
                                     functools.partial(emit, p)) for p in pairs], SB_PAIR_LAG)

    def alive():
        peak = functools.reduce(jnp.maximum, [peak_ref[p] for p in pairs])
        return (peak > SB_SKIP_BELOW).astype(jnp.int32)

    diagonal = (qi, _causal_mask(blk, blk))
    pl.when(qi == 0)(lambda: sweep([diagonal], True))
    pl.when(qi > 0)(lambda: sweep([diagonal, (qi - 1, None)], True))

    def body(state):
        j, _ = state
        sweep([(j, None)], False)
        return j - 1, alive()

    lax.while_loop(lambda s: jnp.logical_and(s[0] >= 0, s[1] > 0), body, (qi - 2, alive()))
    o_ref[0] = acc_ref[...].astype(o_ref.dtype)


def _sb_prompt(qb, kb, vb):
    b, t, _ = qb.shape
    blk = SB_BLOCK
    width = SB_PAIRS_PER_STEP * HEAD_PAIR
    kv_spec = pl.BlockSpec((1, t, width), lambda bi, hp, qi: (bi, 0, hp))
    q_spec = pl.BlockSpec((1, blk, width), lambda bi, hp, qi: (bi, qi, hp))
    return pl.pallas_call(
        _sb_prompt_kernel,
        grid=(b, SB_WIDTH // width, t // blk),
        in_specs=[q_spec, kv_spec, kv_spec],
        out_specs=q_spec,
        out_shape=jax.ShapeDtypeStruct((b, t, SB_WIDTH), BF16),
        scratch_shapes=[pltpu.VMEM((blk, width), F32), pltpu.VMEM((SB_PAIRS_PER_STEP, 2 * blk, 1), F32),
                        pltpu.SMEM((SB_PAIRS_PER_STEP,), F32)],
        compiler_params=pltpu.CompilerParams(dimension_semantics=("parallel", "parallel", "arbitrary"),
                                             vmem_limit_bytes=VMEM_LIMIT_BYTES),
        name="sb_prompt",
    )(qb, kb, vb)


def _sb_heads_stages(q, blocks, carry, emit):
    t = q[0].shape[0]
    logits = []
    for keys, _, _, causal, feature_major in blocks:
        score = _dot if feature_major else _dot_nt
        z = jnp.concatenate([score(qh, kh) for qh, kh in zip(q, keys)], axis=0)
        logits.append(z if causal is None else jnp.where(causal, z, -jnp.inf))
    yield
    softplus = [jnp.maximum(z, 0.0) + jnp.log(1.0 + jnp.exp2(-jnp.abs(z))) * LOG2E for z in logits]
    yield
    sums = [_dot(sp.astype(BF16), neg_tri) for (_, _, neg_tri, _, _), sp in zip(blocks, softplus)]
    yield
    weights = []
    for z, incl in zip(logits, sums):
        weights.append(jnp.exp2(z + incl + carry).astype(BF16))
        carry = carry + incl[:, :1]
    peak = jnp.max(carry)
    yield
    outs = None
    for (_, values, _, _, feature_major), a in zip(blocks, weights):
        mix = _dot_nt if feature_major else _dot
        part = [mix(a[h * t:(h + 1) * t], vh) for h, vh in enumerate(values)]
        outs = part if outs is None else [o + p for o, p in zip(outs, part)]
    emit(outs, carry, peak)


def _sb_sample_kernel(q_ref, k_ref, v_ref, ck_hbm, cv_hbm, o_ref, kbuf, vbuf, acc_ref, carry_ref, peak_ref, sem,
                      *, layer, past_len):
    step = pl.program_id(0)
    slot = step % 2
    blk = SB_BLOCK
    t = q_ref.shape[1]
    heads = range(SB_HEADS)
    local = range(SB_SAMPLE_STREAMS)
    cols = lambda h: slice(h * SB_HEAD_DIM, (h + 1) * SB_HEAD_DIM)
    last = past_len // blk - 1

    def fetch(at_step, j):
        tokens = pl.ds(pl.multiple_of(j * blk, blk), blk)
        copies = []
        for s in local:
            stream = at_step * SB_SAMPLE_STREAMS + s
            copies.append(pltpu.make_async_copy(ck_hbm.at[layer, stream, :, :, tokens],
                                                kbuf.at[at_step % 2, s], sem.at[at_step % 2, s, 0]))
            copies.append(pltpu.make_async_copy(cv_hbm.at[layer, stream, :, :, tokens],
                                                vbuf.at[at_step % 2, s], sem.at[at_step % 2, s, 1]))
        return copies

    def start(copies):
        for n, c in enumerate(copies):
            c.start(priority=n % 2)

    def wait(copies):
        for c in copies:
            c.wait()

    def cached_block(s):
        return ([kbuf[slot, s, h].astype(BF16) for h in heads], [vbuf[slot, s, h].astype(BF16) for h in heads],
                _neg_lower(blk), None, True)

    def sweep(blocks_of, first):
        def emit(s, outs, carry, peak):
            for h in heads:
                if first:
                    acc_ref[s, :, cols(h)] = outs[h]
                else:
                    acc_ref[s, :, cols(h)] += outs[h]
            carry_ref[s] = carry
            peak_ref[s] = peak

        _interleave([_sb_heads_stages([q_ref[s, :, cols(h)] for h in heads], blocks_of(s),
                                      jnp.zeros((SB_HEADS * t, 1), F32) if first else carry_ref[s],
                                      functools.partial(emit, s)) for s in local], 1)

    def alive():
        peak = functools.reduce(jnp.maximum, [peak_ref[s] for s in local])
        return (peak > SB_SKIP_BELOW).astype(jnp.int32)

    pl.when(step == 0)(lambda: start(fetch(step, last)))
    pl.when(step + 1 < pl.num_programs(0))(lambda: start(fetch(step + 1, last)))
    wait(fetch(step, last))
    r = lax.broadcasted_iota(jnp.int32, (SB_HEADS * t, t), 0)
    causal = lax.broadcasted_iota(jnp.int32, (SB_HEADS * t, t), 1) < lax.rem(r, t)
    new_block = lambda s: ([k_ref[s, :, cols(h)] for h in heads], [v_ref[s, :, cols(h)] for h in heads],
                           _neg_lower(t), causal, False)
    sweep(lambda s: [new_block(s), cached_block(s)], True)

    def body(state):
        j, _ = state
        copies = fetch(step, j)
        start(copies)
        wait(copies)
        sweep(lambda s: [cached_block(s)], False)
        return j - 1, alive()

    lax.while_loop(lambda s: jnp.logical_and(s[0] >= 0, s[1] > 0), body, (last - 1, alive()))
    o_ref[...] = acc_ref[...].astype(o_ref.dtype)


def _sb_sample(qb, kb, vb, cache_k, cache_v, layer):
    b, t, _ = qb.shape
    past_len = cache_k.shape[-1]
    n = SB_SAMPLE_STREAMS
    assert past_len % SB_BLOCK == 0 and b % n == 0
    new_spec = pl.BlockSpec((n, t, SB_WIDTH), lambda i: (i, 0, 0))
    hbm_spec = pl.BlockSpec(memory_space=pl.ANY)
    block = (2, n, SB_HEADS, SB_HEAD_DIM, SB_BLOCK)
    return pl.pallas_call(
        functools.partial(_sb_sample_kernel, layer=layer, past_len=past_len),
        grid=(b // n,),
        in_specs=[new_spec, new_spec, new_spec, hbm_spec, hbm_spec],
        out_specs=new_spec,
        out_shape=jax.ShapeDtypeStruct((b, t, SB_WIDTH), BF16),
        scratch_shapes=[pltpu.VMEM(block, F32), pltpu.VMEM(block, F32),
                        pltpu.VMEM((n, t, SB_WIDTH), F32), pltpu.VMEM((n, SB_HEADS * t, 1), F32),
                        pltpu.SMEM((n,), F32), pltpu.SemaphoreType.DMA((2, n, 2))],
        compiler_params=pltpu.CompilerParams(dimension_semantics=("arbitrary",),
                                             vmem_limit_bytes=VMEM_LIMIT_BYTES),
        name="sb_sample",
    )(qb, kb, vb, cache_k, cache_v)


def _pool_diff(ext, first_pos):
    rows = ext.shape[0] - POOL_HALO
    pos = first_pos + lax.broadcasted_iota(jnp.int32, (rows, 1), 0)
    outs = []
    for g, window in enumerate(POOL_WINDOWS):
        cols = ext[:, g * POOL_GROUP_DIM:(g + 1) * POOL_GROUP_DIM]
        acc = cols
        shift = 1
        while shift < window:
            acc = acc + pltpu.roll(acc, shift, axis=0)
            shift *= 2
        inv_count = 1.0 / jnp.minimum(pos + 1, window).astype(F32)
        outs.append(acc[POOL_HALO:] * inv_count - cols[POOL_HALO:])
    return jnp.concatenate(outs, axis=1)


POST_GROUP_INPUTS = 6
POST_WEIGHTS = 12


def _post_kernel(*refs, steps, modes, final_norm):
    groups = len(steps)
    n_in = POST_GROUP_INPUTS * groups
    weights = refs[n_in:n_in + POST_WEIGHTS]
    ys = refs[n_in + POST_WEIGHTS:]
    _run_group(steps, [functools.partial(_post_tile, *refs[POST_GROUP_INPUTS * g:POST_GROUP_INPUTS * (g + 1)], *weights,
                                         ys[g], first_step=sum(steps[:g]), final_norm=final_norm, **modes[g])
                       for g in range(groups)])


def _post_tile(x_ref, u_ref, halo_ref, o_ref, gate_ref, p_ref,
               wgrp_ref, scale_ref, wpu_ref, wsu_ref, wout_ref,
               gmlp_ref, wup_ref, wdown_ref, gple_ref, wpg_ref, wpp_ref, gfin_ref,
               y_ref, *, first_step, tiles_per_stream, streams_per_tile, past_pos, final_norm):
    tm = u_ref.shape[0]
    chunk = tm // POST_CHAINS

    def pooled(c):
        u = u_ref[c * chunk:(c + 1) * chunk, :]
        if streams_per_tile == 1:
            step = (pl.program_id(0) - first_step) % tiles_per_stream
            if c == 0:
                halo = jnp.where(step == 0, 0.0, halo_ref[...])
            else:
                halo = u_ref[c * chunk - POOL_HALO:c * chunk, :]
            diff = _pool_diff(jnp.concatenate([halo, u], axis=0), past_pos + step * tm + c * chunk)
        else:
            t = tm // streams_per_tile
            per_chunk = streams_per_tile // POST_CHAINS
            diff = jnp.concatenate(
                [_pool_diff(jnp.concatenate([halo_ref[c * per_chunk + s], u[s * t:(s + 1) * t]], axis=0), past_pos)
                 for s in range(per_chunk)], axis=0)
        return diff.astype(BF16)

    def token_chain(c):
        rows = slice(c * chunk, (c + 1) * chunk)
        diff = pooled(c)
        y_pool = jnp.concatenate(
            [_dot(diff[:, g * POOL_GROUP_DIM:(g + 1) * POOL_GROUP_DIM], wgrp_ref[g])
             for g in range(len(POOL_WINDOWS))], axis=1) * scale_ref[...]
        gates = gate_ref[rows, :].astype(F32)
        merged = (gates[:, :D_MODEL] * _dot(y_pool.astype(BF16), wpu_ref[...])
                  + gates[:, D_MODEL:] * _dot(o_ref[rows, :], wsu_ref[...]))
        yield
        x = x_ref[rows, :] + _dot(merged.astype(BF16), wout_ref[...])
        xn = _rms_norm(x, gmlp_ref[...]).astype(BF16)
        yield
        h = jnp.square(jnp.maximum(_dot(xn, wup_ref[...]), 0.0)).astype(BF16)
        yield
        x = x + _dot(h, wdown_ref[...])
        xn = _rms_norm(x, gple_ref[...]).astype(BF16)
        yield
        ple_gate = jax.nn.sigmoid(_dot(xn, wpg_ref[...]))
        x = x + ple_gate * _dot(p_ref[rows, :].astype(BF16), wpp_ref[...])
        if final_norm:
            x = _rms_norm(x, gfin_ref[...])
        y_ref[rows, :] = x

    _interleave([token_chain(c) for c in range(POST_CHAINS)], POST_LAG)


def _post(groups, weights, modes, *, tm, final_norm):
    assert len(weights) == POST_WEIGHTS and all(len(g) == POST_GROUP_INPUTS for g in groups)
    steps = [g[0].shape[0] // tm for g in groups]
    maps = _group_steps(steps, lambda j: j)
    in_specs, out_specs, out_shape = [], [], []
    for (x, *_), mode, m in zip(groups, modes, maps):
        row = lambda width, m=m: pl.BlockSpec((tm, width), lambda i: (m(i), 0))
        if mode["streams_per_tile"] == 1:
            per_tile = tm // POOL_HALO
            halo_spec = pl.BlockSpec((POOL_HALO, POOL_WIDTH),
                                     lambda i, m=m: (jnp.maximum(m(i) * per_tile - 1, 0), 0))
        else:
            halo_spec = pl.BlockSpec((mode["streams_per_tile"], POOL_HALO, POOL_WIDTH), lambda i, m=m: (m(i), 0, 0))
        in_specs += [row(D_MODEL), row(POOL_WIDTH), halo_spec, row(SB_WIDTH), row(2 * D_MODEL), row(PLE_DIM)]
        out_specs.append(row(D_MODEL))
        out_shape.append(jax.ShapeDtypeStruct((x.shape[0], D_MODEL), F32))
    in_specs += [_resident(w.shape) for w in weights]
    return pl.pallas_call(
        functools.partial(_post_kernel, steps=steps, modes=modes, final_norm=final_norm),
        grid=(sum(steps),),
        in_specs=in_specs,
        out_specs=out_specs,
        out_shape=out_shape,
        compiler_params=pltpu.CompilerParams(dimension_semantics=("arbitrary",),
                                             vmem_limit_bytes=VMEM_LIMIT_BYTES),
        name="post",
    )(*[a for g in groups for a in g], *weights)


def kernel(x_prompt, x_sample, cache_k, cache_v, state_pool, p_prompt, p_sample, g_mix, w_in, b_gate, w_pool_grp, pool_scale, w_pool_up, w_sb_up, w_out, g_mlp, w_up, w_down, g_ple, w_ple_gate, w_ple_proj, g_final):
    depth = w_in.shape[0]
    bp, tp, _ = x_prompt.shape
    bs, ts, _ = x_sample.shape
    past_len = cache_k.shape[2]
    assert tp % ROW_TILE == 0 and tp % SB_BLOCK == 0
    n_s = bs * ts
    assert ts >= POOL_STATE and ts % 8 == 0 and n_s % ROW_TILE == 0 and (ROW_TILE // ts) % POST_CHAINS == 0
    xp = x_prompt.reshape(bp * tp, D_MODEL)
    xs = x_sample.reshape(n_s, D_MODEL)
    row_vec = lambda a: a.reshape(1, -1)
    cache_rows = lambda c: jnp.transpose(c, (0, 1, 3, 4, 2))

    outs = {name: [] for name in ("kp", "vp", "pp", "ks", "vs", "ps")}
    for d in range(depth):
        final_norm = d == depth - 1

        (((us, ks, vs, qbs, kbs, vbs, gates_s), (up, kp, vp, qbp, kbp, vbp, gates_p)),
         (wpu_bf, wsu_bf, wout_bf, wup_bf, wdown_bf, wpg_bf), (wpp_bf, wgrp_bf)) = _inproj(
            [xs, xp], [None, tp // ROW_TILE], g_mix[d], w_in[d].astype(BF16), b_gate[d],
            [w_pool_up[d], w_sb_up[d], w_out[d], w_up[d], w_down[d], w_ple_gate[d]],
            [w_ple_proj[d], w_pool_grp[d]], ROW_TILE)
        weights = (wgrp_bf, row_vec(pool_scale[d]), wpu_bf, wsu_bf, wout_bf, row_vec(g_mlp[d]), wup_bf,
                   wdown_bf, row_vec(g_ple[d]), wpg_bf, wpp_bf, row_vec(g_final))
        op = _sb_prompt(qbp.reshape(bp, tp, SB_WIDTH), kbp.reshape(bp, tp, SB_WIDTH), vbp.reshape(bp, tp, SB_WIDTH))
        os_ = _sb_sample(qbs.reshape(bs, ts, SB_WIDTH), kbs.reshape(bs, ts, SB_WIDTH), vbs.reshape(bs, ts, SB_WIDTH),
                         cache_rows(cache_k), cache_rows(cache_v), d)
        halo_s = jnp.pad(state_pool[d], ((0, 0), (POOL_HALO - POOL_STATE, 0), (0, 0)))
        xp, = _post([(xp, up, up, op.reshape(bp * tp, SB_WIDTH), gates_p, p_prompt[d].reshape(bp * tp, PLE_DIM))],
                    weights, [dict(tiles_per_stream=tp // ROW_TILE, streams_per_tile=1, past_pos=0)],
                    tm=ROW_TILE, final_norm=final_norm)
        xs, = _post([(xs, us, halo_s, os_.reshape(n_s, SB_WIDTH), gates_s, p_sample[d].reshape(n_s, PLE_DIM))],
                    weights, [dict(tiles_per_stream=1, streams_per_tile=ROW_TILE // ts, past_pos=POOL_STATE)],
                    tm=ROW_TILE, final_norm=final_norm)
        token_major = lambda a: jnp.transpose(a.reshape(bp, SB_HEADS, SB_HEAD_DIM, tp), (0, 3, 1, 2))
        outs["kp"].append(token_major(kp))
        outs["vp"].append(token_major(vp))
        outs["pp"].append(up.reshape(bp, tp, POOL_WIDTH)[:, tp - POOL_STATE:])
        outs["ks"].append(ks.reshape(bs, ts, SB_HEADS, SB_HEAD_DIM))
        outs["vs"].append(vs.reshape(bs, ts, SB_HEADS, SB_HEAD_DIM))
        outs["ps"].append(us.reshape(bs, ts, POOL_WIDTH)[:, ts - POOL_STATE:])

    stack = lambda name: jnp.stack(outs[name])
    return (xp.reshape(bp, tp, D_MODEL), xs.reshape(bs, ts, D_MODEL),
            stack("kp"), stack("vp"), stack("pp"), stack("ks"), stack("vs"), stack("ps"))
```

```python
import functools

import jax
import jax.numpy as jnp
from jax import lax
from jax.experimental import pallas as pl
from jax.experimental.pallas import tpu as pltpu

D_MODEL = 1024
POOL_WIDTH = 512
POOL_WINDOWS = (2, 4, 8, 16)
POOL_GROUP_DIM = POOL_WIDTH // len(POOL_WINDOWS)
POOL_STATE = max(POOL_WINDOWS) - 1
POOL_HALO = 16
SB_HEADS = 8
SB_HEAD_DIM = 64
SB_WIDTH = SB_HEADS * SB_HEAD_DIM
HEAD_PAIR = 2 * SB_HEAD_DIM
D_FF = 4 * D_MODEL
PLE_DIM = 256
EPS = 1e-6

V7X_VMEM_BYTES = 64 * 1024 * 1024
VMEM_LIMIT_BYTES = V7X_VMEM_BYTES - 8 * 1024 * 1024

ROW_TILE = 512
INPROJ_CHAINS = 2
POST_CHAINS = 2
POST_LAG = 1
SB_BLOCK = 256
SB_PAIRS_PER_STEP = 4
SB_SAMPLE_STREAMS = 2
SB_PAIR_LAG = 2

BF16_SUBLANES = 16
W_IN_CHUNK = 512
N_WHOLE_CASTS = 2

LOG2E = 1.4426950408889634
SB_SKIP_BELOW = -160.0

BF16 = jnp.bfloat16
F32 = jnp.float32


def _rms_norm(x, g):
    y = x * lax.rsqrt(jnp.mean(x * x, axis=-1, keepdims=True) + EPS)
    return y * g


def _dot(a, b):
    return jnp.dot(a, b, preferred_element_type=F32)


def _dot_nt(a, b):
    return lax.dot_general(a, b, (((1,), (1,)), ((), ())), preferred_element_type=F32)


def _resident(shape):
    return pl.BlockSpec(shape, lambda *_: (0,) * len(shape), pipeline_mode=pl.Buffered(1))


def _interleave(chains, lag):
    waiting, live, tick = list(chains), [], 0
    while waiting or live:
        if waiting and tick % lag == 0:
            live.append(waiting.pop(0))
        live = [c for c in live if next(c, True) is None]
        tick += 1


def _store_head_rows(ref, row0, x):
    rows = x.shape[0]
    for h in range(SB_HEADS):
        ref[pl.ds(row0 * SB_HEADS + h, rows, stride=SB_HEADS), :] = x[:, h * SB_HEAD_DIM:(h + 1) * SB_HEAD_DIM]


def _group_steps(steps, maps_to):
    starts = [sum(steps[:g]) for g in range(len(steps))]
    return [lambda i, s=s, n=n: maps_to(jnp.clip(i - s, 0, n - 1)) for s, n in zip(starts, steps)]


def _run_group(steps, bodies):
    i = pl.program_id(0)
    start = 0
    for n, body in zip(steps, bodies):
        pl.when(jnp.logical_and(i >= start, i < start + n))(body)
        start += n


def _inproj_kernel(*refs, steps, n_cast, feature_major):
    groups = len(steps)
    cast_group = steps.index(max(steps))
    cast_start = sum(steps[:cast_group])
    g_ref, w_hbm, b_ref = refs[groups:groups + 3]
    cast_in = refs[groups + 3:groups + 3 + n_cast]
    outs = refs[groups + 3 + n_cast:-3]
    cast_out = outs[7 * groups:]
    w_ref, w_stage, w_sem = refs[-3:]
    i = pl.program_id(0)

    n_chunks = w_hbm.shape[1] // W_IN_CHUNK
    converted = [0]

    def chunk_copy(c):
        return pltpu.make_async_copy(w_hbm.at[:, c * W_IN_CHUNK:(c + 1) * W_IN_CHUNK], w_stage.at[c % 2],
                                     w_sem.at[c % 2])

    @pl.when(i == 0)
    def _():
        for c in range(min(2, n_chunks)):
            chunk_copy(c).start()

    def weights_ready(col_stop):
        while converted[0] * W_IN_CHUNK < col_stop:
            c = converted[0]
            converted[0] += 1

            @pl.when(i == 0)
            def _():
                chunk_copy(c).wait()
                w_ref[:, c * W_IN_CHUNK:(c + 1) * W_IN_CHUNK] = w_stage[c % 2].astype(BF16)
                if c + 2 < n_chunks:
                    chunk_copy(c + 2).start()

    @pl.when(i == 0)
    def _():
        for src, dst in zip(cast_in[n_cast - N_WHOLE_CASTS:], cast_out[n_cast - N_WHOLE_CASTS:]):
            dst[...] = src[...].astype(BF16)

    @pl.when(jnp.logical_and(i >= cast_start, i < cast_start + steps[cast_group]))
    def _():
        for src, dst in zip(cast_in[:n_cast - N_WHOLE_CASTS], cast_out[:n_cast - N_WHOLE_CASTS]):
            dst[...] = src[...].astype(BF16)

    _run_group(steps, [functools.partial(_inproj_tile, refs[g], g_ref, w_ref, b_ref, *outs[7 * g:7 * g + 7],
                                         feature_major=feature_major[g],
                                         weights_ready=weights_ready if g == 0 else (lambda col_stop: None))
                       for g in range(groups)])
    assert converted[0] == n_chunks


def _inproj_tile(x_ref, g_ref, w_ref, b_ref, u_ref, k_ref, v_ref, qb_ref, kb_ref, vb_ref, gate_ref, *, feature_major,
                 weights_ready):
    chunk = x_ref.shape[0] // INPROJ_CHAINS

    def store_kv(ref, c, x):
        if feature_major:
            ref[0, :, c * chunk:(c + 1) * chunk] = x.T
        else:
            _store_head_rows(ref, c * chunk, x)

    c0, c1, c2, c3 = POOL_WIDTH, POOL_WIDTH + SB_WIDTH, POOL_WIDTH + 2 * SB_WIDTH, POOL_WIDTH + 3 * SB_WIDTH

    def token_chain(c):
        rows = slice(c * chunk, (c + 1) * chunk)
        xn = _rms_norm(x_ref[rows, :], g_ref[...]).astype(BF16)
        yield
        weights_ready(c1)
        u_ref[rows, :] = _dot(xn, w_ref[:, :c0])
        qb_ref[rows, :] = (_dot(xn, w_ref[:, c0:c1]) * (SB_HEAD_DIM ** -0.5 * LOG2E)).astype(BF16)
        yield
        weights_ready(c2)
        k = _dot(xn, w_ref[:, c1:c2])
        store_kv(k_ref, c, k)
        kb_ref[rows, :] = k.astype(BF16)
        yield
        weights_ready(c3)
        v = _dot(xn, w_ref[:, c2:c3])
        store_kv(v_ref, c, v)
        vb_ref[rows, :] = v.astype(BF16)
        yield
        weights_ready(w_ref.shape[1])
        gate_ref[rows, :] = jax.nn.sigmoid(_dot(xn, w_ref[:, c3:]) + b_ref[...]).astype(BF16)

    _interleave([token_chain(c) for c in range(INPROJ_CHAINS)], 1)


def _inproj(xs, stream_tiles, g_mix, w_in, b_gate, chunked, whole, tm):
    assert len(whole) == N_WHOLE_CASTS
    in_width = w_in.shape[1]
    steps = [x.shape[0] // tm for x in xs]
    maps = _group_steps(steps, lambda j: (j, 0))
    buffers = [dict(pipeline_mode=pl.Buffered(1)) if n == 1 else {} for n in steps]
    in_specs = [pl.BlockSpec((tm, D_MODEL), m, **buf) for m, buf in zip(maps, buffers)]
    in_specs += [_resident((1, D_MODEL)), pl.BlockSpec(memory_space=pl.ANY), _resident((1, 2 * D_MODEL))]
    out_specs, out_shape = [], []
    for x, m, buf, tiles in zip(xs, maps, buffers, stream_tiles):
        n = x.shape[0]
        row = lambda width: pl.BlockSpec((tm, width), m, **buf)
        if tiles is None:
            kv_spec = pl.BlockSpec((tm * SB_HEADS, SB_HEAD_DIM), m, **buf)
            kv_shape = jax.ShapeDtypeStruct((n * SB_HEADS, SB_HEAD_DIM), F32)
        else:
            kv_spec = pl.BlockSpec((1, SB_WIDTH, tm), lambda i, m=m, tiles=tiles: (m(i)[0] // tiles, 0, m(i)[0] % tiles),
                                   **buf)
            kv_shape = jax.ShapeDtypeStruct((n // (tiles * tm), SB_WIDTH, tiles * tm), F32)
        bf_out = jax.ShapeDtypeStruct((n, SB_WIDTH), BF16)
        out_specs += [row(POOL_WIDTH), kv_spec, kv_spec, row(SB_WIDTH), row(SB_WIDTH), row(SB_WIDTH),
                      row(2 * D_MODEL)]
        out_shape += [jax.ShapeDtypeStruct((n, POOL_WIDTH), F32), kv_shape, kv_shape, bf_out, bf_out, bf_out,
                      jax.ShapeDtypeStruct((n, 2 * D_MODEL), BF16)]
    cast_specs = []
    for w in chunked:
        rows = w.shape[0] // max(steps)
        assert rows * max(steps) == w.shape[0] and rows % BF16_SUBLANES == 0
        cast_specs.append(pl.BlockSpec((rows, w.shape[1]), maps[steps.index(max(steps))]))
    cast_specs += [pl.BlockSpec(w.shape, lambda i, nd=w.ndim: (0,) * nd) for w in whole]
    cast_shape = [jax.ShapeDtypeStruct(w.shape, BF16) for w in (*chunked, *whole)]
    outs = pl.pallas_call(
        functools.partial(_inproj_kernel, steps=steps, n_cast=len(cast_specs),
                          feature_major=[tiles is not None for tiles in stream_tiles]),
        grid=(sum(steps),),
        in_specs=in_specs + cast_specs,
        out_specs=out_specs + cast_specs,
        out_shape=out_shape + cast_shape,
        scratch_shapes=[pltpu.VMEM((D_MODEL, in_width), BF16), pltpu.VMEM((2, D_MODEL, W_IN_CHUNK), F32),
                        pltpu.SemaphoreType.DMA((2,))],
        compiler_params=pltpu.CompilerParams(dimension_semantics=("arbitrary",),
                                             vmem_limit_bytes=VMEM_LIMIT_BYTES),
        name="inproj",
    )(*xs, g_mix.reshape(1, D_MODEL), w_in, b_gate.reshape(1, 2 * D_MODEL), *chunked, *whole)
    n_group_outs = 7 * len(xs)
    return ([outs[7 * g:7 * g + 7] for g in range(len(xs))],
            outs[n_group_outs:n_group_outs + len(chunked)], outs[n_group_outs + len(chunked):])


def _neg_lower(n):
    r = lax.broadcasted_iota(jnp.int32, (n, n), 0)
    c = lax.broadcasted_iota(jnp.int32, (n, n), 1)
    return jnp.where(r >= c, -1.0, 0.0).astype(BF16)


def _stack_heads(x):
    lane = lax.broadcasted_iota(jnp.int32, x.shape, 1)
    zero = jnp.zeros_like(x)
    return jnp.concatenate([jnp.where(lane < SB_HEAD_DIM, x, zero), jnp.where(lane >= SB_HEAD_DIM, x, zero)], axis=0)


def _causal_mask(tq, tk):
    r = lax.broadcasted_iota(jnp.int32, (2 * tq, tk), 0)
    c = lax.broadcasted_iota(jnp.int32, (2 * tq, tk), 1)
    return c < jnp.where(r >= tq, r - tq, r)


def _sb_pair_stages(q2, blocks, carry, emit):
    tq = q2.shape[0] // 2
    logits = []
    for k_blk, _, _, causal in blocks:
        z = _dot_nt(q2, k_blk)
        if causal is not None:
            z = jnp.where(causal, z, -jnp.inf)
        logits.append(z)
    yield
    softplus = [jnp.maximum(z, 0.0) + jnp.log(1.0 + jnp.exp2(-jnp.abs(z))) * LOG2E for z in logits]
    yield
    sums = [_dot(sp.astype(BF16), neg_tri) for (_, _, neg_tri, _), sp in zip(blocks, softplus)]
    yield
    weights = []
    for z, incl in zip(logits, sums):
        a = jnp.exp2(z + incl + carry).astype(BF16)
        weights += [a[:tq], a[tq:]]
        carry = carry + incl[:, :1]
    peak = jnp.max(carry)
    yield
    values = jnp.concatenate([v2 for _, v2, _, _ in blocks], axis=0)
    emit(_dot(jnp.concatenate(weights, axis=1), values), carry, peak)


def _sb_prompt_kernel(q_ref, k_ref, v_ref, o_ref, acc_ref, carry_ref, peak_ref):
    qi = pl.program_id(2)
    blk = SB_BLOCK
    pairs = range(SB_PAIRS_PER_STEP)
    lanes = lambda p: slice(p * HEAD_PAIR, (p + 1) * HEAD_PAIR)
    q2 = [_stack_heads(q_ref[0, :, lanes(p)]) for p in pairs]
    neg_tri = _neg_lower(blk)

    def key_block(p, j, causal):
        r0 = pl.multiple_of(j * blk, blk)
        return (k_ref[0, pl.ds(r0, blk), lanes(p)], _stack_heads(v_ref[0, pl.ds(r0, blk), lanes(p)]),
                neg_tri, causal)

    def sweep(block_ids, first):
        def emit(p, out, carry, peak):
            if first:
                acc_ref[:, lanes(p)] = out
            else:
                acc_ref[:, lanes(p)] += out
            carry_ref[p] = carry
            peak_ref[p] = peak

        _interleave([_sb_pair_stages(q2[p], [key_block(p, j, causal) for j, causal in block_ids],
                                     jnp.zeros((2 * blk, 1), F32) if first else carry_ref[p],
                                     functools.partial(emit, p)) for p in pairs], SB_PAIR_LAG)

    def alive():
        peak = functools.reduce(jnp.maximum, [peak_ref[p] for p in pairs])
        return (peak > SB_SKIP_BELOW).astype(jnp.int32)

    diagonal = (qi, _causal_mask(blk, blk))
    pl.when(qi == 0)(lambda: sweep([diagonal], True))
    pl.when(qi > 0)(lambda: sweep([diagonal, (qi - 1, None)], True))

    def body(state):
        j, _ = state
        sweep([(j, None)], False)
        return j - 1, alive()

    lax.while_loop(lambda s: jnp.logical_and(s[0] >= 0, s[1] > 0), body, (qi - 2, alive()))
    o_ref[0] = acc_ref[...].astype(o_ref.dtype)


def _sb_prompt(qb, kb, vb):
    b, t, _ = qb.shape
    blk = SB_BLOCK
    width = SB_PAIRS_PER_STEP * HEAD_PAIR
    kv_spec = pl.BlockSpec((1, t, width), lambda bi, hp, qi: (bi, 0, hp))
    q_spec = pl.BlockSpec((1, blk, width), lambda bi, hp, qi: (bi, qi, hp))
    return pl.pallas_call(
        _sb_prompt_kernel,
        grid=(b, SB_WIDTH // width, t // blk),
        in_specs=[q_spec, kv_spec, kv_spec],
        out_specs=q_spec,
        out_shape=jax.ShapeDtypeStruct((b, t, SB_WIDTH), BF16),
        scratch_shapes=[pltpu.VMEM((blk, width), F32), pltpu.VMEM((SB_PAIRS_PER_STEP, 2 * blk, 1), F32),
                        pltpu.SMEM((SB_PAIRS_PER_STEP,), F32)],
        compiler_params=pltpu.CompilerParams(dimension_semantics=("parallel", "parallel", "arbitrary"),
                                             vmem_limit_bytes=VMEM_LIMIT_BYTES),
        name="sb_prompt",
    )(qb, kb, vb)


def _sb_heads_stages(q, blocks, carry, emit):
    t = q[0].shape[0]
    logits = []
    for keys, _, _, causal, feature_major in blocks:
        score = _dot if feature_major else _dot_nt
        z = jnp.concatenate([score(qh, kh) for qh, kh in zip(q, keys)], axis=0)
        logits.append(z if causal is None else jnp.where(causal, z, -jnp.inf))
    yield
    softplus = [jnp.maximum(z, 0.0) + jnp.log(1.0 + jnp.exp2(-jnp.abs(z))) * LOG2E for z in logits]
    yield
    sums = [_dot(sp.astype(BF16), neg_tri) for (_, _, neg_tri, _, _), sp in zip(blocks, softplus)]
    yield
    weights = []
    for z, incl in zip(logits, sums):
        weights.append(jnp.exp2(z + incl + carry).astype(BF16))
        carry = carry + incl[:, :1]
    peak = jnp.max(carry)
    yield
    outs = None
    for (_, values, _, _, feature_major), a in zip(blocks, weights):
        mix = _dot_nt if feature_major else _dot
        part = [mix(a[h * t:(h + 1) * t], vh) for h, vh in enumerate(values)]
        outs = part if outs is None else [o + p for o, p in zip(outs, part)]
    emit(outs, carry, peak)


def _sb_sample_kernel(q_ref, k_ref, v_ref, ck_hbm, cv_hbm, o_ref, kbuf, vbuf, acc_ref, carry_ref, peak_ref, sem,
                      *, layer, past_len):
    step = pl.program_id(0)
    slot = step % 2
    blk = SB_BLOCK
    t = q_ref.shape[1]
    heads = range(SB_HEADS)
    local = range(SB_SAMPLE_STREAMS)
    cols = lambda h: slice(h * SB_HEAD_DIM, (h + 1) * SB_HEAD_DIM)
    last = past_len // blk - 1

    def fetch(at_step, j):
        tokens = pl.ds(pl.multiple_of(j * blk, blk), blk)
        copies = []
        for s in local:
            stream = at_step * SB_SAMPLE_STREAMS + s
            copies.append(pltpu.make_async_copy(ck_hbm.at[layer, stream, :, :, tokens],
                                                kbuf.at[at_step % 2, s], sem.at[at_step % 2, s, 0]))
            copies.append(pltpu.make_async_copy(cv_hbm.at[layer, stream, :, :, tokens],
                                                vbuf.at[at_step % 2, s], sem.at[at_step % 2, s, 1]))
        return copies

    def start(copies):
        for c in copies:
            c.start()

    def wait(copies):
        for c in copies:
            c.wait()

    def cached_block(s):
        return ([kbuf[slot, s, h].astype(BF16) for h in heads], [vbuf[slot, s, h].astype(BF16) for h in heads],
                _neg_lower(blk), None, True)

    def sweep(blocks_of, first):
        def emit(s, outs, carry, peak):
            for h in heads:
                if first:
                    acc_ref[s, :, cols(h)] = outs[h]
                else:
                    acc_ref[s, :, cols(h)] += outs[h]
            carry_ref[s] = carry
            peak_ref[s] = peak

        _interleave([_sb_heads_stages([q_ref[s, :, cols(h)] for h in heads], blocks_of(s),
                                      jnp.zeros((SB_HEADS * t, 1), F32) if first else carry_ref[s],
                                      functools.partial(emit, s)) for s in local], 1)

    def alive():
        peak = functools.reduce(jnp.maximum, [peak_ref[s] for s in local])
        return (peak > SB_SKIP_BELOW).astype(jnp.int32)

    pl.when(step == 0)(lambda: start(fetch(step, last)))
    pl.when(step + 1 < pl.num_programs(0))(lambda: start(fetch(step + 1, last)))
    wait(fetch(step, last))
    r = lax.broadcasted_iota(jnp.int32, (SB_HEADS * t, t), 0)
    causal = lax.broadcasted_iota(jnp.int32, (SB_HEADS * t, t), 1) < lax.rem(r, t)
    new_block = lambda s: ([k_ref[s, :, cols(h)] for h in heads], [v_ref[s, :, cols(h)] for h in heads],
                           _neg_lower(t), causal, False)
    sweep(lambda s: [new_block(s), cached_block(s)], True)

    def body(state):
        j, _ = state
        copies = fetch(step, j)
        start(copies)
        wait(copies)
        sweep(lambda s: [cached_block(s)], False)
        return j - 1, alive()

    lax.while_loop(lambda s: jnp.logical_and(s[0] >= 0, s[1] > 0), body, (last - 1, alive()))
    o_ref[...] = acc_ref[...].astype(o_ref.dtype)


def _sb_sample(qb, kb, vb, cache_k, cache_v, layer):
    b, t, _ = qb.shape
    past_len = cache_k.shape[-1]
    n = SB_SAMPLE_STREAMS
    assert past_len % SB_BLOCK == 0 and b % n == 0
    new_spec = pl.BlockSpec((n, t, SB_WIDTH), lambda i: (i, 0, 0))
    hbm_spec = pl.BlockSpec(memory_space=pl.ANY)
    block = (2, n, SB_HEADS, SB_HEAD_DIM, SB_BLOCK)
    return pl.pallas_call(
        functools.partial(_sb_sample_kernel, layer=layer, past_len=past_len),
        grid=(b // n,),
        in_specs=[new_spec, new_spec, new_spec, hbm_spec, hbm_spec],
        out_specs=new_spec,
        out_shape=jax.ShapeDtypeStruct((b, t, SB_WIDTH), BF16),
        scratch_shapes=[pltpu.VMEM(block, F32), pltpu.VMEM(block, F32),
                        pltpu.VMEM((n, t, SB_WIDTH), F32), pltpu.VMEM((n, SB_HEADS * t, 1), F32),
                        pltpu.SMEM((n,), F32), pltpu.SemaphoreType.DMA((2, n, 2))],
        compiler_params=pltpu.CompilerParams(dimension_semantics=("arbitrary",),
                                             vmem_limit_bytes=VMEM_LIMIT_BYTES),
        name="sb_sample",
    )(qb, kb, vb, cache_k, cache_v)


def _pool_diff(ext, first_pos):
    rows = ext.shape[0] - POOL_HALO
    pos = first_pos + lax.broadcasted_iota(jnp.int32, (rows, 1), 0)
    outs = []
    for g, window in enumerate(POOL_WINDOWS):
        cols = ext[:, g * POOL_GROUP_DIM:(g + 1) * POOL_GROUP_DIM]
        acc = cols
        shift = 1
        while shift < window:
            acc = acc + pltpu.roll(acc, shift, axis=0)
            shift *= 2
        inv_count = 1.0 / jnp.minimum(pos + 1, window).astype(F32)
        outs.append(acc[POOL_HALO:] * inv_count - cols[POOL_HALO:])
    return jnp.concatenate(outs, axis=1)


POST_GROUP_INPUTS = 6
POST_WEIGHTS = 12


def _post_kernel(*refs, steps, modes, final_norm):
    groups = len(steps)
    n_in = POST_GROUP_INPUTS * groups
    weights = refs[n_in:n_in + POST_WEIGHTS]
    ys = refs[n_in + POST_WEIGHTS:]
    _run_group(steps, [functools.partial(_post_tile, *refs[POST_GROUP_INPUTS * g:POST_GROUP_INPUTS * (g + 1)], *weights,
                                         ys[g], first_step=sum(steps[:g]), final_norm=final_norm, **modes[g])
                       for g in range(groups)])


def _post_tile(x_ref, u_ref, halo_ref, o_ref, gate_ref, p_ref,
               wgrp_ref, scale_ref, wpu_ref, wsu_ref, wout_ref,
               gmlp_ref, wup_ref, wdown_ref, gple_ref, wpg_ref, wpp_ref, gfin_ref,
               y_ref, *, first_step, tiles_per_stream, streams_per_tile, past_pos, final_norm):
    tm = u_ref.shape[0]
    chunk = tm // POST_CHAINS

    def pooled(c):
        u = u_ref[c * chunk:(c + 1) * chunk, :]
        if streams_per_tile == 1:
            step = (pl.program_id(0) - first_step) % tiles_per_stream
            if c == 0:
                halo = jnp.where(step == 0, 0.0, halo_ref[...])
            else:
                halo = u_ref[c * chunk - POOL_HALO:c * chunk, :]
            diff = _pool_diff(jnp.concatenate([halo, u], axis=0), past_pos + step * tm + c * chunk)
        else:
            t = tm // streams_per_tile
            per_chunk = streams_per_tile // POST_CHAINS
            diff = jnp.concatenate(
                [_pool_diff(jnp.concatenate([halo_ref[c * per_chunk + s], u[s * t:(s + 1) * t]], axis=0), past_pos)
                 for s in range(per_chunk)], axis=0)
        return diff.astype(BF16)

    def token_chain(c):
        rows = slice(c * chunk, (c + 1) * chunk)
        diff = pooled(c)
        y_pool = jnp.concatenate(
            [_dot(diff[:, g * POOL_GROUP_DIM:(g + 1) * POOL_GROUP_DIM], wgrp_ref[g])
             for g in range(len(POOL_WINDOWS))], axis=1) * scale_ref[...]
        gates = gate_ref[rows, :].astype(F32)
        merged = (gates[:, :D_MODEL] * _dot(y_pool.astype(BF16), wpu_ref[...])
                  + gates[:, D_MODEL:] * _dot(o_ref[rows, :], wsu_ref[...]))
        yield
        x = x_ref[rows, :] + _dot(merged.astype(BF16), wout_ref[...])
        xn = _rms_norm(x, gmlp_ref[...]).astype(BF16)
        yield
        h = jnp.square(jnp.maximum(_dot(xn, wup_ref[...]), 0.0)).astype(BF16)
        yield
        x = x + _dot(h, wdown_ref[...])
        xn = _rms_norm(x, gple_ref[...]).astype(BF16)
        yield
        ple_gate = jax.nn.sigmoid(_dot(xn, wpg_ref[...]))
        x = x + ple_gate * _dot(p_ref[rows, :].astype(BF16), wpp_ref[...])
        if final_norm:
            x = _rms_norm(x, gfin_ref[...])
        y_ref[rows, :] = x

    _interleave([token_chain(c) for c in range(POST_CHAINS)], POST_LAG)


def _post(groups, weights, modes, *, tm, final_norm):
    assert len(weights) == POST_WEIGHTS and all(len(g) == POST_GROUP_INPUTS for g in groups)
    steps = [g[0].shape[0] // tm for g in groups]
    maps = _group_steps(steps, lambda j: j)
    in_specs, out_specs, out_shape = [], [], []
    for (x, *_), mode, m in zip(groups, modes, maps):
        row = lambda width, m=m: pl.BlockSpec((tm, width), lambda i: (m(i), 0))
        if mode["streams_per_tile"] == 1:
            per_tile = tm // POOL_HALO
            halo_spec = pl.BlockSpec((POOL_HALO, POOL_WIDTH),
                                     lambda i, m=m: (jnp.maximum(m(i) * per_tile - 1, 0), 0))
        else:
            halo_spec = pl.BlockSpec((mode["streams_per_tile"], POOL_HALO, POOL_WIDTH), lambda i, m=m: (m(i), 0, 0))
        in_specs += [row(D_MODEL), row(POOL_WIDTH), halo_spec, row(SB_WIDTH), row(2 * D_MODEL), row(PLE_DIM)]
        out_specs.append(row(D_MODEL))
        out_shape.append(jax.ShapeDtypeStruct((x.shape[0], D_MODEL), F32))
    in_specs += [_resident(w.shape) for w in weights]
    return pl.pallas_call(
        functools.partial(_post_kernel, steps=steps, modes=modes, final_norm=final_norm),
        grid=(sum(steps),),
        in_specs=in_specs,
        out_specs=out_specs,
        out_shape=out_shape,
        compiler_params=pltpu.CompilerParams(dimension_semantics=("arbitrary",),
                                             vmem_limit_bytes=VMEM_LIMIT_BYTES),
        name="post",
    )(*[a for g in groups for a in g], *weights)


def kernel(x_prompt, x_sample, cache_k, cache_v, state_pool, p_prompt, p_sample, g_mix, w_in, b_gate, w_pool_grp, pool_scale, w_pool_up, w_sb_up, w_out, g_mlp, w_up, w_down, g_ple, w_ple_gate, w_ple_proj, g_final):
    depth = w_in.shape[0]
    bp, tp, _ = x_prompt.shape
    bs, ts, _ = x_sample.shape
    past_len = cache_k.shape[2]
    assert tp % ROW_TILE == 0 and tp % SB_BLOCK == 0
    n_s = bs * ts
    assert ts >= POOL_STATE and ts % 8 == 0 and n_s % ROW_TILE == 0 and (ROW_TILE // ts) % POST_CHAINS == 0
    xp = x_prompt.reshape(bp * tp, D_MODEL)
    xs = x_sample.reshape(n_s, D_MODEL)
    row_vec = lambda a: a.reshape(1, -1)
    cache_rows = lambda c: jnp.transpose(c, (0, 1, 3, 4, 2))

    outs = {name: [] for name in ("kp", "vp", "pp", "ks", "vs", "ps")}
    for d in range(depth):
        final_norm = d == depth - 1

        (((us, ks, vs, qbs, kbs, vbs, gates_s), (up, kp, vp, qbp, kbp, vbp, gates_p)),
         (wpu_bf, wsu_bf, wout_bf, wup_bf, wdown_bf, wpg_bf), (wpp_bf, wgrp_bf)) = _inproj(
            [xs, xp], [None, tp // ROW_TILE], g_mix[d], w_in[d], b_gate[d],
            [w_pool_up[d], w_sb_up[d], w_out[d], w_up[d], w_down[d], w_ple_gate[d]],
            [w_ple_proj[d], w_pool_grp[d]], ROW_TILE)
        weights = (wgrp_bf, row_vec(pool_scale[d]), wpu_bf, wsu_bf, wout_bf, row_vec(g_mlp[d]), wup_bf,
                   wdown_bf, row_vec(g_ple[d]), wpg_bf, wpp_bf, row_vec(g_final))
        op = _sb_prompt(qbp.reshape(bp, tp, SB_WIDTH), kbp.reshape(bp, tp, SB_WIDTH), vbp.reshape(bp, tp, SB_WIDTH))
        os_ = _sb_sample(qbs.reshape(bs, ts, SB_WIDTH), kbs.reshape(bs, ts, SB_WIDTH), vbs.reshape(bs, ts, SB_WIDTH),
                         cache_rows(cache_k), cache_rows(cache_v), d)
        halo_s = jnp.pad(state_pool[d], ((0, 0), (POOL_HALO - POOL_STATE, 0), (0, 0)))
        xp, = _post([(xp, up, up, op.reshape(bp * tp, SB_WIDTH), gates_p, p_prompt[d].reshape(bp * tp, PLE_DIM))],
                    weights, [dict(tiles_per_stream=tp // ROW_TILE, streams_per_tile=1, past_pos=0)],
                    tm=ROW_TILE, final_norm=final_norm)
        xs, = _post([(xs, us, halo_s, os_.reshape(n_s, SB_WIDTH), gates_s, p_sample[d].reshape(n_s, PLE_DIM))],
                    weights, [dict(tiles_per_stream=1, streams_per_tile=ROW_TILE // ts, past_pos=POOL_STATE)],
                    tm=ROW_TILE, final_norm=final_norm)
        token_major = lambda a: jnp.transpose(a.reshape(bp, SB_HEADS, SB_HEAD_DIM, tp), (0, 3, 1, 2))
        outs["kp"].append(token_major(kp))
        outs["vp"].append(token_major(vp))
        outs["pp"].append(up.reshape(bp, tp, POOL_WIDTH)[:, tp - POOL_STATE:])
        outs["ks"].append(ks.reshape(bs, ts, SB_HEADS, SB_HEAD_DIM))
        outs["vs"].append(vs.reshape(bs, ts, SB_HEADS, SB_HEAD_DIM))
        outs["ps"].append(us.reshape(bs, ts, POOL_WIDTH)[:, ts - POOL_STATE:])

    stack = lambda name: jnp.stack(outs[name])
    return (xp.reshape(bp, tp, D_MODEL), xs.reshape(bs, ts, D_MODEL),
            stack("kp"), stack("vp"), stack("pp"), stack("ks"), stack("vs"), stack("ps"))
```

```python
import functools

import jax
import jax.numpy as jnp
from jax import lax
from jax.experimental import pallas as pl
from jax.experimental.pallas import tpu as pltpu

D_MODEL = 1024
POOL_WIDTH = 512
POOL_WINDOWS = (2, 4, 8, 16)
POOL_GROUP_DIM = POOL_WIDTH // len(POOL_WINDOWS)
POOL_STATE = max(POOL_WINDOWS) - 1
POOL_HALO = 16
SB_HEADS = 8
SB_HEAD_DIM = 64
SB_WIDTH = SB_HEADS * SB_HEAD_DIM
HEAD_PAIR = 2 * SB_HEAD_DIM
D_FF = 4 * D_MODEL
PLE_DIM = 256
EPS = 1e-6

V7X_VMEM_BYTES = 64 * 1024 * 1024
VMEM_LIMIT_BYTES = V7X_VMEM_BYTES - 8 * 1024 * 1024

ROW_TILE = 512
INPROJ_CHAINS = 2
POST_CHAINS = 2
POST_LAG = 1
SB_BLOCK = 256
SB_PAIRS_PER_STEP = 4
SB_SAMPLE_STREAMS = 2
SB_PAIR_LAG = 2

BF16_SUBLANES = 16
W_IN_CHUNK = 512
N_WHOLE_CASTS = 2

LOG2E = 1.4426950408889634
SB_SKIP_BELOW = -160.0

BF16 = jnp.bfloat16
F32 = jnp.float32


def _inv_rms(x):
    return lax.rsqrt(jnp.mean(x * x, axis=-1, keepdims=True) + EPS)


def _rms_norm(x, g):
    return x * _inv_rms(x) * g


def _dot(a, b):
    return jnp.dot(a, b, preferred_element_type=F32)


def _dot_nt(a, b):
    return lax.dot_general(a, b, (((1,), (1,)), ((), ())), preferred_element_type=F32)


def _resident(shape):
    return pl.BlockSpec(shape, lambda *_: (0,) * len(shape), pipeline_mode=pl.Buffered(1))


def _interleave(chains, lag):
    waiting, live, tick = list(chains), [], 0
    while waiting or live:
        if waiting and tick % lag == 0:
            live.append(waiting.pop(0))
        live = [c for c in live if next(c, True) is None]
        tick += 1


def _store_head_rows(ref, row0, x):
    rows = x.shape[0]
    for h in range(SB_HEADS):
        ref[pl.ds(row0 * SB_HEADS + h, rows, stride=SB_HEADS), :] = x[:, h * SB_HEAD_DIM:(h + 1) * SB_HEAD_DIM]


def _group_steps(steps, maps_to):
    starts = [sum(steps[:g]) for g in range(len(steps))]
    return [lambda i, s=s, n=n: maps_to(jnp.clip(i - s, 0, n - 1)) for s, n in zip(starts, steps)]


def _run_group(steps, bodies):
    i = pl.program_id(0)
    start = 0
    for n, body in zip(steps, bodies):
        pl.when(jnp.logical_and(i >= start, i < start + n))(body)
        start += n


def _inproj_kernel(*refs, steps, n_cast, feature_major):
    groups = len(steps)
    cast_group = steps.index(max(steps))
    cast_start = sum(steps[:cast_group])
    g_ref, w_hbm, b_ref = refs[groups:groups + 3]
    cast_in = refs[groups + 3:groups + 3 + n_cast]
    outs = refs[groups + 3 + n_cast:-3]
    cast_out = outs[7 * groups:]
    w_ref, w_stage, w_sem = refs[-3:]
    i = pl.program_id(0)

    n_chunks = w_hbm.shape[1] // W_IN_CHUNK
    converted = [0]

    def chunk_copy(c):
        return pltpu.make_async_copy(w_hbm.at[:, c * W_IN_CHUNK:(c + 1) * W_IN_CHUNK], w_stage.at[c % 2],
                                     w_sem.at[c % 2])

    @pl.when(i == 0)
    def _():
        for c in range(min(2, n_chunks)):
            chunk_copy(c).start()

    def weights_ready(col_stop):
        while converted[0] * W_IN_CHUNK < col_stop:
            c = converted[0]
            converted[0] += 1

            @pl.when(i == 0)
            def _():
                chunk_copy(c).wait()
                w_ref[:, c * W_IN_CHUNK:(c + 1) * W_IN_CHUNK] = w_stage[c % 2].astype(BF16)
                if c + 2 < n_chunks:
                    chunk_copy(c + 2).start()

    @pl.when(i == 0)
    def _():
        for src, dst in zip(cast_in[n_cast - N_WHOLE_CASTS:], cast_out[n_cast - N_WHOLE_CASTS:]):
            dst[...] = src[...].astype(BF16)

    @pl.when(jnp.logical_and(i >= cast_start, i < cast_start + steps[cast_group]))
    def _():
        for src, dst in zip(cast_in[:n_cast - N_WHOLE_CASTS], cast_out[:n_cast - N_WHOLE_CASTS]):
            dst[...] = src[...].astype(BF16)

    _run_group(steps, [functools.partial(_inproj_tile, refs[g], g_ref, w_ref, b_ref, *outs[7 * g:7 * g + 7],
                                         feature_major=feature_major[g],
                                         weights_ready=weights_ready if g == 0 else (lambda col_stop: None))
                       for g in range(groups)])
    assert converted[0] == n_chunks


def _inproj_tile(x_ref, g_ref, w_ref, b_ref, u_ref, k_ref, v_ref, qb_ref, kb_ref, vb_ref, gate_ref, *, feature_major,
                 weights_ready):
    chunk = x_ref.shape[0] // INPROJ_CHAINS

    def store_kv(ref, c, x):
        if feature_major:
            ref[0, :, c * chunk:(c + 1) * chunk] = x.T
        else:
            _store_head_rows(ref, c * chunk, x)

    c0, c1, c2, c3 = POOL_WIDTH, POOL_WIDTH + SB_WIDTH, POOL_WIDTH + 2 * SB_WIDTH, POOL_WIDTH + 3 * SB_WIDTH

    def token_chain(c):
        rows = slice(c * chunk, (c + 1) * chunk)
        x = x_ref[rows, :]
        r = _inv_rms(x)
        xg = (x * g_ref[...]).astype(BF16)
        yield
        weights_ready(c1)
        u_ref[rows, :] = r * _dot(xg, w_ref[:, :c0])
        qb_ref[rows, :] = (_dot(xg, w_ref[:, c0:c1]) * (r * (SB_HEAD_DIM ** -0.5 * LOG2E))).astype(BF16)
        yield
        weights_ready(c2)
        k = r * _dot(xg, w_ref[:, c1:c2])
        store_kv(k_ref, c, k)
        kb_ref[rows, :] = k.astype(BF16)
        yield
        weights_ready(c3)
        v = r * _dot(xg, w_ref[:, c2:c3])
        store_kv(v_ref, c, v)
        vb_ref[rows, :] = v.astype(BF16)
        yield
        weights_ready(w_ref.shape[1])
        gate_ref[rows, :] = jax.nn.sigmoid(r * _dot(xg, w_ref[:, c3:]) + b_ref[...]).astype(BF16)

    _interleave([token_chain(c) for c in range(INPROJ_CHAINS)], 1)


def _inproj(xs, stream_tiles, g_mix, w_in, b_gate, chunked, whole, tm):
    assert len(whole) == N_WHOLE_CASTS
    in_width = w_in.shape[1]
    steps = [x.shape[0] // tm for x in xs]
    maps = _group_steps(steps, lambda j: (j, 0))
    buffers = [dict(pipeline_mode=pl.Buffered(1)) if n == 1 else {} for n in steps]
    in_specs = [pl.BlockSpec((tm, D_MODEL), m, **buf) for m, buf in zip(maps, buffers)]
    in_specs += [_resident((1, D_MODEL)), pl.BlockSpec(memory_space=pl.ANY), _resident((1, 2 * D_MODEL))]
    out_specs, out_shape = [], []
    for x, m, buf, tiles in zip(xs, maps, buffers, stream_tiles):
        n = x.shape[0]
        row = lambda width: pl.BlockSpec((tm, width), m, **buf)
        if tiles is None:
            kv_spec = pl.BlockSpec((tm * SB_HEADS, SB_HEAD_DIM), m, **buf)
            kv_shape = jax.ShapeDtypeStruct((n * SB_HEADS, SB_HEAD_DIM), F32)
        else:
            kv_spec = pl.BlockSpec((1, SB_WIDTH, tm), lambda i, m=m, tiles=tiles: (m(i)[0] // tiles, 0, m(i)[0] % tiles),
                                   **buf)
            kv_shape = jax.ShapeDtypeStruct((n // (tiles * tm), SB_WIDTH, tiles * tm), F32)
        bf_out = jax.ShapeDtypeStruct((n, SB_WIDTH), BF16)
        out_specs += [row(POOL_WIDTH), kv_spec, kv_spec, row(SB_WIDTH), row(SB_WIDTH), row(SB_WIDTH),
                      row(2 * D_MODEL)]
        out_shape += [jax.ShapeDtypeStruct((n, POOL_WIDTH), F32), kv_shape, kv_shape, bf_out, bf_out, bf_out,
                      jax.ShapeDtypeStruct((n, 2 * D_MODEL), BF16)]
    cast_specs = []
    for w in chunked:
        rows = w.shape[0] // max(steps)
        assert rows * max(steps) == w.shape[0] and rows % BF16_SUBLANES == 0
        cast_specs.append(pl.BlockSpec((rows, w.shape[1]), maps[steps.index(max(steps))]))
    cast_specs += [pl.BlockSpec(w.shape, lambda i, nd=w.ndim: (0,) * nd) for w in whole]
    cast_shape = [jax.ShapeDtypeStruct(w.shape, BF16) for w in (*chunked, *whole)]
    outs = pl.pallas_call(
        functools.partial(_inproj_kernel, steps=steps, n_cast=len(cast_specs),
                          feature_major=[tiles is not None for tiles in stream_tiles]),
        grid=(sum(steps),),
        in_specs=in_specs + cast_specs,
        out_specs=out_specs + cast_specs,
        out_shape=out_shape + cast_shape,
        scratch_shapes=[pltpu.VMEM((D_MODEL, in_width), BF16), pltpu.VMEM((2, D_MODEL, W_IN_CHUNK), F32),
                        pltpu.SemaphoreType.DMA((2,))],
        compiler_params=pltpu.CompilerParams(dimension_semantics=("arbitrary",),
                                             vmem_limit_bytes=VMEM_LIMIT_BYTES),
        name="inproj",
    )(*xs, g_mix.reshape(1, D_MODEL), w_in, b_gate.reshape(1, 2 * D_MODEL), *chunked, *whole)
    n_group_outs = 7 * len(xs)
    return ([outs[7 * g:7 * g + 7] for g in range(len(xs))],
            outs[n_group_outs:n_group_outs + len(chunked)], outs[n_group_outs + len(chunked):])


def _neg_lower(n):
    r = lax.broadcasted_iota(jnp.int32, (n, n), 0)
    c = lax.broadcasted_iota(jnp.int32, (n, n), 1)
    return jnp.where(r >= c, -1.0, 0.0).astype(BF16)


def _stack_heads(x):
    lane = lax.broadcasted_iota(jnp.int32, x.shape, 1)
    zero = jnp.zeros_like(x)
    return jnp.concatenate([jnp.where(lane < SB_HEAD_DIM, x, zero), jnp.where(lane >= SB_HEAD_DIM, x, zero)], axis=0)


def _causal_mask(tq, tk):
    r = lax.broadcasted_iota(jnp.int32, (2 * tq, tk), 0)
    c = lax.broadcasted_iota(jnp.int32, (2 * tq, tk), 1)
    return c < jnp.where(r >= tq, r - tq, r)


def _sb_pair_stages(q2, blocks, carry, emit):
    tq = q2.shape[0] // 2
    logits = []
    for k_blk, _, _, causal in blocks:
        z = _dot_nt(q2, k_blk)
        if causal is not None:
            z = jnp.where(causal, z, -jnp.inf)
        logits.append(z)
    yield
    softplus = [jnp.maximum(z, 0.0) + jnp.log(1.0 + jnp.exp2(-jnp.abs(z))) * LOG2E for z in logits]
    yield
    sums = [_dot(sp.astype(BF16), neg_tri) for (_, _, neg_tri, _), sp in zip(blocks, softplus)]
    yield
    weights = []
    for z, incl in zip(logits, sums):
        a = jnp.exp2(z + incl + carry).astype(BF16)
        weights += [a[:tq], a[tq:]]
        carry = carry + incl[:, :1]
    peak = jnp.max(carry)
    yield
    values = jnp.concatenate([v2 for _, v2, _, _ in blocks], axis=0)
    emit(_dot(jnp.concatenate(weights, axis=1), values), carry, peak)


def _sb_prompt_kernel(q_ref, k_ref, v_ref, o_ref, acc_ref, carry_ref, peak_ref):
    qi = pl.program_id(2)
    blk = SB_BLOCK
    pairs = range(SB_PAIRS_PER_STEP)
    lanes = lambda p: slice(p * HEAD_PAIR, (p + 1) * HEAD_PAIR)
    q2 = [_stack_heads(q_ref[0, :, lanes(p)]) for p in pairs]
    neg_tri = _neg_lower(blk)

    def key_block(p, j, causal):
        r0 = pl.multiple_of(j * blk, blk)
        return (k_ref[0, pl.ds(r0, blk), lanes(p)], _stack_heads(v_ref[0, pl.ds(r0, blk), lanes(p)]),
                neg_tri, causal)

    def sweep(block_ids, first):
        def emit(p, out, carry, peak):
            if first:
                acc_ref[:, lanes(p)] = out
            else:
                acc_ref[:, lanes(p)] += out
            carry_ref[p] = carry
            peak_ref[p] = peak

        _interleave([_sb_pair_stages(q2[p], [key_block(p, j, causal) for j, causal in block_ids],
                                     jnp.zeros((2 * blk, 1), F32) if first else carry_ref[p],
                                     functools.partial(emit, p)) for p in pairs], SB_PAIR_LAG)

    def alive():
        peak = functools.reduce(jnp.maximum, [peak_ref[p] for p in pairs])
        return (peak > SB_SKIP_BELOW).astype(jnp.int32)

    diagonal = (qi, _causal_mask(blk, blk))
    pl.when(qi == 0)(lambda: sweep([diagonal], True))
    pl.when(qi > 0)(lambda: sweep([diagonal, (qi - 1, None)], True))

    def body(state):
        j, _ = state
        sweep([(j, None)], False)
        return j - 1, alive()

    lax.while_loop(lambda s: jnp.logical_and(s[0] >= 0, s[1] > 0), body, (qi - 2, alive()))
    o_ref[0] = acc_ref[...].astype(o_ref.dtype)


def _sb_prompt(qb, kb, vb):
    b, t, _ = qb.shape
    blk = SB_BLOCK
    width = SB_PAIRS_PER_STEP * HEAD_PAIR
    kv_spec = pl.BlockSpec((1, t, width), lambda bi, hp, qi: (bi, 0, hp))
    q_spec = pl.BlockSpec((1, blk, width), lambda bi, hp, qi: (bi, qi, hp))
    return pl.pallas_call(
        _sb_prompt_kernel,
        grid=(b, SB_WIDTH // width, t // blk),
        in_specs=[q_spec, kv_spec, kv_spec],
        out_specs=q_spec,
        out_shape=jax.ShapeDtypeStruct((b, t, SB_WIDTH), BF16),
        scratch_shapes=[pltpu.VMEM((blk, width), F32), pltpu.VMEM((SB_PAIRS_PER_STEP, 2 * blk, 1), F32),
                        pltpu.SMEM((SB_PAIRS_PER_STEP,), F32)],
        compiler_params=pltpu.CompilerParams(dimension_semantics=("parallel", "parallel", "arbitrary"),
                                             vmem_limit_bytes=VMEM_LIMIT_BYTES),
        name="sb_prompt",
    )(qb, kb, vb)


def _sb_heads_stages(q, blocks, carry, emit):
    t = q[0].shape[0]
    logits = []
    for keys, _, _, causal, feature_major in blocks:
        score = _dot if feature_major else _dot_nt
        z = jnp.concatenate([score(qh, kh) for qh, kh in zip(q, keys)], axis=0)
        logits.append(z if causal is None else jnp.where(causal, z, -jnp.inf))
    yield
    softplus = [jnp.maximum(z, 0.0) + jnp.log(1.0 + jnp.exp2(-jnp.abs(z))) * LOG2E for z in logits]
    yield
    sums = [_dot(sp.astype(BF16), neg_tri) for (_, _, neg_tri, _, _), sp in zip(blocks, softplus)]
    yield
    weights = []
    for z, incl in zip(logits, sums):
        weights.append(jnp.exp2(z + incl + carry).astype(BF16))
        carry = carry + incl[:, :1]
    peak = jnp.max(carry)
    yield
    outs = None
    for (_, values, _, _, feature_major), a in zip(blocks, weights):
        mix = _dot_nt if feature_major else _dot
        part = [mix(a[h * t:(h + 1) * t], vh) for h, vh in enumerate(values)]
        outs = part if outs is None else [o + p for o, p in zip(outs, part)]
    emit(outs, carry, peak)


def _sb_sample_kernel(q_ref, k_ref, v_ref, ck_hbm, cv_hbm, o_ref, kbuf, vbuf, acc_ref, carry_ref, peak_ref, sem,
                      *, layer, past_len):
    step = pl.program_id(0)
    slot = step % 2
    blk = SB_BLOCK
    t = q_ref.shape[1]
    heads = range(SB_HEADS)
    local = range(SB_SAMPLE_STREAMS)
    cols = lambda h: slice(h * SB_HEAD_DIM, (h + 1) * SB_HEAD_DIM)
    last = past_len // blk - 1

    def fetch(at_step, j):
        tokens = pl.ds(pl.multiple_of(j * blk, blk), blk)
        copies = []
        for s in local:
            stream = at_step * SB_SAMPLE_STREAMS + s
            copies.append(pltpu.make_async_copy(ck_hbm.at[layer, stream, :, :, tokens],
                                                kbuf.at[at_step % 2, s], sem.at[at_step % 2, s, 0]))
            copies.append(pltpu.make_async_copy(cv_hbm.at[layer, stream, :, :, tokens],
                                                vbuf.at[at_step % 2, s], sem.at[at_step % 2, s, 1]))
        return copies

    def start(copies):
        for c in copies:
            c.start()

    def wait(copies):
        for c in copies:
            c.wait()

    def cached_block(s):
        return ([kbuf[slot, s, h].astype(BF16) for h in heads], [vbuf[slot, s, h].astype(BF16) for h in heads],
                _neg_lower(blk), None, True)

    def sweep(blocks_of, first):
        def emit(s, outs, carry, peak):
            for h in heads:
                if first:
                    acc_ref[s, :, cols(h)] = outs[h]
                else:
                    acc_ref[s, :, cols(h)] += outs[h]
            carry_ref[s] = carry
            peak_ref[s] = peak

        _interleave([_sb_heads_stages([q_ref[s, :, cols(h)] for h in heads], blocks_of(s),
                                      jnp.zeros((SB_HEADS * t, 1), F32) if first else carry_ref[s],
                                      functools.partial(emit, s)) for s in local], 1)

    def alive():
        peak = functools.reduce(jnp.maximum, [peak_ref[s] for s in local])
        return (peak > SB_SKIP_BELOW).astype(jnp.int32)

    pl.when(step == 0)(lambda: start(fetch(step, last)))
    pl.when(step + 1 < pl.num_programs(0))(lambda: start(fetch(step + 1, last)))
    wait(fetch(step, last))
    r = lax.broadcasted_iota(jnp.int32, (SB_HEADS * t, t), 0)
    causal = lax.broadcasted_iota(jnp.int32, (SB_HEADS * t, t), 1) < lax.rem(r, t)
    new_block = lambda s: ([k_ref[s, :, cols(h)] for h in heads], [v_ref[s, :, cols(h)] for h in heads],
                           _neg_lower(t), causal, False)
    sweep(lambda s: [new_block(s), cached_block(s)], True)

    def body(state):
        j, _ = state
        copies = fetch(step, j)
        start(copies)
        wait(copies)
        sweep(lambda s: [cached_block(s)], False)
        return j - 1, alive()

    lax.while_loop(lambda s: jnp.logical_and(s[0] >= 0, s[1] > 0), body, (last - 1, alive()))
    o_ref[...] = acc_ref[...].astype(o_ref.dtype)


def _sb_sample(qb, kb, vb, cache_k, cache_v, layer):
    b, t, _ = qb.shape
    past_len = cache_k.shape[-1]
    n = SB_SAMPLE_STREAMS
    assert past_len % SB_BLOCK == 0 and b % n == 0
    new_spec = pl.BlockSpec((n, t, SB_WIDTH), lambda i: (i, 0, 0))
    hbm_spec = pl.BlockSpec(memory_space=pl.ANY)
    block = (2, n, SB_HEADS, SB_HEAD_DIM, SB_BLOCK)
    return pl.pallas_call(
        functools.partial(_sb_sample_kernel, layer=layer, past_len=past_len),
        grid=(b // n,),
        in_specs=[new_spec, new_spec, new_spec, hbm_spec, hbm_spec],
        out_specs=new_spec,
        out_shape=jax.ShapeDtypeStruct((b, t, SB_WIDTH), BF16),
        scratch_shapes=[pltpu.VMEM(block, F32), pltpu.VMEM(block, F32),
                        pltpu.VMEM((n, t, SB_WIDTH), F32), pltpu.VMEM((n, SB_HEADS * t, 1), F32),
                        pltpu.SMEM((n,), F32), pltpu.SemaphoreType.DMA((2, n, 2))],
        compiler_params=pltpu.CompilerParams(dimension_semantics=("arbitrary",),
                                             vmem_limit_bytes=VMEM_LIMIT_BYTES),
        name="sb_sample",
    )(qb, kb, vb, cache_k, cache_v)


def _pool_diff(ext, first_pos):
    short_windows = not (isinstance(first_pos, int) and first_pos >= POOL_STATE)
    if short_windows:
        pos = first_pos + lax.broadcasted_iota(jnp.int32, (POOL_HALO, 1), 0)
    outs = []
    for g, window in enumerate(POOL_WINDOWS):
        cols = ext[:, g * POOL_GROUP_DIM:(g + 1) * POOL_GROUP_DIM]
        acc = cols
        shift = 1
        while shift < window:
            acc = acc + pltpu.roll(acc, shift, axis=0)
            shift *= 2
        out = acc[POOL_HALO:] * (1.0 / window) - cols[POOL_HALO:]
        if short_windows:
            inv_count = 1.0 / jnp.minimum(pos + 1, window).astype(F32)
            head = acc[POOL_HALO:2 * POOL_HALO] * inv_count - cols[POOL_HALO:2 * POOL_HALO]
            out = jnp.concatenate([head, out[POOL_HALO:]], axis=0)
        outs.append(out)
    return jnp.concatenate(outs, axis=1)


POST_GROUP_INPUTS = 6
POST_WEIGHTS = 12


def _post_kernel(*refs, steps, modes, final_norm):
    groups = len(steps)
    n_in = POST_GROUP_INPUTS * groups
    weights = refs[n_in:n_in + POST_WEIGHTS]
    ys = refs[n_in + POST_WEIGHTS:]
    _run_group(steps, [functools.partial(_post_tile, *refs[POST_GROUP_INPUTS * g:POST_GROUP_INPUTS * (g + 1)], *weights,
                                         ys[g], first_step=sum(steps[:g]), final_norm=final_norm, **modes[g])
                       for g in range(groups)])


def _post_tile(x_ref, u_ref, halo_ref, o_ref, gate_ref, p_ref,
               wgrp_ref, scale_ref, wpu_ref, wsu_ref, wout_ref,
               gmlp_ref, wup_ref, wdown_ref, gple_ref, wpg_ref, wpp_ref, gfin_ref,
               y_ref, *, first_step, tiles_per_stream, streams_per_tile, past_pos, final_norm):
    tm = u_ref.shape[0]
    chunk = tm // POST_CHAINS

    def pooled(c):
        u = u_ref[c * chunk:(c + 1) * chunk, :]
        if streams_per_tile == 1:
            step = (pl.program_id(0) - first_step) % tiles_per_stream
            if c == 0:
                halo = jnp.where(step == 0, 0.0, halo_ref[...])
            else:
                halo = u_ref[c * chunk - POOL_HALO:c * chunk, :]
            at_least = past_pos + c * chunk
            first_pos = at_least if at_least >= POOL_STATE else at_least + step * tm
            diff = _pool_diff(jnp.concatenate([halo, u], axis=0), first_pos)
        else:
            t = tm // streams_per_tile
            per_chunk = streams_per_tile // POST_CHAINS
            diff = jnp.concatenate(
                [_pool_diff(jnp.concatenate([halo_ref[c * per_chunk + s], u[s * t:(s + 1) * t]], axis=0), past_pos)
                 for s in range(per_chunk)], axis=0)
        return diff.astype(BF16)

    def token_chain(c):
        rows = slice(c * chunk, (c + 1) * chunk)
        diff = pooled(c)
        y_pool = jnp.concatenate(
            [_dot(diff[:, g * POOL_GROUP_DIM:(g + 1) * POOL_GROUP_DIM], wgrp_ref[g])
             for g in range(len(POOL_WINDOWS))], axis=1) * scale_ref[...]
        gates = gate_ref[rows, :].astype(F32)
        merged = (gates[:, :D_MODEL] * _dot(y_pool.astype(BF16), wpu_ref[...])
                  + gates[:, D_MODEL:] * _dot(o_ref[rows, :], wsu_ref[...]))
        yield
        x = x_ref[rows, :] + _dot(merged.astype(BF16), wout_ref[...])
        r = _inv_rms(x)
        xg = (x * gmlp_ref[...]).astype(BF16)
        yield
        h = jnp.square(jnp.maximum(_dot(xg, wup_ref[...]), 0.0)).astype(BF16)
        yield
        x = x + (r * r) * _dot(h, wdown_ref[...])
        r = _inv_rms(x)
        xg = (x * gple_ref[...]).astype(BF16)
        yield
        ple_gate = jax.nn.sigmoid(r * _dot(xg, wpg_ref[...]))
        x = x + ple_gate * _dot(p_ref[rows, :].astype(BF16), wpp_ref[...])
        if final_norm:
            x = _rms_norm(x, gfin_ref[...])
        y_ref[rows, :] = x

    _interleave([token_chain(c) for c in range(POST_CHAINS)], POST_LAG)


def _post(groups, weights, modes, *, tm, final_norm):
    assert len(weights) == POST_WEIGHTS and all(len(g) == POST_GROUP_INPUTS for g in groups)
    steps = [g[0].shape[0] // tm for g in groups]
    maps = _group_steps(steps, lambda j: j)
    in_specs, out_specs, out_shape = [], [], []
    for (x, *_), mode, m in zip(groups, modes, maps):
        row = lambda width, m=m: pl.BlockSpec((tm, width), lambda i: (m(i), 0))
        if mode["streams_per_tile"] == 1:
            per_tile = tm // POOL_HALO
            halo_spec = pl.BlockSpec((POOL_HALO, POOL_WIDTH),
                                     lambda i, m=m: (jnp.maximum(m(i) * per_tile - 1, 0), 0))
        else:
            halo_spec = pl.BlockSpec((mode["streams_per_tile"], POOL_HALO, POOL_WIDTH), lambda i, m=m: (m(i), 0, 0))
        in_specs += [row(D_MODEL), row(POOL_WIDTH), halo_spec, row(SB_WIDTH), row(2 * D_MODEL), row(PLE_DIM)]
        out_specs.append(row(D_MODEL))
        out_shape.append(jax.ShapeDtypeStruct((x.shape[0], D_MODEL), F32))
    in_specs += [_resident(w.shape) for w in weights]
    return pl.pallas_call(
        functools.partial(_post_kernel, steps=steps, modes=modes, final_norm=final_norm),
        grid=(sum(steps),),
        in_specs=in_specs,
        out_specs=out_specs,
        out_shape=out_shape,
        compiler_params=pltpu.CompilerParams(dimension_semantics=("arbitrary",),
                                             vmem_limit_bytes=VMEM_LIMIT_BYTES),
        name="post",
    )(*[a for g in groups for a in g], *weights)


def kernel(x_prompt, x_sample, cache_k, cache_v, state_pool, p_prompt, p_sample, g_mix, w_in, b_gate, w_pool_grp, pool_scale, w_pool_up, w_sb_up, w_out, g_mlp, w_up, w_down, g_ple, w_ple_gate, w_ple_proj, g_final):
    depth = w_in.shape[0]
    bp, tp, _ = x_prompt.shape
    bs, ts, _ = x_sample.shape
    past_len = cache_k.shape[2]
    assert tp % ROW_TILE == 0 and tp % SB_BLOCK == 0
    n_s = bs * ts
    assert ts >= POOL_STATE and ts % 8 == 0 and n_s % ROW_TILE == 0 and (ROW_TILE // ts) % POST_CHAINS == 0
    xp = x_prompt.reshape(bp * tp, D_MODEL)
    xs = x_sample.reshape(n_s, D_MODEL)
    row_vec = lambda a: a.reshape(1, -1)
    cache_rows = lambda c: jnp.transpose(c, (0, 1, 3, 4, 2))

    outs = {name: [] for name in ("kp", "vp", "pp", "ks", "vs", "ps")}
    for d in range(depth):
        final_norm = d == depth - 1

        (((us, ks, vs, qbs, kbs, vbs, gates_s), (up, kp, vp, qbp, kbp, vbp, gates_p)),
         (wpu_bf, wsu_bf, wout_bf, wup_bf, wdown_bf, wpg_bf), (wpp_bf, wgrp_bf)) = _inproj(
            [xs, xp], [None, tp // ROW_TILE], g_mix[d], w_in[d], b_gate[d],
            [w_pool_up[d], w_sb_up[d], w_out[d], w_up[d], w_down[d], w_ple_gate[d]],
            [w_ple_proj[d], w_pool_grp[d]], ROW_TILE)
        weights = (wgrp_bf, row_vec(pool_scale[d]), wpu_bf, wsu_bf, wout_bf, row_vec(g_mlp[d]), wup_bf,
                   wdown_bf, row_vec(g_ple[d]), wpg_bf, wpp_bf, row_vec(g_final))
        op = _sb_prompt(qbp.reshape(bp, tp, SB_WIDTH), kbp.reshape(bp, tp, SB_WIDTH), vbp.reshape(bp, tp, SB_WIDTH))
        os_ = _sb_sample(qbs.reshape(bs, ts, SB_WIDTH), kbs.reshape(bs, ts, SB_WIDTH), vbs.reshape(bs, ts, SB_WIDTH),
                         cache_rows(cache_k), cache_rows(cache_v), d)
        halo_s = jnp.pad(state_pool[d], ((0, 0), (POOL_HALO - POOL_STATE, 0), (0, 0)))
        xp, = _post([(xp, up, up, op.reshape(bp * tp, SB_WIDTH), gates_p, p_prompt[d].reshape(bp * tp, PLE_DIM))],
                    weights, [dict(tiles_per_stream=tp // ROW_TILE, streams_per_tile=1, past_pos=0)],
                    tm=ROW_TILE, final_norm=final_norm)
        xs, = _post([(xs, us, halo_s, os_.reshape(n_s, SB_WIDTH), gates_s, p_sample[d].reshape(n_s, PLE_DIM))],
                    weights, [dict(tiles_per_stream=1, streams_per_tile=ROW_TILE // ts, past_pos=POOL_STATE)],
                    tm=ROW_TILE, final_norm=final_norm)
        token_major = lambda a: jnp.transpose(a.reshape(bp, SB_HEADS, SB_HEAD_DIM, tp), (0, 3, 1, 2))
        outs["kp"].append(token_major(kp))
        outs["vp"].append(token_major(vp))
        outs["pp"].append(up.reshape(bp, tp, POOL_WIDTH)[:, tp - POOL_STATE:])
        outs["ks"].append(ks.reshape(bs, ts, SB_HEADS, SB_HEAD_DIM))
        outs["vs"].append(vs.reshape(bs, ts, SB_HEADS, SB_HEAD_DIM))
        outs["ps"].append(us.reshape(bs, ts, POOL_WIDTH)[:, ts - POOL_STATE:])

    stack = lambda name: jnp.stack(outs[name])
    return (xp.reshape(bp, tp, D_MODEL), xs.reshape(bs, ts, D_MODEL),
            stack("kp"), stack("vp"), stack("pp"), stack("ks"), stack("vs"), stack("ps"))
```

```python
import functools

import jax
import jax.numpy as jnp
from jax import lax
from jax.experimental import pallas as pl
from jax.experimental.pallas import tpu as pltpu

D_MODEL = 1024
POOL_WIDTH = 512
POOL_WINDOWS = (2, 4, 8, 16)
POOL_GROUP_DIM = POOL_WIDTH // len(POOL_WINDOWS)
POOL_STATE = max(POOL_WINDOWS) - 1
POOL_HALO = 16
SB_HEADS = 8
SB_HEAD_DIM = 64
SB_WIDTH = SB_HEADS * SB_HEAD_DIM
HEAD_PAIR = 2 * SB_HEAD_DIM
D_FF = 4 * D_MODEL
PLE_DIM = 256
EPS = 1e-6

V7X_VMEM_BYTES = 64 * 1024 * 1024
VMEM_LIMIT_BYTES = V7X_VMEM_BYTES - 8 * 1024 * 1024

ROW_TILE = 512
INPROJ_CHAINS = 2
POST_CHAINS = 2
POST_LAG = 1
SB_BLOCK = 256
SB_PAIRS_PER_STEP = 4
SB_SAMPLE_STREAMS = 2
SB_PAIR_LAG = 2

BF16_SUBLANES = 16
W_IN_CHUNK = 512
N_WHOLE_CASTS = 2

LOG2E = 1.4426950408889634
SB_SKIP_BELOW = -160.0

BF16 = jnp.bfloat16
F32 = jnp.float32


def _inv_rms(x):
    return lax.rsqrt(jnp.mean(x * x, axis=-1, keepdims=True) + EPS)


def _rms_norm(x, g):
    return x * _inv_rms(x) * g


def _dot(a, b):
    return jnp.dot(a, b, preferred_element_type=F32)


def _dot_nt(a, b):
    return lax.dot_general(a, b, (((1,), (1,)), ((), ())), preferred_element_type=F32)


def _resident(shape):
    return pl.BlockSpec(shape, lambda *_: (0,) * len(shape), pipeline_mode=pl.Buffered(1))


def _interleave(chains, lag):
    waiting, live, tick = list(chains), [], 0
    while waiting or live:
        if waiting and tick % lag == 0:
            live.append(waiting.pop(0))
        live = [c for c in live if next(c, True) is None]
        tick += 1


def _store_head_rows(ref, row0, x):
    rows = x.shape[0]
    for h in range(SB_HEADS):
        ref[pl.ds(row0 * SB_HEADS + h, rows, stride=SB_HEADS), :] = x[:, h * SB_HEAD_DIM:(h + 1) * SB_HEAD_DIM]


def _group_steps(steps, maps_to):
    starts = [sum(steps[:g]) for g in range(len(steps))]
    return [lambda i, s=s, n=n: maps_to(jnp.clip(i - s, 0, n - 1)) for s, n in zip(starts, steps)]


def _run_group(steps, bodies):
    i = pl.program_id(0)
    start = 0
    for n, body in zip(steps, bodies):
        pl.when(jnp.logical_and(i >= start, i < start + n))(body)
        start += n


def _inproj_kernel(*refs, steps, n_cast, feature_major):
    groups = len(steps)
    cast_group = steps.index(max(steps))
    cast_start = sum(steps[:cast_group])
    g_ref, w_hbm, b_ref = refs[groups:groups + 3]
    cast_in = refs[groups + 3:groups + 3 + n_cast]
    outs = refs[groups + 3 + n_cast:-3]
    cast_out = outs[7 * groups:]
    w_ref, w_stage, w_sem = refs[-3:]
    i = pl.program_id(0)

    n_chunks = w_hbm.shape[1] // W_IN_CHUNK
    converted = [0]

    def chunk_copy(c):
        return pltpu.make_async_copy(w_hbm.at[:, c * W_IN_CHUNK:(c + 1) * W_IN_CHUNK], w_stage.at[c % 2],
                                     w_sem.at[c % 2])

    @pl.when(i == 0)
    def _():
        for c in range(min(2, n_chunks)):
            chunk_copy(c).start()

    def weights_ready(col_stop):
        while converted[0] * W_IN_CHUNK < col_stop:
            c = converted[0]
            converted[0] += 1

            @pl.when(i == 0)
            def _():
                chunk_copy(c).wait()
                w_ref[:, c * W_IN_CHUNK:(c + 1) * W_IN_CHUNK] = w_stage[c % 2].astype(BF16)
                if c + 2 < n_chunks:
                    chunk_copy(c + 2).start()

    @pl.when(i == 0)
    def _():
        for src, dst in zip(cast_in[n_cast - N_WHOLE_CASTS:], cast_out[n_cast - N_WHOLE_CASTS:]):
            dst[...] = src[...].astype(BF16)

    @pl.when(jnp.logical_and(i >= cast_start, i < cast_start + steps[cast_group]))
    def _():
        for src, dst in zip(cast_in[:n_cast - N_WHOLE_CASTS], cast_out[:n_cast - N_WHOLE_CASTS]):
            dst[...] = src[...].astype(BF16)

    _run_group(steps, [functools.partial(_inproj_tile, refs[g], g_ref, w_ref, b_ref, *outs[7 * g:7 * g + 7],
                                         feature_major=feature_major[g],
                                         weights_ready=weights_ready if g == 0 else (lambda col_stop: None))
                       for g in range(groups)])
    assert converted[0] == n_chunks


def _inproj_tile(x_ref, g_ref, w_ref, b_ref, u_ref, k_ref, v_ref, qb_ref, kb_ref, vb_ref, gate_ref, *, feature_major,
                 weights_ready):
    chunk = x_ref.shape[0] // INPROJ_CHAINS

    def store_kv(ref, c, x):
        if feature_major:
            ref[0, :, c * chunk:(c + 1) * chunk] = x.T
        else:
            _store_head_rows(ref, c * chunk, x)

    c0, c1, c2, c3 = POOL_WIDTH, POOL_WIDTH + SB_WIDTH, POOL_WIDTH + 2 * SB_WIDTH, POOL_WIDTH + 3 * SB_WIDTH

    def token_chain(c):
        rows = slice(c * chunk, (c + 1) * chunk)
        xn = _rms_norm(x_ref[rows, :], g_ref[...]).astype(BF16)
        yield
        weights_ready(c1)
        u_ref[rows, :] = _dot(xn, w_ref[:, :c0])
        qb_ref[rows, :] = (_dot(xn, w_ref[:, c0:c1]) * (SB_HEAD_DIM ** -0.5 * LOG2E)).astype(BF16)
        yield
        weights_ready(c2)
        k = _dot(xn, w_ref[:, c1:c2])
        store_kv(k_ref, c, k)
        kb_ref[rows, :] = k.astype(BF16)
        yield
        weights_ready(c3)
        v = _dot(xn, w_ref[:, c2:c3])
        store_kv(v_ref, c, v)
        vb_ref[rows, :] = v.astype(BF16)
        yield
        weights_ready(w_ref.shape[1])
        gate_ref[rows, :] = jax.nn.sigmoid(_dot(xn, w_ref[:, c3:]) + b_ref[...]).astype(BF16)

    _interleave([token_chain(c) for c in range(INPROJ_CHAINS)], 1)


def _inproj(xs, stream_tiles, g_mix, w_in, b_gate, chunked, whole, tm):
    assert len(whole) == N_WHOLE_CASTS
    in_width = w_in.shape[1]
    steps = [x.shape[0] // tm for x in xs]
    maps = _group_steps(steps, lambda j: (j, 0))
    buffers = [dict(pipeline_mode=pl.Buffered(1)) if n == 1 else {} for n in steps]
    in_specs = [pl.BlockSpec((tm, D_MODEL), m, **buf) for m, buf in zip(maps, buffers)]
    in_specs += [_resident((1, D_MODEL)), pl.BlockSpec(memory_space=pl.ANY), _resident((1, 2 * D_MODEL))]
    out_specs, out_shape = [], []
    for x, m, buf, tiles in zip(xs, maps, buffers, stream_tiles):
        n = x.shape[0]
        row = lambda width: pl.BlockSpec((tm, width), m, **buf)
        if tiles is None:
            kv_spec = pl.BlockSpec((tm * SB_HEADS, SB_HEAD_DIM), m, **buf)
            kv_shape = jax.ShapeDtypeStruct((n * SB_HEADS, SB_HEAD_DIM), F32)
        else:
            kv_spec = pl.BlockSpec((1, SB_WIDTH, tm), lambda i, m=m, tiles=tiles: (m(i)[0] // tiles, 0, m(i)[0] % tiles),
                                   **buf)
            kv_shape = jax.ShapeDtypeStruct((n // (tiles * tm), SB_WIDTH, tiles * tm), F32)
        bf_out = jax.ShapeDtypeStruct((n, SB_WIDTH), BF16)
        out_specs += [row(POOL_WIDTH), kv_spec, kv_spec, row(SB_WIDTH), row(SB_WIDTH), row(SB_WIDTH),
                      row(2 * D_MODEL)]
        out_shape += [jax.ShapeDtypeStruct((n, POOL_WIDTH), F32), kv_shape, kv_shape, bf_out, bf_out, bf_out,
                      jax.ShapeDtypeStruct((n, 2 * D_MODEL), BF16)]
    cast_specs = []
    for w in chunked:
        rows = w.shape[0] // max(steps)
        assert rows * max(steps) == w.shape[0] and rows % BF16_SUBLANES == 0
        cast_specs.append(pl.BlockSpec((rows, w.shape[1]), maps[steps.index(max(steps))]))
    cast_specs += [pl.BlockSpec(w.shape, lambda i, nd=w.ndim: (0,) * nd) for w in whole]
    cast_shape = [jax.ShapeDtypeStruct(w.shape, BF16) for w in (*chunked, *whole)]
    outs = pl.pallas_call(
        functools.partial(_inproj_kernel, steps=steps, n_cast=len(cast_specs),
                          feature_major=[tiles is not None for tiles in stream_tiles]),
        grid=(sum(steps),),
        in_specs=in_specs + cast_specs,
        out_specs=out_specs + cast_specs,
        out_shape=out_shape + cast_shape,
        scratch_shapes=[pltpu.VMEM((D_MODEL, in_width), BF16), pltpu.VMEM((2, D_MODEL, W_IN_CHUNK), F32),
                        pltpu.SemaphoreType.DMA((2,))],
        compiler_params=pltpu.CompilerParams(dimension_semantics=("arbitrary",),
                                             vmem_limit_bytes=VMEM_LIMIT_BYTES),
        name="inproj",
    )(*xs, g_mix.reshape(1, D_MODEL), w_in, b_gate.reshape(1, 2 * D_MODEL), *chunked, *whole)
    n_group_outs = 7 * len(xs)
    return ([outs[7 * g:7 * g + 7] for g in range(len(xs))],
            outs[n_group_outs:n_group_outs + len(chunked)], outs[n_group_outs + len(chunked):])


def _neg_lower(n):
    r = lax.broadcasted_iota(jnp.int32, (n, n), 0)
    c = lax.broadcasted_iota(jnp.int32, (n, n), 1)
    return jnp.where(r >= c, -1.0, 0.0).astype(BF16)


def _stack_heads(x):
    lane = lax.broadcasted_iota(jnp.int32, x.shape, 1)
    zero = jnp.zeros_like(x)
    return jnp.concatenate([jnp.where(lane < SB_HEAD_DIM, x, zero), jnp.where(lane >= SB_HEAD_DIM, x, zero)], axis=0)


def _causal_mask(tq, tk):
    r = lax.broadcasted_iota(jnp.int32, (2 * tq, tk), 0)
    c = lax.broadcasted_iota(jnp.int32, (2 * tq, tk), 1)
    return c < jnp.where(r >= tq, r - tq, r)


def _sb_pair_stages(q2, blocks, carry, emit):
    tq = q2.shape[0] // 2
    logits = []
    for k_blk, _, _, causal in blocks:
        z = _dot_nt(q2, k_blk)
        if causal is not None:
            z = jnp.where(causal, z, -jnp.inf)
        logits.append(z)
    yield
    softplus = [jnp.maximum(z, 0.0) + jnp.log(1.0 + jnp.exp2(-jnp.abs(z))) * LOG2E for z in logits]
    yield
    sums = [_dot(sp.astype(BF16), neg_tri) for (_, _, neg_tri, _), sp in zip(blocks, softplus)]
    yield
    weights = []
    for z, incl in zip(logits, sums):
        a = jnp.exp2(z + incl + carry).astype(BF16)
        weights += [a[:tq], a[tq:]]
        carry = carry + incl[:, :1]
    peak = jnp.max(carry)
    yield
    values = jnp.concatenate([v2 for _, v2, _, _ in blocks], axis=0)
    emit(_dot(jnp.concatenate(weights, axis=1), values), carry, peak)


def _sb_prompt_kernel(q_ref, k_ref, v_ref, o_ref, acc_ref, carry_ref, peak_ref):
    qi = pl.program_id(2)
    blk = SB_BLOCK
    pairs = range(SB_PAIRS_PER_STEP)
    lanes = lambda p: slice(p * HEAD_PAIR, (p + 1) * HEAD_PAIR)
    q2 = [_stack_heads(q_ref[0, :, lanes(p)]) for p in pairs]
    neg_tri = _neg_lower(blk)

    def key_block(p, j, causal):
        r0 = pl.multiple_of(j * blk, blk)
        return (k_ref[0, pl.ds(r0, blk), lanes(p)], _stack_heads(v_ref[0, pl.ds(r0, blk), lanes(p)]),
                neg_tri, causal)

    def sweep(block_ids, first):
        def emit(p, out, carry, peak):
            if first:
                acc_ref[:, lanes(p)] = out
            else:
                acc_ref[:, lanes(p)] += out
            carry_ref[p] = carry
            peak_ref[p] = peak

        _interleave([_sb_pair_stages(q2[p], [key_block(p, j, causal) for j, causal in block_ids],
                                     jnp.zeros((2 * blk, 1), F32) if first else carry_ref[p],
                                     functools.partial(emit, p)) for p in pairs], SB_PAIR_LAG)

    def alive():
        peak = functools.reduce(jnp.maximum, [peak_ref[p] for p in pairs])
        return (peak > SB_SKIP_BELOW).astype(jnp.int32)

    diagonal = (qi, _causal_mask(blk, blk))
    pl.when(qi == 0)(lambda: sweep([diagonal], True))
    pl.when(qi > 0)(lambda: sweep([diagonal, (qi - 1, None)], True))

    def body(state):
        j, _ = state
        sweep([(j, None)], False)
        return j - 1, alive()

    lax.while_loop(lambda s: jnp.logical_and(s[0] >= 0, s[1] > 0), body, (qi - 2, alive()))
    o_ref[0] = acc_ref[...].astype(o_ref.dtype)


def _sb_prompt(qb, kb, vb):
    b, t, _ = qb.shape
    blk = SB_BLOCK
    width = SB_PAIRS_PER_STEP * HEAD_PAIR
    kv_spec = pl.BlockSpec((1, t, width), lambda bi, hp, qi: (bi, 0, hp))
    q_spec = pl.BlockSpec((1, blk, width), lambda bi, hp, qi: (bi, qi, hp))
    return pl.pallas_call(
        _sb_prompt_kernel,
        grid=(b, SB_WIDTH // width, t // blk),
        in_specs=[q_spec, kv_spec, kv_spec],
        out_specs=q_spec,
        out_shape=jax.ShapeDtypeStruct((b, t, SB_WIDTH), BF16),
        scratch_shapes=[pltpu.VMEM((blk, width), F32), pltpu.VMEM((SB_PAIRS_PER_STEP, 2 * blk, 1), F32),
                        pltpu.SMEM((SB_PAIRS_PER_STEP,), F32)],
        compiler_params=pltpu.CompilerParams(dimension_semantics=("parallel", "parallel", "arbitrary"),
                                             vmem_limit_bytes=VMEM_LIMIT_BYTES),
        name="sb_prompt",
    )(qb, kb, vb)


def _sb_heads_stages(q, blocks, carry, emit):
    t = q[0].shape[0]
    logits = []
    for keys, _, _, causal, feature_major in blocks:
        score = _dot if feature_major else _dot_nt
        z = jnp.concatenate([score(qh, kh) for qh, kh in zip(q, keys)], axis=0)
        logits.append(z if causal is None else jnp.where(causal, z, -jnp.inf))
    yield
    softplus = [jnp.maximum(z, 0.0) + jnp.log(1.0 + jnp.exp2(-jnp.abs(z))) * LOG2E for z in logits]
    yield
    sums = [_dot(sp.astype(BF16), neg_tri) for (_, _, neg_tri, _, _), sp in zip(blocks, softplus)]
    yield
    weights = []
    for z, incl in zip(logits, sums):
        weights.append(jnp.exp2(z + incl + carry).astype(BF16))
        carry = carry + incl[:, :1]
    peak = jnp.max(carry)
    yield
    outs = None
    for (_, values, _, _, feature_major), a in zip(blocks, weights):
        mix = _dot_nt if feature_major else _dot
        part = [mix(a[h * t:(h + 1) * t], vh) for h, vh in enumerate(values)]
        outs = part if outs is None else [o + p for o, p in zip(outs, part)]
    emit(outs, carry, peak)


def _sb_sample_kernel(q_ref, k_ref, v_ref, ck_hbm, cv_hbm, o_ref, kbuf, vbuf, acc_ref, carry_ref, peak_ref, sem,
                      *, layer, past_len):
    step = pl.program_id(0)
    slot = step % 2
    blk = SB_BLOCK
    t = q_ref.shape[1]
    heads = range(SB_HEADS)
    local = range(SB_SAMPLE_STREAMS)
    cols = lambda h: slice(h * SB_HEAD_DIM, (h + 1) * SB_HEAD_DIM)
    last = past_len // blk - 1

    def fetch(at_step, j):
        tokens = pl.ds(pl.multiple_of(j * blk, blk), blk)
        copies = []
        for s in local:
            stream = at_step * SB_SAMPLE_STREAMS + s
            copies.append(pltpu.make_async_copy(ck_hbm.at[layer, stream, :, :, tokens],
                                                kbuf.at[at_step % 2, s], sem.at[at_step % 2, s, 0]))
            copies.append(pltpu.make_async_copy(cv_hbm.at[layer, stream, :, :, tokens],
                                                vbuf.at[at_step % 2, s], sem.at[at_step % 2, s, 1]))
        return copies

    def start(copies):
        for c in copies:
            c.start()

    def wait(copies):
        for c in copies:
            c.wait()

    def cached_block(s):
        return ([kbuf[slot, s, h].astype(BF16) for h in heads], [vbuf[slot, s, h].astype(BF16) for h in heads],
                _neg_lower(blk), None, True)

    def sweep(blocks_of, first):
        def emit(s, outs, carry, peak):
            for h in heads:
                if first:
                    acc_ref[s, :, cols(h)] = outs[h]
                else:
                    acc_ref[s, :, cols(h)] += outs[h]
            carry_ref[s] = carry
            peak_ref[s] = peak

        _interleave([_sb_heads_stages([q_ref[s, :, cols(h)] for h in heads], blocks_of(s),
                                      jnp.zeros((SB_HEADS * t, 1), F32) if first else carry_ref[s],
                                      functools.partial(emit, s)) for s in local], 1)

    def alive():
        peak = functools.reduce(jnp.maximum, [peak_ref[s] for s in local])
        return (peak > SB_SKIP_BELOW).astype(jnp.int32)

    pl.when(step == 0)(lambda: start(fetch(step, last)))
    pl.when(step + 1 < pl.num_programs(0))(lambda: start(fetch(step + 1, last)))
    wait(fetch(step, last))
    r = lax.broadcasted_iota(jnp.int32, (SB_HEADS * t, t), 0)
    causal = lax.broadcasted_iota(jnp.int32, (SB_HEADS * t, t), 1) < lax.rem(r, t)
    new_block = lambda s: ([k_ref[s, :, cols(h)] for h in heads], [v_ref[s, :, cols(h)] for h in heads],
                           _neg_lower(t), causal, False)
    sweep(lambda s: [new_block(s), cached_block(s)], True)

    def body(state):
        j, _ = state
        copies = fetch(step, j)
        start(copies)
        wait(copies)
        sweep(lambda s: [cached_block(s)], False)
        return j - 1, alive()

    lax.while_loop(lambda s: jnp.logical_and(s[0] >= 0, s[1] > 0), body, (last - 1, alive()))
    o_ref[...] = acc_ref[...].astype(o_ref.dtype)


def _sb_sample(qb, kb, vb, cache_k, cache_v, layer):
    b, t, _ = qb.shape
    past_len = cache_k.shape[-1]
    n = SB_SAMPLE_STREAMS
    assert past_len % SB_BLOCK == 0 and b % n == 0
    new_spec = pl.BlockSpec((n, t, SB_WIDTH), lambda i: (i, 0, 0))
    hbm_spec = pl.BlockSpec(memory_space=pl.ANY)
    block = (2, n, SB_HEADS, SB_HEAD_DIM, SB_BLOCK)
    return pl.pallas_call(
        functools.partial(_sb_sample_kernel, layer=layer, past_len=past_len),
        grid=(b // n,),
        in_specs=[new_spec, new_spec, new_spec, hbm_spec, hbm_spec],
        out_specs=new_spec,
        out_shape=jax.ShapeDtypeStruct((b, t, SB_WIDTH), BF16),
        scratch_shapes=[pltpu.VMEM(block, F32), pltpu.VMEM(block, F32),
                        pltpu.VMEM((n, t, SB_WIDTH), F32), pltpu.VMEM((n, SB_HEADS * t, 1), F32),
                        pltpu.SMEM((n,), F32), pltpu.SemaphoreType.DMA((2, n, 2))],
        compiler_params=pltpu.CompilerParams(dimension_semantics=("arbitrary",),
                                             vmem_limit_bytes=VMEM_LIMIT_BYTES),
        name="sb_sample",
    )(qb, kb, vb, cache_k, cache_v)


def _pool_diff(ext, first_pos):
    short_windows = not (isinstance(first_pos, int) and first_pos >= POOL_STATE)
    if short_windows:
        pos = first_pos + lax.broadcasted_iota(jnp.int32, (POOL_HALO, 1), 0)
    outs = []
    for g, window in enumerate(POOL_WINDOWS):
        cols = ext[:, g * POOL_GROUP_DIM:(g + 1) * POOL_GROUP_DIM]
        acc = cols
        shift = 1
        while shift < window:
            acc = acc + pltpu.roll(acc, shift, axis=0)
            shift *= 2
        out = acc[POOL_HALO:] * (1.0 / window) - cols[POOL_HALO:]
        if short_windows:
            inv_count = 1.0 / jnp.minimum(pos + 1, window).astype(F32)
            head = acc[POOL_HALO:2 * POOL_HALO] * inv_count - cols[POOL_HALO:2 * POOL_HALO]
            out = jnp.concatenate([head, out[POOL_HALO:]], axis=0)
        outs.append(out)
    return jnp.concatenate(outs, axis=1)


POST_GROUP_INPUTS = 6
POST_WEIGHTS = 12


def _post_kernel(*refs, steps, modes, final_norm):
    groups = len(steps)
    n_in = POST_GROUP_INPUTS * groups
    weights = refs[n_in:n_in + POST_WEIGHTS]
    ys = refs[n_in + POST_WEIGHTS:]
    _run_group(steps, [functools.partial(_post_tile, *refs[POST_GROUP_INPUTS * g:POST_GROUP_INPUTS * (g + 1)], *weights,
                                         ys[g], first_step=sum(steps[:g]), final_norm=final_norm, **modes[g])
                       for g in range(groups)])


def _post_tile(x_ref, u_ref, halo_ref, o_ref, gate_ref, p_ref,
               wgrp_ref, scale_ref, wpu_ref, wsu_ref, wout_ref,
               gmlp_ref, wup_ref, wdown_ref, gple_ref, wpg_ref, wpp_ref, gfin_ref,
               y_ref, *, first_step, tiles_per_stream, streams_per_tile, past_pos, final_norm):
    tm = u_ref.shape[0]
    chunk = tm // POST_CHAINS

    def pooled(c):
        u = u_ref[c * chunk:(c + 1) * chunk, :]
        if streams_per_tile == 1:
            step = (pl.program_id(0) - first_step) % tiles_per_stream
            if c == 0:
                halo = jnp.where(step == 0, 0.0, halo_ref[...])
            else:
                halo = u_ref[c * chunk - POOL_HALO:c * chunk, :]
            at_least = past_pos + c * chunk
            first_pos = at_least if at_least >= POOL_STATE else at_least + step * tm
            diff = _pool_diff(jnp.concatenate([halo, u], axis=0), first_pos)
        else:
            t = tm // streams_per_tile
            per_chunk = streams_per_tile // POST_CHAINS
            diff = jnp.concatenate(
                [_pool_diff(jnp.concatenate([halo_ref[c * per_chunk + s], u[s * t:(s + 1) * t]], axis=0), past_pos)
                 for s in range(per_chunk)], axis=0)
        return diff.astype(BF16)

    def token_chain(c):
        rows = slice(c * chunk, (c + 1) * chunk)
        diff = pooled(c)
        y_pool = jnp.concatenate(
            [_dot(diff[:, g * POOL_GROUP_DIM:(g + 1) * POOL_GROUP_DIM], wgrp_ref[g])
             for g in range(len(POOL_WINDOWS))], axis=1) * scale_ref[...]
        gates = gate_ref[rows, :].astype(F32)
        merged = (gates[:, :D_MODEL] * _dot(y_pool.astype(BF16), wpu_ref[...])
                  + gates[:, D_MODEL:] * _dot(o_ref[rows, :], wsu_ref[...]))
        yield
        x = x_ref[rows, :] + _dot(merged.astype(BF16), wout_ref[...])
        r = _inv_rms(x)
        xg = (x * gmlp_ref[...]).astype(BF16)
        yield
        h = jnp.square(jnp.maximum(_dot(xg, wup_ref[...]), 0.0)).astype(BF16)
        yield
        x = x + (r * r) * _dot(h, wdown_ref[...])
        r = _inv_rms(x)
        xg = (x * gple_ref[...]).astype(BF16)
        yield
        ple_gate = jax.nn.sigmoid(r * _dot(xg, wpg_ref[...]))
        x = x + ple_gate * _dot(p_ref[rows, :].astype(BF16), wpp_ref[...])
        if final_norm:
            x = _rms_norm(x, gfin_ref[...])
        y_ref[rows, :] = x

    _interleave([token_chain(c) for c in range(POST_CHAINS)], POST_LAG)


def _post(groups, weights, modes, *, tm, final_norm):
    assert len(weights) == POST_WEIGHTS and all(len(g) == POST_GROUP_INPUTS for g in groups)
    steps = [g[0].shape[0] // tm for g in groups]
    maps = _group_steps(steps, lambda j: j)
    in_specs, out_specs, out_shape = [], [], []
    for (x, *_), mode, m in zip(groups, modes, maps):
        row = lambda width, m=m: pl.BlockSpec((tm, width), lambda i: (m(i), 0))
        if mode["streams_per_tile"] == 1:
            per_tile = tm // POOL_HALO
            halo_spec = pl.BlockSpec((POOL_HALO, POOL_WIDTH),
                                     lambda i, m=m: (jnp.maximum(m(i) * per_tile - 1, 0), 0))
        else:
            halo_spec = pl.BlockSpec((mode["streams_per_tile"], POOL_HALO, POOL_WIDTH), lambda i, m=m: (m(i), 0, 0))
        in_specs += [row(D_MODEL), row(POOL_WIDTH), halo_spec, row(SB_WIDTH), row(2 * D_MODEL), row(PLE_DIM)]
        out_specs.append(row(D_MODEL))
        out_shape.append(jax.ShapeDtypeStruct((x.shape[0], D_MODEL), F32))
    in_specs += [_resident(w.shape) for w in weights]
    return pl.pallas_call(
        functools.partial(_post_kernel, steps=steps, modes=modes, final_norm=final_norm),
        grid=(sum(steps),),
        in_specs=in_specs,
        out_specs=out_specs,
        out_shape=out_shape,
        compiler_params=pltpu.CompilerParams(dimension_semantics=("arbitrary",),
                                             vmem_limit_bytes=VMEM_LIMIT_BYTES),
        name="post",
    )(*[a for g in groups for a in g], *weights)


def kernel(x_prompt, x_sample, cache_k, cache_v, state_pool, p_prompt, p_sample, g_mix, w_in, b_gate, w_pool_grp, pool_scale, w_pool_up, w_sb_up, w_out, g_mlp, w_up, w_down, g_ple, w_ple_gate, w_ple_proj, g_final):
    depth = w_in.shape[0]
    bp, tp, _ = x_prompt.shape
    bs, ts, _ = x_sample.shape
    past_len = cache_k.shape[2]
    assert tp % ROW_TILE == 0 and tp % SB_BLOCK == 0
    n_s = bs * ts
    assert ts >= POOL_STATE and ts % 8 == 0 and n_s % ROW_TILE == 0 and (ROW_TILE // ts) % POST_CHAINS == 0
    xp = x_prompt.reshape(bp * tp, D_MODEL)
    xs = x_sample.reshape(n_s, D_MODEL)
    row_vec = lambda a: a.reshape(1, -1)
    cache_rows = lambda c: jnp.transpose(c, (0, 1, 3, 4, 2))

    outs = {name: [] for name in ("kp", "vp", "pp", "ks", "vs", "ps")}
    for d in range(depth):
        final_norm = d == depth - 1

        (((us, ks, vs, qbs, kbs, vbs, gates_s), (up, kp, vp, qbp, kbp, vbp, gates_p)),
         (wpu_bf, wsu_bf, wout_bf, wup_bf, wdown_bf, wpg_bf), (wpp_bf, wgrp_bf)) = _inproj(
            [xs, xp], [None, tp // ROW_TILE], g_mix[d], w_in[d], b_gate[d],
            [w_pool_up[d], w_sb_up[d], w_out[d], w_up[d], w_down[d], w_ple_gate[d]],
            [w_ple_proj[d], w_pool_grp[d]], ROW_TILE)
        weights = (wgrp_bf, row_vec(pool_scale[d]), wpu_bf, wsu_bf, wout_bf, row_vec(g_mlp[d]), wup_bf,
                   wdown_bf, row_vec(g_ple[d]), wpg_bf, wpp_bf, row_vec(g_final))
        op = _sb_prompt(qbp.reshape(bp, tp, SB_WIDTH), kbp.reshape(bp, tp, SB_WIDTH), vbp.reshape(bp, tp, SB_WIDTH))
        os_ = _sb_sample(qbs.reshape(bs, ts, SB_WIDTH), kbs.reshape(bs, ts, SB_WIDTH), vbs.reshape(bs, ts, SB_WIDTH),
                         cache_rows(cache_k), cache_rows(cache_v), d)
        halo_s = jnp.pad(state_pool[d], ((0, 0), (POOL_HALO - POOL_STATE, 0), (0, 0)))
        xp, = _post([(xp, up, up, op.reshape(bp * tp, SB_WIDTH), gates_p, p_prompt[d].reshape(bp * tp, PLE_DIM))],
                    weights, [dict(tiles_per_stream=tp // ROW_TILE, streams_per_tile=1, past_pos=0)],
                    tm=ROW_TILE, final_norm=final_norm)
        xs, = _post([(xs, us, halo_s, os_.reshape(n_s, SB_WIDTH), gates_s, p_sample[d].reshape(n_s, PLE_DIM))],
                    weights, [dict(tiles_per_stream=1, streams_per_tile=ROW_TILE // ts, past_pos=POOL_STATE)],
                    tm=ROW_TILE, final_norm=final_norm)
        token_major = lambda a: jnp.transpose(a.reshape(bp, SB_HEADS, SB_HEAD_DIM, tp), (0, 3, 1, 2))
        outs["kp"].append(token_major(kp))
        outs["vp"].append(token_major(vp))
        outs["pp"].append(up.reshape(bp, tp, POOL_WIDTH)[:, tp - POOL_STATE:])
        outs["ks"].append(ks.reshape(bs, ts, SB_HEADS, SB_HEAD_DIM))
        outs["vs"].append(vs.reshape(bs, ts, SB_HEADS, SB_HEAD_DIM))
        outs["ps"].append(us.reshape(bs, ts, POOL_WIDTH)[:, ts - POOL_STATE:])

    stack = lambda name: jnp.stack(outs[name])
    return (xp.reshape(bp, tp, D_MODEL), xs.reshape(bs, ts, D_MODEL),
            stack("kp"), stack("vp"), stack("pp"), stack("ks"), stack("vs"), stack("ps"))
```

```python
import functools

import jax
import jax.numpy as jnp
from jax import lax
from jax.experimental import pallas as pl
from jax.experimental.pallas import tpu as pltpu

D_MODEL = 1024
POOL_WIDTH = 512
POOL_WINDOWS = (2, 4, 8, 16)
POOL_GROUP_DIM = POOL_WIDTH // len(POOL_WINDOWS)
POOL_STATE = max(POOL_WINDOWS) - 1
POOL_HALO = 16
SB_HEADS = 8
SB_HEAD_DIM = 64
SB_WIDTH = SB_HEADS * SB_HEAD_DIM
HEAD_PAIR = 2 * SB_HEAD_DIM
D_FF = 4 * D_MODEL
PLE_DIM = 256
EPS = 1e-6

V7X_VMEM_BYTES = 64 * 1024 * 1024
VMEM_LIMIT_BYTES = V7X_VMEM_BYTES - 8 * 1024 * 1024

ROW_TILE = 512
INPROJ_CHAINS = 2
POST_CHAINS = 2
POST_LAG = 1
SB_BLOCK = 256
SB_PAIRS_PER_STEP = 4
SB_SAMPLE_STREAMS = 2
SB_PAIR_LAG = 2

BF16_SUBLANES = 16
W_IN_CHUNK = 512
N_WHOLE_CASTS = 2

LOG2E = 1.4426950408889634
SB_SKIP_BELOW = -160.0

BF16 = jnp.bfloat16
F32 = jnp.float32


def _inv_rms(x):
    return lax.rsqrt(jnp.mean(x * x, axis=-1, keepdims=True) + EPS)


def _rms_norm(x, g):
    return x * _inv_rms(x) * g


def _sigmoid(x):
    return 0.5 * jnp.tanh(0.5 * x) + 0.5


def _dot(a, b):
    return jnp.dot(a, b, preferred_element_type=F32)


def _dot_nt(a, b):
    return lax.dot_general(a, b, (((1,), (1,)), ((), ())), preferred_element_type=F32)


def _resident(shape):
    return pl.BlockSpec(shape, lambda *_: (0,) * len(shape), pipeline_mode=pl.Buffered(1))


def _interleave(chains, lag):
    waiting, live, tick = list(chains), [], 0
    while waiting or live:
        if waiting and tick % lag == 0:
            live.append(waiting.pop(0))
        live = [c for c in live if next(c, True) is None]
        tick += 1


def _store_head_rows(ref, row0, x):
    rows = x.shape[0]
    for h in range(SB_HEADS):
        ref[pl.ds(row0 * SB_HEADS + h, rows, stride=SB_HEADS), :] = x[:, h * SB_HEAD_DIM:(h + 1) * SB_HEAD_DIM]


def _group_steps(steps, maps_to):
    starts = [sum(steps[:g]) for g in range(len(steps))]
    return [lambda i, s=s, n=n: maps_to(jnp.clip(i - s, 0, n - 1)) for s, n in zip(starts, steps)]


def _run_group(steps, bodies):
    i = pl.program_id(0)
    start = 0
    for n, body in zip(steps, bodies):
        pl.when(jnp.logical_and(i >= start, i < start + n))(body)
        start += n


def _inproj_kernel(*refs, steps, n_cast, feature_major):
    groups = len(steps)
    cast_group = steps.index(max(steps))
    cast_start = sum(steps[:cast_group])
    g_ref, w_hbm, b_ref = refs[groups:groups + 3]
    cast_in = refs[groups + 3:groups + 3 + n_cast]
    outs = refs[groups + 3 + n_cast:-3]
    cast_out = outs[7 * groups:]
    w_ref, w_stage, w_sem = refs[-3:]
    i = pl.program_id(0)

    n_chunks = w_hbm.shape[1] // W_IN_CHUNK
    converted = [0]

    def chunk_copy(c):
        return pltpu.make_async_copy(w_hbm.at[:, c * W_IN_CHUNK:(c + 1) * W_IN_CHUNK], w_stage.at[c % 2],
                                     w_sem.at[c % 2])

    @pl.when(i == 0)
    def _():
        for c in range(min(2, n_chunks)):
            chunk_copy(c).start()

    def weights_ready(col_stop):
        while converted[0] * W_IN_CHUNK < col_stop:
            c = converted[0]
            converted[0] += 1

            @pl.when(i == 0)
            def _():
                chunk_copy(c).wait()
                w_ref[:, c * W_IN_CHUNK:(c + 1) * W_IN_CHUNK] = w_stage[c % 2].astype(BF16)
                if c + 2 < n_chunks:
                    chunk_copy(c + 2).start()

    @pl.when(i == 0)
    def _():
        for src, dst in zip(cast_in[n_cast - N_WHOLE_CASTS:], cast_out[n_cast - N_WHOLE_CASTS:]):
            dst[...] = src[...].astype(BF16)

    @pl.when(jnp.logical_and(i >= cast_start, i < cast_start + steps[cast_group]))
    def _():
        for src, dst in zip(cast_in[:n_cast - N_WHOLE_CASTS], cast_out[:n_cast - N_WHOLE_CASTS]):
            dst[...] = src[...].astype(BF16)

    _run_group(steps, [functools.partial(_inproj_tile, refs[g], g_ref, w_ref, b_ref, *outs[7 * g:7 * g + 7],
                                         feature_major=feature_major[g],
                                         weights_ready=weights_ready if g == 0 else (lambda col_stop: None))
                       for g in range(groups)])
    assert converted[0] == n_chunks


def _inproj_tile(x_ref, g_ref, w_ref, b_ref, u_ref, k_ref, v_ref, qb_ref, kb_ref, vb_ref, gate_ref, *, feature_major,
                 weights_ready):
    chunk = x_ref.shape[0] // INPROJ_CHAINS

    def store_kv(ref, c, x):
        if feature_major:
            ref[0, :, c * chunk:(c + 1) * chunk] = x.T
        else:
            _store_head_rows(ref, c * chunk, x)

    c0, c1, c2, c3 = POOL_WIDTH, POOL_WIDTH + SB_WIDTH, POOL_WIDTH + 2 * SB_WIDTH, POOL_WIDTH + 3 * SB_WIDTH

    def token_chain(c):
        rows = slice(c * chunk, (c + 1) * chunk)
        xn = _rms_norm(x_ref[rows, :], g_ref[...]).astype(BF16)
        yield
        weights_ready(c1)
        u_ref[rows, :] = _dot(xn, w_ref[:, :c0])
        qb_ref[rows, :] = (_dot(xn, w_ref[:, c0:c1]) * (SB_HEAD_DIM ** -0.5 * LOG2E)).astype(BF16)
        yield
        weights_ready(c2)
        k = _dot(xn, w_ref[:, c1:c2])
        store_kv(k_ref, c, k)
        kb_ref[rows, :] = k.astype(BF16)
        yield
        weights_ready(c3)
        v = _dot(xn, w_ref[:, c2:c3])
        store_kv(v_ref, c, v)
        vb_ref[rows, :] = v.astype(BF16)
        yield
        weights_ready(w_ref.shape[1])
        gate_ref[rows, :] = _sigmoid(_dot(xn, w_ref[:, c3:]) + b_ref[...]).astype(BF16)

    _interleave([token_chain(c) for c in range(INPROJ_CHAINS)], 1)


def _inproj(xs, stream_tiles, g_mix, w_in, b_gate, chunked, whole, tm):
    assert len(whole) == N_WHOLE_CASTS
    in_width = w_in.shape[1]
    steps = [x.shape[0] // tm for x in xs]
    maps = _group_steps(steps, lambda j: (j, 0))
    buffers = [dict(pipeline_mode=pl.Buffered(1)) if n == 1 else {} for n in steps]
    in_specs = [pl.BlockSpec((tm, D_MODEL), m, **buf) for m, buf in zip(maps, buffers)]
    in_specs += [_resident((1, D_MODEL)), pl.BlockSpec(memory_space=pl.ANY), _resident((1, 2 * D_MODEL))]
    out_specs, out_shape = [], []
    for x, m, buf, tiles in zip(xs, maps, buffers, stream_tiles):
        n = x.shape[0]
        row = lambda width: pl.BlockSpec((tm, width), m, **buf)
        if tiles is None:
            kv_spec = pl.BlockSpec((tm * SB_HEADS, SB_HEAD_DIM), m, **buf)
            kv_shape = jax.ShapeDtypeStruct((n * SB_HEADS, SB_HEAD_DIM), F32)
        else:
            kv_spec = pl.BlockSpec((1, SB_WIDTH, tm), lambda i, m=m, tiles=tiles: (m(i)[0] // tiles, 0, m(i)[0] % tiles),
                                   **buf)
            kv_shape = jax.ShapeDtypeStruct((n // (tiles * tm), SB_WIDTH, tiles * tm), F32)
        bf_out = jax.ShapeDtypeStruct((n, SB_WIDTH), BF16)
        out_specs += [row(POOL_WIDTH), kv_spec, kv_spec, row(SB_WIDTH), row(SB_WIDTH), row(SB_WIDTH),
                      row(2 * D_MODEL)]
        out_shape += [jax.ShapeDtypeStruct((n, POOL_WIDTH), F32), kv_shape, kv_shape, bf_out, bf_out, bf_out,
                      jax.ShapeDtypeStruct((n, 2 * D_MODEL), BF16)]
    cast_specs = []
    for w in chunked:
        rows = w.shape[0] // max(steps)
        assert rows * max(steps) == w.shape[0] and rows % BF16_SUBLANES == 0
        cast_specs.append(pl.BlockSpec((rows, w.shape[1]), maps[steps.index(max(steps))]))
    cast_specs += [pl.BlockSpec(w.shape, lambda i, nd=w.ndim: (0,) * nd) for w in whole]
    cast_shape = [jax.ShapeDtypeStruct(w.shape, BF16) for w in (*chunked, *whole)]
    outs = pl.pallas_call(
        functools.partial(_inproj_kernel, steps=steps, n_cast=len(cast_specs),
                          feature_major=[tiles is not None for tiles in stream_tiles]),
        grid=(sum(steps),),
        in_specs=in_specs + cast_specs,
        out_specs=out_specs + cast_specs,
        out_shape=out_shape + cast_shape,
        scratch_shapes=[pltpu.VMEM((D_MODEL, in_width), BF16), pltpu.VMEM((2, D_MODEL, W_IN_CHUNK), F32),
                        pltpu.SemaphoreType.DMA((2,))],
        compiler_params=pltpu.CompilerParams(dimension_semantics=("arbitrary",),
                                             vmem_limit_bytes=VMEM_LIMIT_BYTES),
        name="inproj",
    )(*xs, g_mix.reshape(1, D_MODEL), w_in, b_gate.reshape(1, 2 * D_MODEL), *chunked, *whole)
    n_group_outs = 7 * len(xs)
    return ([outs[7 * g:7 * g + 7] for g in range(len(xs))],
            outs[n_group_outs:n_group_outs + len(chunked)], outs[n_group_outs + len(chunked):])


def _neg_lower(n):
    r = lax.broadcasted_iota(jnp.int32, (n, n), 0)
    c = lax.broadcasted_iota(jnp.int32, (n, n), 1)
    return jnp.where(r >= c, -1.0, 0.0).astype(BF16)


def _stack_heads(x):
    lane = lax.broadcasted_iota(jnp.int32, x.shape, 1)
    zero = jnp.zeros_like(x)
    return jnp.concatenate([jnp.where(lane < SB_HEAD_DIM, x, zero), jnp.where(lane >= SB_HEAD_DIM, x, zero)], axis=0)


def _causal_mask(tq, tk):
    r = lax.broadcasted_iota(jnp.int32, (2 * tq, tk), 0)
    c = lax.broadcasted_iota(jnp.int32, (2 * tq, tk), 1)
    return c < jnp.where(r >= tq, r - tq, r)


def _sb_pair_stages(q2, blocks, carry, emit):
    tq = q2.shape[0] // 2
    logits = []
    for k_blk, _, _, causal in blocks:
        z = _dot_nt(q2, k_blk)
        if causal is not None:
            z = jnp.where(causal, z, -jnp.inf)
        logits.append(z)
    yield
    softplus = [jnp.maximum(z, 0.0) + jnp.log(1.0 + jnp.exp2(-jnp.abs(z))) * LOG2E for z in logits]
    yield
    sums = [_dot(sp.astype(BF16), neg_tri) for (_, _, neg_tri, _), sp in zip(blocks, softplus)]
    yield
    weights = []
    for z, incl in zip(logits, sums):
        a = jnp.exp2(z + incl + carry).astype(BF16)
        weights += [a[:tq], a[tq:]]
        carry = carry + incl[:, :1]
    peak = jnp.max(carry)
    yield
    values = jnp.concatenate([v2 for _, v2, _, _ in blocks], axis=0)
    emit(_dot(jnp.concatenate(weights, axis=1), values), carry, peak)


def _sb_prompt_kernel(q_ref, k_ref, v_ref, o_ref, acc_ref, carry_ref, peak_ref):
    qi = pl.program_id(2)
    blk = SB_BLOCK
    pairs = range(SB_PAIRS_PER_STEP)
    lanes = lambda p: slice(p * HEAD_PAIR, (p + 1) * HEAD_PAIR)
    q2 = [_stack_heads(q_ref[0, :, lanes(p)]) for p in pairs]
    neg_tri = _neg_lower(blk)

    def key_block(p, j, causal):
        r0 = pl.multiple_of(j * blk, blk)
        return (k_ref[0, pl.ds(r0, blk), lanes(p)], _stack_heads(v_ref[0, pl.ds(r0, blk), lanes(p)]),
                neg_tri, causal)

    def sweep(block_ids, first):
        def emit(p, out, carry, peak):
            if first:
                acc_ref[:, lanes(p)] = out
            else:
                acc_ref[:, lanes(p)] += out
            carry_ref[p] = carry
            peak_ref[p] = peak

        _interleave([_sb_pair_stages(q2[p], [key_block(p, j, causal) for j, causal in block_ids],
                                     jnp.zeros((2 * blk, 1), F32) if first else carry_ref[p],
                                     functools.partial(emit, p)) for p in pairs], SB_PAIR_LAG)

    def alive():
        peak = functools.reduce(jnp.maximum, [peak_ref[p] for p in pairs])
        return (peak > SB_SKIP_BELOW).astype(jnp.int32)

    diagonal = (qi, _causal_mask(blk, blk))
    pl.when(qi == 0)(lambda: sweep([diagonal], True))
    pl.when(qi > 0)(lambda: sweep([diagonal, (qi - 1, None)], True))

    def body(state):
        j, _ = state
        sweep([(j, None)], False)
        return j - 1, alive()

    lax.while_loop(lambda s: jnp.logical_and(s[0] >= 0, s[1] > 0), body, (qi - 2, alive()))
    o_ref[0] = acc_ref[...].astype(o_ref.dtype)


def _sb_prompt(qb, kb, vb):
    b, t, _ = qb.shape
    blk = SB_BLOCK
    width = SB_PAIRS_PER_STEP * HEAD_PAIR
    kv_spec = pl.BlockSpec((1, t, width), lambda bi, hp, qi: (bi, 0, hp))
    q_spec = pl.BlockSpec((1, blk, width), lambda bi, hp, qi: (bi, qi, hp))
    return pl.pallas_call(
        _sb_prompt_kernel,
        grid=(b, SB_WIDTH // width, t // blk),
        in_specs=[q_spec, kv_spec, kv_spec],
        out_specs=q_spec,
        out_shape=jax.ShapeDtypeStruct((b, t, SB_WIDTH), BF16),
        scratch_shapes=[pltpu.VMEM((blk, width), F32), pltpu.VMEM((SB_PAIRS_PER_STEP, 2 * blk, 1), F32),
                        pltpu.SMEM((SB_PAIRS_PER_STEP,), F32)],
        compiler_params=pltpu.CompilerParams(dimension_semantics=("parallel", "parallel", "arbitrary"),
                                             vmem_limit_bytes=VMEM_LIMIT_BYTES),
        name="sb_prompt",
    )(qb, kb, vb)


def _sb_heads_stages(q, blocks, carry, emit):
    t = q[0].shape[0]
    logits = []
    for keys, _, _, causal, feature_major in blocks:
        score = _dot if feature_major else _dot_nt
        z = jnp.concatenate([score(qh, kh) for qh, kh in zip(q, keys)], axis=0)
        logits.append(z if causal is None else jnp.where(causal, z, -jnp.inf))
    yield
    softplus = [jnp.maximum(z, 0.0) + jnp.log(1.0 + jnp.exp2(-jnp.abs(z))) * LOG2E for z in logits]
    yield
    sums = [_dot(sp.astype(BF16), neg_tri) for (_, _, neg_tri, _, _), sp in zip(blocks, softplus)]
    yield
    weights = []
    for z, incl in zip(logits, sums):
        weights.append(jnp.exp2(z + incl + carry).astype(BF16))
        carry = carry + incl[:, :1]
    peak = jnp.max(carry)
    yield
    outs = None
    for (_, values, _, _, feature_major), a in zip(blocks, weights):
        mix = _dot_nt if feature_major else _dot
        part = [mix(a[h * t:(h + 1) * t], vh) for h, vh in enumerate(values)]
        outs = part if outs is None else [o + p for o, p in zip(outs, part)]
    emit(outs, carry, peak)


def _sb_sample_kernel(q_ref, k_ref, v_ref, ck_hbm, cv_hbm, o_ref, kbuf, vbuf, acc_ref, carry_ref, peak_ref, sem,
                      *, layer, past_len):
    step = pl.program_id(0)
    slot = step % 2
    blk = SB_BLOCK
    t = q_ref.shape[1]
    heads = range(SB_HEADS)
    local = range(SB_SAMPLE_STREAMS)
    cols = lambda h: slice(h * SB_HEAD_DIM, (h + 1) * SB_HEAD_DIM)
    last = past_len // blk - 1

    def fetch(at_step, j):
        tokens = pl.ds(pl.multiple_of(j * blk, blk), blk)
        copies = []
        for s in local:
            stream = at_step * SB_SAMPLE_STREAMS + s
            copies.append(pltpu.make_async_copy(ck_hbm.at[layer, stream, :, :, tokens],
                                                kbuf.at[at_step % 2, s], sem.at[at_step % 2, s, 0]))
            copies.append(pltpu.make_async_copy(cv_hbm.at[layer, stream, :, :, tokens],
                                                vbuf.at[at_step % 2, s], sem.at[at_step % 2, s, 1]))
        return copies

    def start(copies):
        for c in copies:
            c.start()

    def wait(copies):
        for c in copies:
            c.wait()

    def cached_block(s):
        return ([kbuf[slot, s, h].astype(BF16) for h in heads], [vbuf[slot, s, h].astype(BF16) for h in heads],
                _neg_lower(blk), None, True)

    def sweep(blocks_of, first):
        def emit(s, outs, carry, peak):
            for h in heads:
                if first:
                    acc_ref[s, :, cols(h)] = outs[h]
                else:
                    acc_ref[s, :, cols(h)] += outs[h]
            carry_ref[s] = carry
            peak_ref[s] = peak

        _interleave([_sb_heads_stages([q_ref[s, :, cols(h)] for h in heads], blocks_of(s),
                                      jnp.zeros((SB_HEADS * t, 1), F32) if first else carry_ref[s],
                                      functools.partial(emit, s)) for s in local], 1)

    def alive():
        peak = functools.reduce(jnp.maximum, [peak_ref[s] for s in local])
        return (peak > SB_SKIP_BELOW).astype(jnp.int32)

    pl.when(step == 0)(lambda: start(fetch(step, last)))
    pl.when(step + 1 < pl.num_programs(0))(lambda: start(fetch(step + 1, last)))
    wait(fetch(step, last))
    r = lax.broadcasted_iota(jnp.int32, (SB_HEADS * t, t), 0)
    causal = lax.broadcasted_iota(jnp.int32, (SB_HEADS * t, t), 1) < lax.rem(r, t)
    new_block = lambda s: ([k_ref[s, :, cols(h)] for h in heads], [v_ref[s, :, cols(h)] for h in heads],
                           _neg_lower(t), causal, False)
    sweep(lambda s: [new_block(s), cached_block(s)], True)

    def body(state):
        j, _ = state
        copies = fetch(step, j)
        start(copies)
        wait(copies)
        sweep(lambda s: [cached_block(s)], False)
        return j - 1, alive()

    lax.while_loop(lambda s: jnp.logical_and(s[0] >= 0, s[1] > 0), body, (last - 1, alive()))
    o_ref[...] = acc_ref[...].astype(o_ref.dtype)


def _sb_sample(qb, kb, vb, cache_k, cache_v, layer):
    b, t, _ = qb.shape
    past_len = cache_k.shape[-1]
    n = SB_SAMPLE_STREAMS
    assert past_len % SB_BLOCK == 0 and b % n == 0
    new_spec = pl.BlockSpec((n, t, SB_WIDTH), lambda i: (i, 0, 0))
    hbm_spec = pl.BlockSpec(memory_space=pl.ANY)
    block = (2, n, SB_HEADS, SB_HEAD_DIM, SB_BLOCK)
    return pl.pallas_call(
        functools.partial(_sb_sample_kernel, layer=layer, past_len=past_len),
        grid=(b // n,),
        in_specs=[new_spec, new_spec, new_spec, hbm_spec, hbm_spec],
        out_specs=new_spec,
        out_shape=jax.ShapeDtypeStruct((b, t, SB_WIDTH), BF16),
        scratch_shapes=[pltpu.VMEM(block, F32), pltpu.VMEM(block, F32),
                        pltpu.VMEM((n, t, SB_WIDTH), F32), pltpu.VMEM((n, SB_HEADS * t, 1), F32),
                        pltpu.SMEM((n,), F32), pltpu.SemaphoreType.DMA((2, n, 2))],
        compiler_params=pltpu.CompilerParams(dimension_semantics=("arbitrary",),
                                             vmem_limit_bytes=VMEM_LIMIT_BYTES),
        name="sb_sample",
    )(qb, kb, vb, cache_k, cache_v)


def _pool_diff(ext, first_pos):
    short_windows = not (isinstance(first_pos, int) and first_pos >= POOL_STATE)
    if short_windows:
        pos = first_pos + lax.broadcasted_iota(jnp.int32, (POOL_HALO, 1), 0)
    outs = []
    for g, window in enumerate(POOL_WINDOWS):
        cols = ext[:, g * POOL_GROUP_DIM:(g + 1) * POOL_GROUP_DIM]
        acc = cols
        shift = 1
        while shift < window:
            acc = acc + pltpu.roll(acc, shift, axis=0)
            shift *= 2
        out = acc[POOL_HALO:] * (1.0 / window) - cols[POOL_HALO:]
        if short_windows:
            inv_count = 1.0 / jnp.minimum(pos + 1, window).astype(F32)
            head = acc[POOL_HALO:2 * POOL_HALO] * inv_count - cols[POOL_HALO:2 * POOL_HALO]
            out = jnp.concatenate([head, out[POOL_HALO:]], axis=0)
        outs.append(out)
    return jnp.concatenate(outs, axis=1)


POST_GROUP_INPUTS = 6
POST_WEIGHTS = 12


def _post_kernel(*refs, steps, modes, final_norm):
    groups = len(steps)
    n_in = POST_GROUP_INPUTS * groups
    weights = refs[n_in:n_in + POST_WEIGHTS]
    ys = refs[n_in + POST_WEIGHTS:]
    _run_group(steps, [functools.partial(_post_tile, *refs[POST_GROUP_INPUTS * g:POST_GROUP_INPUTS * (g + 1)], *weights,
                                         ys[g], first_step=sum(steps[:g]), final_norm=final_norm, **modes[g])
                       for g in range(groups)])


def _post_tile(x_ref, u_ref, halo_ref, o_ref, gate_ref, p_ref,
               wgrp_ref, scale_ref, wpu_ref, wsu_ref, wout_ref,
               gmlp_ref, wup_ref, wdown_ref, gple_ref, wpg_ref, wpp_ref, gfin_ref,
               y_ref, *, first_step, tiles_per_stream, streams_per_tile, past_pos, final_norm):
    tm = u_ref.shape[0]
    chunk = tm // POST_CHAINS

    def pooled(c):
        u = u_ref[c * chunk:(c + 1) * chunk, :]
        if streams_per_tile == 1:
            step = (pl.program_id(0) - first_step) % tiles_per_stream
            if c == 0:
                halo = jnp.where(step == 0, 0.0, halo_ref[...])
            else:
                halo = u_ref[c * chunk - POOL_HALO:c * chunk, :]
            at_least = past_pos + c * chunk
            first_pos = at_least if at_least >= POOL_STATE else at_least + step * tm
            diff = _pool_diff(jnp.concatenate([halo, u], axis=0), first_pos)
        else:
            t = tm // streams_per_tile
            per_chunk = streams_per_tile // POST_CHAINS
            diff = jnp.concatenate(
                [_pool_diff(jnp.concatenate([halo_ref[c * per_chunk + s], u[s * t:(s + 1) * t]], axis=0), past_pos)
                 for s in range(per_chunk)], axis=0)
        return diff.astype(BF16)

    def token_chain(c):
        rows = slice(c * chunk, (c + 1) * chunk)
        diff = pooled(c)
        y_pool = jnp.concatenate(
            [_dot(diff[:, g * POOL_GROUP_DIM:(g + 1) * POOL_GROUP_DIM], wgrp_ref[g])
             for g in range(len(POOL_WINDOWS))], axis=1) * scale_ref[...]
        gates = gate_ref[rows, :].astype(F32)
        merged = (gates[:, :D_MODEL] * _dot(y_pool.astype(BF16), wpu_ref[...])
                  + gates[:, D_MODEL:] * _dot(o_ref[rows, :], wsu_ref[...]))
        yield
        x = x_ref[rows, :] + _dot(merged.astype(BF16), wout_ref[...])
        r = _inv_rms(x)
        xg = (x * gmlp_ref[...]).astype(BF16)
        yield
        h = jnp.square(jnp.maximum(_dot(xg, wup_ref[...]), 0.0)).astype(BF16)
        yield
        x = x + (r * r) * _dot(h, wdown_ref[...])
        r = _inv_rms(x)
        xg = (x * gple_ref[...]).astype(BF16)
        yield
        ple_gate = _sigmoid(r * _dot(xg, wpg_ref[...]))
        x = x + ple_gate * _dot(p_ref[rows, :].astype(BF16), wpp_ref[...])
        if final_norm:
            x = _rms_norm(x, gfin_ref[...])
        y_ref[rows, :] = x

    _interleave([token_chain(c) for c in range(POST_CHAINS)], POST_LAG)


def _post(groups, weights, modes, *, tm, final_norm):
    assert len(weights) == POST_WEIGHTS and all(len(g) == POST_GROUP_INPUTS for g in groups)
    steps = [g[0].shape[0] // tm for g in groups]
    maps = _group_steps(steps, lambda j: j)
    in_specs, out_specs, out_shape = [], [], []
    for (x, *_), mode, m in zip(groups, modes, maps):
        row = lambda width, m=m: pl.BlockSpec((tm, width), lambda i: (m(i), 0))
        if mode["streams_per_tile"] == 1:
            per_tile = tm // POOL_HALO
            halo_spec = pl.BlockSpec((POOL_HALO, POOL_WIDTH),
                                     lambda i, m=m: (jnp.maximum(m(i) * per_tile - 1, 0), 0))
        else:
            halo_spec = pl.BlockSpec((mode["streams_per_tile"], POOL_HALO, POOL_WIDTH), lambda i, m=m: (m(i), 0, 0))
        in_specs += [row(D_MODEL), row(POOL_WIDTH), halo_spec, row(SB_WIDTH), row(2 * D_MODEL), row(PLE_DIM)]
        out_specs.append(row(D_MODEL))
        out_shape.append(jax.ShapeDtypeStruct((x.shape[0], D_MODEL), F32))
    in_specs += [_resident(w.shape) for w in weights]
    return pl.pallas_call(
        functools.partial(_post_kernel, steps=steps, modes=modes, final_norm=final_norm),
        grid=(sum(steps),),
        in_specs=in_specs,
        out_specs=out_specs,
        out_shape=out_shape,
        compiler_params=pltpu.CompilerParams(dimension_semantics=("arbitrary",),
                                             vmem_limit_bytes=VMEM_LIMIT_BYTES),
        name="post",
    )(*[a for g in groups for a in g], *weights)


def kernel(x_prompt, x_sample, cache_k, cache_v, state_pool, p_prompt, p_sample, g_mix, w_in, b_gate, w_pool_grp, pool_scale, w_pool_up, w_sb_up, w_out, g_mlp, w_up, w_down, g_ple, w_ple_gate, w_ple_proj, g_final):
    depth = w_in.shape[0]
    bp, tp, _ = x_prompt.shape
    bs, ts, _ = x_sample.shape
    past_len = cache_k.shape[2]
    assert tp % ROW_TILE == 0 and tp % SB_BLOCK == 0
    n_s = bs * ts
    assert ts >= POOL_STATE and ts % 8 == 0 and n_s % ROW_TILE == 0 and (ROW_TILE // ts) % POST_CHAINS == 0
    xp = x_prompt.reshape(bp * tp, D_MODEL)
    xs = x_sample.reshape(n_s, D_MODEL)
    row_vec = lambda a: a.reshape(1, -1)
    cache_rows = lambda c: jnp.transpose(c, (0, 1, 3, 4, 2))

    outs = {name: [] for name in ("kp", "vp", "pp", "ks", "vs", "ps")}
    for d in range(depth):
        final_norm = d == depth - 1

        (((us, ks, vs, qbs, kbs, vbs, gates_s), (up, kp, vp, qbp, kbp, vbp, gates_p)),
         (wpu_bf, wsu_bf, wout_bf, wup_bf, wdown_bf, wpg_bf), (wpp_bf, wgrp_bf)) = _inproj(
            [xs, xp], [None, tp // ROW_TILE], g_mix[d], w_in[d], b_gate[d],
            [w_pool_up[d], w_sb_up[d], w_out[d], w_up[d], w_down[d], w_ple_gate[d]],
            [w_ple_proj[d], w_pool_grp[d]], ROW_TILE)
        weights = (wgrp_bf, row_vec(pool_scale[d]), wpu_bf, wsu_bf, wout_bf, row_vec(g_mlp[d]), wup_bf,
                   wdown_bf, row_vec(g_ple[d]), wpg_bf, wpp_bf, row_vec(g_final))
        op = _sb_prompt(qbp.reshape(bp, tp, SB_WIDTH), kbp.reshape(bp, tp, SB_WIDTH), vbp.reshape(bp, tp, SB_WIDTH))
        os_ = _sb_sample(qbs.reshape(bs, ts, SB_WIDTH), kbs.reshape(bs, ts, SB_WIDTH), vbs.reshape(bs, ts, SB_WIDTH),
                         cache_rows(cache_k), cache_rows(cache_v), d)
        halo_s = jnp.pad(state_pool[d], ((0, 0), (POOL_HALO - POOL_STATE, 0), (0, 0)))
        xp, = _post([(xp, up, up, op.reshape(bp * tp, SB_WIDTH), gates_p, p_prompt[d].reshape(bp * tp, PLE_DIM))],
                    weights, [dict(tiles_per_stream=tp // ROW_TILE, streams_per_tile=1, past_pos=0)],
                    tm=ROW_TILE, final_norm=final_norm)
        xs, = _post([(xs, us, halo_s, os_.reshape(n_s, SB_WIDTH), gates_s, p_sample[d].reshape(n_s, PLE_DIM))],
                    weights, [dict(tiles_per_stream=1, streams_per_tile=ROW_TILE // ts, past_pos=POOL_STATE)],
                    tm=ROW_TILE, final_norm=final_norm)
        token_major = lambda a: jnp.transpose(a.reshape(bp, SB_HEADS, SB_HEAD_DIM, tp), (0, 3, 1, 2))
        outs["kp"].append(token_major(kp))
        outs["vp"].append(token_major(vp))
        outs["pp"].append(up.reshape(bp, tp, POOL_WIDTH)[:, tp - POOL_STATE:])
        outs["ks"].append(ks.reshape(bs, ts, SB_HEADS, SB_HEAD_DIM))
        outs["vs"].append(vs.reshape(bs, ts, SB_HEADS, SB_HEAD_DIM))
        outs["ps"].append(us.reshape(bs, ts, POOL_WIDTH)[:, ts - POOL_STATE:])

    stack = lambda name: jnp.stack(outs[name])
    return (xp.reshape(bp, tp, D_MODEL), xs.reshape(bs, ts, D_MODEL),
            stack("kp"), stack("vp"), stack("pp"), stack("ks"), stack("vs"), stack("ps"))
```

```python
import functools

import jax
import jax.numpy as jnp
from jax import lax
from jax.experimental import pallas as pl
from jax.experimental.pallas import tpu as pltpu

D_MODEL = 1024
POOL_WIDTH = 512
POOL_WINDOWS = (2, 4, 8, 16)
POOL_GROUP_DIM = POOL_WIDTH // len(POOL_WINDOWS)
POOL_STATE = max(POOL_WINDOWS) - 1
POOL_HALO = 16
SB_HEADS = 8
SB_HEAD_DIM = 64
SB_WIDTH = SB_HEADS * SB_HEAD_DIM
HEAD_PAIR = 2 * SB_HEAD_DIM
D_FF = 4 * D_MODEL
PLE_DIM = 256
EPS = 1e-6

V7X_VMEM_BYTES = 64 * 1024 * 1024
VMEM_LIMIT_BYTES = V7X_VMEM_BYTES - 8 * 1024 * 1024

ROW_TILE = 512
INPROJ_CHAINS = 2
POST_CHAINS = 2
POST_LAG = 1
SB_BLOCK = 256
SB_PAIRS_PER_STEP = 4
SB_SAMPLE_STREAMS = 4
SB_PAIR_LAG = 2

BF16_SUBLANES = 16
W_IN_CHUNK = 512
N_WHOLE_CASTS = 2

LOG2E = 1.4426950408889634
SB_SKIP_BELOW = -160.0

BF16 = jnp.bfloat16
F32 = jnp.float32


def _inv_rms(x):
    return lax.rsqrt(jnp.mean(x * x, axis=-1, keepdims=True) + EPS)


def _rms_norm(x, g):
    return x * _inv_rms(x) * g


def _dot(a, b):
    return jnp.dot(a, b, preferred_element_type=F32)


def _dot_nt(a, b):
    return lax.dot_general(a, b, (((1,), (1,)), ((), ())), preferred_element_type=F32)


def _resident(shape):
    return pl.BlockSpec(shape, lambda *_: (0,) * len(shape), pipeline_mode=pl.Buffered(1))


def _interleave(chains, lag):
    waiting, live, tick = list(chains), [], 0
    while waiting or live:
        if waiting and tick % lag == 0:
            live.append(waiting.pop(0))
        live = [c for c in live if next(c, True) is None]
        tick += 1


def _store_head_rows(ref, row0, x):
    rows = x.shape[0]
    for h in range(SB_HEADS):
        ref[pl.ds(row0 * SB_HEADS + h, rows, stride=SB_HEADS), :] = x[:, h * SB_HEAD_DIM:(h + 1) * SB_HEAD_DIM]


def _group_steps(steps, maps_to):
    starts = [sum(steps[:g]) for g in range(len(steps))]
    return [lambda i, s=s, n=n: maps_to(jnp.clip(i - s, 0, n - 1)) for s, n in zip(starts, steps)]


def _run_group(steps, bodies):
    i = pl.program_id(0)
    start = 0
    for n, body in zip(steps, bodies):
        pl.when(jnp.logical_and(i >= start, i < start + n))(body)
        start += n


def _inproj_kernel(*refs, steps, n_cast, feature_major):
    groups = len(steps)
    cast_group = steps.index(max(steps))
    cast_start = sum(steps[:cast_group])
    g_ref, w_hbm, b_ref = refs[groups:groups + 3]
    cast_in = refs[groups + 3:groups + 3 + n_cast]
    outs = refs[groups + 3 + n_cast:-3]
    cast_out = outs[7 * groups:]
    w_ref, w_stage, w_sem = refs[-3:]
    i = pl.program_id(0)

    n_chunks = w_hbm.shape[1] // W_IN_CHUNK
    converted = [0]

    def chunk_copy(c):
        return pltpu.make_async_copy(w_hbm.at[:, c * W_IN_CHUNK:(c + 1) * W_IN_CHUNK], w_stage.at[c % 2],
                                     w_sem.at[c % 2])

    @pl.when(i == 0)
    def _():
        for c in range(min(2, n_chunks)):
            chunk_copy(c).start()

    def weights_ready(col_stop):
        while converted[0] * W_IN_CHUNK < col_stop:
            c = converted[0]
            converted[0] += 1

            @pl.when(i == 0)
            def _():
                chunk_copy(c).wait()
                w_ref[:, c * W_IN_CHUNK:(c + 1) * W_IN_CHUNK] = w_stage[c % 2].astype(BF16)
                if c + 2 < n_chunks:
                    chunk_copy(c + 2).start()

    @pl.when(i == 0)
    def _():
        for src, dst in zip(cast_in[n_cast - N_WHOLE_CASTS:], cast_out[n_cast - N_WHOLE_CASTS:]):
            dst[...] = src[...].astype(BF16)

    @pl.when(jnp.logical_and(i >= cast_start, i < cast_start + steps[cast_group]))
    def _():
        for src, dst in zip(cast_in[:n_cast - N_WHOLE_CASTS], cast_out[:n_cast - N_WHOLE_CASTS]):
            dst[...] = src[...].astype(BF16)

    _run_group(steps, [functools.partial(_inproj_tile, refs[g], g_ref, w_ref, b_ref, *outs[7 * g:7 * g + 7],
                                         feature_major=feature_major[g],
                                         weights_ready=weights_ready if g == 0 else (lambda col_stop: None))
                       for g in range(groups)])
    assert converted[0] == n_chunks


def _inproj_tile(x_ref, g_ref, w_ref, b_ref, u_ref, k_ref, v_ref, qb_ref, kb_ref, vb_ref, gate_ref, *, feature_major,
                 weights_ready):
    chunk = x_ref.shape[0] // INPROJ_CHAINS

    def store_kv(ref, c, x):
        if feature_major:
            ref[0, :, c * chunk:(c + 1) * chunk] = x.T
        else:
            _store_head_rows(ref, c * chunk, x)

    c0, c1, c2, c3 = POOL_WIDTH, POOL_WIDTH + SB_WIDTH, POOL_WIDTH + 2 * SB_WIDTH, POOL_WIDTH + 3 * SB_WIDTH

    def token_chain(c):
        rows = slice(c * chunk, (c + 1) * chunk)
        xn = _rms_norm(x_ref[rows, :], g_ref[...]).astype(BF16)
        yield
        weights_ready(c1)
        u_ref[rows, :] = _dot(xn, w_ref[:, :c0])
        qb_ref[rows, :] = (_dot(xn, w_ref[:, c0:c1]) * (SB_HEAD_DIM ** -0.5 * LOG2E)).astype(BF16)
        yield
        weights_ready(c2)
        k = _dot(xn, w_ref[:, c1:c2])
        store_kv(k_ref, c, k)
        kb_ref[rows, :] = k.astype(BF16)
        yield
        weights_ready(c3)
        v = _dot(xn, w_ref[:, c2:c3])
        store_kv(v_ref, c, v)
        vb_ref[rows, :] = v.astype(BF16)
        yield
        weights_ready(w_ref.shape[1])
        gate_ref[rows, :] = jax.nn.sigmoid(_dot(xn, w_ref[:, c3:]) + b_ref[...]).astype(BF16)

    _interleave([token_chain(c) for c in range(INPROJ_CHAINS)], 1)


def _inproj(xs, stream_tiles, g_mix, w_in, b_gate, chunked, whole, tm):
    assert len(whole) == N_WHOLE_CASTS
    in_width = w_in.shape[1]
    steps = [x.shape[0] // tm for x in xs]
    maps = _group_steps(steps, lambda j: (j, 0))
    buffers = [dict(pipeline_mode=pl.Buffered(1)) if n == 1 else {} for n in steps]
    in_specs = [pl.BlockSpec((tm, D_MODEL), m, **buf) for m, buf in zip(maps, buffers)]
    in_specs += [_resident((1, D_MODEL)), pl.BlockSpec(memory_space=pl.ANY), _resident((1, 2 * D_MODEL))]
    out_specs, out_shape = [], []
    for x, m, buf, tiles in zip(xs, maps, buffers, stream_tiles):
        n = x.shape[0]
        row = lambda width: pl.BlockSpec((tm, width), m, **buf)
        if tiles is None:
            kv_spec = pl.BlockSpec((tm * SB_HEADS, SB_HEAD_DIM), m, **buf)
            kv_shape = jax.ShapeDtypeStruct((n * SB_HEADS, SB_HEAD_DIM), F32)
        else:
            kv_spec = pl.BlockSpec((1, SB_WIDTH, tm), lambda i, m=m, tiles=tiles: (m(i)[0] // tiles, 0, m(i)[0] % tiles),
                                   **buf)
            kv_shape = jax.ShapeDtypeStruct((n // (tiles * tm), SB_WIDTH, tiles * tm), F32)
        bf_out = jax.ShapeDtypeStruct((n, SB_WIDTH), BF16)
        out_specs += [row(POOL_WIDTH), kv_spec, kv_spec, row(SB_WIDTH), row(SB_WIDTH), row(SB_WIDTH),
                      row(2 * D_MODEL)]
        out_shape += [jax.ShapeDtypeStruct((n, POOL_WIDTH), F32), kv_shape, kv_shape, bf_out, bf_out, bf_out,
                      jax.ShapeDtypeStruct((n, 2 * D_MODEL), BF16)]
    cast_specs = []
    for w in chunked:
        rows = w.shape[0] // max(steps)
        assert rows * max(steps) == w.shape[0] and rows % BF16_SUBLANES == 0
        cast_specs.append(pl.BlockSpec((rows, w.shape[1]), maps[steps.index(max(steps))]))
    cast_specs += [pl.BlockSpec(w.shape, lambda i, nd=w.ndim: (0,) * nd) for w in whole]
    cast_shape = [jax.ShapeDtypeStruct(w.shape, BF16) for w in (*chunked, *whole)]
    outs = pl.pallas_call(
        functools.partial(_inproj_kernel, steps=steps, n_cast=len(cast_specs),
                          feature_major=[tiles is not None for tiles in stream_tiles]),
        grid=(sum(steps),),
        in_specs=in_specs + cast_specs,
        out_specs=out_specs + cast_specs,
        out_shape=out_shape + cast_shape,
        scratch_shapes=[pltpu.VMEM((D_MODEL, in_width), BF16), pltpu.VMEM((2, D_MODEL, W_IN_CHUNK), F32),
                        pltpu.SemaphoreType.DMA((2,))],
        compiler_params=pltpu.CompilerParams(dimension_semantics=("arbitrary",),
                                             vmem_limit_bytes=VMEM_LIMIT_BYTES),
        name="inproj",
    )(*xs, g_mix.reshape(1, D_MODEL), w_in, b_gate.reshape(1, 2 * D_MODEL), *chunked, *whole)
    n_group_outs = 7 * len(xs)
    return ([outs[7 * g:7 * g + 7] for g in range(len(xs))],
            outs[n_group_outs:n_group_outs + len(chunked)], outs[n_group_outs + len(chunked):])


def _neg_lower(n):
    r = lax.broadcasted_iota(jnp.int32, (n, n), 0)
    c = lax.broadcasted_iota(jnp.int32, (n, n), 1)
    return jnp.where(r >= c, -1.0, 0.0).astype(BF16)


def _stack_heads(x):
    lane = lax.broadcasted_iota(jnp.int32, x.shape, 1)
    zero = jnp.zeros_like(x)
    return jnp.concatenate([jnp.where(lane < SB_HEAD_DIM, x, zero), jnp.where(lane >= SB_HEAD_DIM, x, zero)], axis=0)


def _causal_mask(tq, tk):
    r = lax.broadcasted_iota(jnp.int32, (2 * tq, tk), 0)
    c = lax.broadcasted_iota(jnp.int32, (2 * tq, tk), 1)
    return c < jnp.where(r >= tq, r - tq, r)


def _sb_pair_stages(q2, blocks, carry, emit):
    tq = q2.shape[0] // 2
    logits = []
    for k_blk, _, _, causal in blocks:
        z = _dot_nt(q2, k_blk)
        if causal is not None:
            z = jnp.where(causal, z, -jnp.inf)
        logits.append(z)
    yield
    softplus = [jnp.maximum(z, 0.0) + jnp.log(1.0 + jnp.exp2(-jnp.abs(z))) * LOG2E for z in logits]
    yield
    sums = [_dot(sp.astype(BF16), neg_tri) for (_, _, neg_tri, _), sp in zip(blocks, softplus)]
    yield
    weights = []
    for z, incl in zip(logits, sums):
        a = jnp.exp2(z + incl + carry).astype(BF16)
        weights += [a[:tq], a[tq:]]
        carry = carry + incl[:, :1]
    peak = jnp.max(carry)
    yield
    values = jnp.concatenate([v2 for _, v2, _, _ in blocks], axis=0)
    emit(_dot(jnp.concatenate(weights, axis=1), values), carry, peak)


def _sb_prompt_kernel(q_ref, k_ref, v_ref, o_ref, acc_ref, carry_ref, peak_ref):
    qi = pl.program_id(2)
    blk = SB_BLOCK
    pairs = range(SB_PAIRS_PER_STEP)
    lanes = lambda p: slice(p * HEAD_PAIR, (p + 1) * HEAD_PAIR)
    q2 = [_stack_heads(q_ref[0, :, lanes(p)]) for p in pairs]
    neg_tri = _neg_lower(blk)

    def key_block(p, j, causal):
        r0 = pl.multiple_of(j * blk, blk)
        return (k_ref[0, pl.ds(r0, blk), lanes(p)], _stack_heads(v_ref[0, pl.ds(r0, blk), lanes(p)]),
                neg_tri, causal)

    def sweep(block_ids, first):
        def emit(p, out, carry, peak):
            if first:
                acc_ref[:, lanes(p)] = out
            else:
                acc_ref[:, lanes(p)] += out
            carry_ref[p] = carry
            peak_ref[p] = peak

        _interleave([_sb_pair_stages(q2[p], [key_block(p, j, causal) for j, causal in block_ids],
                                     jnp.zeros((2 * blk, 1), F32) if first else carry_ref[p],
                                     functools.partial(emit, p)) for p in pairs], SB_PAIR_LAG)

    def alive():
        peak = functools.reduce(jnp.maximum, [peak_ref[p] for p in pairs])
        return (peak > SB_SKIP_BELOW).astype(jnp.int32)

    diagonal = (qi, _causal_mask(blk, blk))
    pl.when(qi == 0)(lambda: sweep([diagonal], True))
    pl.when(qi > 0)(lambda: sweep([diagonal, (qi - 1, None)], True))

    def body(state):
        j, _ = state
        sweep([(j, None)], False)
        return j - 1, alive()

    lax.while_loop(lambda s: jnp.logical_and(s[0] >= 0, s[1] > 0), body, (qi - 2, alive()))
    o_ref[0] = acc_ref[...].astype(o_ref.dtype)


def _sb_prompt(qb, kb, vb):
    b, t, _ = qb.shape
    blk = SB_BLOCK
    width = SB_PAIRS_PER_STEP * HEAD_PAIR
    kv_spec = pl.BlockSpec((1, t, width), lambda bi, hp, qi: (bi, 0, hp))
    q_spec = pl.BlockSpec((1, blk, width), lambda bi, hp, qi: (bi, qi, hp))
    return pl.pallas_call(
        _sb_prompt_kernel,
        grid=(b, SB_WIDTH // width, t // blk),
        in_specs=[q_spec, kv_spec, kv_spec],
        out_specs=q_spec,
        out_shape=jax.ShapeDtypeStruct((b, t, SB_WIDTH), BF16),
        scratch_shapes=[pltpu.VMEM((blk, width), F32), pltpu.VMEM((SB_PAIRS_PER_STEP, 2 * blk, 1), F32),
                        pltpu.SMEM((SB_PAIRS_PER_STEP,), F32)],
        compiler_params=pltpu.CompilerParams(dimension_semantics=("parallel", "parallel", "arbitrary"),
                                             vmem_limit_bytes=VMEM_LIMIT_BYTES),
        name="sb_prompt",
    )(qb, kb, vb)


def _sb_heads_stages(q, blocks, carry, emit):
    t = q[0].shape[0]
    logits = []
    for keys, _, _, causal, feature_major in blocks:
        score = _dot if feature_major else _dot_nt
        z = jnp.concatenate([score(qh, kh) for qh, kh in zip(q, keys)], axis=0)
        logits.append(z if causal is None else jnp.where(causal, z, -jnp.inf))
    yield
    softplus = [jnp.maximum(z, 0.0) + jnp.log(1.0 + jnp.exp2(-jnp.abs(z))) * LOG2E for z in logits]
    yield
    sums = [_dot(sp.astype(BF16), neg_tri) for (_, _, neg_tri, _, _), sp in zip(blocks, softplus)]
    yield
    weights = []
    for z, incl in zip(logits, sums):
        weights.append(jnp.exp2(z + incl + carry).astype(BF16))
        carry = carry + incl[:, :1]
    peak = jnp.max(carry)
    yield
    outs = None
    for (_, values, _, _, feature_major), a in zip(blocks, weights):
        mix = _dot_nt if feature_major else _dot
        part = [mix(a[h * t:(h + 1) * t], vh) for h, vh in enumerate(values)]
        outs = part if outs is None else [o + p for o, p in zip(outs, part)]
    emit(outs, carry, peak)


def _sb_sample_kernel(q_ref, k_ref, v_ref, ck_hbm, cv_hbm, o_ref, kbuf, vbuf, acc_ref, carry_ref, peak_ref, sem,
                      *, layer, past_len):
    step = pl.program_id(0)
    slot = step % 2
    blk = SB_BLOCK
    t = q_ref.shape[1]
    heads = range(SB_HEADS)
    local = range(SB_SAMPLE_STREAMS)
    cols = lambda h: slice(h * SB_HEAD_DIM, (h + 1) * SB_HEAD_DIM)
    last = past_len // blk - 1

    def fetch(at_step, j):
        tokens = pl.ds(pl.multiple_of(j * blk, blk), blk)
        copies = []
        for s in local:
            stream = at_step * SB_SAMPLE_STREAMS + s
            copies.append(pltpu.make_async_copy(ck_hbm.at[layer, stream, :, :, tokens],
                                                kbuf.at[at_step % 2, s], sem.at[at_step % 2, s, 0]))
            copies.append(pltpu.make_async_copy(cv_hbm.at[layer, stream, :, :, tokens],
                                                vbuf.at[at_step % 2, s], sem.at[at_step % 2, s, 1]))
        return copies

    def start(copies):
        for c in copies:
            c.start()

    def wait(copies):
        for c in copies:
            c.wait()

    def cached_block(s):
        return ([kbuf[slot, s, h].astype(BF16) for h in heads], [vbuf[slot, s, h].astype(BF16) for h in heads],
                _neg_lower(blk), None, True)

    def sweep(blocks_of, first):
        def emit(s, outs, carry, peak):
            for h in heads:
                if first:
                    acc_ref[s, :, cols(h)] = outs[h]
                else:
                    acc_ref[s, :, cols(h)] += outs[h]
            carry_ref[s] = carry
            peak_ref[s] = peak

        _interleave([_sb_heads_stages([q_ref[s, :, cols(h)] for h in heads], blocks_of(s),
                                      jnp.zeros((SB_HEADS * t, 1), F32) if first else carry_ref[s],
                                      functools.partial(emit, s)) for s in local], 1)

    def alive():
        peak = functools.reduce(jnp.maximum, [peak_ref[s] for s in local])
        return (peak > SB_SKIP_BELOW).astype(jnp.int32)

    pl.when(step == 0)(lambda: start(fetch(step, last)))
    pl.when(step + 1 < pl.num_programs(0))(lambda: start(fetch(step + 1, last)))
    wait(fetch(step, last))
    r = lax.broadcasted_iota(jnp.int32, (SB_HEADS * t, t), 0)
    causal = lax.broadcasted_iota(jnp.int32, (SB_HEADS * t, t), 1) < lax.rem(r, t)
    new_block = lambda s: ([k_ref[s, :, cols(h)] for h in heads], [v_ref[s, :, cols(h)] for h in heads],
                           _neg_lower(t), causal, False)
    sweep(lambda s: [new_block(s), cached_block(s)], True)

    def body(state):
        j, _ = state
        copies = fetch(step, j)
        start(copies)
        wait(copies)
        sweep(lambda s: [cached_block(s)], False)
        return j - 1, alive()

    lax.while_loop(lambda s: jnp.logical_and(s[0] >= 0, s[1] > 0), body, (last - 1, alive()))
    o_ref[...] = acc_ref[...].astype(o_ref.dtype)


def _sb_sample(qb, kb, vb, cache_k, cache_v, layer):
    b, t, _ = qb.shape
    past_len = cache_k.shape[-1]
    n = SB_SAMPLE_STREAMS
    assert past_len % SB_BLOCK == 0 and b % n == 0
    new_spec = pl.BlockSpec((n, t, SB_WIDTH), lambda i: (i, 0, 0))
    hbm_spec = pl.BlockSpec(memory_space=pl.ANY)
    block = (2, n, SB_HEADS, SB_HEAD_DIM, SB_BLOCK)
    return pl.pallas_call(
        functools.partial(_sb_sample_kernel, layer=layer, past_len=past_len),
        grid=(b // n,),
        in_specs=[new_spec, new_spec, new_spec, hbm_spec, hbm_spec],
        out_specs=new_spec,
        out_shape=jax.ShapeDtypeStruct((b, t, SB_WIDTH), BF16),
        scratch_shapes=[pltpu.VMEM(block, F32), pltpu.VMEM(block, F32),
                        pltpu.VMEM((n, t, SB_WIDTH), F32), pltpu.VMEM((n, SB_HEADS * t, 1), F32),
                        pltpu.SMEM((n,), F32), pltpu.SemaphoreType.DMA((2, n, 2))],
        compiler_params=pltpu.CompilerParams(dimension_semantics=("arbitrary",),
                                             vmem_limit_bytes=VMEM_LIMIT_BYTES),
        name="sb_sample",
    )(qb, kb, vb, cache_k, cache_v)


def _pool_diff(ext, first_pos):
    short_windows = not (isinstance(first_pos, int) and first_pos >= POOL_STATE)
    if short_windows:
        pos = first_pos + lax.broadcasted_iota(jnp.int32, (POOL_HALO, 1), 0)
    outs = []
    for g, window in enumerate(POOL_WINDOWS):
        cols = ext[:, g * POOL_GROUP_DIM:(g + 1) * POOL_GROUP_DIM]
        acc = cols
        shift = 1
        while shift < window:
            acc = acc + pltpu.roll(acc, shift, axis=0)
            shift *= 2
        out = acc[POOL_HALO:] * (1.0 / window) - cols[POOL_HALO:]
        if short_windows:
            inv_count = 1.0 / jnp.minimum(pos + 1, window).astype(F32)
            head = acc[POOL_HALO:2 * POOL_HALO] * inv_count - cols[POOL_HALO:2 * POOL_HALO]
            out = jnp.concatenate([head, out[POOL_HALO:]], axis=0)
        outs.append(out)
    return jnp.concatenate(outs, axis=1)


POST_GROUP_INPUTS = 6
POST_WEIGHTS = 12


def _post_kernel(*refs, steps, modes, final_norm):
    groups = len(steps)
    n_in = POST_GROUP_INPUTS * groups
    weights = refs[n_in:n_in + POST_WEIGHTS]
    ys = refs[n_in + POST_WEIGHTS:]
    _run_group(steps, [functools.partial(_post_tile, *refs[POST_GROUP_INPUTS * g:POST_GROUP_INPUTS * (g + 1)], *weights,
                                         ys[g], first_step=sum(steps[:g]), final_norm=final_norm, **modes[g])
                       for g in range(groups)])


def _post_tile(x_ref, u_ref, halo_ref, o_ref, gate_ref, p_ref,
               wgrp_ref, scale_ref, wpu_ref, wsu_ref, wout_ref,
               gmlp_ref, wup_ref, wdown_ref, gple_ref, wpg_ref, wpp_ref, gfin_ref,
               y_ref, *, first_step, tiles_per_stream, streams_per_tile, past_pos, final_norm):
    tm = u_ref.shape[0]
    chunk = tm // POST_CHAINS

    def pooled(c):
        u = u_ref[c * chunk:(c + 1) * chunk, :]
        if streams_per_tile == 1:
            step = (pl.program_id(0) - first_step) % tiles_per_stream
            if c == 0:
                halo = jnp.where(step == 0, 0.0, halo_ref[...])
            else:
                halo = u_ref[c * chunk - POOL_HALO:c * chunk, :]
            at_least = past_pos + c * chunk
            first_pos = at_least if at_least >= POOL_STATE else at_least + step * tm
            diff = _pool_diff(jnp.concatenate([halo, u], axis=0), first_pos)
        else:
            t = tm // streams_per_tile
            per_chunk = streams_per_tile // POST_CHAINS
            diff = jnp.concatenate(
                [_pool_diff(jnp.concatenate([halo_ref[c * per_chunk + s], u[s * t:(s + 1) * t]], axis=0), past_pos)
                 for s in range(per_chunk)], axis=0)
        return diff.astype(BF16)

    def token_chain(c):
        rows = slice(c * chunk, (c + 1) * chunk)
        diff = pooled(c)
        y_pool = jnp.concatenate(
            [_dot(diff[:, g * POOL_GROUP_DIM:(g + 1) * POOL_GROUP_DIM], wgrp_ref[g])
             for g in range(len(POOL_WINDOWS))], axis=1) * scale_ref[...]
        gates = gate_ref[rows, :].astype(F32)
        merged = (gates[:, :D_MODEL] * _dot(y_pool.astype(BF16), wpu_ref[...])
                  + gates[:, D_MODEL:] * _dot(o_ref[rows, :], wsu_ref[...]))
        yield
        x = x_ref[rows, :] + _dot(merged.astype(BF16), wout_ref[...])
        r = _inv_rms(x)
        xg = (x * gmlp_ref[...]).astype(BF16)
        yield
        h = jnp.square(jnp.maximum(_dot(xg, wup_ref[...]), 0.0)).astype(BF16)
        yield
        x = x + (r * r) * _dot(h, wdown_ref[...])
        r = _inv_rms(x)
        xg = (x * gple_ref[...]).astype(BF16)
        yield
        ple_gate = jax.nn.sigmoid(r * _dot(xg, wpg_ref[...]))
        x = x + ple_gate * _dot(p_ref[rows, :].astype(BF16), wpp_ref[...])
        if final_norm:
            x = _rms_norm(x, gfin_ref[...])
        y_ref[rows, :] = x

    _interleave([token_chain(c) for c in range(POST_CHAINS)], POST_LAG)


def _post(groups, weights, modes, *, tm, final_norm):
    assert len(weights) == POST_WEIGHTS and all(len(g) == POST_GROUP_INPUTS for g in groups)
    steps = [g[0].shape[0] // tm for g in groups]
    maps = _group_steps(steps, lambda j: j)
    in_specs, out_specs, out_shape = [], [], []
    for (x, *_), mode, m in zip(groups, modes, maps):
        row = lambda width, m=m: pl.BlockSpec((tm, width), lambda i: (m(i), 0))
        if mode["streams_per_tile"] == 1:
            per_tile = tm // POOL_HALO
            halo_spec = pl.BlockSpec((POOL_HALO, POOL_WIDTH),
                                     lambda i, m=m: (jnp.maximum(m(i) * per_tile - 1, 0), 0))
        else:
            halo_spec = pl.BlockSpec((mode["streams_per_tile"], POOL_HALO, POOL_WIDTH), lambda i, m=m: (m(i), 0, 0))
        in_specs += [row(D_MODEL), row(POOL_WIDTH), halo_spec, row(SB_WIDTH), row(2 * D_MODEL), row(PLE_DIM)]
        out_specs.append(row(D_MODEL))
        out_shape.append(jax.ShapeDtypeStruct((x.shape[0], D_MODEL), F32))
    in_specs += [_resident(w.shape) for w in weights]
    return pl.pallas_call(
        functools.partial(_post_kernel, steps=steps, modes=modes, final_norm=final_norm),
        grid=(sum(steps),),
        in_specs=in_specs,
        out_specs=out_specs,
        out_shape=out_shape,
        compiler_params=pltpu.CompilerParams(dimension_semantics=("arbitrary",),
                                             vmem_limit_bytes=VMEM_LIMIT_BYTES),
        name="post",
    )(*[a for g in groups for a in g], *weights)


def kernel(x_prompt, x_sample, cache_k, cache_v, state_pool, p_prompt, p_sample, g_mix, w_in, b_gate, w_pool_grp, pool_scale, w_pool_up, w_sb_up, w_out, g_mlp, w_up, w_down, g_ple, w_ple_gate, w_ple_proj, g_final):
    depth = w_in.shape[0]
    bp, tp, _ = x_prompt.shape
    bs, ts, _ = x_sample.shape
    past_len = cache_k.shape[2]
    assert tp % ROW_TILE == 0 and tp % SB_BLOCK == 0
    n_s = bs * ts
    assert ts >= POOL_STATE and ts % 8 == 0 and n_s % ROW_TILE == 0 and (ROW_TILE // ts) % POST_CHAINS == 0
    xp = x_prompt.reshape(bp * tp, D_MODEL)
    xs = x_sample.reshape(n_s, D_MODEL)
    row_vec = lambda a: a.reshape(1, -1)
    cache_rows = lambda c: jnp.transpose(c, (0, 1, 3, 4, 2))

    outs = {name: [] for name in ("kp", "vp", "pp", "ks", "vs", "ps")}
    for d in range(depth):
        final_norm = d == depth - 1

        (((us, ks, vs, qbs, kbs, vbs, gates_s), (up, kp, vp, qbp, kbp, vbp, gates_p)),
         (wpu_bf, wsu_bf, wout_bf, wup_bf, wdown_bf, wpg_bf), (wpp_bf, wgrp_bf)) = _inproj(
            [xs, xp], [None, tp // ROW_TILE], g_mix[d], w_in[d], b_gate[d],
            [w_pool_up[d], w_sb_up[d], w_out[d], w_up[d], w_down[d], w_ple_gate[d]],
            [w_ple_proj[d], w_pool_grp[d]], ROW_TILE)
        weights = (wgrp_bf, row_vec(pool_scale[d]), wpu_bf, wsu_bf, wout_bf, row_vec(g_mlp[d]), wup_bf,
                   wdown_bf, row_vec(g_ple[d]), wpg_bf, wpp_bf, row_vec(g_final))
        op = _sb_prompt(qbp.reshape(bp, tp, SB_WIDTH), kbp.reshape(bp, tp, SB_WIDTH), vbp.reshape(bp, tp, SB_WIDTH))
        os_ = _sb_sample(qbs.reshape(bs, ts, SB_WIDTH), kbs.reshape(bs, ts, SB_WIDTH), vbs.reshape(bs, ts, SB_WIDTH),
                         cache_rows(cache_k), cache_rows(cache_v), d)
        halo_s = jnp.pad(state_pool[d], ((0, 0), (POOL_HALO - POOL_STATE, 0), (0, 0)))
        xp, = _post([(xp, up, up, op.reshape(bp * tp, SB_WIDTH), gates_p, p_prompt[d].reshape(bp * tp, PLE_DIM))],
                    weights, [dict(tiles_per_stream=tp // ROW_TILE, streams_per_tile=1, past_pos=0)],
                    tm=ROW_TILE, final_norm=final_norm)
        xs, = _post([(xs, us, halo_s, os_.reshape(n_s, SB_WIDTH), gates_s, p_sample[d].reshape(n_s, PLE_DIM))],
                    weights, [dict(tiles_per_stream=1, streams_per_tile=ROW_TILE // ts, past_pos=POOL_STATE)],
                    tm=ROW_TILE, final_norm=final_norm)
        token_major = lambda a: jnp.transpose(a.reshape(bp, SB_HEADS, SB_HEAD_DIM, tp), (0, 3, 1, 2))
        outs["kp"].append(token_major(kp))
        outs["vp"].append(token_major(vp))
        outs["pp"].append(up.reshape(bp, tp, POOL_WIDTH)[:, tp - POOL_STATE:])
        outs["ks"].append(ks.reshape(bs, ts, SB_HEADS, SB_HEAD_DIM))
        outs["vs"].append(vs.reshape(bs, ts, SB_HEADS, SB_HEAD_DIM))
        outs["ps"].append(us.reshape(bs, ts, POOL_WIDTH)[:, ts - POOL_STATE:])

    stack = lambda name: jnp.stack(outs[name])
    return (xp.reshape(bp, tp, D_MODEL), xs.reshape(bs, ts, D_MODEL),
            stack("kp"), stack("vp"), stack("pp"), stack("ks"), stack("vs"), stack("ps"))
```

```python
import functools

import jax
import jax.numpy as jnp
from jax import lax
from jax.experimental import pallas as pl
from jax.experimental.pallas import tpu as pltpu

D_MODEL = 1024
POOL_WIDTH = 512
POOL_WINDOWS = (2, 4, 8, 16)
POOL_GROUP_DIM = POOL_WIDTH // len(POOL_WINDOWS)
POOL_STATE = max(POOL_WINDOWS) - 1
POOL_HALO = 16
SB_HEADS = 8
SB_HEAD_DIM = 64
SB_WIDTH = SB_HEADS * SB_HEAD_DIM
HEAD_PAIR = 2 * SB_HEAD_DIM
D_FF = 4 * D_MODEL
PLE_DIM = 256
EPS = 1e-6

V7X_VMEM_BYTES = 64 * 1024 * 1024
VMEM_LIMIT_BYTES = V7X_VMEM_BYTES - 8 * 1024 * 1024

ROW_TILE = 512
INPROJ_CHAINS = 2
POST_CHAINS = 2
POST_LAG = 1
SB_BLOCK = 256
SB_PAIRS_PER_STEP = 4
SB_SAMPLE_STREAMS = 8
SB_PAIR_LAG = 2

BF16_SUBLANES = 16
W_IN_CHUNK = 512
N_WHOLE_CASTS = 2

LOG2E = 1.4426950408889634
SB_SKIP_BELOW = -160.0

BF16 = jnp.bfloat16
F32 = jnp.float32


def _inv_rms(x):
    return lax.rsqrt(jnp.mean(x * x, axis=-1, keepdims=True) + EPS)


def _rms_norm(x, g):
    return x * _inv_rms(x) * g


def _dot(a, b):
    return jnp.dot(a, b, preferred_element_type=F32)


def _dot_nt(a, b):
    return lax.dot_general(a, b, (((1,), (1,)), ((), ())), preferred_element_type=F32)


def _resident(shape):
    return pl.BlockSpec(shape, lambda *_: (0,) * len(shape), pipeline_mode=pl.Buffered(1))


def _interleave(chains, lag):
    waiting, live, tick = list(chains), [], 0
    while waiting or live:
        if waiting and tick % lag == 0:
            live.append(waiting.pop(0))
        live = [c for c in live if next(c, True) is None]
        tick += 1


def _store_head_rows(ref, row0, x):
    rows = x.shape[0]
    for h in range(SB_HEADS):
        ref[pl.ds(row0 * SB_HEADS + h, rows, stride=SB_HEADS), :] = x[:, h * SB_HEAD_DIM:(h + 1) * SB_HEAD_DIM]


def _group_steps(steps, maps_to):
    starts = [sum(steps[:g]) for g in range(len(steps))]
    return [lambda i, s=s, n=n: maps_to(jnp.clip(i - s, 0, n - 1)) for s, n in zip(starts, steps)]


def _run_group(steps, bodies):
    i = pl.program_id(0)
    start = 0
    for n, body in zip(steps, bodies):
        pl.when(jnp.logical_and(i >= start, i < start + n))(body)
        start += n


def _inproj_kernel(*refs, steps, n_cast, feature_major):
    groups = len(steps)
    cast_group = steps.index(max(steps))
    cast_start = sum(steps[:cast_group])
    g_ref, w_hbm, b_ref = refs[groups:groups + 3]
    cast_in = refs[groups + 3:groups + 3 + n_cast]
    outs = refs[groups + 3 + n_cast:-3]
    cast_out = outs[7 * groups:]
    w_ref, w_stage, w_sem = refs[-3:]
    i = pl.program_id(0)

    n_chunks = w_hbm.shape[1] // W_IN_CHUNK
    converted = [0]

    def chunk_copy(c):
        return pltpu.make_async_copy(w_hbm.at[:, c * W_IN_CHUNK:(c + 1) * W_IN_CHUNK], w_stage.at[c % 2],
                                     w_sem.at[c % 2])

    @pl.when(i == 0)
    def _():
        for c in range(min(2, n_chunks)):
            chunk_copy(c).start()

    def weights_ready(col_stop):
        while converted[0] * W_IN_CHUNK < col_stop:
            c = converted[0]
            converted[0] += 1

            @pl.when(i == 0)
            def _():
                chunk_copy(c).wait()
                w_ref[:, c * W_IN_CHUNK:(c + 1) * W_IN_CHUNK] = w_stage[c % 2].astype(BF16)
                if c + 2 < n_chunks:
                    chunk_copy(c + 2).start()

    @pl.when(i == 0)
    def _():
        for src, dst in zip(cast_in[n_cast - N_WHOLE_CASTS:], cast_out[n_cast - N_WHOLE_CASTS:]):
            dst[...] = src[...].astype(BF16)

    @pl.when(jnp.logical_and(i >= cast_start, i < cast_start + steps[cast_group]))
    def _():
        for src, dst in zip(cast_in[:n_cast - N_WHOLE_CASTS], cast_out[:n_cast - N_WHOLE_CASTS]):
            dst[...] = src[...].astype(BF16)

    _run_group(steps, [functools.partial(_inproj_tile, refs[g], g_ref, w_ref, b_ref, *outs[7 * g:7 * g + 7],
                                         feature_major=feature_major[g],
                                         weights_ready=weights_ready if g == 0 else (lambda col_stop: None))
                       for g in range(groups)])
    assert converted[0] == n_chunks


def _inproj_tile(x_ref, g_ref, w_ref, b_ref, u_ref, k_ref, v_ref, qb_ref, kb_ref, vb_ref, gate_ref, *, feature_major,
                 weights_ready):
    chunk = x_ref.shape[0] // INPROJ_CHAINS

    def store_kv(ref, c, x):
        if feature_major:
            ref[0, :, c * chunk:(c + 1) * chunk] = x.T
        else:
            _store_head_rows(ref, c * chunk, x)

    c0, c1, c2, c3 = POOL_WIDTH, POOL_WIDTH + SB_WIDTH, POOL_WIDTH + 2 * SB_WIDTH, POOL_WIDTH + 3 * SB_WIDTH

    def token_chain(c):
        rows = slice(c * chunk, (c + 1) * chunk)
        xn = _rms_norm(x_ref[rows, :], g_ref[...]).astype(BF16)
        yield
        weights_ready(c1)
        u_ref[rows, :] = _dot(xn, w_ref[:, :c0])
        qb_ref[rows, :] = (_dot(xn, w_ref[:, c0:c1]) * (SB_HEAD_DIM ** -0.5 * LOG2E)).astype(BF16)
        yield
        weights_ready(c2)
        k = _dot(xn, w_ref[:, c1:c2])
        store_kv(k_ref, c, k)
        kb_ref[rows, :] = k.astype(BF16)
        yield
        weights_ready(c3)
        v = _dot(xn, w_ref[:, c2:c3])
        store_kv(v_ref, c, v)
        vb_ref[rows, :] = v.astype(BF16)
        yield
        weights_ready(w_ref.shape[1])
        gate_ref[rows, :] = jax.nn.sigmoid(_dot(xn, w_ref[:, c3:]) + b_ref[...]).astype(BF16)

    _interleave([token_chain(c) for c in range(INPROJ_CHAINS)], 1)


def _inproj(xs, stream_tiles, g_mix, w_in, b_gate, chunked, whole, tm):
    assert len(whole) == N_WHOLE_CASTS
    in_width = w_in.shape[1]
    steps = [x.shape[0] // tm for x in xs]
    maps = _group_steps(steps, lambda j: (j, 0))
    buffers = [dict(pipeline_mode=pl.Buffered(1)) if n == 1 else {} for n in steps]
    in_specs = [pl.BlockSpec((tm, D_MODEL), m, **buf) for m, buf in zip(maps, buffers)]
    in_specs += [_resident((1, D_MODEL)), pl.BlockSpec(memory_space=pl.ANY), _resident((1, 2 * D_MODEL))]
    out_specs, out_shape = [], []
    for x, m, buf, tiles in zip(xs, maps, buffers, stream_tiles):
        n = x.shape[0]
        row = lambda width: pl.BlockSpec((tm, width), m, **buf)
        if tiles is None:
            kv_spec = pl.BlockSpec((tm * SB_HEADS, SB_HEAD_DIM), m, **buf)
            kv_shape = jax.ShapeDtypeStruct((n * SB_HEADS, SB_HEAD_DIM), F32)
        else:
            kv_spec = pl.BlockSpec((1, SB_WIDTH, tm), lambda i, m=m, tiles=tiles: (m(i)[0] // tiles, 0, m(i)[0] % tiles),
                                   **buf)
            kv_shape = jax.ShapeDtypeStruct((n // (tiles * tm), SB_WIDTH, tiles * tm), F32)
        bf_out = jax.ShapeDtypeStruct((n, SB_WIDTH), BF16)
        out_specs += [row(POOL_WIDTH), kv_spec, kv_spec, row(SB_WIDTH), row(SB_WIDTH), row(SB_WIDTH),
                      row(2 * D_MODEL)]
        out_shape += [jax.ShapeDtypeStruct((n, POOL_WIDTH), F32), kv_shape, kv_shape, bf_out, bf_out, bf_out,
                      jax.ShapeDtypeStruct((n, 2 * D_MODEL), BF16)]
    cast_specs = []
    for w in chunked:
        rows = w.shape[0] // max(steps)
        assert rows * max(steps) == w.shape[0] and rows % BF16_SUBLANES == 0
        cast_specs.append(pl.BlockSpec((rows, w.shape[1]), maps[steps.index(max(steps))]))
    cast_specs += [pl.BlockSpec(w.shape, lambda i, nd=w.ndim: (0,) * nd) for w in whole]
    cast_shape = [jax.ShapeDtypeStruct(w.shape, BF16) for w in (*chunked, *whole)]
    outs = pl.pallas_call(
        functools.partial(_inproj_kernel, steps=steps, n_cast=len(cast_specs),
                          feature_major=[tiles is not None for tiles in stream_tiles]),
        grid=(sum(steps),),
        in_specs=in_specs + cast_specs,
        out_specs=out_specs + cast_specs,
        out_shape=out_shape + cast_shape,
        scratch_shapes=[pltpu.VMEM((D_MODEL, in_width), BF16), pltpu.VMEM((2, D_MODEL, W_IN_CHUNK), F32),
                        pltpu.SemaphoreType.DMA((2,))],
        compiler_params=pltpu.CompilerParams(dimension_semantics=("arbitrary",),
                                             vmem_limit_bytes=VMEM_LIMIT_BYTES),
        name="inproj",
    )(*xs, g_mix.reshape(1, D_MODEL), w_in, b_gate.reshape(1, 2 * D_MODEL), *chunked, *whole)
    n_group_outs = 7 * len(xs)
    return ([outs[7 * g:7 * g + 7] for g in range(len(xs))],
            outs[n_group_outs:n_group_outs + len(chunked)], outs[n_group_outs + len(chunked):])


def _neg_lower(n):
    r = lax.broadcasted_iota(jnp.int32, (n, n), 0)
    c = lax.broadcasted_iota(jnp.int32, (n, n), 1)
    return jnp.where(r >= c, -1.0, 0.0).astype(BF16)


def _stack_heads(x):
    lane = lax.broadcasted_iota(jnp.int32, x.shape, 1)
    zero = jnp.zeros_like(x)
    return jnp.concatenate([jnp.where(lane < SB_HEAD_DIM, x, zero), jnp.where(lane >= SB_HEAD_DIM, x, zero)], axis=0)


def _causal_mask(tq, tk):
    r = lax.broadcasted_iota(jnp.int32, (2 * tq, tk), 0)
    c = lax.broadcasted_iota(jnp.int32, (2 * tq, tk), 1)
    return c < jnp.where(r >= tq, r - tq, r)


def _sb_pair_stages(q2, blocks, carry, emit):
    tq = q2.shape[0] // 2
    logits = []
    for k_blk, _, _, causal in blocks:
        z = _dot_nt(q2, k_blk)
        if causal is not None:
            z = jnp.where(causal, z, -jnp.inf)
        logits.append(z)
    yield
    softplus = [jnp.maximum(z, 0.0) + jnp.log(1.0 + jnp.exp2(-jnp.abs(z))) * LOG2E for z in logits]
    yield
    sums = [_dot(sp.astype(BF16), neg_tri) for (_, _, neg_tri, _), sp in zip(blocks, softplus)]
    yield
    weights = []
    for z, incl in zip(logits, sums):
        a = jnp.exp2(z + incl + carry).astype(BF16)
        weights += [a[:tq], a[tq:]]
        carry = carry + incl[:, :1]
    peak = jnp.max(carry)
    yield
    values = jnp.concatenate([v2 for _, v2, _, _ in blocks], axis=0)
    emit(_dot(jnp.concatenate(weights, axis=1), values), carry, peak)


def _sb_prompt_kernel(q_ref, k_ref, v_ref, o_ref, acc_ref, carry_ref, peak_ref):
    qi = pl.program_id(2)
    blk = SB_BLOCK
    pairs = range(SB_PAIRS_PER_STEP)
    lanes = lambda p: slice(p * HEAD_PAIR, (p + 1) * HEAD_PAIR)
    q2 = [_stack_heads(q_ref[0, :, lanes(p)]) for p in pairs]
    neg_tri = _neg_lower(blk)

    def key_block(p, j, causal):
        r0 = pl.multiple_of(j * blk, blk)
        return (k_ref[0, pl.ds(r0, blk), lanes(p)], _stack_heads(v_ref[0, pl.ds(r0, blk), lanes(p)]),
                neg_tri, causal)

    def sweep(block_ids, first):
        def emit(p, out, carry, peak):
            if first:
                acc_ref[:, lanes(p)] = out
            else:
                acc_ref[:, lanes(p)] += out
            carry_ref[p] = carry
            peak_ref[p] = peak

        _interleave([_sb_pair_stages(q2[p], [key_block(p, j, causal) for j, causal in block_ids],
                                     jnp.zeros((2 * blk, 1), F32) if first else carry_ref[p],
                                     functools.partial(emit, p)) for p in pairs], SB_PAIR_LAG)

    def alive():
        peak = functools.reduce(jnp.maximum, [peak_ref[p] for p in pairs])
        return (peak > SB_SKIP_BELOW).astype(jnp.int32)

    diagonal = (qi, _causal_mask(blk, blk))
    pl.when(qi == 0)(lambda: sweep([diagonal], True))
    pl.when(qi > 0)(lambda: sweep([diagonal, (qi - 1, None)], True))

    def body(state):
        j, _ = state
        sweep([(j, None)], False)
        return j - 1, alive()

    lax.while_loop(lambda s: jnp.logical_and(s[0] >= 0, s[1] > 0), body, (qi - 2, alive()))
    o_ref[0] = acc_ref[...].astype(o_ref.dtype)


def _sb_prompt(qb, kb, vb):
    b, t, _ = qb.shape
    blk = SB_BLOCK
    width = SB_PAIRS_PER_STEP * HEAD_PAIR
    kv_spec = pl.BlockSpec((1, t, width), lambda bi, hp, qi: (bi, 0, hp))
    q_spec = pl.BlockSpec((1, blk, width), lambda bi, hp, qi: (bi, qi, hp))
    return pl.pallas_call(
        _sb_prompt_kernel,
        grid=(b, SB_WIDTH // width, t // blk),
        in_specs=[q_spec, kv_spec, kv_spec],
        out_specs=q_spec,
        out_shape=jax.ShapeDtypeStruct((b, t, SB_WIDTH), BF16),
        scratch_shapes=[pltpu.VMEM((blk, width), F32), pltpu.VMEM((SB_PAIRS_PER_STEP, 2 * blk, 1), F32),
                        pltpu.SMEM((SB_PAIRS_PER_STEP,), F32)],
        compiler_params=pltpu.CompilerParams(dimension_semantics=("parallel", "parallel", "arbitrary"),
                                             vmem_limit_bytes=VMEM_LIMIT_BYTES),
        name="sb_prompt",
    )(qb, kb, vb)


def _sb_heads_stages(q, blocks, carry, emit):
    t = q[0].shape[0]
    logits = []
    for keys, _, _, causal, feature_major in blocks:
        score = _dot if feature_major else _dot_nt
        z = jnp.concatenate([score(qh, kh) for qh, kh in zip(q, keys)], axis=0)
        logits.append(z if causal is None else jnp.where(causal, z, -jnp.inf))
    yield
    softplus = [jnp.maximum(z, 0.0) + jnp.log(1.0 + jnp.exp2(-jnp.abs(z))) * LOG2E for z in logits]
    yield
    sums = [_dot(sp.astype(BF16), neg_tri) for (_, _, neg_tri, _, _), sp in zip(blocks, softplus)]
    yield
    weights = []
    for z, incl in zip(logits, sums):
        weights.append(jnp.exp2(z + incl + carry).astype(BF16))
        carry = carry + incl[:, :1]
    peak = jnp.max(carry)
    yield
    outs = None
    for (_, values, _, _, feature_major), a in zip(blocks, weights):
        mix = _dot_nt if feature_major else _dot
        part = [mix(a[h * t:(h + 1) * t], vh) for h, vh in enumerate(values)]
        outs = part if outs is None else [o + p for o, p in zip(outs, part)]
    emit(outs, carry, peak)


def _sb_sample_kernel(q_ref, k_ref, v_ref, ck_hbm, cv_hbm, o_ref, kbuf, vbuf, acc_ref, carry_ref, peak_ref, sem,
                      *, layer, past_len):
    step = pl.program_id(0)
    slot = step % 2
    blk = SB_BLOCK
    t = q_ref.shape[1]
    heads = range(SB_HEADS)
    local = range(SB_SAMPLE_STREAMS)
    cols = lambda h: slice(h * SB_HEAD_DIM, (h + 1) * SB_HEAD_DIM)
    last = past_len // blk - 1

    def fetch(at_step, j):
        tokens = pl.ds(pl.multiple_of(j * blk, blk), blk)
        copies = []
        for s in local:
            stream = at_step * SB_SAMPLE_STREAMS + s
            copies.append(pltpu.make_async_copy(ck_hbm.at[layer, stream, :, :, tokens],
                                                kbuf.at[at_step % 2, s], sem.at[at_step % 2, s, 0]))
            copies.append(pltpu.make_async_copy(cv_hbm.at[layer, stream, :, :, tokens],
                                                vbuf.at[at_step % 2, s], sem.at[at_step % 2, s, 1]))
        return copies

    def start(copies):
        for c in copies:
            c.start()

    def wait(copies):
        for c in copies:
            c.wait()

    def cached_block(s):
        return ([kbuf[slot, s, h].astype(BF16) for h in heads], [vbuf[slot, s, h].astype(BF16) for h in heads],
                _neg_lower(blk), None, True)

    def sweep(blocks_of, first):
        def emit(s, outs, carry, peak):
            for h in heads:
                if first:
                    acc_ref[s, :, cols(h)] = outs[h]
                else:
                    acc_ref[s, :, cols(h)] += outs[h]
            carry_ref[s] = carry
            peak_ref[s] = peak

        _interleave([_sb_heads_stages([q_ref[s, :, cols(h)] for h in heads], blocks_of(s),
                                      jnp.zeros((SB_HEADS * t, 1), F32) if first else carry_ref[s],
                                      functools.partial(emit, s)) for s in local], 1)

    def alive():
        peak = functools.reduce(jnp.maximum, [peak_ref[s] for s in local])
        return (peak > SB_SKIP_BELOW).astype(jnp.int32)

    pl.when(step == 0)(lambda: start(fetch(step, last)))
    pl.when(step + 1 < pl.num_programs(0))(lambda: start(fetch(step + 1, last)))
    wait(fetch(step, last))
    r = lax.broadcasted_iota(jnp.int32, (SB_HEADS * t, t), 0)
    causal = lax.broadcasted_iota(jnp.int32, (SB_HEADS * t, t), 1) < lax.rem(r, t)
    new_block = lambda s: ([k_ref[s, :, cols(h)] for h in heads], [v_ref[s, :, cols(h)] for h in heads],
                           _neg_lower(t), causal, False)
    sweep(lambda s: [new_block(s), cached_block(s)], True)

    def body(state):
        j, _ = state
        copies = fetch(step, j)
        start(copies)
        wait(copies)
        sweep(lambda s: [cached_block(s)], False)
        return j - 1, alive()

    lax.while_loop(lambda s: jnp.logical_and(s[0] >= 0, s[1] > 0), body, (last - 1, alive()))
    o_ref[...] = acc_ref[...].astype(o_ref.dtype)


def _sb_sample(qb, kb, vb, cache_k, cache_v, layer):
    b, t, _ = qb.shape
    past_len = cache_k.shape[-1]
    n = SB_SAMPLE_STREAMS
    assert past_len % SB_BLOCK == 0 and b % n == 0
    new_spec = pl.BlockSpec((n, t, SB_WIDTH), lambda i: (i, 0, 0))
    hbm_spec = pl.BlockSpec(memory_space=pl.ANY)
    block = (2, n, SB_HEADS, SB_HEAD_DIM, SB_BLOCK)
    return pl.pallas_call(
        functools.partial(_sb_sample_kernel, layer=layer, past_len=past_len),
        grid=(b // n,),
        in_specs=[new_spec, new_spec, new_spec, hbm_spec, hbm_spec],
        out_specs=new_spec,
        out_shape=jax.ShapeDtypeStruct((b, t, SB_WIDTH), BF16),
        scratch_shapes=[pltpu.VMEM(block, F32), pltpu.VMEM(block, F32),
                        pltpu.VMEM((n, t, SB_WIDTH), F32), pltpu.VMEM((n, SB_HEADS * t, 1), F32),
                        pltpu.SMEM((n,), F32), pltpu.SemaphoreType.DMA((2, n, 2))],
        compiler_params=pltpu.CompilerParams(dimension_semantics=("arbitrary",),
                                             vmem_limit_bytes=VMEM_LIMIT_BYTES),
        name="sb_sample",
    )(qb, kb, vb, cache_k, cache_v)


def _pool_diff(ext, first_pos):
    short_windows = not (isinstance(first_pos, int) and first_pos >= POOL_STATE)
    if short_windows:
        pos = first_pos + lax.broadcasted_iota(jnp.int32, (POOL_HALO, 1), 0)
    outs = []
    for g, window in enumerate(POOL_WINDOWS):
        cols = ext[:, g * POOL_GROUP_DIM:(g + 1) * POOL_GROUP_DIM]
        acc = cols
        shift = 1
        while shift < window:
            acc = acc + pltpu.roll(acc, shift, axis=0)
            shift *= 2
        out = acc[POOL_HALO:] * (1.0 / window) - cols[POOL_HALO:]
        if short_windows:
            inv_count = 1.0 / jnp.minimum(pos + 1, window).astype(F32)
            head = acc[POOL_HALO:2 * POOL_HALO] * inv_count - cols[POOL_HALO:2 * POOL_HALO]
            out = jnp.concatenate([head, out[POOL_HALO:]], axis=0)
        outs.append(out)
    return jnp.concatenate(outs, axis=1)


POST_GROUP_INPUTS = 6
POST_WEIGHTS = 12


def _post_kernel(*refs, steps, modes, final_norm):
    groups = len(steps)
    n_in = POST_GROUP_INPUTS * groups
    weights = refs[n_in:n_in + POST_WEIGHTS]
    ys = refs[n_in + POST_WEIGHTS:]
    _run_group(steps, [functools.partial(_post_tile, *refs[POST_GROUP_INPUTS * g:POST_GROUP_INPUTS * (g + 1)], *weights,
                                         ys[g], first_step=sum(steps[:g]), final_norm=final_norm, **modes[g])
                       for g in range(groups)])


def _post_tile(x_ref, u_ref, halo_ref, o_ref, gate_ref, p_ref,
               wgrp_ref, scale_ref, wpu_ref, wsu_ref, wout_ref,
               gmlp_ref, wup_ref, wdown_ref, gple_ref, wpg_ref, wpp_ref, gfin_ref,
               y_ref, *, first_step, tiles_per_stream, streams_per_tile, past_pos, final_norm):
    tm = u_ref.shape[0]
    chunk = tm // POST_CHAINS

    def pooled(c):
        u = u_ref[c * chunk:(c + 1) * chunk, :]
        if streams_per_tile == 1:
            step = (pl.program_id(0) - first_step) % tiles_per_stream
            if c == 0:
                halo = jnp.where(step == 0, 0.0, halo_ref[...])
            else:
                halo = u_ref[c * chunk - POOL_HALO:c * chunk, :]
            at_least = past_pos + c * chunk
            first_pos = at_least if at_least >= POOL_STATE else at_least + step * tm
            diff = _pool_diff(jnp.concatenate([halo, u], axis=0), first_pos)
        else:
            t = tm // streams_per_tile
            per_chunk = streams_per_tile // POST_CHAINS
            diff = jnp.concatenate(
                [_pool_diff(jnp.concatenate([halo_ref[c * per_chunk + s], u[s * t:(s + 1) * t]], axis=0), past_pos)
                 for s in range(per_chunk)], axis=0)
        return diff.astype(BF16)

    def token_chain(c):
        rows = slice(c * chunk, (c + 1) * chunk)
        diff = pooled(c)
        y_pool = jnp.concatenate(
            [_dot(diff[:, g * POOL_GROUP_DIM:(g + 1) * POOL_GROUP_DIM], wgrp_ref[g])
             for g in range(len(POOL_WINDOWS))], axis=1) * scale_ref[...]
        gates = gate_ref[rows, :].astype(F32)
        merged = (gates[:, :D_MODEL] * _dot(y_pool.astype(BF16), wpu_ref[...])
                  + gates[:, D_MODEL:] * _dot(o_ref[rows, :], wsu_ref[...]))
        yield
        x = x_ref[rows, :] + _dot(merged.astype(BF16), wout_ref[...])
        r = _inv_rms(x)
        xg = (x * gmlp_ref[...]).astype(BF16)
        yield
        h = jnp.square(jnp.maximum(_dot(xg, wup_ref[...]), 0.0)).astype(BF16)
        yield
        x = x + (r * r) * _dot(h, wdown_ref[...])
        r = _inv_rms(x)
        xg = (x * gple_ref[...]).astype(BF16)
        yield
        ple_gate = jax.nn.sigmoid(r * _dot(xg, wpg_ref[...]))
        x = x + ple_gate * _dot(p_ref[rows, :].astype(BF16), wpp_ref[...])
        if final_norm:
            x = _rms_norm(x, gfin_ref[...])
        y_ref[rows, :] = x

    _interleave([token_chain(c) for c in range(POST_CHAINS)], POST_LAG)


def _post(groups, weights, modes, *, tm, final_norm):
    assert len(weights) == POST_WEIGHTS and all(len(g) == POST_GROUP_INPUTS for g in groups)
    steps = [g[0].shape[0] // tm for g in groups]
    maps = _group_steps(steps, lambda j: j)
    in_specs, out_specs, out_shape = [], [], []
    for (x, *_), mode, m in zip(groups, modes, maps):
        row = lambda width, m=m: pl.BlockSpec((tm, width), lambda i: (m(i), 0))
        if mode["streams_per_tile"] == 1:
            per_tile = tm // POOL_HALO
            halo_spec = pl.BlockSpec((POOL_HALO, POOL_WIDTH),
                                     lambda i, m=m: (jnp.maximum(m(i) * per_tile - 1, 0), 0))
        else:
            halo_spec = pl.BlockSpec((mode["streams_per_tile"], POOL_HALO, POOL_WIDTH), lambda i, m=m: (m(i), 0, 0))
        in_specs += [row(D_MODEL), row(POOL_WIDTH), halo_spec, row(SB_WIDTH), row(2 * D_MODEL), row(PLE_DIM)]
        out_specs.append(row(D_MODEL))
        out_shape.append(jax.ShapeDtypeStruct((x.shape[0], D_MODEL), F32))
    in_specs += [_resident(w.shape) for w in weights]
    return pl.pallas_call(
        functools.partial(_post_kernel, steps=steps, modes=modes, final_norm=final_norm),
        grid=(sum(steps),),
        in_specs=in_specs,
        out_specs=out_specs,
        out_shape=out_shape,
        compiler_params=pltpu.CompilerParams(dimension_semantics=("arbitrary",),
                                             vmem_limit_bytes=VMEM_LIMIT_BYTES),
        name="post",
    )(*[a for g in groups for a in g], *weights)


def kernel(x_prompt, x_sample, cache_k, cache_v, state_pool, p_prompt, p_sample, g_mix, w_in, b_gate, w_pool_grp, pool_scale, w_pool_up, w_sb_up, w_out, g_mlp, w_up, w_down, g_ple, w_ple_gate, w_ple_proj, g_final):
    depth = w_in.shape[0]
    bp, tp, _ = x_prompt.shape
    bs, ts, _ = x_sample.shape
    past_len = cache_k.shape[2]
    assert tp % ROW_TILE == 0 and tp % SB_BLOCK == 0
    n_s = bs * ts
    assert ts >= POOL_STATE and ts % 8 == 0 and n_s % ROW_TILE == 0 and (ROW_TILE // ts) % POST_CHAINS == 0
    xp = x_prompt.reshape(bp * tp, D_MODEL)
    xs = x_sample.reshape(n_s, D_MODEL)
    row_vec = lambda a: a.reshape(1, -1)
    cache_rows = lambda c: jnp.transpose(c, (0, 1, 3, 4, 2))

    outs = {name: [] for name in ("kp", "vp", "pp", "ks", "vs", "ps")}
    for d in range(depth):
        final_norm = d == depth - 1

        (((us, ks, vs, qbs, kbs, vbs, gates_s), (up, kp, vp, qbp, kbp, vbp, gates_p)),
         (wpu_bf, wsu_bf, wout_bf, wup_bf, wdown_bf, wpg_bf), (wpp_bf, wgrp_bf)) = _inproj(
            [xs, xp], [None, tp // ROW_TILE], g_mix[d], w_in[d], b_gate[d],
            [w_pool_up[d], w_sb_up[d], w_out[d], w_up[d], w_down[d], w_ple_gate[d]],
            [w_ple_proj[d], w_pool_grp[d]], ROW_TILE)
        weights = (wgrp_bf, row_vec(pool_scale[d]), wpu_bf, wsu_bf, wout_bf, row_vec(g_mlp[d]), wup_bf,
                   wdown_bf, row_vec(g_ple[d]), wpg_bf, wpp_bf, row_vec(g_final))
        op = _sb_prompt(qbp.reshape(bp, tp, SB_WIDTH), kbp.reshape(bp, tp, SB_WIDTH), vbp.reshape(bp, tp, SB_WIDTH))
        os_ = _sb_sample(qbs.reshape(bs, ts, SB_WIDTH), kbs.reshape(bs, ts, SB_WIDTH), vbs.reshape(bs, ts, SB_WIDTH),
                         cache_rows(cache_k), cache_rows(cache_v), d)
        halo_s = jnp.pad(state_pool[d], ((0, 0), (POOL_HALO - POOL_STATE, 0), (0, 0)))
        xp, = _post([(xp, up, up, op.reshape(bp * tp, SB_WIDTH), gates_p, p_prompt[d].reshape(bp * tp, PLE_DIM))],
                    weights, [dict(tiles_per_stream=tp // ROW_TILE, streams_per_tile=1, past_pos=0)],
                    tm=ROW_TILE, final_norm=final_norm)
        xs, = _post([(xs, us, halo_s, os_.reshape(n_s, SB_WIDTH), gates_s, p_sample[d].reshape(n_s, PLE_DIM))],
                    weights, [dict(tiles_per_stream=1, streams_per_tile=ROW_TILE // ts, past_pos=POOL_STATE)],
                    tm=ROW_TILE, final_norm=final_norm)
        token_major = lambda a: jnp.transpose(a.reshape(bp, SB_HEADS, SB_HEAD_DIM, tp), (0, 3, 1, 2))
        outs["kp"].append(token_major(kp))
        outs["vp"].append(token_major(vp))
        outs["pp"].append(up.reshape(bp, tp, POOL_WIDTH)[:, tp - POOL_STATE:])
        outs["ks"].append(ks.reshape(bs, ts, SB_HEADS, SB_HEAD_DIM))
        outs["vs"].append(vs.reshape(bs, ts, SB_HEADS, SB_HEAD_DIM))
        outs["ps"].append(us.reshape(bs, ts, POOL_WIDTH)[:, ts - POOL_STATE:])

    stack = lambda name: jnp.stack(outs[name])
    return (xp.reshape(bp, tp, D_MODEL), xs.reshape(bs, ts, D_MODEL),
            stack("kp"), stack("vp"), stack("pp"), stack("ks"), stack("vs"), stack("ps"))
```

```python
import functools

import jax
import jax.numpy as jnp
from jax import lax
from jax.experimental import pallas as pl
from jax.experimental.pallas import tpu as pltpu

D_MODEL = 1024
POOL_WIDTH = 512
POOL_WINDOWS = (2, 4, 8, 16)
POOL_GROUP_DIM = POOL_WIDTH // len(POOL_WINDOWS)
POOL_STATE = max(POOL_WINDOWS) - 1
POOL_HALO = 16
SB_HEADS = 8
SB_HEAD_DIM = 64
SB_WIDTH = SB_HEADS * SB_HEAD_DIM
HEAD_PAIR = 2 * SB_HEAD_DIM
D_FF = 4 * D_MODEL
PLE_DIM = 256
EPS = 1e-6

V7X_VMEM_BYTES = 64 * 1024 * 1024
VMEM_LIMIT_BYTES = V7X_VMEM_BYTES - 1 * 1024 * 1024

ROW_TILE = 512
INPROJ_CHAINS = 2
POST_CHAINS = 2
POST_LAG = 1
SB_BLOCK = 256
SB_PAIRS_PER_STEP = 4
SB_SAMPLE_STREAMS = 8
SB_PAIR_LAG = 2

BF16_SUBLANES = 16
W_IN_CHUNK = 512
N_WHOLE_CASTS = 2

LOG2E = 1.4426950408889634
SB_SKIP_BELOW = -160.0

BF16 = jnp.bfloat16
F32 = jnp.float32


def _inv_rms(x):
    return lax.rsqrt(jnp.mean(x * x, axis=-1, keepdims=True) + EPS)


def _rms_norm(x, g):
    return x * _inv_rms(x) * g


def _dot(a, b):
    return jnp.dot(a, b, preferred_element_type=F32)


def _dot_nt(a, b):
    return lax.dot_general(a, b, (((1,), (1,)), ((), ())), preferred_element_type=F32)


def _resident(shape):
    return pl.BlockSpec(shape, lambda *_: (0,) * len(shape), pipeline_mode=pl.Buffered(1))


def _interleave(chains, lag):
    waiting, live, tick = list(chains), [], 0
    while waiting or live:
        if waiting and tick % lag == 0:
            live.append(waiting.pop(0))
        live = [c for c in live if next(c, True) is None]
        tick += 1


def _store_head_rows(ref, row0, x):
    rows = x.shape[0]
    for h in range(SB_HEADS):
        ref[pl.ds(row0 * SB_HEADS + h, rows, stride=SB_HEADS), :] = x[:, h * SB_HEAD_DIM:(h + 1) * SB_HEAD_DIM]


def _group_steps(steps, maps_to):
    starts = [sum(steps[:g]) for g in range(len(steps))]
    return [lambda i, s=s, n=n: maps_to(jnp.clip(i - s, 0, n - 1)) for s, n in zip(starts, steps)]


def _run_group(steps, bodies):
    i = pl.program_id(0)
    start = 0
    for n, body in zip(steps, bodies):
        pl.when(jnp.logical_and(i >= start, i < start + n))(body)
        start += n


def _inproj_kernel(*refs, steps, n_cast, feature_major):
    groups = len(steps)
    cast_group = steps.index(max(steps))
    cast_start = sum(steps[:cast_group])
    g_ref, w_hbm, b_ref = refs[groups:groups + 3]
    cast_in = refs[groups + 3:groups + 3 + n_cast]
    outs = refs[groups + 3 + n_cast:-3]
    cast_out = outs[7 * groups:]
    w_ref, w_stage, w_sem = refs[-3:]
    i = pl.program_id(0)

    n_chunks = w_hbm.shape[1] // W_IN_CHUNK
    converted = [0]

    def chunk_copy(c):
        return pltpu.make_async_copy(w_hbm.at[:, c * W_IN_CHUNK:(c + 1) * W_IN_CHUNK], w_stage.at[c % 2],
                                     w_sem.at[c % 2])

    @pl.when(i == 0)
    def _():
        for c in range(min(2, n_chunks)):
            chunk_copy(c).start()

    def weights_ready(col_stop):
        while converted[0] * W_IN_CHUNK < col_stop:
            c = converted[0]
            converted[0] += 1

            @pl.when(i == 0)
            def _():
                chunk_copy(c).wait()
                w_ref[:, c * W_IN_CHUNK:(c + 1) * W_IN_CHUNK] = w_stage[c % 2].astype(BF16)
                if c + 2 < n_chunks:
                    chunk_copy(c + 2).start()

    @pl.when(i == 0)
    def _():
        for src, dst in zip(cast_in[n_cast - N_WHOLE_CASTS:], cast_out[n_cast - N_WHOLE_CASTS:]):
            dst[...] = src[...].astype(BF16)

    @pl.when(jnp.logical_and(i >= cast_start, i < cast_start + steps[cast_group]))
    def _():
        for src, dst in zip(cast_in[:n_cast - N_WHOLE_CASTS], cast_out[:n_cast - N_WHOLE_CASTS]):
            dst[...] = src[...].astype(BF16)

    _run_group(steps, [functools.partial(_inproj_tile, refs[g], g_ref, w_ref, b_ref, *outs[7 * g:7 * g + 7],
                                         feature_major=feature_major[g],
                                         weights_ready=weights_ready if g == 0 else (lambda col_stop: None))
                       for g in range(groups)])
    assert converted[0] == n_chunks


def _inproj_tile(x_ref, g_ref, w_ref, b_ref, u_ref, k_ref, v_ref, qb_ref, kb_ref, vb_ref, gate_ref, *, feature_major,
                 weights_ready):
    chunk = x_ref.shape[0] // INPROJ_CHAINS

    def store_kv(ref, c, x):
        if feature_major:
            ref[0, :, c * chunk:(c + 1) * chunk] = x.T
        else:
            _store_head_rows(ref, c * chunk, x)

    c0, c1, c2, c3 = POOL_WIDTH, POOL_WIDTH + SB_WIDTH, POOL_WIDTH + 2 * SB_WIDTH, POOL_WIDTH + 3 * SB_WIDTH

    def token_chain(c):
        rows = slice(c * chunk, (c + 1) * chunk)
        xn = _rms_norm(x_ref[rows, :], g_ref[...]).astype(BF16)
        yield
        weights_ready(c1)
        u_ref[rows, :] = _dot(xn, w_ref[:, :c0])
        qb_ref[rows, :] = (_dot(xn, w_ref[:, c0:c1]) * (SB_HEAD_DIM ** -0.5 * LOG2E)).astype(BF16)
        yield
        weights_ready(c2)
        k = _dot(xn, w_ref[:, c1:c2])
        store_kv(k_ref, c, k)
        kb_ref[rows, :] = k.astype(BF16)
        yield
        weights_ready(c3)
        v = _dot(xn, w_ref[:, c2:c3])
        store_kv(v_ref, c, v)
        vb_ref[rows, :] = v.astype(BF16)
        yield
        weights_ready(w_ref.shape[1])
        gate_ref[rows, :] = jax.nn.sigmoid(_dot(xn, w_ref[:, c3:]) + b_ref[...]).astype(BF16)

    _interleave([token_chain(c) for c in range(INPROJ_CHAINS)], 1)


def _inproj(xs, stream_tiles, g_mix, w_in, b_gate, chunked, whole, tm):
    assert len(whole) == N_WHOLE_CASTS
    in_width = w_in.shape[1]
    steps = [x.shape[0] // tm for x in xs]
    maps = _group_steps(steps, lambda j: (j, 0))
    buffers = [dict(pipeline_mode=pl.Buffered(1)) if n == 1 else {} for n in steps]
    in_specs = [pl.BlockSpec((tm, D_MODEL), m, **buf) for m, buf in zip(maps, buffers)]
    in_specs += [_resident((1, D_MODEL)), pl.BlockSpec(memory_space=pl.ANY), _resident((1, 2 * D_MODEL))]
    out_specs, out_shape = [], []
    for x, m, buf, tiles in zip(xs, maps, buffers, stream_tiles):
        n = x.shape[0]
        row = lambda width: pl.BlockSpec((tm, width), m, **buf)
        if tiles is None:
            kv_spec = pl.BlockSpec((tm * SB_HEADS, SB_HEAD_DIM), m, **buf)
            kv_shape = jax.ShapeDtypeStruct((n * SB_HEADS, SB_HEAD_DIM), F32)
        else:
            kv_spec = pl.BlockSpec((1, SB_WIDTH, tm), lambda i, m=m, tiles=tiles: (m(i)[0] // tiles, 0, m(i)[0] % tiles),
                                   **buf)
            kv_shape = jax.ShapeDtypeStruct((n // (tiles * tm), SB_WIDTH, tiles * tm), F32)
        bf_out = jax.ShapeDtypeStruct((n, SB_WIDTH), BF16)
        out_specs += [row(POOL_WIDTH), kv_spec, kv_spec, row(SB_WIDTH), row(SB_WIDTH), row(SB_WIDTH),
                      row(2 * D_MODEL)]
        out_shape += [jax.ShapeDtypeStruct((n, POOL_WIDTH), F32), kv_shape, kv_shape, bf_out, bf_out, bf_out,
                      jax.ShapeDtypeStruct((n, 2 * D_MODEL), BF16)]
    cast_specs = []
    for w in chunked:
        rows = w.shape[0] // max(steps)
        assert rows * max(steps) == w.shape[0] and rows % BF16_SUBLANES == 0
        cast_specs.append(pl.BlockSpec((rows, w.shape[1]), maps[steps.index(max(steps))]))
    cast_specs += [pl.BlockSpec(w.shape, lambda i, nd=w.ndim: (0,) * nd) for w in whole]
    cast_shape = [jax.ShapeDtypeStruct(w.shape, BF16) for w in (*chunked, *whole)]
    outs = pl.pallas_call(
        functools.partial(_inproj_kernel, steps=steps, n_cast=len(cast_specs),
                          feature_major=[tiles is not None for tiles in stream_tiles]),
        grid=(sum(steps),),
        in_specs=in_specs + cast_specs,
        out_specs=out_specs + cast_specs,
        out_shape=out_shape + cast_shape,
        scratch_shapes=[pltpu.VMEM((D_MODEL, in_width), BF16), pltpu.VMEM((2, D_MODEL, W_IN_CHUNK), F32),
                        pltpu.SemaphoreType.DMA((2,))],
        compiler_params=pltpu.CompilerParams(dimension_semantics=("arbitrary",),
                                             vmem_limit_bytes=VMEM_LIMIT_BYTES),
        name="inproj",
    )(*xs, g_mix.reshape(1, D_MODEL), w_in, b_gate.reshape(1, 2 * D_MODEL), *chunked, *whole)
    n_group_outs = 7 * len(xs)
    return ([outs[7 * g:7 * g + 7] for g in range(len(xs))],
            outs[n_group_outs:n_group_outs + len(chunked)], outs[n_group_outs + len(chunked):])


def _neg_lower(n):
    r = lax.broadcasted_iota(jnp.int32, (n, n), 0)
    c = lax.broadcasted_iota(jnp.int32, (n, n), 1)
    return jnp.where(r >= c, -1.0, 0.0).astype(BF16)


def _stack_heads(x):
    lane = lax.broadcasted_iota(jnp.int32, x.shape, 1)
    zero = jnp.zeros_like(x)
    return jnp.concatenate([jnp.where(lane < SB_HEAD_DIM, x, zero), jnp.where(lane >= SB_HEAD_DIM, x, zero)], axis=0)


def _causal_mask(tq, tk):
    r = lax.broadcasted_iota(jnp.int32, (2 * tq, tk), 0)
    c = lax.broadcasted_iota(jnp.int32, (2 * tq, tk), 1)
    return c < jnp.where(r >= tq, r - tq, r)


def _sb_pair_stages(q2, blocks, carry, emit):
    tq = q2.shape[0] // 2
    logits = []
    for k_blk, _, _, causal in blocks:
        z = _dot_nt(q2, k_blk)
        if causal is not None:
            z = jnp.where(causal, z, -jnp.inf)
        logits.append(z)
    yield
    softplus = [jnp.maximum(z, 0.0) + jnp.log(1.0 + jnp.exp2(-jnp.abs(z))) * LOG2E for z in logits]
    yield
    sums = [_dot(sp.astype(BF16), neg_tri) for (_, _, neg_tri, _), sp in zip(blocks, softplus)]
    yield
    weights = []
    for z, incl in zip(logits, sums):
        a = jnp.exp2(z + incl + carry).astype(BF16)
        weights += [a[:tq], a[tq:]]
        carry = carry + incl[:, :1]
    peak = jnp.max(carry)
    yield
    values = jnp.concatenate([v2 for _, v2, _, _ in blocks], axis=0)
    emit(_dot(jnp.concatenate(weights, axis=1), values), carry, peak)


def _sb_prompt_kernel(q_ref, k_ref, v_ref, o_ref, acc_ref, carry_ref, peak_ref):
    qi = pl.program_id(2)
    blk = SB_BLOCK
    pairs = range(SB_PAIRS_PER_STEP)
    lanes = lambda p: slice(p * HEAD_PAIR, (p + 1) * HEAD_PAIR)
    q2 = [_stack_heads(q_ref[0, :, lanes(p)]) for p in pairs]
    neg_tri = _neg_lower(blk)

    def key_block(p, j, causal):
        r0 = pl.multiple_of(j * blk, blk)
        return (k_ref[0, pl.ds(r0, blk), lanes(p)], _stack_heads(v_ref[0, pl.ds(r0, blk), lanes(p)]),
                neg_tri, causal)

    def sweep(block_ids, first):
        def emit(p, out, carry, peak):
            if first:
                acc_ref[:, lanes(p)] = out
            else:
                acc_ref[:, lanes(p)] += out
            carry_ref[p] = carry
            peak_ref[p] = peak

        _interleave([_sb_pair_stages(q2[p], [key_block(p, j, causal) for j, causal in block_ids],
                                     jnp.zeros((2 * blk, 1), F32) if first else carry_ref[p],
                                     functools.partial(emit, p)) for p in pairs], SB_PAIR_LAG)

    def alive():
        peak = functools.reduce(jnp.maximum, [peak_ref[p] for p in pairs])
        return (peak > SB_SKIP_BELOW).astype(jnp.int32)

    diagonal = (qi, _causal_mask(blk, blk))
    pl.when(qi == 0)(lambda: sweep([diagonal], True))
    pl.when(qi > 0)(lambda: sweep([diagonal, (qi - 1, None)], True))

    def body(state):
        j, _ = state
        sweep([(j, None)], False)
        return j - 1, alive()

    lax.while_loop(lambda s: jnp.logical_and(s[0] >= 0, s[1] > 0), body, (qi - 2, alive()))
    o_ref[0] = acc_ref[...].astype(o_ref.dtype)


def _sb_prompt(qb, kb, vb):
    b, t, _ = qb.shape
    blk = SB_BLOCK
    width = SB_PAIRS_PER_STEP * HEAD_PAIR
    kv_spec = pl.BlockSpec((1, t, width), lambda bi, hp, qi: (bi, 0, hp))
    q_spec = pl.BlockSpec((1, blk, width), lambda bi, hp, qi: (bi, qi, hp))
    return pl.pallas_call(
        _sb_prompt_kernel,
        grid=(b, SB_WIDTH // width, t // blk),
        in_specs=[q_spec, kv_spec, kv_spec],
        out_specs=q_spec,
        out_shape=jax.ShapeDtypeStruct((b, t, SB_WIDTH), BF16),
        scratch_shapes=[pltpu.VMEM((blk, width), F32), pltpu.VMEM((SB_PAIRS_PER_STEP, 2 * blk, 1), F32),
                        pltpu.SMEM((SB_PAIRS_PER_STEP,), F32)],
        compiler_params=pltpu.CompilerParams(dimension_semantics=("parallel", "parallel", "arbitrary"),
                                             vmem_limit_bytes=VMEM_LIMIT_BYTES),
        name="sb_prompt",
    )(qb, kb, vb)


def _sb_heads_stages(q, blocks, carry, emit):
    t = q[0].shape[0]
    logits = []
    for keys, _, _, causal, feature_major in blocks:
        score = _dot if feature_major else _dot_nt
        z = jnp.concatenate([score(qh, kh) for qh, kh in zip(q, keys)], axis=0)
        logits.append(z if causal is None else jnp.where(causal, z, -jnp.inf))
    yield
    softplus = [jnp.maximum(z, 0.0) + jnp.log(1.0 + jnp.exp2(-jnp.abs(z))) * LOG2E for z in logits]
    yield
    sums = [_dot(sp.astype(BF16), neg_tri) for (_, _, neg_tri, _, _), sp in zip(blocks, softplus)]
    yield
    weights = []
    for z, incl in zip(logits, sums):
        weights.append(jnp.exp2(z + incl + carry).astype(BF16))
        carry = carry + incl[:, :1]
    peak = jnp.max(carry)
    yield
    outs = None
    for (_, values, _, _, feature_major), a in zip(blocks, weights):
        mix = _dot_nt if feature_major else _dot
        part = [mix(a[h * t:(h + 1) * t], vh) for h, vh in enumerate(values)]
        outs = part if outs is None else [o + p for o, p in zip(outs, part)]
    emit(outs, carry, peak)


def _sb_sample_kernel(q_ref, k_ref, v_ref, ck_hbm, cv_hbm, o_ref, kbuf, vbuf, acc_ref, carry_ref, peak_ref, sem,
                      *, layer, past_len):
    step = pl.program_id(0)
    slot = step % 2
    blk = SB_BLOCK
    t = q_ref.shape[1]
    heads = range(SB_HEADS)
    local = range(SB_SAMPLE_STREAMS)
    cols = lambda h: slice(h * SB_HEAD_DIM, (h + 1) * SB_HEAD_DIM)
    last = past_len // blk - 1

    def fetch(at_step, j):
        tokens = pl.ds(pl.multiple_of(j * blk, blk), blk)
        copies = []
        for s in local:
            stream = at_step * SB_SAMPLE_STREAMS + s
            copies.append(pltpu.make_async_copy(ck_hbm.at[layer, stream, :, :, tokens],
                                                kbuf.at[at_step % 2, s], sem.at[at_step % 2, s, 0]))
            copies.append(pltpu.make_async_copy(cv_hbm.at[layer, stream, :, :, tokens],
                                                vbuf.at[at_step % 2, s], sem.at[at_step % 2, s, 1]))
        return copies

    def start(copies):
        for c in copies:
            c.start()

    def wait(copies):
        for c in copies:
            c.wait()

    def cached_block(s):
        return ([kbuf[slot, s, h].astype(BF16) for h in heads], [vbuf[slot, s, h].astype(BF16) for h in heads],
                _neg_lower(blk), None, True)

    def sweep(blocks_of, first):
        def emit(s, outs, carry, peak):
            for h in heads:
                if first:
                    acc_ref[s, :, cols(h)] = outs[h]
                else:
                    acc_ref[s, :, cols(h)] += outs[h]
            carry_ref[s] = carry
            peak_ref[s] = peak

        _interleave([_sb_heads_stages([q_ref[s, :, cols(h)] for h in heads], blocks_of(s),
                                      jnp.zeros((SB_HEADS * t, 1), F32) if first else carry_ref[s],
                                      functools.partial(emit, s)) for s in local], 1)

    def alive():
        peak = functools.reduce(jnp.maximum, [peak_ref[s] for s in local])
        return (peak > SB_SKIP_BELOW).astype(jnp.int32)

    pl.when(step == 0)(lambda: start(fetch(step, last)))
    pl.when(step + 1 < pl.num_programs(0))(lambda: start(fetch(step + 1, last)))
    wait(fetch(step, last))
    r = lax.broadcasted_iota(jnp.int32, (SB_HEADS * t, t), 0)
    causal = lax.broadcasted_iota(jnp.int32, (SB_HEADS * t, t), 1) < lax.rem(r, t)
    new_block = lambda s: ([k_ref[s, :, cols(h)] for h in heads], [v_ref[s, :, cols(h)] for h in heads],
                           _neg_lower(t), causal, False)
    sweep(lambda s: [new_block(s), cached_block(s)], True)

    def body(state):
        j, _ = state
        copies = fetch(step, j)
        start(copies)
        wait(copies)
        sweep(lambda s: [cached_block(s)], False)
        return j - 1, alive()

    lax.while_loop(lambda s: jnp.logical_and(s[0] >= 0, s[1] > 0), body, (last - 1, alive()))
    o_ref[...] = acc_ref[...].astype(o_ref.dtype)


def _sb_sample(qb, kb, vb, cache_k, cache_v, layer):
    b, t, _ = qb.shape
    past_len = cache_k.shape[-1]
    n = SB_SAMPLE_STREAMS
    assert past_len % SB_BLOCK == 0 and b % n == 0
    new_spec = pl.BlockSpec((n, t, SB_WIDTH), lambda i: (i, 0, 0))
    hbm_spec = pl.BlockSpec(memory_space=pl.ANY)
    block = (2, n, SB_HEADS, SB_HEAD_DIM, SB_BLOCK)
    return pl.pallas_call(
        functools.partial(_sb_sample_kernel, layer=layer, past_len=past_len),
        grid=(b // n,),
        in_specs=[new_spec, new_spec, new_spec, hbm_spec, hbm_spec],
        out_specs=new_spec,
        out_shape=jax.ShapeDtypeStruct((b, t, SB_WIDTH), BF16),
        scratch_shapes=[pltpu.VMEM(block, F32), pltpu.VMEM(block, F32),
                        pltpu.VMEM((n, t, SB_WIDTH), F32), pltpu.VMEM((n, SB_HEADS * t, 1), F32),
                        pltpu.SMEM((n,), F32), pltpu.SemaphoreType.DMA((2, n, 2))],
        compiler_params=pltpu.CompilerParams(dimension_semantics=("arbitrary",),
                                             vmem_limit_bytes=VMEM_LIMIT_BYTES),
        name="sb_sample",
    )(qb, kb, vb, cache_k, cache_v)


def _pool_diff(ext, first_pos):
    short_windows = not (isinstance(first_pos, int) and first_pos >= POOL_STATE)
    if short_windows:
        pos = first_pos + lax.broadcasted_iota(jnp.int32, (POOL_HALO, 1), 0)
    outs = []
    for g, window in enumerate(POOL_WINDOWS):
        cols = ext[:, g * POOL_GROUP_DIM:(g + 1) * POOL_GROUP_DIM]
        acc = cols
        shift = 1
        while shift < window:
            acc = acc + pltpu.roll(acc, shift, axis=0)
            shift *= 2
        out = acc[POOL_HALO:] * (1.0 / window) - cols[POOL_HALO:]
        if short_windows:
            inv_count = 1.0 / jnp.minimum(pos + 1, window).astype(F32)
            head = acc[POOL_HALO:2 * POOL_HALO] * inv_count - cols[POOL_HALO:2 * POOL_HALO]
            out = jnp.concatenate([head, out[POOL_HALO:]], axis=0)
        outs.append(out)
    return jnp.concatenate(outs, axis=1)


POST_GROUP_INPUTS = 6
POST_WEIGHTS = 12


def _post_kernel(*refs, steps, modes, final_norm):
    groups = len(steps)
    n_in = POST_GROUP_INPUTS * groups
    weights = refs[n_in:n_in + POST_WEIGHTS]
    ys = refs[n_in + POST_WEIGHTS:]
    _run_group(steps, [functools.partial(_post_tile, *refs[POST_GROUP_INPUTS * g:POST_GROUP_INPUTS * (g + 1)], *weights,
                                         ys[g], first_step=sum(steps[:g]), final_norm=final_norm, **modes[g])
                       for g in range(groups)])


def _post_tile(x_ref, u_ref, halo_ref, o_ref, gate_ref, p_ref,
               wgrp_ref, scale_ref, wpu_ref, wsu_ref, wout_ref,
               gmlp_ref, wup_ref, wdown_ref, gple_ref, wpg_ref, wpp_ref, gfin_ref,
               y_ref, *, first_step, tiles_per_stream, streams_per_tile, past_pos, final_norm):
    tm = u_ref.shape[0]
    chunk = tm // POST_CHAINS

    def pooled(c):
        u = u_ref[c * chunk:(c + 1) * chunk, :]
        if streams_per_tile == 1:
            step = (pl.program_id(0) - first_step) % tiles_per_stream
            if c == 0:
                halo = jnp.where(step == 0, 0.0, halo_ref[...])
            else:
                halo = u_ref[c * chunk - POOL_HALO:c * chunk, :]
            at_least = past_pos + c * chunk
            first_pos = at_least if at_least >= POOL_STATE else at_least + step * tm
            diff = _pool_diff(jnp.concatenate([halo, u], axis=0), first_pos)
        else:
            t = tm // streams_per_tile
            per_chunk = streams_per_tile // POST_CHAINS
            diff = jnp.concatenate(
                [_pool_diff(jnp.concatenate([halo_ref[c * per_chunk + s], u[s * t:(s + 1) * t]], axis=0), past_pos)
                 for s in range(per_chunk)], axis=0)
        return diff.astype(BF16)

    def token_chain(c):
        rows = slice(c * chunk, (c + 1) * chunk)
        diff = pooled(c)
        y_pool = jnp.concatenate(
            [_dot(diff[:, g * POOL_GROUP_DIM:(g + 1) * POOL_GROUP_DIM], wgrp_ref[g])
             for g in range(len(POOL_WINDOWS))], axis=1) * scale_ref[...]
        gates = gate_ref[rows, :].astype(F32)
        merged = (gates[:, :D_MODEL] * _dot(y_pool.astype(BF16), wpu_ref[...])
                  + gates[:, D_MODEL:] * _dot(o_ref[rows, :], wsu_ref[...]))
        yield
        x = x_ref[rows, :] + _dot(merged.astype(BF16), wout_ref[...])
        r = _inv_rms(x)
        xg = (x * gmlp_ref[...]).astype(BF16)
        yield
        h = jnp.square(jnp.maximum(_dot(xg, wup_ref[...]), 0.0)).astype(BF16)
        yield
        x = x + (r * r) * _dot(h, wdown_ref[...])
        r = _inv_rms(x)
        xg = (x * gple_ref[...]).astype(BF16)
        yield
        ple_gate = jax.nn.sigmoid(r * _dot(xg, wpg_ref[...]))
        x = x + ple_gate * _dot(p_ref[rows, :].astype(BF16), wpp_ref[...])
        if final_norm:
            x = _rms_norm(x, gfin_ref[...])
        y_ref[rows, :] = x

    _interleave([token_chain(c) for c in range(POST_CHAINS)], POST_LAG)


def _post(groups, weights, modes, *, tm, final_norm):
    assert len(weights) == POST_WEIGHTS and all(len(g) == POST_GROUP_INPUTS for g in groups)
    steps = [g[0].shape[0] // tm for g in groups]
    maps = _group_steps(steps, lambda j: j)
    in_specs, out_specs, out_shape = [], [], []
    for (x, *_), mode, m, n_steps in zip(groups, modes, maps, steps):
        buf = dict(pipeline_mode=pl.Buffered(1)) if n_steps == 1 else {}
        row = lambda width, m=m, buf=buf: pl.BlockSpec((tm, width), lambda i: (m(i), 0), **buf)
        if mode["streams_per_tile"] == 1:
            per_tile = tm // POOL_HALO
            halo_spec = pl.BlockSpec((POOL_HALO, POOL_WIDTH),
                                     lambda i, m=m: (jnp.maximum(m(i) * per_tile - 1, 0), 0))
        else:
            halo_spec = pl.BlockSpec((mode["streams_per_tile"], POOL_HALO, POOL_WIDTH), lambda i, m=m: (m(i), 0, 0),
                                     **buf)
        in_specs += [row(D_MODEL), row(POOL_WIDTH), halo_spec, row(SB_WIDTH), row(2 * D_MODEL), row(PLE_DIM)]
        out_specs.append(row(D_MODEL))
        out_shape.append(jax.ShapeDtypeStruct((x.shape[0], D_MODEL), F32))
    in_specs += [_resident(w.shape) for w in weights]
    return pl.pallas_call(
        functools.partial(_post_kernel, steps=steps, modes=modes, final_norm=final_norm),
        grid=(sum(steps),),
        in_specs=in_specs,
        out_specs=out_specs,
        out_shape=out_shape,
        compiler_params=pltpu.CompilerParams(dimension_semantics=("arbitrary",),
                                             vmem_limit_bytes=VMEM_LIMIT_BYTES),
        name="post",
    )(*[a for g in groups for a in g], *weights)


def kernel(x_prompt, x_sample, cache_k, cache_v, state_pool, p_prompt, p_sample, g_mix, w_in, b_gate, w_pool_grp, pool_scale, w_pool_up, w_sb_up, w_out, g_mlp, w_up, w_down, g_ple, w_ple_gate, w_ple_proj, g_final):
    depth = w_in.shape[0]
    bp, tp, _ = x_prompt.shape
    bs, ts, _ = x_sample.shape
    past_len = cache_k.shape[2]
    assert tp % ROW_TILE == 0 and tp % SB_BLOCK == 0
    n_s = bs * ts
    assert ts >= POOL_STATE and ts % 8 == 0 and n_s % ROW_TILE == 0 and (ROW_TILE // ts) % POST_CHAINS == 0
    xp = x_prompt.reshape(bp * tp, D_MODEL)
    xs = x_sample.reshape(n_s, D_MODEL)
    row_vec = lambda a: a.reshape(1, -1)
    cache_rows = lambda c: jnp.transpose(c, (0, 1, 3, 4, 2))

    outs = {name: [] for name in ("kp", "vp", "pp", "ks", "vs", "ps")}
    for d in range(depth):
        final_norm = d == depth - 1

        (((us, ks, vs, qbs, kbs, vbs, gates_s), (up, kp, vp, qbp, kbp, vbp, gates_p)),
         (wpu_bf, wsu_bf, wout_bf, wup_bf, wdown_bf, wpg_bf), (wpp_bf, wgrp_bf)) = _inproj(
            [xs, xp], [None, tp // ROW_TILE], g_mix[d], w_in[d], b_gate[d],
            [w_pool_up[d], w_sb_up[d], w_out[d], w_up[d], w_down[d], w_ple_gate[d]],
            [w_ple_proj[d], w_pool_grp[d]], ROW_TILE)
        weights = (wgrp_bf, row_vec(pool_scale[d]), wpu_bf, wsu_bf, wout_bf, row_vec(g_mlp[d]), wup_bf,
                   wdown_bf, row_vec(g_ple[d]), wpg_bf, wpp_bf, row_vec(g_final))
        op = _sb_prompt(qbp.reshape(bp, tp, SB_WIDTH), kbp.reshape(bp, tp, SB_WIDTH), vbp.reshape(bp, tp, SB_WIDTH))
        os_ = _sb_sample(qbs.reshape(bs, ts, SB_WIDTH), kbs.reshape(bs, ts, SB_WIDTH), vbs.reshape(bs, ts, SB_WIDTH),
                         cache_rows(cache_k), cache_rows(cache_v), d)
        halo_s = jnp.pad(state_pool[d], ((0, 0), (POOL_HALO - POOL_STATE, 0), (0, 0)))
        xs, xp = _post(
            [(xs, us, halo_s, os_.reshape(n_s, SB_WIDTH), gates_s, p_sample[d].reshape(n_s, PLE_DIM)),
             (xp, up, up, op.reshape(bp * tp, SB_WIDTH), gates_p, p_prompt[d].reshape(bp * tp, PLE_DIM))],
            weights,
            [dict(tiles_per_stream=1, streams_per_tile=ROW_TILE // ts, past_pos=POOL_STATE),
             dict(tiles_per_stream=tp // ROW_TILE, streams_per_tile=1, past_pos=0)],
            tm=ROW_TILE, final_norm=final_norm)
        token_major = lambda a: jnp.transpose(a.reshape(bp, SB_HEADS, SB_HEAD_DIM, tp), (0, 3, 1, 2))
        outs["kp"].append(token_major(kp))
        outs["vp"].append(token_major(vp))
        outs["pp"].append(up.reshape(bp, tp, POOL_WIDTH)[:, tp - POOL_STATE:])
        outs["ks"].append(ks.reshape(bs, ts, SB_HEADS, SB_HEAD_DIM))
        outs["vs"].append(vs.reshape(bs, ts, SB_HEADS, SB_HEAD_DIM))
        outs["ps"].append(us.reshape(bs, ts, POOL_WIDTH)[:, ts - POOL_STATE:])

    stack = lambda name: jnp.stack(outs[name])
    return (xp.reshape(bp, tp, D_MODEL), xs.reshape(bs, ts, D_MODEL),
            stack("kp"), stack("vp"), stack("pp"), stack("ks"), stack("vs"), stack("ps"))
```

```python
import functools

import jax
import jax.numpy as jnp
from jax import lax
from jax.experimental import pallas as pl
from jax.experimental.pallas import tpu as pltpu

D_MODEL = 1024
POOL_WIDTH = 512
POOL_WINDOWS = (2, 4, 8, 16)
POOL_GROUP_DIM = POOL_WIDTH // len(POOL_WINDOWS)
POOL_STATE = max(POOL_WINDOWS) - 1
POOL_HALO = 16
SB_HEADS = 8
SB_HEAD_DIM = 64
SB_WIDTH = SB_HEADS * SB_HEAD_DIM
HEAD_PAIR = 2 * SB_HEAD_DIM
D_FF = 4 * D_MODEL
PLE_DIM = 256
EPS = 1e-6

V7X_VMEM_BYTES = 64 * 1024 * 1024
VMEM_LIMIT_BYTES = V7X_VMEM_BYTES - 1 * 1024 * 1024

ROW_TILE = 512
INPROJ_CHAINS = 2
POST_CHAINS = 2
POST_LAG = 1
SB_BLOCK = 256
SB_PAIRS_PER_STEP = 4
SB_SAMPLE_STREAMS = 16
SB_PAIR_LAG = 2

BF16_SUBLANES = 16
W_IN_CHUNK = 512
N_WHOLE_CASTS = 2

LOG2E = 1.4426950408889634
SB_SKIP_BELOW = -160.0

BF16 = jnp.bfloat16
F32 = jnp.float32


def _inv_rms(x):
    return lax.rsqrt(jnp.mean(x * x, axis=-1, keepdims=True) + EPS)


def _rms_norm(x, g):
    return x * _inv_rms(x) * g


def _dot(a, b):
    return jnp.dot(a, b, preferred_element_type=F32)


def _dot_nt(a, b):
    return lax.dot_general(a, b, (((1,), (1,)), ((), ())), preferred_element_type=F32)


def _resident(shape):
    return pl.BlockSpec(shape, lambda *_: (0,) * len(shape), pipeline_mode=pl.Buffered(1))


def _interleave(chains, lag):
    waiting, live, tick = list(chains), [], 0
    while waiting or live:
        if waiting and tick % lag == 0:
            live.append(waiting.pop(0))
        live = [c for c in live if next(c, True) is None]
        tick += 1


def _store_head_rows(ref, row0, x):
    rows = x.shape[0]
    for h in range(SB_HEADS):
        ref[pl.ds(row0 * SB_HEADS + h, rows, stride=SB_HEADS), :] = x[:, h * SB_HEAD_DIM:(h + 1) * SB_HEAD_DIM]


def _group_steps(steps, maps_to):
    starts = [sum(steps[:g]) for g in range(len(steps))]
    return [lambda i, s=s, n=n: maps_to(jnp.clip(i - s, 0, n - 1)) for s, n in zip(starts, steps)]


def _run_group(steps, bodies):
    i = pl.program_id(0)
    start = 0
    for n, body in zip(steps, bodies):
        pl.when(jnp.logical_and(i >= start, i < start + n))(body)
        start += n


def _inproj_kernel(*refs, steps, n_cast, feature_major):
    groups = len(steps)
    cast_group = steps.index(max(steps))
    cast_start = sum(steps[:cast_group])
    g_ref, w_hbm, b_ref = refs[groups:groups + 3]
    cast_in = refs[groups + 3:groups + 3 + n_cast]
    outs = refs[groups + 3 + n_cast:-3]
    cast_out = outs[7 * groups:]
    w_ref, w_stage, w_sem = refs[-3:]
    i = pl.program_id(0)

    n_chunks = w_hbm.shape[1] // W_IN_CHUNK
    converted = [0]

    def chunk_copy(c):
        return pltpu.make_async_copy(w_hbm.at[:, c * W_IN_CHUNK:(c + 1) * W_IN_CHUNK], w_stage.at[c % 2],
                                     w_sem.at[c % 2])

    @pl.when(i == 0)
    def _():
        for c in range(min(2, n_chunks)):
            chunk_copy(c).start()

    def weights_ready(col_stop):
        while converted[0] * W_IN_CHUNK < col_stop:
            c = converted[0]
            converted[0] += 1

            @pl.when(i == 0)
            def _():
                chunk_copy(c).wait()
                w_ref[:, c * W_IN_CHUNK:(c + 1) * W_IN_CHUNK] = w_stage[c % 2].astype(BF16)
                if c + 2 < n_chunks:
                    chunk_copy(c + 2).start()

    @pl.when(i == 0)
    def _():
        for src, dst in zip(cast_in[n_cast - N_WHOLE_CASTS:], cast_out[n_cast - N_WHOLE_CASTS:]):
            dst[...] = src[...].astype(BF16)

    @pl.when(jnp.logical_and(i >= cast_start, i < cast_start + steps[cast_group]))
    def _():
        for src, dst in zip(cast_in[:n_cast - N_WHOLE_CASTS], cast_out[:n_cast - N_WHOLE_CASTS]):
            dst[...] = src[...].astype(BF16)

    _run_group(steps, [functools.partial(_inproj_tile, refs[g], g_ref, w_ref, b_ref, *outs[7 * g:7 * g + 7],
                                         feature_major=feature_major[g],
                                         weights_ready=weights_ready if g == 0 else (lambda col_stop: None))
                       for g in range(groups)])
    assert converted[0] == n_chunks


def _inproj_tile(x_ref, g_ref, w_ref, b_ref, u_ref, k_ref, v_ref, qb_ref, kb_ref, vb_ref, gate_ref, *, feature_major,
                 weights_ready):
    chunk = x_ref.shape[0] // INPROJ_CHAINS

    def store_kv(ref, c, x):
        if feature_major:
            ref[0, :, c * chunk:(c + 1) * chunk] = x.T
        else:
            _store_head_rows(ref, c * chunk, x)

    c0, c1, c2, c3 = POOL_WIDTH, POOL_WIDTH + SB_WIDTH, POOL_WIDTH + 2 * SB_WIDTH, POOL_WIDTH + 3 * SB_WIDTH

    def token_chain(c):
        rows = slice(c * chunk, (c + 1) * chunk)
        xn = _rms_norm(x_ref[rows, :], g_ref[...]).astype(BF16)
        yield
        weights_ready(c1)
        u_ref[rows, :] = _dot(xn, w_ref[:, :c0])
        qb_ref[rows, :] = (_dot(xn, w_ref[:, c0:c1]) * (SB_HEAD_DIM ** -0.5 * LOG2E)).astype(BF16)
        yield
        weights_ready(c2)
        k = _dot(xn, w_ref[:, c1:c2])
        store_kv(k_ref, c, k)
        kb_ref[rows, :] = k.astype(BF16)
        yield
        weights_ready(c3)
        v = _dot(xn, w_ref[:, c2:c3])
        store_kv(v_ref, c, v)
        vb_ref[rows, :] = v.astype(BF16)
        yield
        weights_ready(w_ref.shape[1])
        gate_ref[rows, :] = jax.nn.sigmoid(_dot(xn, w_ref[:, c3:]) + b_ref[...]).astype(BF16)

    _interleave([token_chain(c) for c in range(INPROJ_CHAINS)], 1)


def _inproj(xs, stream_tiles, g_mix, w_in, b_gate, chunked, whole, tm):
    assert len(whole) == N_WHOLE_CASTS
    in_width = w_in.shape[1]
    steps = [x.shape[0] // tm for x in xs]
    maps = _group_steps(steps, lambda j: (j, 0))
    buffers = [dict(pipeline_mode=pl.Buffered(1)) if n == 1 else {} for n in steps]
    in_specs = [pl.BlockSpec((tm, D_MODEL), m, **buf) for m, buf in zip(maps, buffers)]
    in_specs += [_resident((1, D_MODEL)), pl.BlockSpec(memory_space=pl.ANY), _resident((1, 2 * D_MODEL))]
    out_specs, out_shape = [], []
    for x, m, buf, tiles in zip(xs, maps, buffers, stream_tiles):
        n = x.shape[0]
        row = lambda width: pl.BlockSpec((tm, width), m, **buf)
        if tiles is None:
            kv_spec = pl.BlockSpec((tm * SB_HEADS, SB_HEAD_DIM), m, **buf)
            kv_shape = jax.ShapeDtypeStruct((n * SB_HEADS, SB_HEAD_DIM), F32)
        else:
            kv_spec = pl.BlockSpec((1, SB_WIDTH, tm), lambda i, m=m, tiles=tiles: (m(i)[0] // tiles, 0, m(i)[0] % tiles),
                                   **buf)
            kv_shape = jax.ShapeDtypeStruct((n // (tiles * tm), SB_WIDTH, tiles * tm), F32)
        bf_out = jax.ShapeDtypeStruct((n, SB_WIDTH), BF16)
        out_specs += [row(POOL_WIDTH), kv_spec, kv_spec, row(SB_WIDTH), row(SB_WIDTH), row(SB_WIDTH),
                      row(2 * D_MODEL)]
        out_shape += [jax.ShapeDtypeStruct((n, POOL_WIDTH), F32), kv_shape, kv_shape, bf_out, bf_out, bf_out,
                      jax.ShapeDtypeStruct((n, 2 * D_MODEL), BF16)]
    cast_specs = []
    for w in chunked:
        rows = w.shape[0] // max(steps)
        assert rows * max(steps) == w.shape[0] and rows % BF16_SUBLANES == 0
        cast_specs.append(pl.BlockSpec((rows, w.shape[1]), maps[steps.index(max(steps))]))
    cast_specs += [pl.BlockSpec(w.shape, lambda i, nd=w.ndim: (0,) * nd) for w in whole]
    cast_shape = [jax.ShapeDtypeStruct(w.shape, BF16) for w in (*chunked, *whole)]
    outs = pl.pallas_call(
        functools.partial(_inproj_kernel, steps=steps, n_cast=len(cast_specs),
                          feature_major=[tiles is not None for tiles in stream_tiles]),
        grid=(sum(steps),),
        in_specs=in_specs + cast_specs,
        out_specs=out_specs + cast_specs,
        out_shape=out_shape + cast_shape,
        scratch_shapes=[pltpu.VMEM((D_MODEL, in_width), BF16), pltpu.VMEM((2, D_MODEL, W_IN_CHUNK), F32),
                        pltpu.SemaphoreType.DMA((2,))],
        compiler_params=pltpu.CompilerParams(dimension_semantics=("arbitrary",),
                                             vmem_limit_bytes=VMEM_LIMIT_BYTES),
        name="inproj",
    )(*xs, g_mix.reshape(1, D_MODEL), w_in, b_gate.reshape(1, 2 * D_MODEL), *chunked, *whole)
    n_group_outs = 7 * len(xs)
    return ([outs[7 * g:7 * g + 7] for g in range(len(xs))],
            outs[n_group_outs:n_group_outs + len(chunked)], outs[n_group_outs + len(chunked):])


def _neg_lower(n):
    r = lax.broadcasted_iota(jnp.int32, (n, n), 0)
    c = lax.broadcasted_iota(jnp.int32, (n, n), 1)
    return jnp.where(r >= c, -1.0, 0.0).astype(BF16)


def _stack_heads(x):
    lane = lax.broadcasted_iota(jnp.int32, x.shape, 1)
    zero = jnp.zeros_like(x)
    return jnp.concatenate([jnp.where(lane < SB_HEAD_DIM, x, zero), jnp.where(lane >= SB_HEAD_DIM, x, zero)], axis=0)


def _causal_mask(tq, tk):
    r = lax.broadcasted_iota(jnp.int32, (2 * tq, tk), 0)
    c = lax.broadcasted_iota(jnp.int32, (2 * tq, tk), 1)
    return c < jnp.where(r >= tq, r - tq, r)


def _sb_pair_stages(q2, blocks, carry, emit):
    tq = q2.shape[0] // 2
    logits = []
    for k_blk, _, _, causal in blocks:
        z = _dot_nt(q2, k_blk)
        if causal is not None:
            z = jnp.where(causal, z, -jnp.inf)
        logits.append(z)
    yield
    softplus = [jnp.maximum(z, 0.0) + jnp.log(1.0 + jnp.exp2(-jnp.abs(z))) * LOG2E for z in logits]
    yield
    sums = [_dot(sp.astype(BF16), neg_tri) for (_, _, neg_tri, _), sp in zip(blocks, softplus)]
    yield
    weights = []
    for z, incl in zip(logits, sums):
        a = jnp.exp2(z + incl + carry).astype(BF16)
        weights += [a[:tq], a[tq:]]
        carry = carry + incl[:, :1]
    peak = jnp.max(carry)
    yield
    values = jnp.concatenate([v2 for _, v2, _, _ in blocks], axis=0)
    emit(_dot(jnp.concatenate(weights, axis=1), values), carry, peak)


def _sb_prompt_kernel(q_ref, k_ref, v_ref, o_ref, acc_ref, carry_ref, peak_ref):
    qi = pl.program_id(2)
    blk = SB_BLOCK
    pairs = range(SB_PAIRS_PER_STEP)
    lanes = lambda p: slice(p * HEAD_PAIR, (p + 1) * HEAD_PAIR)
    q2 = [_stack_heads(q_ref[0, :, lanes(p)]) for p in pairs]
    neg_tri = _neg_lower(blk)

    def key_block(p, j, causal):
        r0 = pl.multiple_of(j * blk, blk)
        return (k_ref[0, pl.ds(r0, blk), lanes(p)], _stack_heads(v_ref[0, pl.ds(r0, blk), lanes(p)]),
                neg_tri, causal)

    def sweep(block_ids, first):
        def emit(p, out, carry, peak):
            if first:
                acc_ref[:, lanes(p)] = out
            else:
                acc_ref[:, lanes(p)] += out
            carry_ref[p] = carry
            peak_ref[p] = peak

        _interleave([_sb_pair_stages(q2[p], [key_block(p, j, causal) for j, causal in block_ids],
                                     jnp.zeros((2 * blk, 1), F32) if first else carry_ref[p],
                                     functools.partial(emit, p)) for p in pairs], SB_PAIR_LAG)

    def alive():
        peak = functools.reduce(jnp.maximum, [peak_ref[p] for p in pairs])
        return (peak > SB_SKIP_BELOW).astype(jnp.int32)

    diagonal = (qi, _causal_mask(blk, blk))
    pl.when(qi == 0)(lambda: sweep([diagonal], True))
    pl.when(qi > 0)(lambda: sweep([diagonal, (qi - 1, None)], True))

    def body(state):
        j, _ = state
        sweep([(j, None)], False)
        return j - 1, alive()

    lax.while_loop(lambda s: jnp.logical_and(s[0] >= 0, s[1] > 0), body, (qi - 2, alive()))
    o_ref[0] = acc_ref[...].astype(o_ref.dtype)


def _sb_prompt(qb, kb, vb):
    b, t, _ = qb.shape
    blk = SB_BLOCK
    width = SB_PAIRS_PER_STEP * HEAD_PAIR
    kv_spec = pl.BlockSpec((1, t, width), lambda bi, hp, qi: (bi, 0, hp))
    q_spec = pl.BlockSpec((1, blk, width), lambda bi, hp, qi: (bi, qi, hp))
    return pl.pallas_call(
        _sb_prompt_kernel,
        grid=(b, SB_WIDTH // width, t // blk),
        in_specs=[q_spec, kv_spec, kv_spec],
        out_specs=q_spec,
        out_shape=jax.ShapeDtypeStruct((b, t, SB_WIDTH), BF16),
        scratch_shapes=[pltpu.VMEM((blk, width), F32), pltpu.VMEM((SB_PAIRS_PER_STEP, 2 * blk, 1), F32),
                        pltpu.SMEM((SB_PAIRS_PER_STEP,), F32)],
        compiler_params=pltpu.CompilerParams(dimension_semantics=("parallel", "parallel", "arbitrary"),
                                             vmem_limit_bytes=VMEM_LIMIT_BYTES),
        name="sb_prompt",
    )(qb, kb, vb)


def _sb_heads_stages(q, blocks, carry, emit):
    t = q[0].shape[0]
    logits = []
    for keys, _, _, causal, feature_major in blocks:
        score = _dot if feature_major else _dot_nt
        z = jnp.concatenate([score(qh, kh) for qh, kh in zip(q, keys)], axis=0)
        logits.append(z if causal is None else jnp.where(causal, z, -jnp.inf))
    yield
    softplus = [jnp.maximum(z, 0.0) + jnp.log(1.0 + jnp.exp2(-jnp.abs(z))) * LOG2E for z in logits]
    yield
    sums = [_dot(sp.astype(BF16), neg_tri) for (_, _, neg_tri, _, _), sp in zip(blocks, softplus)]
    yield
    weights = []
    for z, incl in zip(logits, sums):
        weights.append(jnp.exp2(z + incl + carry).astype(BF16))
        carry = carry + incl[:, :1]
    peak = jnp.max(carry)
    yield
    outs = None
    for (_, values, _, _, feature_major), a in zip(blocks, weights):
        mix = _dot_nt if feature_major else _dot
        part = [mix(a[h * t:(h + 1) * t], vh) for h, vh in enumerate(values)]
        outs = part if outs is None else [o + p for o, p in zip(outs, part)]
    emit(outs, carry, peak)


def _sb_sample_kernel(q_ref, k_ref, v_ref, ck_hbm, cv_hbm, o_ref, kbuf, vbuf, acc_ref, carry_ref, peak_ref, sem,
                      *, layer, past_len):
    step = pl.program_id(0)
    slot = step % 2
    blk = SB_BLOCK
    t = q_ref.shape[1]
    heads = range(SB_HEADS)
    local = range(SB_SAMPLE_STREAMS)
    cols = lambda h: slice(h * SB_HEAD_DIM, (h + 1) * SB_HEAD_DIM)
    last = past_len // blk - 1

    def fetch(at_step, j):
        tokens = pl.ds(pl.multiple_of(j * blk, blk), blk)
        copies = []
        for s in local:
            stream = at_step * SB_SAMPLE_STREAMS + s
            copies.append(pltpu.make_async_copy(ck_hbm.at[layer, stream, :, :, tokens],
                                                kbuf.at[at_step % 2, s], sem.at[at_step % 2, s, 0]))
            copies.append(pltpu.make_async_copy(cv_hbm.at[layer, stream, :, :, tokens],
                                                vbuf.at[at_step % 2, s], sem.at[at_step % 2, s, 1]))
        return copies

    def start(copies):
        for c in copies:
            c.start()

    def wait(copies):
        for c in copies:
            c.wait()

    def cached_block(s):
        return ([kbuf[slot, s, h].astype(BF16) for h in heads], [vbuf[slot, s, h].astype(BF16) for h in heads],
                _neg_lower(blk), None, True)

    def sweep(blocks_of, first):
        def emit(s, outs, carry, peak):
            for h in heads:
                if first:
                    acc_ref[s, :, cols(h)] = outs[h]
                else:
                    acc_ref[s, :, cols(h)] += outs[h]
            carry_ref[s] = carry
            peak_ref[s] = peak

        _interleave([_sb_heads_stages([q_ref[s, :, cols(h)] for h in heads], blocks_of(s),
                                      jnp.zeros((SB_HEADS * t, 1), F32) if first else carry_ref[s],
                                      functools.partial(emit, s)) for s in local], 1)

    def alive():
        peak = functools.reduce(jnp.maximum, [peak_ref[s] for s in local])
        return (peak > SB_SKIP_BELOW).astype(jnp.int32)

    pl.when(step == 0)(lambda: start(fetch(step, last)))
    pl.when(step + 1 < pl.num_programs(0))(lambda: start(fetch(step + 1, last)))
    wait(fetch(step, last))
    r = lax.broadcasted_iota(jnp.int32, (SB_HEADS * t, t), 0)
    causal = lax.broadcasted_iota(jnp.int32, (SB_HEADS * t, t), 1) < lax.rem(r, t)
    new_block = lambda s: ([k_ref[s, :, cols(h)] for h in heads], [v_ref[s, :, cols(h)] for h in heads],
                           _neg_lower(t), causal, False)
    sweep(lambda s: [new_block(s), cached_block(s)], True)

    def body(state):
        j, _ = state
        copies = fetch(step, j)
        start(copies)
        wait(copies)
        sweep(lambda s: [cached_block(s)], False)
        return j - 1, alive()

    lax.while_loop(lambda s: jnp.logical_and(s[0] >= 0, s[1] > 0), body, (last - 1, alive()))
    o_ref[...] = acc_ref[...].astype(o_ref.dtype)


def _sb_sample(qb, kb, vb, cache_k, cache_v, layer):
    b, t, _ = qb.shape
    past_len = cache_k.shape[-1]
    n = SB_SAMPLE_STREAMS
    assert past_len % SB_BLOCK == 0 and b % n == 0
    new_spec = pl.BlockSpec((n, t, SB_WIDTH), lambda i: (i, 0, 0))
    hbm_spec = pl.BlockSpec(memory_space=pl.ANY)
    block = (2, n, SB_HEADS, SB_HEAD_DIM, SB_BLOCK)
    return pl.pallas_call(
        functools.partial(_sb_sample_kernel, layer=layer, past_len=past_len),
        grid=(b // n,),
        in_specs=[new_spec, new_spec, new_spec, hbm_spec, hbm_spec],
        out_specs=new_spec,
        out_shape=jax.ShapeDtypeStruct((b, t, SB_WIDTH), BF16),
        scratch_shapes=[pltpu.VMEM(block, F32), pltpu.VMEM(block, F32),
                        pltpu.VMEM((n, t, SB_WIDTH), F32), pltpu.VMEM((n, SB_HEADS * t, 1), F32),
                        pltpu.SMEM((n,), F32), pltpu.SemaphoreType.DMA((2, n, 2))],
        compiler_params=pltpu.CompilerParams(dimension_semantics=("arbitrary",),
                                             vmem_limit_bytes=VMEM_LIMIT_BYTES),
        name="sb_sample",
    )(qb, kb, vb, cache_k, cache_v)


def _pool_diff(ext, first_pos):
    short_windows = not (isinstance(first_pos, int) and first_pos >= POOL_STATE)
    if short_windows:
        pos = first_pos + lax.broadcasted_iota(jnp.int32, (POOL_HALO, 1), 0)
    outs = []
    for g, window in enumerate(POOL_WINDOWS):
        cols = ext[:, g * POOL_GROUP_DIM:(g + 1) * POOL_GROUP_DIM]
        acc = cols
        shift = 1
        while shift < window:
            acc = acc + pltpu.roll(acc, shift, axis=0)
            shift *= 2
        out = acc[POOL_HALO:] * (1.0 / window) - cols[POOL_HALO:]
        if short_windows:
            inv_count = 1.0 / jnp.minimum(pos + 1, window).astype(F32)
            head = acc[POOL_HALO:2 * POOL_HALO] * inv_count - cols[POOL_HALO:2 * POOL_HALO]
            out = jnp.concatenate([head, out[POOL_HALO:]], axis=0)
        outs.append(out)
    return jnp.concatenate(outs, axis=1)


POST_GROUP_INPUTS = 6
POST_WEIGHTS = 12


def _post_kernel(*refs, steps, modes, final_norm):
    groups = len(steps)
    n_in = POST_GROUP_INPUTS * groups
    weights = refs[n_in:n_in + POST_WEIGHTS]
    ys = refs[n_in + POST_WEIGHTS:]
    _run_group(steps, [functools.partial(_post_tile, *refs[POST_GROUP_INPUTS * g:POST_GROUP_INPUTS * (g + 1)], *weights,
                                         ys[g], first_step=sum(steps[:g]), final_norm=final_norm, **modes[g])
                       for g in range(groups)])


def _post_tile(x_ref, u_ref, halo_ref, o_ref, gate_ref, p_ref,
               wgrp_ref, scale_ref, wpu_ref, wsu_ref, wout_ref,
               gmlp_ref, wup_ref, wdown_ref, gple_ref, wpg_ref, wpp_ref, gfin_ref,
               y_ref, *, first_step, tiles_per_stream, streams_per_tile, past_pos, final_norm):
    tm = u_ref.shape[0]
    chunk = tm // POST_CHAINS

    def pooled(c):
        u = u_ref[c * chunk:(c + 1) * chunk, :]
        if streams_per_tile == 1:
            step = (pl.program_id(0) - first_step) % tiles_per_stream
            if c == 0:
                halo = jnp.where(step == 0, 0.0, halo_ref[...])
            else:
                halo = u_ref[c * chunk - POOL_HALO:c * chunk, :]
            at_least = past_pos + c * chunk
            first_pos = at_least if at_least >= POOL_STATE else at_least + step * tm
            diff = _pool_diff(jnp.concatenate([halo, u], axis=0), first_pos)
        else:
            t = tm // streams_per_tile
            per_chunk = streams_per_tile // POST_CHAINS
            diff = jnp.concatenate(
                [_pool_diff(jnp.concatenate([halo_ref[c * per_chunk + s], u[s * t:(s + 1) * t]], axis=0), past_pos)
                 for s in range(per_chunk)], axis=0)
        return diff.astype(BF16)

    def token_chain(c):
        rows = slice(c * chunk, (c + 1) * chunk)
        diff = pooled(c)
        y_pool = jnp.concatenate(
            [_dot(diff[:, g * POOL_GROUP_DIM:(g + 1) * POOL_GROUP_DIM], wgrp_ref[g])
             for g in range(len(POOL_WINDOWS))], axis=1) * scale_ref[...]
        gates = gate_ref[rows, :].astype(F32)
        merged = (gates[:, :D_MODEL] * _dot(y_pool.astype(BF16), wpu_ref[...])
                  + gates[:, D_MODEL:] * _dot(o_ref[rows, :], wsu_ref[...]))
        yield
        x = x_ref[rows, :] + _dot(merged.astype(BF16), wout_ref[...])
        r = _inv_rms(x)
        xg = (x * gmlp_ref[...]).astype(BF16)
        yield
        h = jnp.square(jnp.maximum(_dot(xg, wup_ref[...]), 0.0)).astype(BF16)
        yield
        x = x + (r * r) * _dot(h, wdown_ref[...])
        r = _inv_rms(x)
        xg = (x * gple_ref[...]).astype(BF16)
        yield
        ple_gate = jax.nn.sigmoid(r * _dot(xg, wpg_ref[...]))
        x = x + ple_gate * _dot(p_ref[rows, :].astype(BF16), wpp_ref[...])
        if final_norm:
            x = _rms_norm(x, gfin_ref[...])
        y_ref[rows, :] = x

    _interleave([token_chain(c) for c in range(POST_CHAINS)], POST_LAG)


def _post(groups, weights, modes, *, tm, final_norm):
    assert len(weights) == POST_WEIGHTS and all(len(g) == POST_GROUP_INPUTS for g in groups)
    steps = [g[0].shape[0] // tm for g in groups]
    maps = _group_steps(steps, lambda j: j)
    in_specs, out_specs, out_shape = [], [], []
    for (x, *_), mode, m, n_steps in zip(groups, modes, maps, steps):
        buf = dict(pipeline_mode=pl.Buffered(1)) if n_steps == 1 else {}
        row = lambda width, m=m, buf=buf: pl.BlockSpec((tm, width), lambda i: (m(i), 0), **buf)
        if mode["streams_per_tile"] == 1:
            per_tile = tm // POOL_HALO
            halo_spec = pl.BlockSpec((POOL_HALO, POOL_WIDTH),
                                     lambda i, m=m: (jnp.maximum(m(i) * per_tile - 1, 0), 0))
        else:
            halo_spec = pl.BlockSpec((mode["streams_per_tile"], POOL_HALO, POOL_WIDTH), lambda i, m=m: (m(i), 0, 0),
                                     **buf)
        in_specs += [row(D_MODEL), row(POOL_WIDTH), halo_spec, row(SB_WIDTH), row(2 * D_MODEL), row(PLE_DIM)]
        out_specs.append(row(D_MODEL))
        out_shape.append(jax.ShapeDtypeStruct((x.shape[0], D_MODEL), F32))
    in_specs += [_resident(w.shape) for w in weights]
    return pl.pallas_call(
        functools.partial(_post_kernel, steps=steps, modes=modes, final_norm=final_norm),
        grid=(sum(steps),),
        in_specs=in_specs,
        out_specs=out_specs,
        out_shape=out_shape,
        compiler_params=pltpu.CompilerParams(dimension_semantics=("arbitrary",),
                                             vmem_limit_bytes=VMEM_LIMIT_BYTES),
        name="post",
    )(*[a for g in groups for a in g], *weights)


def kernel(x_prompt, x_sample, cache_k, cache_v, state_pool, p_prompt, p_sample, g_mix, w_in, b_gate, w_pool_grp, pool_scale, w_pool_up, w_sb_up, w_out, g_mlp, w_up, w_down, g_ple, w_ple_gate, w_ple_proj, g_final):
    depth = w_in.shape[0]
    bp, tp, _ = x_prompt.shape
    bs, ts, _ = x_sample.shape
    past_len = cache_k.shape[2]
    assert tp % ROW_TILE == 0 and tp % SB_BLOCK == 0
    n_s = bs * ts
    assert ts >= POOL_STATE and ts % 8 == 0 and n_s % ROW_TILE == 0 and (ROW_TILE // ts) % POST_CHAINS == 0
    xp = x_prompt.reshape(bp * tp, D_MODEL)
    xs = x_sample.reshape(n_s, D_MODEL)
    row_vec = lambda a: a.reshape(1, -1)
    cache_rows = lambda c: jnp.transpose(c, (0, 1, 3, 4, 2))

    outs = {name: [] for name in ("kp", "vp", "pp", "ks", "vs", "ps")}
    for d in range(depth):
        final_norm = d == depth - 1

        (((us, ks, vs, qbs, kbs, vbs, gates_s), (up, kp, vp, qbp, kbp, vbp, gates_p)),
         (wpu_bf, wsu_bf, wout_bf, wup_bf, wdown_bf, wpg_bf), (wpp_bf, wgrp_bf)) = _inproj(
            [xs, xp], [None, tp // ROW_TILE], g_mix[d], w_in[d], b_gate[d],
            [w_pool_up[d], w_sb_up[d], w_out[d], w_up[d], w_down[d], w_ple_gate[d]],
            [w_ple_proj[d], w_pool_grp[d]], ROW_TILE)
        weights = (wgrp_bf, row_vec(pool_scale[d]), wpu_bf, wsu_bf, wout_bf, row_vec(g_mlp[d]), wup_bf,
                   wdown_bf, row_vec(g_ple[d]), wpg_bf, wpp_bf, row_vec(g_final))
        op = _sb_prompt(qbp.reshape(bp, tp, SB_WIDTH), kbp.reshape(bp, tp, SB_WIDTH), vbp.reshape(bp, tp, SB_WIDTH))
        os_ = _sb_sample(qbs.reshape(bs, ts, SB_WIDTH), kbs.reshape(bs, ts, SB_WIDTH), vbs.reshape(bs, ts, SB_WIDTH),
                         cache_rows(cache_k), cache_rows(cache_v), d)
        halo_s = jnp.pad(state_pool[d], ((0, 0), (POOL_HALO - POOL_STATE, 0), (0, 0)))
        xs, xp = _post(
            [(xs, us, halo_s, os_.reshape(n_s, SB_WIDTH), gates_s, p_sample[d].reshape(n_s, PLE_DIM)),
             (xp, up, up, op.reshape(bp * tp, SB_WIDTH), gates_p, p_prompt[d].reshape(bp * tp, PLE_DIM))],
            weights,
            [dict(tiles_per_stream=1, streams_per_tile=ROW_TILE // ts, past_pos=POOL_STATE),
             dict(tiles_per_stream=tp // ROW_TILE, streams_per_tile=1, past_pos=0)],
            tm=ROW_TILE, final_norm=final_norm)
        token_major = lambda a: jnp.transpose(a.reshape(bp, SB_HEADS, SB_HEAD_DIM, tp), (0, 3, 1, 2))
        outs["kp"].append(token_major(kp))
        outs["vp"].append(token_major(vp))
        outs["pp"].append(up.reshape(bp, tp, POOL_WIDTH)[:, tp - POOL_STATE:])
        outs["ks"].append(ks.reshape(bs, ts, SB_HEADS, SB_HEAD_DIM))
        outs["vs"].append(vs.reshape(bs, ts, SB_HEADS, SB_HEAD_DIM))
        outs["ps"].append(us.reshape(bs, ts, POOL_WIDTH)[:, ts - POOL_STATE:])

    stack = lambda name: jnp.stack(outs[name])
    return (xp.reshape(bp, tp, D_MODEL), xs.reshape(bs, ts, D_MODEL),
            stack("kp"), stack("vp"), stack("pp"), stack("ks"), stack("vs"), stack("ps"))
```
